```python
import math
import jax, jax.numpy as jnp
from jax import lax
import numpy as np

D_MODEL = 2048
BATCH = 8
SEQ = 4096
DEPTH = 2

MEM_LEN = 256
CONV_WIDTH = D_MODEL // 4
CONV_KERNEL = 31
HEAD_DIM = 128
ATTN_WIDTH = D_MODEL // 2
ATTN_HEADS = ATTN_WIDTH // HEAD_DIM
SB_BLOCK = 128
LRU_WIDTH = D_MODEL // 4
LRU_HEAD_DIM = 128
LRU_HEADS = LRU_WIDTH // LRU_HEAD_DIM
LRU_CONV_KERNEL = 4
LRU_C = 8.0
MIX_WIDTH = CONV_WIDTH + ATTN_WIDTH + LRU_WIDTH
XATTN_HEADS = 4
XATTN_HEAD_DIM = 128
XATTN_WIDTH = XATTN_HEADS * XATTN_HEAD_DIM
IN_SIZES = (CONV_WIDTH, CONV_WIDTH, CONV_WIDTH,
            ATTN_WIDTH, ATTN_WIDTH, ATTN_WIDTH, ATTN_WIDTH,
            LRU_WIDTH, LRU_WIDTH)
IN_WIDTH = 3 * CONV_WIDTH + 4 * ATTN_WIDTH + 2 * LRU_WIDTH

kernel_name = "hymba_style_conv_stickbreak_rglru_trunk"


def rms_norm(x, g, eps=1e-6):
    xf = x.astype(jnp.float32)
    y = xf * lax.rsqrt(jnp.mean(xf * xf, axis=-1, keepdims=True) + eps)
    return (y * g.astype(jnp.float32)).astype(x.dtype)


def layer_norm(x, g, b, eps=1e-5):
    xf = x.astype(jnp.float32)
    mu = jnp.mean(xf, axis=-1, keepdims=True)
    var = jnp.mean(jnp.square(xf - mu), axis=-1, keepdims=True)
    y = (xf - mu) * lax.rsqrt(var + eps)
    return (y * g.astype(jnp.float32) + b.astype(jnp.float32)).astype(x.dtype)


def causal_depthwise_conv(x, w, b):
    K, C = w.shape
    y = lax.conv_general_dilated(
        x, w[:, None, :].astype(x.dtype), window_strides=(1,),
        padding=((K - 1, 0),), dimension_numbers=("NWC", "WIO", "NWC"),
        feature_group_count=C)
    return y + b.astype(x.dtype)


def conformer_conv(val, glu_gate, dw_w, dw_b, ln_g, ln_b, pw_w):
    u = val * jax.nn.sigmoid(glu_gate)
    u = causal_depthwise_conv(u, dw_w, dw_b)
    u = jax.nn.silu(layer_norm(u, ln_g, ln_b))
    return u @ pw_w


def stick_breaking_attention(q, k, v):
    S = q.shape[2]
    scale = HEAD_DIM ** -0.5
    outs = []
    for qb in range(S // SB_BLOCK):
        start = qb * SB_BLOCK
        end = start + SB_BLOCK
        z = jnp.einsum("bhqd,bhkd->bhqk", q[:, :, start:end], k[:, :, :end]).astype(jnp.float32) * scale
        t_pos = start + jnp.arange(SB_BLOCK)
        s_pos = jnp.arange(end)
        causal = s_pos[None, :] < t_pos[:, None]
        log_1mb = jnp.where(causal, jax.nn.log_sigmoid(-z), 0.0)
        after = jnp.sum(log_1mb, axis=-1, keepdims=True) - jnp.cumsum(log_1mb, axis=-1)
        w = jnp.where(causal, jnp.exp(jax.nn.log_sigmoid(z) + after), 0.0)
        outs.append(jnp.einsum("bhqk,bhkd->bhqd", w.astype(v.dtype), v[:, :, :end]))
    return jnp.concatenate(outs, axis=2)


def rg_lru(xc, wa, ba, wx, bx, lam):
    B, S, W = xc.shape
    xh = xc.reshape(B, S, LRU_HEADS, LRU_HEAD_DIM)
    r = jax.nn.sigmoid(jnp.einsum("bsnd,nde->bsne", xh, wa).reshape(B, S, W) + ba)
    i = jax.nn.sigmoid(jnp.einsum("bsnd,nde->bsne", xh, wx).reshape(B, S, W) + bx)
    log_a = -LRU_C * r.astype(jnp.float32) * jax.nn.softplus(-lam.astype(jnp.float32))
    a = jnp.exp(log_a)
    mult = jnp.sqrt(-jnp.expm1(2.0 * log_a))
    b = mult * (i * xc).astype(jnp.float32)

    def combine(left, right):
        a1, b1 = left
        a2, b2 = right
        return a1 * a2, a2 * b1 + b2

    _, h = lax.associative_scan(combine, (a, b), axis=1)
    return h.astype(xc.dtype)


def hybrid_mixer(h, w_in, conv_dw_w, conv_dw_b, conv_ln_g, conv_ln_b, conv_pw_w,
                 lru_conv_w, lru_conv_b, lru_wa, lru_ba, lru_wx, lru_bx, lru_lambda,
                 out_norm_conv, out_norm_attn, out_norm_lru, w_out):
    B, S, _ = h.shape
    u = h @ w_in
    points = []
    acc = 0
    for size in IN_SIZES[:-1]:
        acc += size
        points.append(acc)
    c_val, c_glu, c_gate, q, k, v, a_gate, r_x, r_gate = jnp.split(u, points, axis=-1)

    y_conv = conformer_conv(c_val, c_glu, conv_dw_w, conv_dw_b, conv_ln_g, conv_ln_b, conv_pw_w)

    def heads(t):
        return t.reshape(B, S, ATTN_HEADS, HEAD_DIM).transpose(0, 2, 1, 3)
    y_attn = stick_breaking_attention(heads(q), heads(k), heads(v))
    y_attn = y_attn.transpose(0, 2, 1, 3).reshape(B, S, ATTN_WIDTH)

    xc = causal_depthwise_conv(r_x, lru_conv_w, lru_conv_b)
    y_lru = rg_lru(xc, lru_wa, lru_ba, lru_wx, lru_bx, lru_lambda)

    y = jnp.concatenate([
        rms_norm(y_conv, out_norm_conv) * jax.nn.silu(c_gate),
        rms_norm(y_attn, out_norm_attn) * jax.nn.silu(a_gate),
        rms_norm(y_lru, out_norm_lru) * jax.nn.silu(r_gate),
    ], axis=-1)
    return y @ w_out


def memory_cross_attention(h, memn, wq, wkv, wo):
    B, S, _ = h.shape
    M = memn.shape[1]
    q = (h @ wq).reshape(B, S, XATTN_HEADS, XATTN_HEAD_DIM)
    k, v = jnp.split(memn @ wkv, 2, axis=-1)
    k = k.reshape(B, M, XATTN_HEADS, XATTN_HEAD_DIM)
    v = v.reshape(B, M, XATTN_HEADS, XATTN_HEAD_DIM)
    s = jnp.einsum("bqhd,bkhd->bhqk", q, k).astype(jnp.float32) * (XATTN_HEAD_DIM ** -0.5)
    p = jax.nn.softmax(s, axis=-1).astype(v.dtype)
    o = jnp.einsum("bhqk,bkhd->bqhd", p, v).reshape(B, S, XATTN_WIDTH)
    return o @ wo


def _fwd_setup_inputs(seed: int = 0) -> dict:
    key = jax.random.key(seed)
    ks = iter(jax.random.split(key, 32))
    f32 = jnp.float32

    def normal(shape, scale):
        return jax.random.normal(next(ks), shape, f32) * scale

    def gain(shape):
        return 1.0 + normal(shape, 0.02)

    u = jax.random.uniform(next(ks), (DEPTH, LRU_WIDTH), f32, minval=0.9, maxval=0.999)
    a0 = u ** (1.0 / LRU_C)
    lru_lambda = jnp.log(a0) - jnp.log1p(-a0)

    return {
        "x": normal((BATCH, SEQ, D_MODEL), 1.0),
        "mem": normal((BATCH, MEM_LEN, D_MODEL), 1.0),
        "mix_norm_g": gain((DEPTH, D_MODEL)),
        "w_in": normal((DEPTH, D_MODEL, IN_WIDTH), D_MODEL ** -0.5),
        "conv_dw_w": normal((DEPTH, CONV_KERNEL, CONV_WIDTH), CONV_KERNEL ** -0.5),
        "conv_dw_b": normal((DEPTH, CONV_WIDTH), 0.01),
        "conv_ln_g": gain((DEPTH, CONV_WIDTH)),
        "conv_ln_b": normal((DEPTH, CONV_WIDTH), 0.01),
        "conv_pw_w": normal((DEPTH, CONV_WIDTH, CONV_WIDTH), CONV_WIDTH ** -0.5),
        "lru_conv_w": normal((DEPTH, LRU_CONV_KERNEL, LRU_WIDTH), LRU_CONV_KERNEL ** -0.5),
        "lru_conv_b": normal((DEPTH, LRU_WIDTH), 0.01),
        "lru_wa": normal((DEPTH, LRU_HEADS, LRU_HEAD_DIM, LRU_HEAD_DIM), LRU_HEAD_DIM ** -0.5),
        "lru_ba": normal((DEPTH, LRU_WIDTH), 0.01),
        "lru_wx": normal((DEPTH, LRU_HEADS, LRU_HEAD_DIM, LRU_HEAD_DIM), LRU_HEAD_DIM ** -0.5),
        "lru_bx": normal((DEPTH, LRU_WIDTH), 0.01),
        "lru_lambda": lru_lambda,
        "out_norm_conv": gain((DEPTH, CONV_WIDTH)),
        "out_norm_attn": gain((DEPTH, ATTN_WIDTH)),
        "out_norm_lru": gain((DEPTH, LRU_WIDTH)),
        "w_out": normal((DEPTH, MIX_WIDTH, D_MODEL), MIX_WIDTH ** -0.5),
        "xattn_norm_g": gain((DEPTH, D_MODEL)),
        "mem_norm_g": gain((DEPTH, D_MODEL)),
        "xattn_wq": normal((DEPTH, D_MODEL, XATTN_WIDTH), D_MODEL ** -0.5),
        "xattn_wkv": normal((DEPTH, D_MODEL, 2 * XATTN_WIDTH), D_MODEL ** -0.5),
        "xattn_wo": normal((DEPTH, XATTN_WIDTH, D_MODEL), XATTN_WIDTH ** -0.5),
        "final_norm_g": gain((D_MODEL,)),
    }


def _fwd_reference(x, mem, mix_norm_g, w_in, conv_dw_w, conv_dw_b, conv_ln_g, conv_ln_b, conv_pw_w,
              lru_conv_w, lru_conv_b, lru_wa, lru_ba, lru_wx, lru_bx, lru_lambda,
              out_norm_conv, out_norm_attn, out_norm_lru, w_out,
              xattn_norm_g, mem_norm_g, xattn_wq, xattn_wkv, xattn_wo, final_norm_g):
    for l in range(DEPTH):
        h = rms_norm(x, mix_norm_g[l])
        x = x + hybrid_mixer(h, w_in[l], conv_dw_w[l], conv_dw_b[l], conv_ln_g[l], conv_ln_b[l],
                             conv_pw_w[l], lru_conv_w[l], lru_conv_b[l], lru_wa[l], lru_ba[l],
                             lru_wx[l], lru_bx[l], lru_lambda[l], out_norm_conv[l],
                             out_norm_attn[l], out_norm_lru[l], w_out[l])
        h = rms_norm(x, xattn_norm_g[l])
        memn = rms_norm(mem, mem_norm_g[l])
        x = x + memory_cross_attention(h, memn, xattn_wq[l], xattn_wkv[l], xattn_wo[l])
    return rms_norm(x, final_norm_g)


import jax as _jax
import jax.numpy as _jnp

TWIN_FORMAT = 'train_step'
FWD_PARAMS = ['x', 'mem', 'mix_norm_g', 'w_in', 'conv_dw_w', 'conv_dw_b', 'conv_ln_g', 'conv_ln_b', 'conv_pw_w', 'lru_conv_w', 'lru_conv_b', 'lru_wa', 'lru_ba', 'lru_wx', 'lru_bx', 'lru_lambda', 'out_norm_conv', 'out_norm_attn', 'out_norm_lru', 'w_out', 'xattn_norm_g', 'mem_norm_g', 'xattn_wq', 'xattn_wkv', 'xattn_wo', 'final_norm_g']
TWIN_WEIGHTS = ['mix_norm_g', 'w_in', 'conv_dw_w', 'conv_dw_b', 'conv_ln_g', 'conv_ln_b', 'conv_pw_w', 'lru_conv_w', 'lru_conv_b', 'lru_wa', 'lru_ba', 'lru_wx', 'lru_bx', 'lru_lambda', 'out_norm_conv', 'out_norm_attn', 'out_norm_lru', 'w_out', 'xattn_norm_g', 'mem_norm_g', 'xattn_wq', 'xattn_wkv', 'xattn_wo', 'final_norm_g']
TWIN_DIFF_INPUT = 'x'
TWIN_INPUTS = ['x', 'mem', 'mix_norm_g', 'w_in', 'conv_dw_w', 'conv_dw_b', 'conv_ln_g', 'conv_ln_b', 'conv_pw_w', 'lru_conv_w', 'lru_conv_b', 'lru_wa', 'lru_ba', 'lru_wx', 'lru_bx', 'lru_lambda', 'out_norm_conv', 'out_norm_attn', 'out_norm_lru', 'w_out', 'xattn_norm_g', 'mem_norm_g', 'xattn_wq', 'xattn_wkv', 'xattn_wo', 'final_norm_g', 'loss_target', 'm_mix_norm_g', 'm_w_in', 'm_conv_dw_w', 'm_conv_dw_b', 'm_conv_ln_g', 'm_conv_ln_b', 'm_conv_pw_w', 'm_lru_conv_w', 'm_lru_conv_b', 'm_lru_wa', 'm_lru_ba', 'm_lru_wx', 'm_lru_bx', 'm_lru_lambda', 'm_out_norm_conv', 'm_out_norm_attn', 'm_out_norm_lru', 'm_w_out', 'm_xattn_norm_g', 'm_mem_norm_g', 'm_xattn_wq', 'm_xattn_wkv', 'm_xattn_wo', 'm_final_norm_g', 'v_mix_norm_g', 'v_w_in', 'v_conv_dw_w', 'v_conv_dw_b', 'v_conv_ln_g', 'v_conv_ln_b', 'v_conv_pw_w', 'v_lru_conv_w', 'v_lru_conv_b', 'v_lru_wa', 'v_lru_ba', 'v_lru_wx', 'v_lru_bx', 'v_lru_lambda', 'v_out_norm_conv', 'v_out_norm_attn', 'v_out_norm_lru', 'v_w_out', 'v_xattn_norm_g', 'v_mem_norm_g', 'v_xattn_wq', 'v_xattn_wkv', 'v_xattn_wo', 'v_final_norm_g']
TWIN_OUTPUTS = ['loss', 'grad_x', 'grad_mix_norm_g', 'grad_w_in', 'grad_conv_dw_w', 'grad_conv_dw_b', 'grad_conv_ln_g', 'grad_conv_ln_b', 'grad_conv_pw_w', 'grad_lru_conv_w', 'grad_lru_conv_b', 'grad_lru_wa', 'grad_lru_ba', 'grad_lru_wx', 'grad_lru_bx', 'grad_lru_lambda', 'grad_out_norm_conv', 'grad_out_norm_attn', 'grad_out_norm_lru', 'grad_w_out', 'grad_xattn_norm_g', 'grad_mem_norm_g', 'grad_xattn_wq', 'grad_xattn_wkv', 'grad_xattn_wo', 'grad_final_norm_g', 'delta_mix_norm_g', 'delta_w_in', 'delta_conv_dw_w', 'delta_conv_dw_b', 'delta_conv_ln_g', 'delta_conv_ln_b', 'delta_conv_pw_w', 'delta_lru_conv_w', 'delta_lru_conv_b', 'delta_lru_wa', 'delta_lru_ba', 'delta_lru_wx', 'delta_lru_bx', 'delta_lru_lambda', 'delta_out_norm_conv', 'delta_out_norm_attn', 'delta_out_norm_lru', 'delta_w_out', 'delta_xattn_norm_g', 'delta_mem_norm_g', 'delta_xattn_wq', 'delta_xattn_wkv', 'delta_xattn_wo', 'delta_final_norm_g', 'new_m_mix_norm_g', 'new_m_w_in', 'new_m_conv_dw_w', 'new_m_conv_dw_b', 'new_m_conv_ln_g', 'new_m_conv_ln_b', 'new_m_conv_pw_w', 'new_m_lru_conv_w', 'new_m_lru_conv_b', 'new_m_lru_wa', 'new_m_lru_ba', 'new_m_lru_wx', 'new_m_lru_bx', 'new_m_lru_lambda', 'new_m_out_norm_conv', 'new_m_out_norm_attn', 'new_m_out_norm_lru', 'new_m_w_out', 'new_m_xattn_norm_g', 'new_m_mem_norm_g', 'new_m_xattn_wq', 'new_m_xattn_wkv', 'new_m_xattn_wo', 'new_m_final_norm_g', 'new_v_mix_norm_g', 'new_v_w_in', 'new_v_conv_dw_w', 'new_v_conv_dw_b', 'new_v_conv_ln_g', 'new_v_conv_ln_b', 'new_v_conv_pw_w', 'new_v_lru_conv_w', 'new_v_lru_conv_b', 'new_v_lru_wa', 'new_v_lru_ba', 'new_v_lru_wx', 'new_v_lru_bx', 'new_v_lru_lambda', 'new_v_out_norm_conv', 'new_v_out_norm_attn', 'new_v_out_norm_lru', 'new_v_w_out', 'new_v_xattn_norm_g', 'new_v_mem_norm_g', 'new_v_xattn_wq', 'new_v_xattn_wkv', 'new_v_xattn_wo', 'new_v_final_norm_g']
TWIN_LEAF_KINDS = {'loss': 'loss', 'grad_x': 'grad_x', 'grad_mix_norm_g': 'grad_w', 'grad_w_in': 'grad_w', 'grad_conv_dw_w': 'grad_w', 'grad_conv_dw_b': 'grad_w', 'grad_conv_ln_g': 'grad_w', 'grad_conv_ln_b': 'grad_w', 'grad_conv_pw_w': 'grad_w', 'grad_lru_conv_w': 'grad_w', 'grad_lru_conv_b': 'grad_w', 'grad_lru_wa': 'grad_w', 'grad_lru_ba': 'grad_w', 'grad_lru_wx': 'grad_w', 'grad_lru_bx': 'grad_w', 'grad_lru_lambda': 'grad_w', 'grad_out_norm_conv': 'grad_w', 'grad_out_norm_attn': 'grad_w', 'grad_out_norm_lru': 'grad_w', 'grad_w_out': 'grad_w', 'grad_xattn_norm_g': 'grad_w', 'grad_mem_norm_g': 'grad_w', 'grad_xattn_wq': 'grad_w', 'grad_xattn_wkv': 'grad_w', 'grad_xattn_wo': 'grad_w', 'grad_final_norm_g': 'grad_w', 'delta_mix_norm_g': 'delta_w', 'delta_w_in': 'delta_w', 'delta_conv_dw_w': 'delta_w', 'delta_conv_dw_b': 'delta_w', 'delta_conv_ln_g': 'delta_w', 'delta_conv_ln_b': 'delta_w', 'delta_conv_pw_w': 'delta_w', 'delta_lru_conv_w': 'delta_w', 'delta_lru_conv_b': 'delta_w', 'delta_lru_wa': 'delta_w', 'delta_lru_ba': 'delta_w', 'delta_lru_wx': 'delta_w', 'delta_lru_bx': 'delta_w', 'delta_lru_lambda': 'delta_w', 'delta_out_norm_conv': 'delta_w', 'delta_out_norm_attn': 'delta_w', 'delta_out_norm_lru': 'delta_w', 'delta_w_out': 'delta_w', 'delta_xattn_norm_g': 'delta_w', 'delta_mem_norm_g': 'delta_w', 'delta_xattn_wq': 'delta_w', 'delta_xattn_wkv': 'delta_w', 'delta_xattn_wo': 'delta_w', 'delta_final_norm_g': 'delta_w', 'new_m_mix_norm_g': 'new_m', 'new_m_w_in': 'new_m', 'new_m_conv_dw_w': 'new_m', 'new_m_conv_dw_b': 'new_m', 'new_m_conv_ln_g': 'new_m', 'new_m_conv_ln_b': 'new_m', 'new_m_conv_pw_w': 'new_m', 'new_m_lru_conv_w': 'new_m', 'new_m_lru_conv_b': 'new_m', 'new_m_lru_wa': 'new_m', 'new_m_lru_ba': 'new_m', 'new_m_lru_wx': 'new_m', 'new_m_lru_bx': 'new_m', 'new_m_lru_lambda': 'new_m', 'new_m_out_norm_conv': 'new_m', 'new_m_out_norm_attn': 'new_m', 'new_m_out_norm_lru': 'new_m', 'new_m_w_out': 'new_m', 'new_m_xattn_norm_g': 'new_m', 'new_m_mem_norm_g': 'new_m', 'new_m_xattn_wq': 'new_m', 'new_m_xattn_wkv': 'new_m', 'new_m_xattn_wo': 'new_m', 'new_m_final_norm_g': 'new_m', 'new_v_mix_norm_g': 'new_v', 'new_v_w_in': 'new_v', 'new_v_conv_dw_w': 'new_v', 'new_v_conv_dw_b': 'new_v', 'new_v_conv_ln_g': 'new_v', 'new_v_conv_ln_b': 'new_v', 'new_v_conv_pw_w': 'new_v', 'new_v_lru_conv_w': 'new_v', 'new_v_lru_conv_b': 'new_v', 'new_v_lru_wa': 'new_v', 'new_v_lru_ba': 'new_v', 'new_v_lru_wx': 'new_v', 'new_v_lru_bx': 'new_v', 'new_v_lru_lambda': 'new_v', 'new_v_out_norm_conv': 'new_v', 'new_v_out_norm_attn': 'new_v', 'new_v_out_norm_lru': 'new_v', 'new_v_w_out': 'new_v', 'new_v_xattn_norm_g': 'new_v', 'new_v_mem_norm_g': 'new_v', 'new_v_xattn_wq': 'new_v', 'new_v_xattn_wkv': 'new_v', 'new_v_xattn_wo': 'new_v', 'new_v_final_norm_g': 'new_v'}


def _forward(args):
    return _fwd_reference(*[args[k] for k in FWD_PARAMS])


def _output_shape():
    def fwd():
        inp = _fwd_setup_inputs(0)
        return _fwd_reference(*[inp[k] for k in FWD_PARAMS])
    out = _jax.eval_shape(fwd)
    return out.shape, out.dtype

N_MICROBATCH = 1
ADAM_LR = 0.001
ADAM_B1 = 0.9
ADAM_B2 = 0.999
ADAM_EPS = 1e-08
ADAM_WD = 0.01
ADAM_STEP = 10
PER_EXAMPLE_BATCH_AXIS = {'x': 0, 'mem': 0, 'loss_target': 0}
SHARED_INPUTS = []
_WEIGHT_DTYPES = {'mix_norm_g': _jnp.float32, 'w_in': _jnp.float32, 'conv_dw_w': _jnp.float32, 'conv_dw_b': _jnp.float32, 'conv_ln_g': _jnp.float32, 'conv_ln_b': _jnp.float32, 'conv_pw_w': _jnp.float32, 'lru_conv_w': _jnp.float32, 'lru_conv_b': _jnp.float32, 'lru_wa': _jnp.float32, 'lru_ba': _jnp.float32, 'lru_wx': _jnp.float32, 'lru_bx': _jnp.float32, 'lru_lambda': _jnp.float32, 'out_norm_conv': _jnp.float32, 'out_norm_attn': _jnp.float32, 'out_norm_lru': _jnp.float32, 'w_out': _jnp.float32, 'xattn_norm_g': _jnp.float32, 'mem_norm_g': _jnp.float32, 'xattn_wq': _jnp.float32, 'xattn_wkv': _jnp.float32, 'xattn_wo': _jnp.float32, 'final_norm_g': _jnp.float32}
MOMENT_SCALE = {'mix_norm_g': 7.107526e-02, 'w_in': 3.907951e-02, 'conv_dw_w': 4.787680e-02, 'conv_dw_b': 1.055433e-01, 'conv_ln_g': 5.876753e-02, 'conv_ln_b': 5.286878e-02, 'conv_pw_w': 4.689201e-02, 'lru_conv_w': 4.966643e-02, 'lru_conv_b': 4.526528e-01, 'lru_wa': 1.354228e-02, 'lru_ba': 1.137310e-02, 'lru_wx': 2.409020e-02, 'lru_bx': 1.754660e-02, 'lru_lambda': 2.335316e-02, 'out_norm_conv': 4.562196e-02, 'out_norm_attn': 4.659584e-02, 'out_norm_lru': 4.920014e-02, 'w_out': 4.646614e-02, 'xattn_norm_g': 7.964389e-03, 'mem_norm_g': 1.305297e-02, 'xattn_wq': 1.574835e-02, 'xattn_wkv': 1.811389e-02, 'xattn_wo': 1.002657e-02, 'final_norm_g': 1.599588e+01}


def _to_microbatches(a, axis):
    t = _jnp.moveaxis(a, axis, 0)
    t = t.reshape((N_MICROBATCH, t.shape[0] // N_MICROBATCH) + t.shape[1:])
    return _jnp.moveaxis(t, 1, axis + 1)


def setup_inputs(seed: int = 0) -> dict:
    inp = _fwd_setup_inputs(seed)
    key = _jax.random.fold_in(_jax.random.key(seed), 7919)
    shape, _ = _output_shape()
    out = dict(inp)
    out["loss_target"] = _jax.random.normal(_jax.random.fold_in(key, 0), shape, _jnp.float32)
    for i, name in enumerate(TWIN_WEIGHTS):
        w = inp[name].astype(_jnp.float32)
        if MOMENT_SCALE is None:
            s = _jnp.sqrt(_jnp.mean(_jnp.square(w)) + 1e-30)
        else:
            s = MOMENT_SCALE[name]
        km, kv = _jax.random.split(_jax.random.fold_in(key, i + 1))
        out[name] = w
        out["m_" + name] = s * _jax.random.normal(km, w.shape, _jnp.float32)
        out["v_" + name] = (s * s) * _jax.random.uniform(kv, w.shape, _jnp.float32, 0.5, 1.5)
    if N_MICROBATCH > 1:
        for name, axis in PER_EXAMPLE_BATCH_AXIS.items():
            out[name] = _to_microbatches(out[name], axis)
    return {'x': out['x'], 'mem': out['mem'], 'mix_norm_g': out['mix_norm_g'], 'w_in': out['w_in'], 'conv_dw_w': out['conv_dw_w'], 'conv_dw_b': out['conv_dw_b'], 'conv_ln_g': out['conv_ln_g'], 'conv_ln_b': out['conv_ln_b'], 'conv_pw_w': out['conv_pw_w'], 'lru_conv_w': out['lru_conv_w'], 'lru_conv_b': out['lru_conv_b'], 'lru_wa': out['lru_wa'], 'lru_ba': out['lru_ba'], 'lru_wx': out['lru_wx'], 'lru_bx': out['lru_bx'], 'lru_lambda': out['lru_lambda'], 'out_norm_conv': out['out_norm_conv'], 'out_norm_attn': out['out_norm_attn'], 'out_norm_lru': out['out_norm_lru'], 'w_out': out['w_out'], 'xattn_norm_g': out['xattn_norm_g'], 'mem_norm_g': out['mem_norm_g'], 'xattn_wq': out['xattn_wq'], 'xattn_wkv': out['xattn_wkv'], 'xattn_wo': out['xattn_wo'], 'final_norm_g': out['final_norm_g'], 'loss_target': out['loss_target'], 'm_mix_norm_g': out['m_mix_norm_g'], 'm_w_in': out['m_w_in'], 'm_conv_dw_w': out['m_conv_dw_w'], 'm_conv_dw_b': out['m_conv_dw_b'], 'm_conv_ln_g': out['m_conv_ln_g'], 'm_conv_ln_b': out['m_conv_ln_b'], 'm_conv_pw_w': out['m_conv_pw_w'], 'm_lru_conv_w': out['m_lru_conv_w'], 'm_lru_conv_b': out['m_lru_conv_b'], 'm_lru_wa': out['m_lru_wa'], 'm_lru_ba': out['m_lru_ba'], 'm_lru_wx': out['m_lru_wx'], 'm_lru_bx': out['m_lru_bx'], 'm_lru_lambda': out['m_lru_lambda'], 'm_out_norm_conv': out['m_out_norm_conv'], 'm_out_norm_attn': out['m_out_norm_attn'], 'm_out_norm_lru': out['m_out_norm_lru'], 'm_w_out': out['m_w_out'], 'm_xattn_norm_g': out['m_xattn_norm_g'], 'm_mem_norm_g': out['m_mem_norm_g'], 'm_xattn_wq': out['m_xattn_wq'], 'm_xattn_wkv': out['m_xattn_wkv'], 'm_xattn_wo': out['m_xattn_wo'], 'm_final_norm_g': out['m_final_norm_g'], 'v_mix_norm_g': out['v_mix_norm_g'], 'v_w_in': out['v_w_in'], 'v_conv_dw_w': out['v_conv_dw_w'], 'v_conv_dw_b': out['v_conv_dw_b'], 'v_conv_ln_g': out['v_conv_ln_g'], 'v_conv_ln_b': out['v_conv_ln_b'], 'v_conv_pw_w': out['v_conv_pw_w'], 'v_lru_conv_w': out['v_lru_conv_w'], 'v_lru_conv_b': out['v_lru_conv_b'], 'v_lru_wa': out['v_lru_wa'], 'v_lru_ba': out['v_lru_ba'], 'v_lru_wx': out['v_lru_wx'], 'v_lru_bx': out['v_lru_bx'], 'v_lru_lambda': out['v_lru_lambda'], 'v_out_norm_conv': out['v_out_norm_conv'], 'v_out_norm_attn': out['v_out_norm_attn'], 'v_out_norm_lru': out['v_out_norm_lru'], 'v_w_out': out['v_w_out'], 'v_xattn_norm_g': out['v_xattn_norm_g'], 'v_mem_norm_g': out['v_mem_norm_g'], 'v_xattn_wq': out['v_xattn_wq'], 'v_xattn_wkv': out['v_xattn_wkv'], 'v_xattn_wo': out['v_xattn_wo'], 'v_final_norm_g': out['v_final_norm_g']}


def _loss(weights, diff, rest, loss_target):
    with _jax.named_scope("forward"):
        args = {**rest, TWIN_DIFF_INPUT: diff, **{k: w.astype(_WEIGHT_DTYPES[k]) for k, w in weights.items()}}
        y = _forward(args)
    with _jax.named_scope("loss_head"):
        err = _jnp.square(y.astype(_jnp.float32) - loss_target)
        return 0.5 * _jnp.sum(_jnp.mean(err, axis=-1)) if err.ndim else 0.5 * err


def _adamw(w, g, m, v):
    m = ADAM_B1 * m + (1.0 - ADAM_B1) * g
    v = ADAM_B2 * v + (1.0 - ADAM_B2) * _jnp.square(g)
    m_hat = m / (1.0 - ADAM_B1 ** ADAM_STEP)
    v_hat = v / (1.0 - ADAM_B2 ** ADAM_STEP)
    delta = -ADAM_LR * (m_hat / (_jnp.sqrt(v_hat) + ADAM_EPS) + ADAM_WD * w)
    return delta, m, v


def reference(x, mem, mix_norm_g, w_in, conv_dw_w, conv_dw_b, conv_ln_g, conv_ln_b, conv_pw_w, lru_conv_w, lru_conv_b, lru_wa, lru_ba, lru_wx, lru_bx, lru_lambda, out_norm_conv, out_norm_attn, out_norm_lru, w_out, xattn_norm_g, mem_norm_g, xattn_wq, xattn_wkv, xattn_wo, final_norm_g, loss_target, m_mix_norm_g, m_w_in, m_conv_dw_w, m_conv_dw_b, m_conv_ln_g, m_conv_ln_b, m_conv_pw_w, m_lru_conv_w, m_lru_conv_b, m_lru_wa, m_lru_ba, m_lru_wx, m_lru_bx, m_lru_lambda, m_out_norm_conv, m_out_norm_attn, m_out_norm_lru, m_w_out, m_xattn_norm_g, m_mem_norm_g, m_xattn_wq, m_xattn_wkv, m_xattn_wo, m_final_norm_g, v_mix_norm_g, v_w_in, v_conv_dw_w, v_conv_dw_b, v_conv_ln_g, v_conv_ln_b, v_conv_pw_w, v_lru_conv_w, v_lru_conv_b, v_lru_wa, v_lru_ba, v_lru_wx, v_lru_bx, v_lru_lambda, v_out_norm_conv, v_out_norm_attn, v_out_norm_lru, v_w_out, v_xattn_norm_g, v_mem_norm_g, v_xattn_wq, v_xattn_wkv, v_xattn_wo, v_final_norm_g):
    given = dict(x=x, mem=mem, mix_norm_g=mix_norm_g, w_in=w_in, conv_dw_w=conv_dw_w, conv_dw_b=conv_dw_b, conv_ln_g=conv_ln_g, conv_ln_b=conv_ln_b, conv_pw_w=conv_pw_w, lru_conv_w=lru_conv_w, lru_conv_b=lru_conv_b, lru_wa=lru_wa, lru_ba=lru_ba, lru_wx=lru_wx, lru_bx=lru_bx, lru_lambda=lru_lambda, out_norm_conv=out_norm_conv, out_norm_attn=out_norm_attn, out_norm_lru=out_norm_lru, w_out=w_out, xattn_norm_g=xattn_norm_g, mem_norm_g=mem_norm_g, xattn_wq=xattn_wq, xattn_wkv=xattn_wkv, xattn_wo=xattn_wo, final_norm_g=final_norm_g, loss_target=loss_target, m_mix_norm_g=m_mix_norm_g, m_w_in=m_w_in, m_conv_dw_w=m_conv_dw_w, m_conv_dw_b=m_conv_dw_b, m_conv_ln_g=m_conv_ln_g, m_conv_ln_b=m_conv_ln_b, m_conv_pw_w=m_conv_pw_w, m_lru_conv_w=m_lru_conv_w, m_lru_conv_b=m_lru_conv_b, m_lru_wa=m_lru_wa, m_lru_ba=m_lru_ba, m_lru_wx=m_lru_wx, m_lru_bx=m_lru_bx, m_lru_lambda=m_lru_lambda, m_out_norm_conv=m_out_norm_conv, m_out_norm_attn=m_out_norm_attn, m_out_norm_lru=m_out_norm_lru, m_w_out=m_w_out, m_xattn_norm_g=m_xattn_norm_g, m_mem_norm_g=m_mem_norm_g, m_xattn_wq=m_xattn_wq, m_xattn_wkv=m_xattn_wkv, m_xattn_wo=m_xattn_wo, m_final_norm_g=m_final_norm_g, v_mix_norm_g=v_mix_norm_g, v_w_in=v_w_in, v_conv_dw_w=v_conv_dw_w, v_conv_dw_b=v_conv_dw_b, v_conv_ln_g=v_conv_ln_g, v_conv_ln_b=v_conv_ln_b, v_conv_pw_w=v_conv_pw_w, v_lru_conv_w=v_lru_conv_w, v_lru_conv_b=v_lru_conv_b, v_lru_wa=v_lru_wa, v_lru_ba=v_lru_ba, v_lru_wx=v_lru_wx, v_lru_bx=v_lru_bx, v_lru_lambda=v_lru_lambda, v_out_norm_conv=v_out_norm_conv, v_out_norm_attn=v_out_norm_attn, v_out_norm_lru=v_out_norm_lru, v_w_out=v_w_out, v_xattn_norm_g=v_xattn_norm_g, v_mem_norm_g=v_mem_norm_g, v_xattn_wq=v_xattn_wq, v_xattn_wkv=v_xattn_wkv, v_xattn_wo=v_xattn_wo, v_final_norm_g=v_final_norm_g)
    weights = {n: given[n] for n in TWIN_WEIGHTS}
    shared = {n: given[n] for n in SHARED_INPUTS}
    per_example = {n: given[n] for n in ['x', 'mem']}
    grad_fn = _jax.value_and_grad(_loss, argnums=(0, 1))

    def one_microbatch(ex, loss_target):
        ex = dict(ex)
        diff = ex.pop(TWIN_DIFF_INPUT)
        return grad_fn(weights, diff, {**shared, **ex}, loss_target)

    if N_MICROBATCH == 1:
        loss, (grad_w, grad_x) = one_microbatch(per_example, given["loss_target"])
    else:
        def body(carry, xs):
            loss_sum, grad_sum = carry
            l_k, (gw_k, gx_k) = one_microbatch(xs[0], xs[1])
            with _jax.named_scope("update"):
                return (loss_sum + l_k, _jax.tree.map(_jnp.add, grad_sum, gw_k)), gx_k

        init = (_jnp.zeros((), _jnp.float32), _jax.tree.map(_jnp.zeros_like, weights))
        (loss, grad_w), grad_x = _jax.lax.scan(body, init, (per_example, given["loss_target"]))
    with _jax.named_scope("update"):
        delta_w, new_m, new_v = {}, {}, {}
        for n in TWIN_WEIGHTS:
            delta_w[n], new_m[n], new_v[n] = _adamw(weights[n], grad_w[n], given["m_" + n], given["v_" + n])
    return (loss, grad_x, *[grad_w[n] for n in TWIN_WEIGHTS], *[delta_w[n] for n in TWIN_WEIGHTS],
            *[new_m[n] for n in TWIN_WEIGHTS], *[new_v[n] for n in TWIN_WEIGHTS])
```

```python
import functools
import math

import jax
import jax.numpy as jnp
from jax import lax
from jax.experimental import pallas as pl
from jax.experimental.pallas import tpu as pltpu

F32 = jnp.float32
BF16 = jnp.bfloat16

N_DEV = 8
LANE = 128
HEAD = 128
VMEM_LIMIT = 56 * 1024 * 1024
PACK_COLS = 512
RMS_EPS = 1e-6
LN_EPS = 1e-5
LRU_C = 8.0
CONV_HALO = 32
LRU_HALO = 8

ADAM_LR, ADAM_B1, ADAM_B2, ADAM_EPS, ADAM_WD, ADAM_STEP = 0.001, 0.9, 0.999, 1e-08, 0.01, 10

MESH = pl.DeviceIdType.MESH


def _tile(n, cands):
    for c in cands:
        if n % c == 0:
            return c
    raise ValueError(f"no tile of {cands} divides {n}")


def _params(*sem):
    return pltpu.CompilerParams(dimension_semantics=sem, vmem_limit_bytes=VMEM_LIMIT)


def _dot(a, b):
    return lax.dot_general(a, b, (((1,), (0,)), ((), ())), preferred_element_type=F32)


def _dot_nt(a, b):
    return lax.dot_general(a, b, (((1,), (1,)), ((), ())), preferred_element_type=F32)


def _dot_tn(a, b):
    return lax.dot_general(a, b, (((0,), (0,)), ((), ())), preferred_element_type=F32)


def _sigmoid(x):
    return 1.0 / (1.0 + jnp.exp(-x))


def _expm1(x):
    series = x * (1.0 + x * (0.5 + x * (1.0 / 6.0 + x * (1.0 / 24.0))))
    return jnp.where(jnp.abs(x) < 0.05, series, jnp.exp(x) - 1.0)


def _my_place():
    return lax.axis_index("x"), lax.axis_index("y"), lax.axis_index("c")


def _flip(v, d):
    return 1 - v if d else v


def _all_gather(x2d, name):
    rows, cols = x2d.shape

    def body(x_ref, out_ref, send_sems, recv_sems, local_sem):
        x, y, c = _my_place()
        me, sibling = (x, y, c), (x, y, 1 - c)
        chips = [(1 - x, y), (x, 1 - y), (1 - x, 1 - y)]

        def blk(px, py, pc):
            return out_ref.at[4 * px + 2 * py + pc]

        def copy(k, block, to, src=None):
            return pltpu.make_async_remote_copy(
                src_ref=blk(*block) if src is None else src, dst_ref=blk(*block),
                send_sem=send_sems.at[k], recv_sem=recv_sems.at[k], device_id=to, device_id_type=MESH)

        mine = pltpu.make_async_copy(x_ref, blk(*me), local_sem)
        mine.start()
        first = [copy(0, me, sibling, src=x_ref)]
        first += [copy(1 + j, me, (*chip, c), src=x_ref) for j, chip in enumerate(chips)]
        for cp in first:
            cp.start()
        passed = [copy(4 + j, (*chip, c), sibling) for j, chip in enumerate(chips)]
        for j, chip in enumerate(chips):
            copy(1 + j, (*chip, c), me).wait_recv()
            passed[j].start()
        copy(0, sibling, me).wait_recv()
        for j, chip in enumerate(chips):
            copy(4 + j, (*chip, 1 - c), me).wait_recv()
        for cp in first + passed:
            cp.wait_send()
        mine.wait()

    return pl.pallas_call(
        body, name=name,
        out_shape=jax.ShapeDtypeStruct((N_DEV, rows, cols), x2d.dtype),
        in_specs=[pl.BlockSpec(memory_space=pl.ANY)],
        out_specs=pl.BlockSpec(memory_space=pl.ANY),
        scratch_shapes=[pltpu.SemaphoreType.DMA((7,)), pltpu.SemaphoreType.DMA((7,)), pltpu.SemaphoreType.DMA],
    )(x2d)


def _all_to_all(x3d, name):
    _, rows, cols = x3d.shape

    def body(in_ref, out_ref, send_sems, recv_sems, local_sem):
        x, y, c = _my_place()
        me = 4 * x + 2 * y + c
        mine = pltpu.make_async_copy(in_ref.at[me], out_ref.at[me], local_sem)
        mine.start()
        sends, recvs = [], []
        for k in range(1, N_DEV):
            px, py, pc = _flip(x, k & 4), _flip(y, k & 2), _flip(c, k & 1)
            peer = 4 * px + 2 * py + pc
            sends.append(pltpu.make_async_remote_copy(
                src_ref=in_ref.at[peer], dst_ref=out_ref.at[me], send_sem=send_sems.at[k - 1],
                recv_sem=recv_sems.at[k - 1], device_id=(px, py, pc), device_id_type=MESH))
            recvs.append(pltpu.make_async_remote_copy(
                src_ref=in_ref.at[me], dst_ref=out_ref.at[peer], send_sem=send_sems.at[k - 1],
                recv_sem=recv_sems.at[k - 1], device_id=(px, py, pc), device_id_type=MESH))
        for cp in sends:
            cp.start()
        for cp in recvs:
            cp.wait_recv()
        for cp in sends:
            cp.wait_send()
        mine.wait()

    return pl.pallas_call(
        body, name=name,
        out_shape=jax.ShapeDtypeStruct((N_DEV, rows, cols), x3d.dtype),
        in_specs=[pl.BlockSpec(memory_space=pl.ANY)],
        out_specs=pl.BlockSpec(memory_space=pl.ANY),
        scratch_shapes=[pltpu.SemaphoreType.DMA((7,)), pltpu.SemaphoreType.DMA((7,)), pltpu.SemaphoreType.DMA],
    )(x3d)


def _sum_blocks(x3d, name):
    n, rows, cols = x3d.shape
    tr = _tile(rows, (512, 256, 128, 64, 32, 16))

    def body(x_ref, o_ref):
        acc = x_ref[0].astype(F32)
        for j in range(1, n):
            acc = acc + x_ref[j].astype(F32)
        o_ref[...] = acc

    return pl.pallas_call(
        body, name=name, out_shape=jax.ShapeDtypeStruct((rows, cols), F32), grid=(rows // tr,),
        in_specs=[pl.BlockSpec((n, tr, cols), lambda i: (0, i, 0))],
        out_specs=pl.BlockSpec((tr, cols), lambda i: (i, 0)),
        compiler_params=_params("parallel"),
    )(x3d)


def _pack(arrs, dtype, lead=None):
    if lead is None:
        flat = jnp.concatenate([a.reshape(-1).astype(dtype) for a in arrs])
        n = flat.shape[0]
        total = -(-n // (16 * PACK_COLS)) * (16 * PACK_COLS)
        return jnp.pad(flat, (0, total - n)).reshape(-1, PACK_COLS)
    flat = jnp.concatenate([a.reshape(lead, -1).astype(dtype) for a in arrs], axis=1)
    n = flat.shape[1]
    total = -(-n // (16 * PACK_COLS)) * (16 * PACK_COLS)
    return jnp.pad(flat, ((0, 0), (0, total - n))).reshape(lead, -1, PACK_COLS)


def _unpack(packed, shapes, lead=None):
    out, off = [], 0
    if lead is None:
        flat = packed.reshape(-1)
        for s in shapes:
            n = math.prod(s)
            out.append(flat[off:off + n].reshape(s))
            off += n
        return out
    flat = packed.reshape(lead, -1)
    for s in shapes:
        n = math.prod(s)
        out.append(flat[:, off:off + n].reshape((lead,) + tuple(s)))
        off += n
    return out


def _join_blocks(g, axis):
    g = jnp.moveaxis(g, 0, axis)
    s = g.shape
    return g.reshape(s[:axis] + (s[axis] * s[axis + 1],) + s[axis + 2:])


def _split_blocks(full, axis):
    s = full.shape
    g = full.reshape(s[:axis] + (N_DEV, s[axis] // N_DEV) + s[axis + 1:])
    return jnp.moveaxis(g, axis, 0)


def _mm(a, b, *, ta=False, tb=False, out_dtype=F32, add=None, name):
    if ta:
        kdim, m = a.shape
    else:
        m, kdim = a.shape
    n = b.shape[0] if tb else b.shape[1]
    tm = _tile(m, (512, 256, 128, 64, 32, 16))
    tn = _tile(n, (512, 256, 128))
    tk = _tile(kdim, (1024, 512, 256, 128, 64, 32, 16))
    nk = kdim // tk
    a_spec = pl.BlockSpec((tk, tm), lambda i, j, k: (k, i)) if ta else pl.BlockSpec((tm, tk), lambda i, j, k: (i, k))
    b_spec = pl.BlockSpec((tn, tk), lambda i, j, k: (j, k)) if tb else pl.BlockSpec((tk, tn), lambda i, j, k: (k, j))
    o_spec = pl.BlockSpec((tm, tn), lambda i, j, k: (i, j))
    dims = (((0 if ta else 1,), (1 if tb else 0,)), ((), ()))

    def body(*refs):
        if add is None:
            a_ref, b_ref, o_ref, acc_ref = refs
        else:
            a_ref, b_ref, add_ref, o_ref, acc_ref = refs
        k = pl.program_id(2)

        @pl.when(k == 0)
        def _():
            acc_ref[...] = jnp.zeros_like(acc_ref)

        acc_ref[...] += lax.dot_general(a_ref[...].astype(BF16), b_ref[...].astype(BF16), dims,
                                        preferred_element_type=F32)

        @pl.when(k == nk - 1)
        def _():
            r = acc_ref[...]
            if add is not None:
                r = r + add_ref[...]
            o_ref[...] = r.astype(out_dtype)

    ins, specs = [a, b], [a_spec, b_spec]
    if add is not None:
        ins.append(add)
        specs.append(o_spec)
    return pl.pallas_call(
        body, name=name, out_shape=jax.ShapeDtypeStruct((m, n), out_dtype), grid=(m // tm, n // tn, nk),
        in_specs=specs, out_specs=o_spec, scratch_shapes=[pltpu.VMEM((tm, tn), F32)],
        compiler_params=_params("parallel", "parallel", "arbitrary"),
    )(*ins)


def _rms_fwd(x, g, name):
    s, d = x.shape
    tr = _tile(s, (256, 128, 64, 32, 16))

    def body(x_ref, g_ref, o_ref):
        xv = x_ref[...]
        r = lax.rsqrt(jnp.mean(xv * xv, axis=-1, keepdims=True) + RMS_EPS)
        o_ref[...] = (xv * r * g_ref[...]).astype(BF16)

    return pl.pallas_call(
        body, name=name, out_shape=jax.ShapeDtypeStruct((s, d), BF16), grid=(s // tr,),
        in_specs=[pl.BlockSpec((tr, d), lambda i: (i, 0)), pl.BlockSpec((1, d), lambda i: (0, 0))],
        out_specs=pl.BlockSpec((tr, d), lambda i: (i, 0)), compiler_params=_params("parallel"),
    )(x, g)


def _rms_bwd(x, g, dh, resid, name):
    s, d = x.shape
    tr = _tile(s, (256, 128, 64, 32, 16))

    def body(*refs):
        if resid is None:
            x_ref, g_ref, dh_ref, dx_ref, dg_ref = refs
        else:
            x_ref, g_ref, dh_ref, res_ref, dx_ref, dg_ref = refs

        @pl.when(pl.program_id(0) == 0)
        def _():
            dg_ref[...] = jnp.zeros_like(dg_ref)

        xv = x_ref[...]
        r = lax.rsqrt(jnp.mean(xv * xv, axis=-1, keepdims=True) + RMS_EPS)
        xh = xv * r
        dhv = dh_ref[...]
        dg_ref[0:1, :] += jnp.sum(dhv * xh, axis=0, keepdims=True)
        dyn = dhv * g_ref[...]
        dx = r * (dyn - xh * jnp.mean(dyn * xh, axis=-1, keepdims=True))
        if resid is not None:
            dx = dx + res_ref[...]
        dx_ref[...] = dx

    row = pl.BlockSpec((tr, d), lambda i: (i, 0))
    ins = [x, g, dh] + ([] if resid is None else [resid])
    specs = [row, pl.BlockSpec((1, d), lambda i: (0, 0)), row] + ([] if resid is None else [row])
    dx, dg = pl.pallas_call(
        body, name=name,
        out_shape=(jax.ShapeDtypeStruct((s, d), F32), jax.ShapeDtypeStruct((8, d), F32)), grid=(s // tr,),
        in_specs=specs, out_specs=(row, pl.BlockSpec((8, d), lambda i: (0, 0))),
        compiler_params=_params("arbitrary"),
    )(*ins)
    return dx, dg[0]


def _loss_bwd(x, g, tgt, name):
    s, d = x.shape
    tr = _tile(s, (256, 128, 64, 32, 16))

    def body(x_ref, g_ref, t_ref, dx_ref, dg_ref, loss_ref):
        @pl.when(pl.program_id(0) == 0)
        def _():
            dg_ref[...] = jnp.zeros_like(dg_ref)
            loss_ref[...] = jnp.zeros_like(loss_ref)

        xv = x_ref[...]
        r = lax.rsqrt(jnp.mean(xv * xv, axis=-1, keepdims=True) + RMS_EPS)
        xh = xv * r
        e = xh * g_ref[...] - t_ref[...]
        per_tok = jnp.mean(e * e, axis=-1, keepdims=True)
        loss_ref[...] += 0.5 * jnp.sum(per_tok, axis=0, keepdims=True)
        dy = e * (1.0 / d)
        dg_ref[0:1, :] += jnp.sum(dy * xh, axis=0, keepdims=True)
        dyn = dy * g_ref[...]
        dx_ref[...] = r * (dyn - xh * jnp.mean(dyn * xh, axis=-1, keepdims=True))

    row = pl.BlockSpec((tr, d), lambda i: (i, 0))
    dx, dg, loss = pl.pallas_call(
        body, name=name,
        out_shape=(jax.ShapeDtypeStruct((s, d), F32), jax.ShapeDtypeStruct((8, d), F32),
                   jax.ShapeDtypeStruct((8, LANE), F32)),
        grid=(s // tr,),
        in_specs=[row, pl.BlockSpec((1, d), lambda i: (0, 0)), row],
        out_specs=(row, pl.BlockSpec((8, d), lambda i: (0, 0)), pl.BlockSpec((8, LANE), lambda i: (0, 0))),
        compiler_params=_params("arbitrary"),
    )(x, g, tgt)
    return loss[0:1, 0:1], dx, dg[0:1]


def _conv_taps(gbuf, w_ref, tt, ntap, lo):
    acc = w_ref[0:1, :] * gbuf[pl.ds(lo, tt), :]
    for k in range(1, ntap):
        acc = acc + w_ref[k:k + 1, :] * gbuf[pl.ds(lo + k, tt), :]
    return acc


def _conv_time_tile(s):
    return _tile(s, (256, 128, 64, 32))


def _conv_fwd(u, wpad, dw_b, ln_g, ln_b, pw, name):
    s = u.shape[0]
    c = pw.shape[0]
    ntap = 31
    tt = _conv_time_tile(s)
    hb = tt // CONV_HALO

    def body(val_ref, glu_ref, valh_ref, gluh_ref, w_ref, b_ref, lg_ref, lb_ref, pw_ref, o_ref, gbuf):
        i = pl.program_id(0)
        glh = valh_ref[...] * _sigmoid(gluh_ref[...])
        gbuf[0:CONV_HALO, :] = jnp.where(i > 0, glh, 0.0)
        gbuf[CONV_HALO:CONV_HALO + tt, :] = val_ref[...] * _sigmoid(glu_ref[...])
        acc = _conv_taps(gbuf, w_ref, tt, ntap, CONV_HALO - (ntap - 1)) + b_ref[...]
        xc = acc - jnp.mean(acc, axis=-1, keepdims=True)
        rstd = lax.rsqrt(jnp.mean(xc * xc, axis=-1, keepdims=True) + LN_EPS)
        ln = xc * rstd * lg_ref[...] + lb_ref[...]
        sw = ln * _sigmoid(ln)
        o_ref[...] = _dot(sw.astype(BF16), pw_ref[...])

    vec = pl.BlockSpec((1, c), lambda i: (0, 0))
    return pl.pallas_call(
        body, name=name, out_shape=jax.ShapeDtypeStruct((s, c), F32), grid=(s // tt,),
        in_specs=[pl.BlockSpec((tt, c), lambda i: (i, 0)), pl.BlockSpec((tt, c), lambda i: (i, 1)),
                  pl.BlockSpec((CONV_HALO, c), lambda i: (jnp.maximum(i * hb - 1, 0), 0)),
                  pl.BlockSpec((CONV_HALO, c), lambda i: (jnp.maximum(i * hb - 1, 0), 1)),
                  pl.BlockSpec((32, c), lambda i: (0, 0)), vec, vec, vec,
                  pl.BlockSpec((c, c), lambda i: (0, 0))],
        out_specs=pl.BlockSpec((tt, c), lambda i: (i, 0)),
        scratch_shapes=[pltpu.VMEM((CONV_HALO + tt, c), F32)],
        compiler_params=_params("parallel"),
    )(u, u, u, u, wpad, dw_b, ln_g, ln_b, pw)


def _conv_bwd_post(u, dyc, wpad, dw_b, ln_g, ln_b, pw, name):
    s = u.shape[0]
    c = pw.shape[0]
    ntap = 31
    tt = _conv_time_tile(s)
    hb = tt // CONV_HALO

    def body(val_ref, glu_ref, valh_ref, gluh_ref, dy_ref, w_ref, b_ref, lg_ref, lb_ref, pw_ref,
             dd_ref, gl_ref, dpw_ref, vec_ref, gbuf):
        i = pl.program_id(0)

        @pl.when(i == 0)
        def _():
            dpw_ref[...] = jnp.zeros_like(dpw_ref)
            vec_ref[...] = jnp.zeros_like(vec_ref)

        glh = valh_ref[...] * _sigmoid(gluh_ref[...])
        gbuf[0:CONV_HALO, :] = jnp.where(i > 0, glh, 0.0)
        gl = val_ref[...] * _sigmoid(glu_ref[...])
        gbuf[CONV_HALO:CONV_HALO + tt, :] = gl
        gl_ref[...] = gl
        acc = _conv_taps(gbuf, w_ref, tt, ntap, CONV_HALO - (ntap - 1)) + b_ref[...]
        xc = acc - jnp.mean(acc, axis=-1, keepdims=True)
        rstd = lax.rsqrt(jnp.mean(xc * xc, axis=-1, keepdims=True) + LN_EPS)
        xh = xc * rstd
        ln = xh * lg_ref[...] + lb_ref[...]
        sig = _sigmoid(ln)
        sw = ln * sig
        dyb = dy_ref[...].astype(BF16)
        dpw_ref[...] += _dot_tn(sw.astype(BF16), dyb)
        dsw = _dot_nt(dyb, pw_ref[...])
        dln = dsw * (sig * (1.0 + ln * (1.0 - sig)))
        vec_ref[0:1, :] += jnp.sum(dln * xh, axis=0, keepdims=True)
        vec_ref[1:2, :] += jnp.sum(dln, axis=0, keepdims=True)
        dxh = dln * lg_ref[...]
        dd = rstd * (dxh - jnp.mean(dxh, axis=-1, keepdims=True)
                     - xh * jnp.mean(dxh * xh, axis=-1, keepdims=True))
        vec_ref[2:3, :] += jnp.sum(dd, axis=0, keepdims=True)
        dd_ref[...] = dd

    vec = pl.BlockSpec((1, c), lambda i: (0, 0))
    tile = pl.BlockSpec((tt, c), lambda i: (i, 0))
    return pl.pallas_call(
        body, name=name,
        out_shape=(jax.ShapeDtypeStruct((s, c), F32), jax.ShapeDtypeStruct((s, c), F32),
                   jax.ShapeDtypeStruct((c, c), F32), jax.ShapeDtypeStruct((8, c), F32)),
        grid=(s // tt,),
        in_specs=[tile, pl.BlockSpec((tt, c), lambda i: (i, 1)),
                  pl.BlockSpec((CONV_HALO, c), lambda i: (jnp.maximum(i * hb - 1, 0), 0)),
                  pl.BlockSpec((CONV_HALO, c), lambda i: (jnp.maximum(i * hb - 1, 0), 1)),
                  tile, pl.BlockSpec((32, c), lambda i: (0, 0)), vec, vec, vec,
                  pl.BlockSpec((c, c), lambda i: (0, 0))],
        out_specs=(tile, tile, pl.BlockSpec((c, c), lambda i: (0, 0)), pl.BlockSpec((8, c), lambda i: (0, 0))),
        scratch_shapes=[pltpu.VMEM((CONV_HALO + tt, c), F32)],
        compiler_params=_params("arbitrary"),
    )(u, u, u, u, dyc, wpad, dw_b, ln_g, ln_b, pw)


def _conv_bwd_dw(u, dd, gl, wpad, name):
    s, c = dd.shape
    ntap = 31
    tt = _conv_time_tile(s)
    hb = tt // CONV_HALO
    nt = s // tt
    last_halo = s // CONV_HALO - 1

    def body(val_ref, glu_ref, dd_ref, ddn_ref, gl_ref, glh_ref, w_ref, dval_ref, dglu_ref, dw_ref, dbuf, gbuf):
        i = pl.program_id(0)

        @pl.when(i == 0)
        def _():
            dw_ref[...] = jnp.zeros_like(dw_ref)

        d = dd_ref[...]
        dbuf[0:tt, :] = d
        dbuf[tt:tt + CONV_HALO, :] = jnp.where(i < nt - 1, ddn_ref[...], 0.0)
        gbuf[0:CONV_HALO, :] = jnp.where(i > 0, glh_ref[...], 0.0)
        gbuf[CONV_HALO:CONV_HALO + tt, :] = gl_ref[...]
        dgl = w_ref[0:1, :] * dbuf[pl.ds(ntap - 1, tt), :]
        for k in range(1, ntap):
            dgl = dgl + w_ref[k:k + 1, :] * dbuf[pl.ds(ntap - 1 - k, tt), :]
        for k in range(ntap):
            dw_ref[k:k + 1, :] += jnp.sum(d * gbuf[pl.ds(CONV_HALO - (ntap - 1) + k, tt), :], axis=0, keepdims=True)
        sg = _sigmoid(glu_ref[...])
        dval_ref[...] = (dgl * sg).astype(BF16)
        dglu_ref[...] = (dgl * val_ref[...] * sg * (1.0 - sg)).astype(BF16)

    tile = pl.BlockSpec((tt, c), lambda i: (i, 0))
    return pl.pallas_call(
        body, name=name,
        out_shape=(jax.ShapeDtypeStruct((s, c), BF16), jax.ShapeDtypeStruct((s, c), BF16),
                   jax.ShapeDtypeStruct((32, c), F32)),
        grid=(nt,),
        in_specs=[tile, pl.BlockSpec((tt, c), lambda i: (i, 1)), tile,
                  pl.BlockSpec((CONV_HALO, c), lambda i: (jnp.minimum((i + 1) * hb, last_halo), 0)),
                  tile, pl.BlockSpec((CONV_HALO, c), lambda i: (jnp.maximum(i * hb - 1, 0), 0)),
                  pl.BlockSpec((32, c), lambda i: (0, 0))],
        out_specs=(tile, tile, pl.BlockSpec((32, c), lambda i: (0, 0))),
        scratch_shapes=[pltpu.VMEM((tt + CONV_HALO, c), F32), pltpu.VMEM((CONV_HALO + tt, c), F32)],
        compiler_params=_params("arbitrary"),
    )(u, u, dd, dd, gl, gl, wpad)


def _tri_ones(n, cmp):
    r = lax.broadcasted_iota(jnp.int32, (n, 2 * n), 0)
    c = lax.broadcasted_iota(jnp.int32, (n, 2 * n), 1)
    return jnp.where((c >= n) | cmp(r, c), 1.0, 0.0).astype(BF16)


def _split_dot(v, m):
    hi = v.astype(BF16)
    lo = (v - hi.astype(F32)).astype(BF16)
    return _dot(hi, m) + _dot(lo, m)


def _sb_logits(qb, kb, scale):
    z = _dot_nt(qb, kb) * scale
    e = jnp.exp(-jnp.abs(z))
    ell = -(jnp.maximum(z, 0.0) + jnp.log(1.0 + e))
    return z, e, ell


def _sb_fwd(u, heads, q_blk, k_blk, v_blk, name):
    s = u.shape[0]
    tq = _tile(s, (256, 128))
    scale = HEAD ** -0.5

    def body(q_ref, k_ref, v_ref, o_ref, tot_ref):
        i = pl.program_id(1)
        qb = q_ref[...].astype(BF16)
        tcat = _tri_ones(tq, lambda r, c: r > c)
        qpos = i * tq + lax.broadcasted_iota(jnp.int32, (tq, tq), 0)
        col = lax.broadcasted_iota(jnp.int32, (tq, tq), 1)

        def step(jj, carry):
            c_a, acc = carry
            j = i - jj
            rows = pl.ds(pl.multiple_of(j * tq, tq), tq)
            kb = k_ref[rows, :].astype(BF16)
            vb = v_ref[rows, :].astype(BF16)
            z, _, ell = _sb_logits(qb, kb, scale)
            causal = (j * tq + col) < qpos
            lm = jnp.where(causal, ell, 0.0)
            r = _split_dot(lm, tcat)
            after = c_a + r[:, :tq]
            w = jnp.where(causal, jnp.exp(ell + z + after), 0.0)
            acc = acc + _dot(w.astype(BF16), vb)
            return c_a + r[:, tq:tq + 1], acc

        c_a, acc = lax.fori_loop(0, i + 1, step, (jnp.zeros((tq, 1), F32), jnp.zeros((tq, HEAD), F32)))
        o_ref[...] = acc
        tot_ref[...] = jnp.broadcast_to(c_a, (tq, HEAD))

    full = lambda off: pl.BlockSpec((s, HEAD), lambda h, i: (0, off + h))
    out = pl.BlockSpec((tq, HEAD), lambda h, i: (i, h))
    return pl.pallas_call(
        body, name=name,
        out_shape=(jax.ShapeDtypeStruct((s, heads * HEAD), F32), jax.ShapeDtypeStruct((s, heads * HEAD), F32)),
        grid=(heads, s // tq),
        in_specs=[pl.BlockSpec((tq, HEAD), lambda h, i: (i, q_blk + h)), full(k_blk), full(v_blk)],
        out_specs=(out, out),
        compiler_params=_params("parallel", "parallel"),
    )(u, u, u)


def _sb_bwd(u, tot, dy, heads, q_blk, k_blk, v_blk, name):
    s = u.shape[0]
    tq = _tile(s, (256, 128))
    scale = HEAD ** -0.5

    def body(q_ref, k_ref, v_ref, tot_ref, dy_ref, dq_ref, dk_ref, dv_ref):
        i = pl.program_id(1)

        @pl.when(i == 0)
        def _():
            dk_ref[...] = jnp.zeros_like(dk_ref)
            dv_ref[...] = jnp.zeros_like(dv_ref)

        qb = q_ref[...].astype(BF16)
        dob = dy_ref[...].astype(BF16)
        total = tot_ref[:, 0:1]
        t_incl = _tri_ones(tq, lambda r, c: r <= c)
        t_excl = _tri_ones(tq, lambda r, c: r < c)
        qpos = i * tq + lax.broadcasted_iota(jnp.int32, (tq, tq), 0)
        col = lax.broadcasted_iota(jnp.int32, (tq, tq), 1)

        def step(j, carry):
            c_p, c_g, dq = carry
            rows = pl.ds(pl.multiple_of(j * tq, tq), tq)
            kb = k_ref[rows, :].astype(BF16)
            vb = v_ref[rows, :].astype(BF16)
            z, e, ell = _sb_logits(qb, kb, scale)
            causal = (j * tq + col) < qpos
            lm = jnp.where(causal, ell, 0.0)
            rp = _split_dot(lm, t_incl)
            after = total - (c_p + rp[:, :tq])
            w = jnp.where(causal, jnp.exp(ell + z + after), 0.0)
            g = w * _dot_nt(dob, vb)
            rg = _split_dot(g, t_excl)
            g_before = c_g + rg[:, :tq]
            inv = 1.0 / (1.0 + e)
            pos = z >= 0.0
            beta = jnp.where(pos, inv, e * inv)
            one_m_beta = jnp.where(pos, e * inv, inv)
            dz = jnp.where(causal, g * one_m_beta - g_before * beta, 0.0) * scale
            dzb = dz.astype(BF16)
            dq = dq + _dot(dzb, kb)
            dk_ref[rows, :] += _dot_tn(dzb, qb)
            dv_ref[rows, :] += _dot_tn(w.astype(BF16), dob)
            return c_p + rp[:, tq:tq + 1], c_g + rg[:, tq:tq + 1], dq

        zero = jnp.zeros((tq, 1), F32)
        _, _, dq = lax.fori_loop(0, i + 1, step, (zero, zero, jnp.zeros((tq, HEAD), F32)))
        dq_ref[...] = dq

    full = lambda off: pl.BlockSpec((s, HEAD), lambda h, i: (0, off + h))
    blk = pl.BlockSpec((tq, HEAD), lambda h, i: (i, h))
    acc = pl.BlockSpec((s, HEAD), lambda h, i: (0, h))
    shape = jax.ShapeDtypeStruct((s, heads * HEAD), F32)
    return pl.pallas_call(
        body, name=name, out_shape=(shape, shape, shape), grid=(heads, s // tq),
        in_specs=[pl.BlockSpec((tq, HEAD), lambda h, i: (i, q_blk + h)), full(k_blk), full(v_blk), blk, blk],
        out_specs=(blk, acc, acc),
        compiler_params=_params("parallel", "arbitrary"),
    )(u, u, u, tot, dy)


def _lru_time_tile(s):
    return _tile(s, (256, 128, 64, 32))


def _lru_gates(xc, wa_ref, ba_ref, wx_ref, bx_ref, lam_ref, nh):
    pr, pi = [], []
    for n in range(nh):
        xn = xc[:, n * HEAD:(n + 1) * HEAD].astype(BF16)
        pr.append(_dot(xn, wa_ref[n]))
        pi.append(_dot(xn, wx_ref[n]))
    r = _sigmoid((pr[0] if nh == 1 else jnp.concatenate(pr, axis=1)) + ba_ref[...])
    ig = _sigmoid((pi[0] if nh == 1 else jnp.concatenate(pi, axis=1)) + bx_ref[...])
    lam = lam_ref[...]
    sp = jnp.maximum(-lam, 0.0) + jnp.log(1.0 + jnp.exp(-jnp.abs(lam)))
    log_a = -LRU_C * r * sp
    a = jnp.exp(log_a)
    mult = jnp.sqrt(-_expm1(2.0 * log_a))
    return r, ig, a, mult, sp


def _lru_fwd(u, x_blk, cw, cb, wa, ba, wx, bx, lam, name):
    s = u.shape[0]
    w = lam.shape[1]
    nh = w // HEAD
    tt = _lru_time_tile(s)
    hb = tt // LRU_HALO

    def body(x_ref, xh_ref, cw_ref, cb_ref, wa_ref, ba_ref, wx_ref, bx_ref, lam_ref, y_ref,
             xbuf, abuf, bbuf, hstate, rowbuf):
        i = pl.program_id(0)

        @pl.when(i == 0)
        def _():
            hstate[...] = jnp.zeros_like(hstate)

        xbuf[0:LRU_HALO, :] = jnp.where(i > 0, xh_ref[...], 0.0)
        xbuf[LRU_HALO:LRU_HALO + tt, :] = x_ref[...]
        xc = _conv_taps(xbuf, cw_ref, tt, 4, LRU_HALO - 3) + cb_ref[...]
        _, ig, a, mult, _ = _lru_gates(xc, wa_ref, ba_ref, wx_ref, bx_ref, lam_ref, nh)
        abuf[...] = a
        bbuf[...] = mult * (ig * xc)

        def group(gi, h):
            rows = pl.ds(pl.multiple_of(gi * 8, 8), 8)
            a8 = abuf[rows, :]
            b8 = bbuf[rows, :]
            for j in range(8):
                h = a8[j:j + 1, :] * h + b8[j:j + 1, :]
                rowbuf[j:j + 1, :] = h
            y_ref[rows, :] = rowbuf[...]
            return h

        hstate[0:1, :] = lax.fori_loop(0, tt // 8, group, hstate[0:1, :])

    vec = pl.BlockSpec((1, w), lambda i: (0, 0))
    gate = pl.BlockSpec((nh, HEAD, HEAD), lambda i: (0, 0, 0))
    return pl.pallas_call(
        body, name=name, out_shape=jax.ShapeDtypeStruct((s, w), F32), grid=(s // tt,),
        in_specs=[pl.BlockSpec((tt, w), lambda i: (i, x_blk)),
                  pl.BlockSpec((LRU_HALO, w), lambda i: (jnp.maximum(i * hb - 1, 0), x_blk)),
                  pl.BlockSpec((8, w), lambda i: (0, 0)), vec, gate, vec, gate, vec, vec],
        out_specs=pl.BlockSpec((tt, w), lambda i: (i, 0)),
        scratch_shapes=[pltpu.VMEM((LRU_HALO + tt, w), F32), pltpu.VMEM((tt, w), F32), pltpu.VMEM((tt, w), F32),
                        pltpu.VMEM((8, w), F32), pltpu.VMEM((8, w), F32)],
        compiler_params=_params("arbitrary"),
    )(u, u, cw, cb, wa, ba, wx, bx, lam)


def _lru_bwd(u, x_blk, hseq, dy, cw, cb, wa, ba, wx, bx, lam, name):
    s = u.shape[0]
    w = lam.shape[1]
    nh = w // HEAD
    tt = _lru_time_tile(s)
    hb = tt // LRU_HALO
    nt = s // tt

    def body(x_ref, xh_ref, h_ref, hh_ref, dy_ref, cw_ref, cb_ref, wa_ref, ba_ref, wx_ref, bx_ref, lam_ref,
             dx_ref, dwa_ref, dwx_ref, vec_ref, xbuf, hbuf, abuf, lbuf, dbuf, cstate, dhalo, rowbuf):
        i = pl.program_id(0)
        rt = nt - 1 - i

        @pl.when(i == 0)
        def _():
            cstate[...] = jnp.zeros_like(cstate)
            dhalo[...] = jnp.zeros_like(dhalo)
            dwa_ref[...] = jnp.zeros_like(dwa_ref)
            dwx_ref[...] = jnp.zeros_like(dwx_ref)
            vec_ref[...] = jnp.zeros_like(vec_ref)

        xbuf[0:LRU_HALO, :] = jnp.where(rt > 0, xh_ref[...], 0.0)
        xbuf[LRU_HALO:LRU_HALO + tt, :] = x_ref[...]
        hbuf[0:LRU_HALO, :] = jnp.where(rt > 0, hh_ref[...], 0.0)
        hbuf[LRU_HALO:LRU_HALO + tt, :] = h_ref[...]
        xc = _conv_taps(xbuf, cw_ref, tt, 4, LRU_HALO - 3) + cb_ref[...]
        r, ig, a, mult, sp = _lru_gates(xc, wa_ref, ba_ref, wx_ref, bx_ref, lam_ref, nh)
        abuf[...] = a

        def group(gi, c):
            rows = pl.ds(pl.multiple_of((tt // 8 - 1 - gi) * 8, 8), 8)
            a8 = abuf[rows, :]
            d8 = dy_ref[rows, :]
            for j in range(7, -1, -1):
                lam_t = d8[j:j + 1, :] + c
                rowbuf[j:j + 1, :] = lam_t
                c = a8[j:j + 1, :] * lam_t
            lbuf[rows, :] = rowbuf[...]
            return c

        cstate[0:1, :] = lax.fori_loop(0, tt // 8, group, cstate[0:1, :])

        lam_t = lbuf[...]
        hprev = hbuf[pl.ds(LRU_HALO - 1, tt), :]
        ixc = ig * xc
        d_ixc = lam_t * mult
        d_ig = d_ixc * xc
        dxc = d_ixc * ig
        dlog_a = lam_t * hprev * a + lam_t * ixc * (-(a * a) / mult)
        dr = dlog_a * (-LRU_C * sp)
        lam_p = lam_ref[...]
        dsp = -_sigmoid(-lam_p)
        vec_ref[2:3, :] += jnp.sum(dlog_a * (-LRU_C * r), axis=0, keepdims=True) * dsp
        dpr = dr * r * (1.0 - r)
        dpi = d_ig * ig * (1.0 - ig)
        vec_ref[0:1, :] += jnp.sum(dpr, axis=0, keepdims=True)
        vec_ref[1:2, :] += jnp.sum(dpi, axis=0, keepdims=True)
        parts = []
        for n in range(nh):
            sl = slice(n * HEAD, (n + 1) * HEAD)
            xn = xc[:, sl].astype(BF16)
            dprn = dpr[:, sl].astype(BF16)
            dpin = dpi[:, sl].astype(BF16)
            dwa_ref[n] += _dot_tn(xn, dprn)
            dwx_ref[n] += _dot_tn(xn, dpin)
            parts.append(_dot_nt(dprn, wa_ref[n]) + _dot_nt(dpin, wx_ref[n]))
        dxc = dxc + (parts[0] if nh == 1 else jnp.concatenate(parts, axis=1))
        vec_ref[3:4, :] += jnp.sum(dxc, axis=0, keepdims=True)
        dbuf[0:tt, :] = dxc
        dbuf[tt:tt + LRU_HALO, :] = dhalo[...]
        dx = cw_ref[0:1, :] * dbuf[pl.ds(3, tt), :]
        for k in range(1, 4):
            dx = dx + cw_ref[k:k + 1, :] * dbuf[pl.ds(3 - k, tt), :]
        dx_ref[...] = dx.astype(BF16)
        for k in range(4):
            vec_ref[4 + k:5 + k, :] += jnp.sum(dxc * xbuf[pl.ds(LRU_HALO - 3 + k, tt), :], axis=0, keepdims=True)
        dhalo[...] = dbuf[0:LRU_HALO, :]

    vec = pl.BlockSpec((1, w), lambda i: (0, 0))
    gate = pl.BlockSpec((nh, HEAD, HEAD), lambda i: (0, 0, 0))
    rev = lambda i: nt - 1 - i
    tile = pl.BlockSpec((tt, w), lambda i: (rev(i), 0))
    halo = lambda col: pl.BlockSpec((LRU_HALO, w), lambda i: (jnp.maximum(rev(i) * hb - 1, 0), col))
    return pl.pallas_call(
        body, name=name,
        out_shape=(jax.ShapeDtypeStruct((s, w), BF16), jax.ShapeDtypeStruct((nh, HEAD, HEAD), F32),
                   jax.ShapeDtypeStruct((nh, HEAD, HEAD), F32), jax.ShapeDtypeStruct((8, w), F32)),
        grid=(nt,),
        in_specs=[pl.BlockSpec((tt, w), lambda i: (rev(i), x_blk)), halo(x_blk), tile, halo(0), tile,
                  pl.BlockSpec((8, w), lambda i: (0, 0)), vec, gate, vec, gate, vec, vec],
        out_specs=(tile, gate, gate, pl.BlockSpec((8, w), lambda i: (0, 0))),
        scratch_shapes=[pltpu.VMEM((LRU_HALO + tt, w), F32), pltpu.VMEM((LRU_HALO + tt, w), F32),
                        pltpu.VMEM((tt, w), F32), pltpu.VMEM((tt, w), F32), pltpu.VMEM((tt + LRU_HALO, w), F32),
                        pltpu.VMEM((8, w), F32), pltpu.VMEM((8, w), F32), pltpu.VMEM((8, w), F32)],
        compiler_params=_params("arbitrary"),
    )(u, u, hseq, hseq, dy, cw, cb, wa, ba, wx, bx, lam)


def _gate_specs(c, tr):
    return [pl.BlockSpec((tr, c), lambda i, b=b: (i, b)) for b in (2, 9, 10, 12)]


def _outgate_fwd(y_conv, y_attn, y_lru, u, n_conv, n_attn, n_lru, name):
    s, c = y_conv.shape
    tr = _tile(s, (256, 128, 64, 32, 16))

    def body(yc_ref, ya_ref, yl_ref, gc_ref, ga0_ref, ga1_ref, gl_ref, nc_ref, na_ref, nl_ref, o_ref):
        def rinv(v):
            return lax.rsqrt(jnp.mean(v * v, axis=-1, keepdims=True) + RMS_EPS)

        def silu(g):
            return g * _sigmoid(g)

        yc = yc_ref[...]
        o_ref[:, 0:c] = (yc * rinv(yc) * nc_ref[...] * silu(gc_ref[...])).astype(BF16)
        ya = ya_ref[...]
        ra = rinv(ya)
        o_ref[:, c:2 * c] = (ya[:, 0:c] * ra * na_ref[:, 0:c] * silu(ga0_ref[...])).astype(BF16)
        o_ref[:, 2 * c:3 * c] = (ya[:, c:2 * c] * ra * na_ref[:, c:2 * c] * silu(ga1_ref[...])).astype(BF16)
        yl = yl_ref[...]
        o_ref[:, 3 * c:4 * c] = (yl * rinv(yl) * nl_ref[...] * silu(gl_ref[...])).astype(BF16)

    row = lambda wd: pl.BlockSpec((tr, wd), lambda i: (i, 0))
    vec = lambda wd: pl.BlockSpec((1, wd), lambda i: (0, 0))
    return pl.pallas_call(
        body, name=name, out_shape=jax.ShapeDtypeStruct((s, 4 * c), BF16), grid=(s // tr,),
        in_specs=[row(c), row(2 * c), row(c)] + _gate_specs(c, tr) + [vec(c), vec(2 * c), vec(c)],
        out_specs=row(4 * c), compiler_params=_params("parallel"),
    )(y_conv, y_attn, y_lru, u, u, u, u, n_conv, n_attn, n_lru)


def _outgate_bwd(dy, y_conv, y_attn, y_lru, u, n_conv, n_attn, n_lru, name):
    s, c = y_conv.shape
    tr = _tile(s, (256, 128, 64, 32, 16))

    def body(dy_ref, yc_ref, ya_ref, yl_ref, gc_ref, ga0_ref, ga1_ref, gl_ref, nc_ref, na_ref, nl_ref,
             dyc_ref, dya_ref, dyl_ref, dgc_ref, dga_ref, dgl_ref, dn_ref):
        @pl.when(pl.program_id(0) == 0)
        def _():
            dn_ref[...] = jnp.zeros_like(dn_ref)

        def group(yv, gate, wv, d):
            r = lax.rsqrt(jnp.mean(yv * yv, axis=-1, keepdims=True) + RMS_EPS)
            yh = yv * r
            sg = _sigmoid(gate)
            dn = d * (gate * sg)
            dgate = d * (yh * wv) * (sg * (1.0 + gate * (1.0 - sg)))
            dw = jnp.sum(dn * yh, axis=0, keepdims=True)
            dyn = dn * wv
            dyv = r * (dyn - yh * jnp.mean(dyn * yh, axis=-1, keepdims=True))
            return dyv, dgate, dw

        dyv, dg, dw = group(yc_ref[...], gc_ref[...], nc_ref[...], dy_ref[:, 0:c])
        dyc_ref[...] = dyv
        dgc_ref[...] = dg.astype(BF16)
        dn_ref[0:1, 0:c] += dw
        gate_a = jnp.concatenate([ga0_ref[...], ga1_ref[...]], axis=1)
        dyv, dg, dw = group(ya_ref[...], gate_a, na_ref[...], dy_ref[:, c:3 * c])
        dya_ref[...] = dyv
        dga_ref[...] = dg.astype(BF16)
        dn_ref[0:1, c:3 * c] += dw
        dyv, dg, dw = group(yl_ref[...], gl_ref[...], nl_ref[...], dy_ref[:, 3 * c:4 * c])
        dyl_ref[...] = dyv
        dgl_ref[...] = dg.astype(BF16)
        dn_ref[0:1, 3 * c:4 * c] += dw

    row = lambda wd: pl.BlockSpec((tr, wd), lambda i: (i, 0))
    vec = lambda wd: pl.BlockSpec((1, wd), lambda i: (0, 0))
    sh = lambda wd, dt: jax.ShapeDtypeStruct((s, wd), dt)
    return pl.pallas_call(
        body, name=name,
        out_shape=(sh(c, F32), sh(2 * c, F32), sh(c, F32), sh(c, BF16), sh(2 * c, BF16), sh(c, BF16),
                   jax.ShapeDtypeStruct((8, 4 * c), F32)),
        grid=(s // tr,),
        in_specs=[row(4 * c), row(c), row(2 * c), row(c)] + _gate_specs(c, tr) + [vec(c), vec(2 * c), vec(c)],
        out_specs=(row(c), row(2 * c), row(c), row(c), row(2 * c), row(c),
                   pl.BlockSpec((8, 4 * c), lambda i: (0, 0))),
        compiler_params=_params("arbitrary"),
    )(dy, y_conv, y_attn, y_lru, u, u, u, u, n_conv, n_attn, n_lru)


def _xattn_probs(qh, kh, scale):
    sc = _dot_nt(qh, kh) * scale
    p = jnp.exp(sc - jnp.max(sc, axis=-1, keepdims=True))
    return p / jnp.sum(p, axis=-1, keepdims=True)


def _xattn_fwd(q, kv, name):
    s, xw = q.shape
    m = kv.shape[0]
    nh = xw // HEAD
    tq = _tile(s, (256, 128, 64, 32, 16))
    scale = HEAD ** -0.5

    def body(q_ref, kv_ref, o_ref):
        for h in range(nh):
            qh = q_ref[:, h * HEAD:(h + 1) * HEAD].astype(BF16)
            kh = kv_ref[:, h * HEAD:(h + 1) * HEAD].astype(BF16)
            vh = kv_ref[:, xw + h * HEAD:xw + (h + 1) * HEAD].astype(BF16)
            p = _xattn_probs(qh, kh, scale)
            o_ref[:, h * HEAD:(h + 1) * HEAD] = _dot(p.astype(BF16), vh).astype(BF16)

    return pl.pallas_call(
        body, name=name, out_shape=jax.ShapeDtypeStruct((s, xw), BF16), grid=(s // tq,),
        in_specs=[pl.BlockSpec((tq, xw), lambda i: (i, 0)), pl.BlockSpec((m, 2 * xw), lambda i: (0, 0))],
        out_specs=pl.BlockSpec((tq, xw), lambda i: (i, 0)), compiler_params=_params("parallel"),
    )(q, kv)


def _xattn_bwd(q, kv, do, name):
    s, xw = q.shape
    m = kv.shape[0]
    nh = xw // HEAD
    tq = _tile(s, (256, 128, 64, 32, 16))
    scale = HEAD ** -0.5

    def body(q_ref, kv_ref, do_ref, dq_ref, dkv_ref):
        @pl.when(pl.program_id(0) == 0)
        def _():
            dkv_ref[...] = jnp.zeros_like(dkv_ref)

        for h in range(nh):
            ks = slice(h * HEAD, (h + 1) * HEAD)
            vs = slice(xw + h * HEAD, xw + (h + 1) * HEAD)
            qh = q_ref[:, ks].astype(BF16)
            kh = kv_ref[:, ks].astype(BF16)
            vh = kv_ref[:, vs].astype(BF16)
            doh = do_ref[:, ks].astype(BF16)
            p = _xattn_probs(qh, kh, scale)
            dkv_ref[:, vs] += _dot_tn(p.astype(BF16), doh)
            dp = _dot_nt(doh, vh)
            ds = (p * (dp - jnp.sum(dp * p, axis=-1, keepdims=True)) * scale).astype(BF16)
            dq_ref[:, ks] = _dot(ds, kh).astype(BF16)
            dkv_ref[:, ks] += _dot_tn(ds, qh)

    row = pl.BlockSpec((tq, xw), lambda i: (i, 0))
    full = pl.BlockSpec((m, 2 * xw), lambda i: (0, 0))
    return pl.pallas_call(
        body, name=name,
        out_shape=(jax.ShapeDtypeStruct((s, xw), BF16), jax.ShapeDtypeStruct((m, 2 * xw), F32)), grid=(s // tq,),
        in_specs=[row, full, row], out_specs=(row, full), compiler_params=_params("arbitrary"),
    )(q, kv, do)


def _adamw(w, g, m, v, name):
    rows, cols = w.shape
    tr = _tile(rows, (512, 256, 128, 64, 32, 16, 8)) if rows % 8 == 0 else rows
    bc1 = 1.0 - ADAM_B1 ** ADAM_STEP
    bc2 = 1.0 - ADAM_B2 ** ADAM_STEP

    def body(w_ref, g_ref, m_ref, v_ref, d_ref, nm_ref, nv_ref):
        gv = g_ref[...]
        nm = ADAM_B1 * m_ref[...] + (1.0 - ADAM_B1) * gv
        nv = ADAM_B2 * v_ref[...] + (1.0 - ADAM_B2) * (gv * gv)
        nm_ref[...] = nm
        nv_ref[...] = nv
        d_ref[...] = -ADAM_LR * ((nm / bc1) / (jnp.sqrt(nv / bc2) + ADAM_EPS) + ADAM_WD * w_ref[...])

    spec = pl.BlockSpec((tr, cols), lambda i: (i, 0))
    sh = jax.ShapeDtypeStruct((rows, cols), F32)
    return pl.pallas_call(
        body, name=name, out_shape=(sh, sh, sh), grid=(rows // tr,), in_specs=[spec] * 4,
        out_specs=(spec, spec, spec), compiler_params=_params("parallel"),
    )(w, g, m, v)


WEIGHTS = ['mix_norm_g', 'w_in', 'conv_dw_w', 'conv_dw_b', 'conv_ln_g', 'conv_ln_b', 'conv_pw_w', 'lru_conv_w',
           'lru_conv_b', 'lru_wa', 'lru_ba', 'lru_wx', 'lru_bx', 'lru_lambda', 'out_norm_conv', 'out_norm_attn',
           'out_norm_lru', 'w_out', 'xattn_norm_g', 'mem_norm_g', 'xattn_wq', 'xattn_wkv', 'xattn_wo',
           'final_norm_g']
BIG_SHARDED = {'w_in': 2, 'conv_pw_w': 1, 'w_out': 1, 'xattn_wq': 1, 'xattn_wkv': 1, 'xattn_wo': 2}
SMALL_SHARDED = {'conv_dw_w': 2, 'lru_conv_w': 2}


def _layer_fwd(x, mem, p, l):
    row = lambda name: p[name][l][None, :]
    c = p['conv_dw_b'].shape[1]
    heads = 2 * c // HEAD
    h = _rms_fwd(x, row('mix_norm_g'), "rms_mix")
    u = _mm(h, p['w_in'][l], name="in_proj")
    wpad = jnp.pad(p['conv_dw_w'][l], ((0, 1), (0, 0)))
    cw = jnp.pad(p['lru_conv_w'][l], ((0, 4), (0, 0)))
    y_conv = _conv_fwd(u, wpad, row('conv_dw_b'), row('conv_ln_g'), row('conv_ln_b'), p['conv_pw_w'][l], "conv_fwd")
    q_blk = 3 * c // HEAD
    y_attn, tot = _sb_fwd(u, heads, q_blk, q_blk + heads, q_blk + 2 * heads, "sb_fwd")
    wa, wx = p['lru_wa'][l].astype(BF16), p['lru_wx'][l].astype(BF16)
    y_lru = _lru_fwd(u, 11, cw, row('lru_conv_b'), wa, row('lru_ba'), wx, row('lru_bx'), row('lru_lambda'), "lru_fwd")
    yc = _outgate_fwd(y_conv, y_attn, y_lru, u, row('out_norm_conv'), row('out_norm_attn'), row('out_norm_lru'),
                      "outgate_fwd")
    x1 = _mm(yc, p['w_out'][l], add=x, name="out_proj")
    h2 = _rms_fwd(x1, row('xattn_norm_g'), "rms_xattn")
    memn = _rms_fwd(mem, row('mem_norm_g'), "rms_mem")
    q2 = _mm(h2, p['xattn_wq'][l], name="xq_proj")
    kv = _mm(memn, p['xattn_wkv'][l], name="xkv_proj")
    o2 = _xattn_fwd(q2, kv, "xattn_fwd")
    x2 = _mm(o2, p['xattn_wo'][l], add=x1, name="xo_proj")
    saved = dict(x=x, h=h, u=u, wpad=wpad, cw=cw, wa=wa, wx=wx, y_conv=y_conv, y_attn=y_attn, tot=tot, y_lru=y_lru,
                 yc=yc, x1=x1, h2=h2, memn=memn, q2=q2, kv=kv, o2=o2)
    return x2, saved


def _layer_bwd(dx2, mem, p, l, sv):
    row = lambda name: p[name][l][None, :]
    c = p['conv_dw_b'].shape[1]
    heads = 2 * c // HEAD
    g = {}
    g['xattn_wo'] = _mm(sv['o2'], dx2, ta=True, name="d_wo")
    do2 = _mm(dx2, p['xattn_wo'][l], tb=True, name="d_o2")
    dq2, dkv = _xattn_bwd(sv['q2'], sv['kv'], do2, "xattn_bwd")
    g['xattn_wq'] = _mm(sv['h2'], dq2, ta=True, name="d_wq")
    dh2 = _mm(dq2, p['xattn_wq'][l], tb=True, name="d_h2")
    g['xattn_wkv'] = _mm(sv['memn'], dkv, ta=True, name="d_wkv")
    dmemn = _mm(dkv, p['xattn_wkv'][l], tb=True, name="d_memn")
    _, g['mem_norm_g'] = _rms_bwd(mem, row('mem_norm_g'), dmemn, None, "rms_mem_bwd")
    dx1, g['xattn_norm_g'] = _rms_bwd(sv['x1'], row('xattn_norm_g'), dh2, dx2, "rms_xattn_bwd")
    g['w_out'] = _mm(sv['yc'], dx1, ta=True, name="d_wout")
    dyc = _mm(dx1, p['w_out'][l], tb=True, name="d_yc")
    u = sv['u']
    d_yconv, d_yattn, d_ylru, dgc, dga, dgl, dn = _outgate_bwd(
        dyc, sv['y_conv'], sv['y_attn'], sv['y_lru'], u, row('out_norm_conv'), row('out_norm_attn'),
        row('out_norm_lru'), "outgate_bwd")
    g['out_norm_conv'], g['out_norm_attn'], g['out_norm_lru'] = dn[0, 0:c], dn[0, c:3 * c], dn[0, 3 * c:4 * c]
    dd, gl, g['conv_pw_w'], cvec = _conv_bwd_post(u, d_yconv, sv['wpad'], row('conv_dw_b'), row('conv_ln_g'),
                                                  row('conv_ln_b'), p['conv_pw_w'][l], "conv_bwd_post")
    g['conv_ln_g'], g['conv_ln_b'], g['conv_dw_b'] = cvec[0], cvec[1], cvec[2]
    dval, dglu, ddw = _conv_bwd_dw(u, dd, gl, sv['wpad'], "conv_bwd_dw")
    g['conv_dw_w'] = ddw[0:31]
    q_blk = 3 * c // HEAD
    dq, dk, dv = _sb_bwd(u, sv['tot'], d_yattn, heads, q_blk, q_blk + heads, q_blk + 2 * heads, "sb_bwd")
    dxr, g['lru_wa'], g['lru_wx'], lvec = _lru_bwd(
        u, 11, sv['y_lru'], d_ylru, sv['cw'], row('lru_conv_b'), sv['wa'], row('lru_ba'), sv['wx'], row('lru_bx'),
        row('lru_lambda'), "lru_bwd")
    g['lru_ba'], g['lru_bx'], g['lru_lambda'], g['lru_conv_b'] = lvec[0], lvec[1], lvec[2], lvec[3]
    g['lru_conv_w'] = lvec[4:8]
    du = jnp.concatenate([dval, dglu, dgc, dq.astype(BF16), dk.astype(BF16), dv.astype(BF16), dga, dxr, dgl], axis=1)
    g['w_in'] = _mm(sv['h'], du, ta=True, name="d_win")
    dh = _mm(du, p['w_in'][l], tb=True, name="d_h")
    dx0, g['mix_norm_g'] = _rms_bwd(sv['x'], row('mix_norm_g'), dh, dx1, "rms_mix_bwd")
    return dx0, g


def kernel(x, mem, mix_norm_g, w_in, conv_dw_w, conv_dw_b, conv_ln_g, conv_ln_b, conv_pw_w, lru_conv_w, lru_conv_b, lru_wa, lru_ba, lru_wx, lru_bx, lru_lambda, out_norm_conv, out_norm_attn, out_norm_lru, w_out, xattn_norm_g, mem_norm_g, xattn_wq, xattn_wkv, xattn_wo, final_norm_g, loss_target, m_mix_norm_g, m_w_in, m_conv_dw_w, m_conv_dw_b, m_conv_ln_g, m_conv_ln_b, m_conv_pw_w, m_lru_conv_w, m_lru_conv_b, m_lru_wa, m_lru_ba, m_lru_wx, m_lru_bx, m_lru_lambda, m_out_norm_conv, m_out_norm_attn, m_out_norm_lru, m_w_out, m_xattn_norm_g, m_mem_norm_g, m_xattn_wq, m_xattn_wkv, m_xattn_wo, m_final_norm_g, v_mix_norm_g, v_w_in, v_conv_dw_w, v_conv_dw_b, v_conv_ln_g, v_conv_ln_b, v_conv_pw_w, v_lru_conv_w, v_lru_conv_b, v_lru_wa, v_lru_ba, v_lru_wx, v_lru_bx, v_lru_lambda, v_out_norm_conv, v_out_norm_attn, v_out_norm_lru, v_w_out, v_xattn_norm_g, v_mem_norm_g, v_xattn_wq, v_xattn_wkv, v_xattn_wo, v_final_norm_g):
    args = locals()
    w = {n: args[n] for n in WEIGHTS}
    mom = {n: args["m_" + n] for n in WEIGHTS}
    var = {n: args["v_" + n] for n in WEIGHTS}
    depth = w_in.shape[0]
    c = conv_dw_b.shape[1]
    assert out_norm_attn.shape[1] == 2 * c and lru_lambda.shape[1] == c and w_in.shape[2] * N_DEV == 13 * c
    assert c % HEAD == 0 and x.shape[0] == 1 and mem.shape[0] == 1
    xs, mems, tgt = x[0], mem[0], loss_target[0]
    me = 4 * lax.axis_index("x") + 2 * lax.axis_index("y") + lax.axis_index("c")

    big = list(BIG_SHARDED)
    gathered = _all_gather(_pack([w[n] for n in big], BF16), "gather_weights")
    full = {n: _join_blocks(blk, BIG_SHARDED[n])
            for n, blk in zip(big, _unpack(gathered, [w[n].shape for n in big], lead=N_DEV))}
    small = list(SMALL_SHARDED)
    gathered = _all_gather(_pack([w[n] for n in small], F32), "gather_conv_taps")
    full.update({n: _join_blocks(blk, SMALL_SHARDED[n])
                 for n, blk in zip(small, _unpack(gathered, [w[n].shape for n in small], lead=N_DEV))})
    p = {**w, **full}

    saved = []
    act = xs
    for l in range(depth):
        act, sv = _layer_fwd(act, mems, p, l)
        saved.append(sv)
    loss_part, dact, d_final = _loss_bwd(act, final_norm_g[None, :], tgt, "loss_bwd")

    layer_grads = [None] * depth
    for l in reversed(range(depth)):
        dact, layer_grads[l] = _layer_bwd(dact, mems, p, l, saved[l])
    grad_x = dact[None]
    partial = {n: jnp.stack([layer_grads[l][n] for l in range(depth)]) for n in WEIGHTS if n != 'final_norm_g'}
    partial['final_norm_g'] = d_final[0]

    send = _pack([_split_blocks(partial[n].astype(BF16), BIG_SHARDED[n]) for n in big], BF16, lead=N_DEV)
    mine = _sum_blocks(_all_to_all(send, "scatter_grads"), "sum_grads")
    grads = dict(zip(big, _unpack(mine, [w[n].shape for n in big])))

    rest = [n for n in WEIGHTS if n not in BIG_SHARDED]
    vec = _pack([partial[n] for n in rest] + [loss_part], F32)
    total = _sum_blocks(_all_gather(vec, "gather_small_grads"), "sum_small_grads")
    pieces = _unpack(total, [partial[n].shape for n in rest] + [(1, 1)])
    loss = pieces[-1][0, 0]
    for n, piece in zip(rest, pieces[:-1]):
        if n in SMALL_SHARDED:
            width = w[n].shape[2]
            piece = lax.dynamic_slice_in_dim(piece, me * width, width, axis=2)
        grads[n] = piece

    delta, new_m, new_v = {}, {}, {}
    for n in big:
        shape = w[n].shape
        two_d = lambda a: a.reshape(-1, shape[-1])
        d, nm, nv = _adamw(two_d(w[n]), two_d(grads[n]), two_d(mom[n]), two_d(var[n]), "adamw_" + n)
        delta[n], new_m[n], new_v[n] = d.reshape(shape), nm.reshape(shape), nv.reshape(shape)
    shapes = [w[n].shape for n in rest]
    packed = [_pack([src[n] for n in rest], F32) for src in (w, grads, mom, var)]
    outs = _adamw(*packed, "adamw_small")
    for dst, o in zip((delta, new_m, new_v), outs):
        dst.update(dict(zip(rest, _unpack(o, shapes))))

    return (loss, grad_x, *[grads[n] for n in WEIGHTS], *[delta[n] for n in WEIGHTS],
            *[new_m[n] for n in WEIGHTS], *[new_v[n] for n in WEIGHTS])
```

```python
import functools
import math

import jax
import jax.numpy as jnp
from jax import lax
from jax.experimental import pallas as pl
from jax.experimental.pallas import tpu as pltpu

F32 = jnp.float32
BF16 = jnp.bfloat16

N_DEV = 8
LANE = 128
HEAD = 128
VMEM_LIMIT = 56 * 1024 * 1024
PACK_COLS = 512
RMS_EPS = 1e-6
LN_EPS = 1e-5
LRU_C = 8.0
CONV_HALO = 32
LRU_HALO = 8

ADAM_LR, ADAM_B1, ADAM_B2, ADAM_EPS, ADAM_WD, ADAM_STEP = 0.001, 0.9, 0.999, 1e-08, 0.01, 10

MESH = pl.DeviceIdType.MESH


def _tile(n, cands):
    for c in cands:
        if n % c == 0:
            return c
    raise ValueError(f"no tile of {cands} divides {n}")


def _params(*sem):
    return pltpu.CompilerParams(dimension_semantics=sem, vmem_limit_bytes=VMEM_LIMIT)


def _dot(a, b):
    return lax.dot_general(a, b, (((1,), (0,)), ((), ())), preferred_element_type=F32)


def _dot_nt(a, b):
    return lax.dot_general(a, b, (((1,), (1,)), ((), ())), preferred_element_type=F32)


def _dot_tn(a, b):
    return lax.dot_general(a, b, (((0,), (0,)), ((), ())), preferred_element_type=F32)


def _sigmoid(x):
    return 1.0 / (1.0 + jnp.exp(-x))


def _expm1(x):
    series = x * (1.0 + x * (0.5 + x * (1.0 / 6.0 + x * (1.0 / 24.0))))
    return jnp.where(jnp.abs(x) < 0.05, series, jnp.exp(x) - 1.0)


def _my_place():
    return lax.axis_index("x"), lax.axis_index("y"), lax.axis_index("c")


def _flip(v, d):
    return 1 - v if d else v


def _window(ref, axis, start, size):
    return ref.at[tuple(pl.ds(start, size) if a == axis else pl.ds(0, ref.shape[a]) for a in range(len(ref.shape)))]


def _all_gather(x2d, name):
    rows, cols = x2d.shape

    def body(x_ref, out_ref, send_sems, recv_sems, local_sem):
        x, y, c = _my_place()
        me, sibling = (x, y, c), (x, y, 1 - c)
        chips = [(1 - x, y), (x, 1 - y), (1 - x, 1 - y)]

        def blk(px, py, pc):
            return out_ref.at[4 * px + 2 * py + pc]

        def copy(k, block, to, src=None):
            return pltpu.make_async_remote_copy(
                src_ref=blk(*block) if src is None else src, dst_ref=blk(*block),
                send_sem=send_sems.at[k], recv_sem=recv_sems.at[k], device_id=to, device_id_type=MESH)

        mine = pltpu.make_async_copy(x_ref, blk(*me), local_sem)
        mine.start()
        first = [copy(0, me, sibling, src=x_ref)]
        first += [copy(1 + j, me, (*chip, c), src=x_ref) for j, chip in enumerate(chips)]
        for cp in first:
            cp.start()
        passed = [copy(4 + j, (*chip, c), sibling) for j, chip in enumerate(chips)]
        for j, chip in enumerate(chips):
            copy(1 + j, (*chip, c), me).wait_recv()
            passed[j].start()
        copy(0, sibling, me).wait_recv()
        for j, chip in enumerate(chips):
            copy(4 + j, (*chip, 1 - c), me).wait_recv()
        for cp in first + passed:
            cp.wait_send()
        mine.wait()

    return pl.pallas_call(
        body, name=name,
        out_shape=jax.ShapeDtypeStruct((N_DEV, rows, cols), x2d.dtype),
        in_specs=[pl.BlockSpec(memory_space=pl.ANY)],
        out_specs=pl.BlockSpec(memory_space=pl.ANY),
        scratch_shapes=[pltpu.SemaphoreType.DMA((7,)), pltpu.SemaphoreType.DMA((7,)), pltpu.SemaphoreType.DMA],
    )(x2d)


def _pair_gather(blk, name):
    def body(x_ref, out_ref, send_sem, recv_sem, local_sem):
        x, y, c = _my_place()
        mine = pltpu.make_async_copy(x_ref, out_ref.at[c], local_sem)
        mine.start()
        send = pltpu.make_async_remote_copy(src_ref=x_ref, dst_ref=out_ref.at[c], send_sem=send_sem,
                                            recv_sem=recv_sem, device_id=(x, y, 1 - c), device_id_type=MESH)
        send.start()
        pltpu.make_async_remote_copy(src_ref=x_ref, dst_ref=out_ref.at[1 - c], send_sem=send_sem, recv_sem=recv_sem,
                                     device_id=(x, y, 1 - c), device_id_type=MESH).wait_recv()
        send.wait_send()
        mine.wait()

    return pl.pallas_call(
        body, name=name, out_shape=jax.ShapeDtypeStruct((2,) + blk.shape, blk.dtype),
        in_specs=[pl.BlockSpec(memory_space=pl.ANY)], out_specs=pl.BlockSpec(memory_space=pl.ANY),
        scratch_shapes=[pltpu.SemaphoreType.DMA, pltpu.SemaphoreType.DMA, pltpu.SemaphoreType.DMA],
    )(blk)


def _gather_weights(chip_block, chip_axis, dev_blocks, dev_axes, name):
    n_items = 1 + len(dev_blocks)
    chip_w = chip_block.shape[chip_axis]
    dev_w = [b.shape[a] for b, a in zip(dev_blocks, dev_axes)]

    def full_shape(b, a, n):
        return b.shape[:a] + (b.shape[a] * n,) + b.shape[a + 1:]

    out_shape = [jax.ShapeDtypeStruct(full_shape(chip_block, chip_axis, 4), chip_block.dtype)]
    out_shape += [jax.ShapeDtypeStruct(full_shape(b, a, N_DEV), b.dtype) for b, a in zip(dev_blocks, dev_axes)]

    def body(*refs):
        ins, outs = refs[:n_items], refs[n_items:2 * n_items]
        send_sems, recv_sems, local_sems = refs[2 * n_items:]
        x, y, c = _my_place()
        me, sibling = (x, y, c), (x, y, 1 - c)
        chips = [(1 - x, y), (x, 1 - y), (1 - x, 1 - y)]

        def dst(t, px, py, pc):
            if t == 0:
                return _window(outs[0].at[pc], chip_axis - 1, chip_w * (2 * px + py), chip_w)
            return _window(outs[t], dev_axes[t - 1], dev_w[t - 1] * (4 * px + 2 * py + pc), dev_w[t - 1])

        def own(t):
            return ins[0].at[c] if t == 0 else ins[t]

        def copy(t, k, block, to, src=None):
            return pltpu.make_async_remote_copy(
                src_ref=dst(t, *block) if src is None else src, dst_ref=dst(t, *block),
                send_sem=send_sems.at[7 * t + k], recv_sem=recv_sems.at[7 * t + k], device_id=to, device_id_type=MESH)

        local = [pltpu.make_async_copy(ins[0], _window(outs[0], chip_axis, chip_w * (2 * x + y), chip_w),
                                       local_sems.at[0])]
        local += [pltpu.make_async_copy(ins[t], dst(t, *me), local_sems.at[t]) for t in range(1, n_items)]
        for cp in local:
            cp.start()
        first = []
        for t in range(n_items):
            if t > 0:
                first.append(copy(t, 0, me, sibling, src=own(t)))
            first += [copy(t, 1 + j, me, (*chip, c), src=own(t)) for j, chip in enumerate(chips)]
        for cp in first:
            cp.start()
        passed = []
        for j, chip in enumerate(chips):
            for t in range(n_items):
                copy(t, 1 + j, (*chip, c), me).wait_recv()
                passed.append(copy(t, 4 + j, (*chip, c), sibling))
                passed[-1].start()
        for t in range(1, n_items):
            copy(t, 0, sibling, me).wait_recv()
        for j, chip in enumerate(chips):
            for t in range(n_items):
                copy(t, 4 + j, (*chip, 1 - c), me).wait_recv()
        for cp in first + passed:
            cp.wait_send()
        for cp in local:
            cp.wait()

    hbm = pl.BlockSpec(memory_space=pl.ANY)
    return pl.pallas_call(
        body, name=name, out_shape=out_shape, in_specs=[hbm] * n_items, out_specs=[hbm] * n_items,
        scratch_shapes=[pltpu.SemaphoreType.DMA((7 * n_items,)), pltpu.SemaphoreType.DMA((7 * n_items,)),
                        pltpu.SemaphoreType.DMA((n_items,))],
    )(chip_block, *dev_blocks)


def _scatter_grads(grads, axes, widths, starts, depth, name):
    n = len(grads)

    def win_shape(g, a, w):
        return g.shape[:a] + (w,) + g.shape[a + 1:]

    out_shape = [jax.ShapeDtypeStruct((N_DEV, depth) + win_shape(g, a, w), g.dtype)
                 for g, a, w in list(zip(grads, axes, widths))[::depth]]

    def body(*refs):
        ins, outs = refs[:n], refs[n:n + n // depth]
        send_sems, recv_sems, local_sems = refs[n + n // depth:]
        x, y, c = _my_place()
        me = 4 * x + 2 * y + c

        def win(t, px, py, pc):
            start = pl.multiple_of(starts[t](px, py, pc), math.gcd(widths[t], 1024))
            return _window(ins[t], axes[t], start, widths[t])

        def slot(t, j):
            return outs[t // depth].at[j, t % depth]

        local = [pltpu.make_async_copy(win(t, x, y, c), slot(t, me), local_sems.at[t]) for t in range(n)]
        for cp in local:
            cp.start()
        sends, recvs = [], []
        for k in range(1, N_DEV):
            px, py, pc = _flip(x, k & 4), _flip(y, k & 2), _flip(c, k & 1)
            peer = 4 * px + 2 * py + pc
            for t in range(n):
                sem = 7 * t + k - 1
                sends.append(pltpu.make_async_remote_copy(
                    src_ref=win(t, px, py, pc), dst_ref=slot(t, me), send_sem=send_sems.at[sem],
                    recv_sem=recv_sems.at[sem], device_id=(px, py, pc), device_id_type=MESH))
                recvs.append(pltpu.make_async_remote_copy(
                    src_ref=win(t, px, py, pc), dst_ref=slot(t, peer), send_sem=send_sems.at[sem],
                    recv_sem=recv_sems.at[sem], device_id=(px, py, pc), device_id_type=MESH))
        for cp in sends:
            cp.start()
        for cp in recvs:
            cp.wait_recv()
        for cp in sends:
            cp.wait_send()
        for cp in local:
            cp.wait()

    hbm = pl.BlockSpec(memory_space=pl.ANY)
    return pl.pallas_call(
        body, name=name, out_shape=out_shape, in_specs=[hbm] * n, out_specs=[hbm] * (n // depth),
        scratch_shapes=[pltpu.SemaphoreType.DMA((7 * n,)), pltpu.SemaphoreType.DMA((7 * n,)),
                        pltpu.SemaphoreType.DMA((n,))],
    )(*grads)


def _sum_blocks(x3d, name):
    n, rows, cols = x3d.shape
    tr = _tile(rows, (512, 256, 128, 64, 32, 16))

    def body(x_ref, o_ref):
        acc = x_ref[0].astype(F32)
        for j in range(1, n):
            acc = acc + x_ref[j].astype(F32)
        o_ref[...] = acc

    return pl.pallas_call(
        body, name=name, out_shape=jax.ShapeDtypeStruct((rows, cols), F32), grid=(rows // tr,),
        in_specs=[pl.BlockSpec((n, tr, cols), lambda i: (0, i, 0))],
        out_specs=pl.BlockSpec((tr, cols), lambda i: (i, 0)),
        compiler_params=_params("parallel"),
    )(x3d)


def _pack(arrs, dtype, lead=None):
    if lead is None:
        flat = jnp.concatenate([a.reshape(-1).astype(dtype) for a in arrs])
        n = flat.shape[0]
        total = -(-n // (16 * PACK_COLS)) * (16 * PACK_COLS)
        return jnp.pad(flat, (0, total - n)).reshape(-1, PACK_COLS)
    flat = jnp.concatenate([a.reshape(lead, -1).astype(dtype) for a in arrs], axis=1)
    n = flat.shape[1]
    total = -(-n // (16 * PACK_COLS)) * (16 * PACK_COLS)
    return jnp.pad(flat, ((0, 0), (0, total - n))).reshape(lead, -1, PACK_COLS)


def _unpack(packed, shapes, lead=None):
    out, off = [], 0
    if lead is None:
        flat = packed.reshape(-1)
        for s in shapes:
            n = math.prod(s)
            out.append(flat[off:off + n].reshape(s))
            off += n
        return out
    flat = packed.reshape(lead, -1)
    for s in shapes:
        n = math.prod(s)
        out.append(flat[:, off:off + n].reshape((lead,) + tuple(s)))
        off += n
    return out


def _join_blocks(g, axis):
    g = jnp.moveaxis(g, 0, axis)
    s = g.shape
    return g.reshape(s[:axis] + (s[axis] * s[axis + 1],) + s[axis + 2:])


def _mm_tiles(m, n, kdim, a_bytes):
    tk = kdim if kdim <= 2048 else _tile(kdim, (2048, 1664, 1024, 832, 512, 416, 256, 128))
    tm = _tile(m, (1024, 512, 256, 128, 64, 32, 16))
    tn = _tile(n, (1024, 512, 256, 128))

    def vmem(tm, tn):
        return 2 * tm * tk * a_bytes + 2 * tn * tk * 2 + 3 * tm * tn * 4

    while vmem(tm, tn) > VMEM_LIMIT * 3 // 4 and tn > 128 and tn % 256 == 0:
        tn //= 2
    while vmem(tm, tn) > VMEM_LIMIT * 3 // 4 and tm > 128 and tm % 256 == 0:
        tm //= 2
    return tm, tn, tk


def _mm(a, b, *, ta=False, tb=False, bl=None, out_dtype=F32, add=None, name):
    if ta:
        kdim, m = a.shape
    else:
        m, kdim = a.shape
    bshape = b.shape if bl is None else b.shape[1:]
    n = bshape[0] if tb else bshape[1]
    tm, tn, tk = _mm_tiles(m, n, kdim, a.dtype.itemsize)
    nk = kdim // tk
    a_spec = pl.BlockSpec((tk, tm), lambda i, j, k: (k, i)) if ta else pl.BlockSpec((tm, tk), lambda i, j, k: (i, k))
    b_blk, b_idx = ((tn, tk), lambda i, j, k: (j, k)) if tb else ((tk, tn), lambda i, j, k: (k, j))
    if bl is None:
        b_spec = pl.BlockSpec(b_blk, b_idx)
    else:
        b_spec = pl.BlockSpec((None,) + b_blk, lambda i, j, k: (bl,) + b_idx(i, j, k))
    o_spec = pl.BlockSpec((tm, tn), lambda i, j, k: (i, j))
    dims = (((0 if ta else 1,), (1 if tb else 0,)), ((), ()))

    def body(*refs):
        a_ref, b_ref = refs[:2]
        add_ref = refs[2] if add is not None else None
        o_ref = refs[3] if add is not None else refs[2]

        def finish(r):
            if add is not None:
                r = r + add_ref[...]
            o_ref[...] = r.astype(out_dtype)

        part = lax.dot_general(a_ref[...].astype(BF16), b_ref[...].astype(BF16), dims, preferred_element_type=F32)
        if nk == 1:
            finish(part)
            return
        acc_ref = refs[-1]
        k = pl.program_id(2)

        @pl.when(k == 0)
        def _():
            acc_ref[...] = part

        @pl.when(k > 0)
        def _():
            acc_ref[...] += part

        @pl.when(k == nk - 1)
        def _():
            finish(acc_ref[...])

    ins, specs = [a, b], [a_spec, b_spec]
    if add is not None:
        ins.append(add)
        specs.append(o_spec)
    return pl.pallas_call(
        body, name=name, out_shape=jax.ShapeDtypeStruct((m, n), out_dtype), grid=(m // tm, n // tn, nk),
        in_specs=specs, out_specs=o_spec, scratch_shapes=[pltpu.VMEM((tm, tn), F32)] if nk > 1 else [],
        compiler_params=_params("parallel", "parallel", "arbitrary"),
    )(*ins)


def _rms_fwd(x, g, name):
    s, d = x.shape
    tr = _tile(s, (256, 128, 64, 32, 16))

    def body(x_ref, g_ref, o_ref):
        xv = x_ref[...]
        r = lax.rsqrt(jnp.mean(xv * xv, axis=-1, keepdims=True) + RMS_EPS)
        o_ref[...] = (xv * r * g_ref[...]).astype(BF16)

    return pl.pallas_call(
        body, name=name, out_shape=jax.ShapeDtypeStruct((s, d), BF16), grid=(s // tr,),
        in_specs=[pl.BlockSpec((tr, d), lambda i: (i, 0)), pl.BlockSpec((1, d), lambda i: (0, 0))],
        out_specs=pl.BlockSpec((tr, d), lambda i: (i, 0)), compiler_params=_params("parallel"),
    )(x, g)


def _rms_bwd(x, g, dh, resid, name):
    s, d = x.shape
    tr = _tile(s, (256, 128, 64, 32, 16))

    def body(*refs):
        if resid is None:
            x_ref, g_ref, dh_ref, dx_ref, dg_ref = refs
        else:
            x_ref, g_ref, dh_ref, res_ref, dx_ref, dg_ref = refs

        @pl.when(pl.program_id(0) == 0)
        def _():
            dg_ref[...] = jnp.zeros_like(dg_ref)

        xv = x_ref[...]
        r = lax.rsqrt(jnp.mean(xv * xv, axis=-1, keepdims=True) + RMS_EPS)
        xh = xv * r
        dhv = dh_ref[...]
        dg_ref[0:1, :] += jnp.sum(dhv * xh, axis=0, keepdims=True)
        dyn = dhv * g_ref[...]
        dx = r * (dyn - xh * jnp.mean(dyn * xh, axis=-1, keepdims=True))
        if resid is not None:
            dx = dx + res_ref[...]
        dx_ref[...] = dx

    row = pl.BlockSpec((tr, d), lambda i: (i, 0))
    ins = [x, g, dh] + ([] if resid is None else [resid])
    specs = [row, pl.BlockSpec((1, d), lambda i: (0, 0)), row] + ([] if resid is None else [row])
    dx, dg = pl.pallas_call(
        body, name=name,
        out_shape=(jax.ShapeDtypeStruct((s, d), F32), jax.ShapeDtypeStruct((8, d), F32)), grid=(s // tr,),
        in_specs=specs, out_specs=(row, pl.BlockSpec((8, d), lambda i: (0, 0))),
        compiler_params=_params("arbitrary"),
    )(*ins)
    return dx, dg[0]


def _loss_bwd(x, g, tgt, name):
    s, d = x.shape
    tr = _tile(s, (256, 128, 64, 32, 16))

    def body(x_ref, g_ref, t_ref, dx_ref, dg_ref, loss_ref):
        @pl.when(pl.program_id(0) == 0)
        def _():
            dg_ref[...] = jnp.zeros_like(dg_ref)
            loss_ref[...] = jnp.zeros_like(loss_ref)

        xv = x_ref[...]
        r = lax.rsqrt(jnp.mean(xv * xv, axis=-1, keepdims=True) + RMS_EPS)
        xh = xv * r
        e = xh * g_ref[...] - t_ref[...]
        per_tok = jnp.mean(e * e, axis=-1, keepdims=True)
        loss_ref[...] += 0.5 * jnp.sum(per_tok, axis=0, keepdims=True)
        dy = e * (1.0 / d)
        dg_ref[0:1, :] += jnp.sum(dy * xh, axis=0, keepdims=True)
        dyn = dy * g_ref[...]
        dx_ref[...] = r * (dyn - xh * jnp.mean(dyn * xh, axis=-1, keepdims=True))

    row = pl.BlockSpec((tr, d), lambda i: (i, 0))
    dx, dg, loss = pl.pallas_call(
        body, name=name,
        out_shape=(jax.ShapeDtypeStruct((s, d), F32), jax.ShapeDtypeStruct((8, d), F32),
                   jax.ShapeDtypeStruct((8, LANE), F32)),
        grid=(s // tr,),
        in_specs=[row, pl.BlockSpec((1, d), lambda i: (0, 0)), row],
        out_specs=(row, pl.BlockSpec((8, d), lambda i: (0, 0)), pl.BlockSpec((8, LANE), lambda i: (0, 0))),
        compiler_params=_params("arbitrary"),
    )(x, g, tgt)
    return loss[0:1, 0:1], dx, dg[0:1]


def _conv_taps(gbuf, w_ref, tt, ntap, lo):
    acc = w_ref[0:1, :] * gbuf[pl.ds(lo, tt), :]
    for k in range(1, ntap):
        acc = acc + w_ref[k:k + 1, :] * gbuf[pl.ds(lo + k, tt), :]
    return acc


def _conv_time_tile(s):
    return _tile(s, (256, 128, 64, 32))


def _conv_fwd(u, wpad, dw_b, ln_g, ln_b, pw, l, name):
    s = u.shape[0]
    c = pw.shape[1]
    ntap = 31
    tt = _conv_time_tile(s)
    hb = tt // CONV_HALO

    def body(val_ref, glu_ref, valh_ref, gluh_ref, w_ref, b_ref, lg_ref, lb_ref, pw_ref, o_ref, gbuf):
        i = pl.program_id(0)
        glh = valh_ref[...] * _sigmoid(gluh_ref[...])
        gbuf[0:CONV_HALO, :] = jnp.where(i > 0, glh, 0.0)
        gbuf[CONV_HALO:CONV_HALO + tt, :] = val_ref[...] * _sigmoid(glu_ref[...])
        acc = _conv_taps(gbuf, w_ref, tt, ntap, CONV_HALO - (ntap - 1)) + b_ref[...]
        xc = acc - jnp.mean(acc, axis=-1, keepdims=True)
        rstd = lax.rsqrt(jnp.mean(xc * xc, axis=-1, keepdims=True) + LN_EPS)
        ln = xc * rstd * lg_ref[...] + lb_ref[...]
        sw = ln * _sigmoid(ln)
        o_ref[...] = _dot(sw.astype(BF16), pw_ref[...])

    vec = pl.BlockSpec((1, c), lambda i: (0, 0))
    return pl.pallas_call(
        body, name=name, out_shape=jax.ShapeDtypeStruct((s, c), F32), grid=(s // tt,),
        in_specs=[pl.BlockSpec((tt, c), lambda i: (i, 0)), pl.BlockSpec((tt, c), lambda i: (i, 1)),
                  pl.BlockSpec((CONV_HALO, c), lambda i: (jnp.maximum(i * hb - 1, 0), 0)),
                  pl.BlockSpec((CONV_HALO, c), lambda i: (jnp.maximum(i * hb - 1, 0), 1)),
                  pl.BlockSpec((32, c), lambda i: (0, 0)), vec, vec, vec,
                  pl.BlockSpec((None, c, c), lambda i: (l, 0, 0))],
        out_specs=pl.BlockSpec((tt, c), lambda i: (i, 0)),
        scratch_shapes=[pltpu.VMEM((CONV_HALO + tt, c), F32)],
        compiler_params=_params("parallel"),
    )(u, u, u, u, wpad, dw_b, ln_g, ln_b, pw)


def _conv_bwd_post(u, dyc, wpad, dw_b, ln_g, ln_b, pw, l, name):
    s = u.shape[0]
    c = pw.shape[1]
    ntap = 31
    tt = _conv_time_tile(s)
    hb = tt // CONV_HALO

    def body(val_ref, glu_ref, valh_ref, gluh_ref, dy_ref, w_ref, b_ref, lg_ref, lb_ref, pw_ref,
             dd_ref, gl_ref, dpw_ref, vec_ref, gbuf):
        i = pl.program_id(0)

        @pl.when(i == 0)
        def _():
            dpw_ref[...] = jnp.zeros_like(dpw_ref)
            vec_ref[...] = jnp.zeros_like(vec_ref)

        glh = valh_ref[...] * _sigmoid(gluh_ref[...])
        gbuf[0:CONV_HALO, :] = jnp.where(i > 0, glh, 0.0)
        gl = val_ref[...] * _sigmoid(glu_ref[...])
        gbuf[CONV_HALO:CONV_HALO + tt, :] = gl
        gl_ref[...] = gl
        acc = _conv_taps(gbuf, w_ref, tt, ntap, CONV_HALO - (ntap - 1)) + b_ref[...]
        xc = acc - jnp.mean(acc, axis=-1, keepdims=True)
        rstd = lax.rsqrt(jnp.mean(xc * xc, axis=-1, keepdims=True) + LN_EPS)
        xh = xc * rstd
        ln = xh * lg_ref[...] + lb_ref[...]
        sig = _sigmoid(ln)
        sw = ln * sig
        dyb = dy_ref[...].astype(BF16)
        dpw_ref[...] += _dot_tn(sw.astype(BF16), dyb)
        dsw = _dot_nt(dyb, pw_ref[...])
        dln = dsw * (sig * (1.0 + ln * (1.0 - sig)))
        vec_ref[0:1, :] += jnp.sum(dln * xh, axis=0, keepdims=True)
        vec_ref[1:2, :] += jnp.sum(dln, axis=0, keepdims=True)
        dxh = dln * lg_ref[...]
        dd = rstd * (dxh - jnp.mean(dxh, axis=-1, keepdims=True)
                     - xh * jnp.mean(dxh * xh, axis=-1, keepdims=True))
        vec_ref[2:3, :] += jnp.sum(dd, axis=0, keepdims=True)
        dd_ref[...] = dd

    vec = pl.BlockSpec((1, c), lambda i: (0, 0))
    tile = pl.BlockSpec((tt, c), lambda i: (i, 0))
    return pl.pallas_call(
        body, name=name,
        out_shape=(jax.ShapeDtypeStruct((s, c), F32), jax.ShapeDtypeStruct((s, c), F32),
                   jax.ShapeDtypeStruct((c, c), F32), jax.ShapeDtypeStruct((8, c), F32)),
        grid=(s // tt,),
        in_specs=[tile, pl.BlockSpec((tt, c), lambda i: (i, 1)),
                  pl.BlockSpec((CONV_HALO, c), lambda i: (jnp.maximum(i * hb - 1, 0), 0)),
                  pl.BlockSpec((CONV_HALO, c), lambda i: (jnp.maximum(i * hb - 1, 0), 1)),
                  tile, pl.BlockSpec((32, c), lambda i: (0, 0)), vec, vec, vec,
                  pl.BlockSpec((None, c, c), lambda i: (l, 0, 0))],
        out_specs=(tile, tile, pl.BlockSpec((c, c), lambda i: (0, 0)), pl.BlockSpec((8, c), lambda i: (0, 0))),
        scratch_shapes=[pltpu.VMEM((CONV_HALO + tt, c), F32)],
        compiler_params=_params("arbitrary"),
    )(u, u, u, u, dyc, wpad, dw_b, ln_g, ln_b, pw)


def _conv_bwd_dw(u, dd, gl, wpad, name):
    s, c = dd.shape
    ntap = 31
    tt = _conv_time_tile(s)
    hb = tt // CONV_HALO
    nt = s // tt
    last_halo = s // CONV_HALO - 1

    def body(val_ref, glu_ref, dd_ref, ddn_ref, gl_ref, glh_ref, w_ref, dval_ref, dglu_ref, dw_ref, dbuf, gbuf):
        i = pl.program_id(0)

        @pl.when(i == 0)
        def _():
            dw_ref[...] = jnp.zeros_like(dw_ref)

        d = dd_ref[...]
        dbuf[0:tt, :] = d
        dbuf[tt:tt + CONV_HALO, :] = jnp.where(i < nt - 1, ddn_ref[...], 0.0)
        gbuf[0:CONV_HALO, :] = jnp.where(i > 0, glh_ref[...], 0.0)
        gbuf[CONV_HALO:CONV_HALO + tt, :] = gl_ref[...]
        dgl = w_ref[0:1, :] * dbuf[pl.ds(ntap - 1, tt), :]
        for k in range(1, ntap):
            dgl = dgl + w_ref[k:k + 1, :] * dbuf[pl.ds(ntap - 1 - k, tt), :]
        for k in range(ntap):
            dw_ref[k:k + 1, :] += jnp.sum(d * gbuf[pl.ds(CONV_HALO - (ntap - 1) + k, tt), :], axis=0, keepdims=True)
        sg = _sigmoid(glu_ref[...])
        dval_ref[...] = (dgl * sg).astype(BF16)
        dglu_ref[...] = (dgl * val_ref[...] * sg * (1.0 - sg)).astype(BF16)

    tile = pl.BlockSpec((tt, c), lambda i: (i, 0))
    return pl.pallas_call(
        body, name=name,
        out_shape=(jax.ShapeDtypeStruct((s, c), BF16), jax.ShapeDtypeStruct((s, c), BF16),
                   jax.ShapeDtypeStruct((32, c), F32)),
        grid=(nt,),
        in_specs=[tile, pl.BlockSpec((tt, c), lambda i: (i, 1)), tile,
                  pl.BlockSpec((CONV_HALO, c), lambda i: (jnp.minimum((i + 1) * hb, last_halo), 0)),
                  tile, pl.BlockSpec((CONV_HALO, c), lambda i: (jnp.maximum(i * hb - 1, 0), 0)),
                  pl.BlockSpec((32, c), lambda i: (0, 0))],
        out_specs=(tile, tile, pl.BlockSpec((32, c), lambda i: (0, 0))),
        scratch_shapes=[pltpu.VMEM((tt + CONV_HALO, c), F32), pltpu.VMEM((CONV_HALO + tt, c), F32)],
        compiler_params=_params("arbitrary"),
    )(u, u, dd, dd, gl, gl, wpad)


def _tri_ones(n, cmp):
    r = lax.broadcasted_iota(jnp.int32, (n, 2 * n), 0)
    c = lax.broadcasted_iota(jnp.int32, (n, 2 * n), 1)
    return jnp.where((c >= n) | cmp(r, c), 1.0, 0.0).astype(BF16)


def _split_dot(v, m):
    hi = v.astype(BF16)
    lo = (v - hi.astype(F32)).astype(BF16)
    return _dot(hi, m) + _dot(lo, m)


def _sb_logits(qb, kb, scale):
    z = _dot_nt(qb, kb) * scale
    e = jnp.exp(-jnp.abs(z))
    ell = -(jnp.maximum(z, 0.0) + jnp.log(1.0 + e))
    return z, e, ell


def _sb_fwd(u, heads, q_blk, k_blk, v_blk, name):
    s = u.shape[0]
    tq = _tile(s, (256, 128))
    scale = HEAD ** -0.5

    def body(q_ref, k_ref, v_ref, o_ref, tot_ref):
        i = pl.program_id(1)
        qb = q_ref[...].astype(BF16)
        tcat = _tri_ones(tq, lambda r, c: r > c)
        qpos = i * tq + lax.broadcasted_iota(jnp.int32, (tq, tq), 0)
        col = lax.broadcasted_iota(jnp.int32, (tq, tq), 1)

        def step(jj, carry):
            c_a, acc = carry
            j = i - jj
            rows = pl.ds(pl.multiple_of(j * tq, tq), tq)
            kb = k_ref[rows, :].astype(BF16)
            vb = v_ref[rows, :].astype(BF16)
            z, _, ell = _sb_logits(qb, kb, scale)
            causal = (j * tq + col) < qpos
            lm = jnp.where(causal, ell, 0.0)
            r = _split_dot(lm, tcat)
            after = c_a + r[:, :tq]
            w = jnp.where(causal, jnp.exp(ell + z + after), 0.0)
            acc = acc + _dot(w.astype(BF16), vb)
            return c_a + r[:, tq:tq + 1], acc

        c_a, acc = lax.fori_loop(0, i + 1, step, (jnp.zeros((tq, 1), F32), jnp.zeros((tq, HEAD), F32)))
        o_ref[...] = acc
        tot_ref[...] = jnp.broadcast_to(c_a, (tq, HEAD))

    full = lambda off: pl.BlockSpec((s, HEAD), lambda h, i: (0, off + h))
    out = pl.BlockSpec((tq, HEAD), lambda h, i: (i, h))
    return pl.pallas_call(
        body, name=name,
        out_shape=(jax.ShapeDtypeStruct((s, heads * HEAD), F32), jax.ShapeDtypeStruct((s, heads * HEAD), F32)),
        grid=(heads, s // tq),
        in_specs=[pl.BlockSpec((tq, HEAD), lambda h, i: (i, q_blk + h)), full(k_blk), full(v_blk)],
        out_specs=(out, out),
        compiler_params=_params("parallel", "parallel"),
    )(u, u, u)


def _sb_bwd(u, tot, dy, heads, q_blk, k_blk, v_blk, name):
    s = u.shape[0]
    tq = _tile(s, (256, 128))
    scale = HEAD ** -0.5

    def body(q_ref, k_ref, v_ref, tot_ref, dy_ref, dq_ref, dk_ref, dv_ref):
        i = pl.program_id(1)

        @pl.when(i == 0)
        def _():
            dk_ref[...] = jnp.zeros_like(dk_ref)
            dv_ref[...] = jnp.zeros_like(dv_ref)

        qb = q_ref[...].astype(BF16)
        dob = dy_ref[...].astype(BF16)
        total = tot_ref[:, 0:1]
        t_incl = _tri_ones(tq, lambda r, c: r <= c)
        t_excl = _tri_ones(tq, lambda r, c: r < c)
        qpos = i * tq + lax.broadcasted_iota(jnp.int32, (tq, tq), 0)
        col = lax.broadcasted_iota(jnp.int32, (tq, tq), 1)

        def step(j, carry):
            c_p, c_g, dq = carry
            rows = pl.ds(pl.multiple_of(j * tq, tq), tq)
            kb = k_ref[rows, :].astype(BF16)
            vb = v_ref[rows, :].astype(BF16)
            z, e, ell = _sb_logits(qb, kb, scale)
            causal = (j * tq + col) < qpos
            lm = jnp.where(causal, ell, 0.0)
            rp = _split_dot(lm, t_incl)
            after = total - (c_p + rp[:, :tq])
            w = jnp.where(causal, jnp.exp(ell + z + after), 0.0)
            g = w * _dot_nt(dob, vb)
            rg = _split_dot(g, t_excl)
            g_before = c_g + rg[:, :tq]
            inv = 1.0 / (1.0 + e)
            pos = z >= 0.0
            beta = jnp.where(pos, inv, e * inv)
            one_m_beta = jnp.where(pos, e * inv, inv)
            dz = jnp.where(causal, g * one_m_beta - g_before * beta, 0.0) * scale
            dzb = dz.astype(BF16)
            dq = dq + _dot(dzb, kb)
            dk_ref[rows, :] += _dot_tn(dzb, qb)
            dv_ref[rows, :] += _dot_tn(w.astype(BF16), dob)
            return c_p + rp[:, tq:tq + 1], c_g + rg[:, tq:tq + 1], dq

        zero = jnp.zeros((tq, 1), F32)
        _, _, dq = lax.fori_loop(0, i + 1, step, (zero, zero, jnp.zeros((tq, HEAD), F32)))
        dq_ref[...] = dq

    full = lambda off: pl.BlockSpec((s, HEAD), lambda h, i: (0, off + h))
    blk = pl.BlockSpec((tq, HEAD), lambda h, i: (i, h))
    acc = pl.BlockSpec((s, HEAD), lambda h, i: (0, h))
    shape = jax.ShapeDtypeStruct((s, heads * HEAD), F32)
    return pl.pallas_call(
        body, name=name, out_shape=(shape, shape, shape), grid=(heads, s // tq),
        in_specs=[pl.BlockSpec((tq, HEAD), lambda h, i: (i, q_blk + h)), full(k_blk), full(v_blk), blk, blk],
        out_specs=(blk, acc, acc),
        compiler_params=_params("parallel", "arbitrary"),
    )(u, u, u, tot, dy)


def _lru_time_tile(s):
    return _tile(s, (256, 128, 64, 32))


def _lru_gates(xc, wa_ref, ba_ref, wx_ref, bx_ref, lam_ref, nh):
    pr, pi = [], []
    for n in range(nh):
        xn = xc[:, n * HEAD:(n + 1) * HEAD].astype(BF16)
        pr.append(_dot(xn, wa_ref[n]))
        pi.append(_dot(xn, wx_ref[n]))
    r = _sigmoid((pr[0] if nh == 1 else jnp.concatenate(pr, axis=1)) + ba_ref[...])
    ig = _sigmoid((pi[0] if nh == 1 else jnp.concatenate(pi, axis=1)) + bx_ref[...])
    lam = lam_ref[...]
    sp = jnp.maximum(-lam, 0.0) + jnp.log(1.0 + jnp.exp(-jnp.abs(lam)))
    log_a = -LRU_C * r * sp
    a = jnp.exp(log_a)
    mult = jnp.sqrt(-_expm1(2.0 * log_a))
    return r, ig, a, mult, sp


def _lru_fwd(u, x_blk, cw, cb, wa, ba, wx, bx, lam, name):
    s = u.shape[0]
    w = lam.shape[1]
    nh = w // HEAD
    tt = _lru_time_tile(s)
    hb = tt // LRU_HALO

    def body(x_ref, xh_ref, cw_ref, cb_ref, wa_ref, ba_ref, wx_ref, bx_ref, lam_ref, y_ref,
             xbuf, abuf, bbuf, hstate, rowbuf):
        i = pl.program_id(0)

        @pl.when(i == 0)
        def _():
            hstate[...] = jnp.zeros_like(hstate)

        xbuf[0:LRU_HALO, :] = jnp.where(i > 0, xh_ref[...], 0.0)
        xbuf[LRU_HALO:LRU_HALO + tt, :] = x_ref[...]
        xc = _conv_taps(xbuf, cw_ref, tt, 4, LRU_HALO - 3) + cb_ref[...]
        _, ig, a, mult, _ = _lru_gates(xc, wa_ref, ba_ref, wx_ref, bx_ref, lam_ref, nh)
        abuf[...] = a
        bbuf[...] = mult * (ig * xc)

        def group(gi, h):
            rows = pl.ds(pl.multiple_of(gi * 8, 8), 8)
            a8 = abuf[rows, :]
            b8 = bbuf[rows, :]
            for j in range(8):
                h = a8[j:j + 1, :] * h + b8[j:j + 1, :]
                rowbuf[j:j + 1, :] = h
            y_ref[rows, :] = rowbuf[...]
            return h

        hstate[0:1, :] = lax.fori_loop(0, tt // 8, group, hstate[0:1, :])

    vec = pl.BlockSpec((1, w), lambda i: (0, 0))
    gate = pl.BlockSpec((nh, HEAD, HEAD), lambda i: (0, 0, 0))
    return pl.pallas_call(
        body, name=name, out_shape=jax.ShapeDtypeStruct((s, w), F32), grid=(s // tt,),
        in_specs=[pl.BlockSpec((tt, w), lambda i: (i, x_blk)),
                  pl.BlockSpec((LRU_HALO, w), lambda i: (jnp.maximum(i * hb - 1, 0), x_blk)),
                  pl.BlockSpec((8, w), lambda i: (0, 0)), vec, gate, vec, gate, vec, vec],
        out_specs=pl.BlockSpec((tt, w), lambda i: (i, 0)),
        scratch_shapes=[pltpu.VMEM((LRU_HALO + tt, w), F32), pltpu.VMEM((tt, w), F32), pltpu.VMEM((tt, w), F32),
                        pltpu.VMEM((8, w), F32), pltpu.VMEM((8, w), F32)],
        compiler_params=_params("arbitrary"),
    )(u, u, cw, cb, wa, ba, wx, bx, lam)


def _lru_bwd(u, x_blk, hseq, dy, cw, cb, wa, ba, wx, bx, lam, name):
    s = u.shape[0]
    w = lam.shape[1]
    nh = w // HEAD
    tt = _lru_time_tile(s)
    hb = tt // LRU_HALO
    nt = s // tt

    def body(x_ref, xh_ref, h_ref, hh_ref, dy_ref, cw_ref, cb_ref, wa_ref, ba_ref, wx_ref, bx_ref, lam_ref,
             dx_ref, dwa_ref, dwx_ref, vec_ref, xbuf, hbuf, abuf, lbuf, dbuf, cstate, dhalo, rowbuf):
        i = pl.program_id(0)
        rt = nt - 1 - i

        @pl.when(i == 0)
        def _():
            cstate[...] = jnp.zeros_like(cstate)
            dhalo[...] = jnp.zeros_like(dhalo)
            dwa_ref[...] = jnp.zeros_like(dwa_ref)
            dwx_ref[...] = jnp.zeros_like(dwx_ref)
            vec_ref[...] = jnp.zeros_like(vec_ref)

        xbuf[0:LRU_HALO, :] = jnp.where(rt > 0, xh_ref[...], 0.0)
        xbuf[LRU_HALO:LRU_HALO + tt, :] = x_ref[...]
        hbuf[0:LRU_HALO, :] = jnp.where(rt > 0, hh_ref[...], 0.0)
        hbuf[LRU_HALO:LRU_HALO + tt, :] = h_ref[...]
        xc = _conv_taps(xbuf, cw_ref, tt, 4, LRU_HALO - 3) + cb_ref[...]
        r, ig, a, mult, sp = _lru_gates(xc, wa_ref, ba_ref, wx_ref, bx_ref, lam_ref, nh)
        abuf[...] = a

        def group(gi, c):
            rows = pl.ds(pl.multiple_of((tt // 8 - 1 - gi) * 8, 8), 8)
            a8 = abuf[rows, :]
            d8 = dy_ref[rows, :]
            for j in range(7, -1, -1):
                lam_t = d8[j:j + 1, :] + c
                rowbuf[j:j + 1, :] = lam_t
                c = a8[j:j + 1, :] * lam_t
            lbuf[rows, :] = rowbuf[...]
            return c

        cstate[0:1, :] = lax.fori_loop(0, tt // 8, group, cstate[0:1, :])

        lam_t = lbuf[...]
        hprev = hbuf[pl.ds(LRU_HALO - 1, tt), :]
        ixc = ig * xc
        d_ixc = lam_t * mult
        d_ig = d_ixc * xc
        dxc = d_ixc * ig
        dlog_a = lam_t * hprev * a + lam_t * ixc * (-(a * a) / mult)
        dr = dlog_a * (-LRU_C * sp)
        lam_p = lam_ref[...]
        dsp = -_sigmoid(-lam_p)
        vec_ref[2:3, :] += jnp.sum(dlog_a * (-LRU_C * r), axis=0, keepdims=True) * dsp
        dpr = dr * r * (1.0 - r)
        dpi = d_ig * ig * (1.0 - ig)
        vec_ref[0:1, :] += jnp.sum(dpr, axis=0, keepdims=True)
        vec_ref[1:2, :] += jnp.sum(dpi, axis=0, keepdims=True)
        parts = []
        for n in range(nh):
            sl = slice(n * HEAD, (n + 1) * HEAD)
            xn = xc[:, sl].astype(BF16)
            dprn = dpr[:, sl].astype(BF16)
            dpin = dpi[:, sl].astype(BF16)
            dwa_ref[n] += _dot_tn(xn, dprn)
            dwx_ref[n] += _dot_tn(xn, dpin)
            parts.append(_dot_nt(dprn, wa_ref[n]) + _dot_nt(dpin, wx_ref[n]))
        dxc = dxc + (parts[0] if nh == 1 else jnp.concatenate(parts, axis=1))
        vec_ref[3:4, :] += jnp.sum(dxc, axis=0, keepdims=True)
        dbuf[0:tt, :] = dxc
        dbuf[tt:tt + LRU_HALO, :] = dhalo[...]
        dx = cw_ref[0:1, :] * dbuf[pl.ds(3, tt), :]
        for k in range(1, 4):
            dx = dx + cw_ref[k:k + 1, :] * dbuf[pl.ds(3 - k, tt), :]
        dx_ref[...] = dx.astype(BF16)
        for k in range(4):
            vec_ref[4 + k:5 + k, :] += jnp.sum(dxc * xbuf[pl.ds(LRU_HALO - 3 + k, tt), :], axis=0, keepdims=True)
        dhalo[...] = dbuf[0:LRU_HALO, :]

    vec = pl.BlockSpec((1, w), lambda i: (0, 0))
    gate = pl.BlockSpec((nh, HEAD, HEAD), lambda i: (0, 0, 0))
    rev = lambda i: nt - 1 - i
    tile = pl.BlockSpec((tt, w), lambda i: (rev(i), 0))
    halo = lambda col: pl.BlockSpec((LRU_HALO, w), lambda i: (jnp.maximum(rev(i) * hb - 1, 0), col))
    return pl.pallas_call(
        body, name=name,
        out_shape=(jax.ShapeDtypeStruct((s, w), BF16), jax.ShapeDtypeStruct((nh, HEAD, HEAD), F32),
                   jax.ShapeDtypeStruct((nh, HEAD, HEAD), F32), jax.ShapeDtypeStruct((8, w), F32)),
        grid=(nt,),
        in_specs=[pl.BlockSpec((tt, w), lambda i: (rev(i), x_blk)), halo(x_blk), tile, halo(0), tile,
                  pl.BlockSpec((8, w), lambda i: (0, 0)), vec, gate, vec, gate, vec, vec],
        out_specs=(tile, gate, gate, pl.BlockSpec((8, w), lambda i: (0, 0))),
        scratch_shapes=[pltpu.VMEM((LRU_HALO + tt, w), F32), pltpu.VMEM((LRU_HALO + tt, w), F32),
                        pltpu.VMEM((tt, w), F32), pltpu.VMEM((tt, w), F32), pltpu.VMEM((tt + LRU_HALO, w), F32),
                        pltpu.VMEM((8, w), F32), pltpu.VMEM((8, w), F32), pltpu.VMEM((8, w), F32)],
        compiler_params=_params("arbitrary"),
    )(u, u, hseq, hseq, dy, cw, cb, wa, ba, wx, bx, lam)


def _gate_specs(c, tr):
    return [pl.BlockSpec((tr, c), lambda i, b=b: (i, b)) for b in (2, 9, 10, 12)]


def _outgate_fwd(y_conv, y_attn, y_lru, u, n_conv, n_attn, n_lru, name):
    s, c = y_conv.shape
    tr = _tile(s, (256, 128, 64, 32, 16))

    def body(yc_ref, ya_ref, yl_ref, gc_ref, ga0_ref, ga1_ref, gl_ref, nc_ref, na_ref, nl_ref, o_ref):
        def rinv(v):
            return lax.rsqrt(jnp.mean(v * v, axis=-1, keepdims=True) + RMS_EPS)

        def silu(g):
            return g * _sigmoid(g)

        yc = yc_ref[...]
        o_ref[:, 0:c] = (yc * rinv(yc) * nc_ref[...] * silu(gc_ref[...])).astype(BF16)
        ya = ya_ref[...]
        ra = rinv(ya)
        o_ref[:, c:2 * c] = (ya[:, 0:c] * ra * na_ref[:, 0:c] * silu(ga0_ref[...])).astype(BF16)
        o_ref[:, 2 * c:3 * c] = (ya[:, c:2 * c] * ra * na_ref[:, c:2 * c] * silu(ga1_ref[...])).astype(BF16)
        yl = yl_ref[...]
        o_ref[:, 3 * c:4 * c] = (yl * rinv(yl) * nl_ref[...] * silu(gl_ref[...])).astype(BF16)

    row = lambda wd: pl.BlockSpec((tr, wd), lambda i: (i, 0))
    vec = lambda wd: pl.BlockSpec((1, wd), lambda i: (0, 0))
    return pl.pallas_call(
        body, name=name, out_shape=jax.ShapeDtypeStruct((s, 4 * c), BF16), grid=(s // tr,),
        in_specs=[row(c), row(2 * c), row(c)] + _gate_specs(c, tr) + [vec(c), vec(2 * c), vec(c)],
        out_specs=row(4 * c), compiler_params=_params("parallel"),
    )(y_conv, y_attn, y_lru, u, u, u, u, n_conv, n_attn, n_lru)


def _outgate_bwd(dy, y_conv, y_attn, y_lru, u, n_conv, n_attn, n_lru, name):
    s, c = y_conv.shape
    tr = _tile(s, (256, 128, 64, 32, 16))

    def body(dy_ref, yc_ref, ya_ref, yl_ref, gc_ref, ga0_ref, ga1_ref, gl_ref, nc_ref, na_ref, nl_ref,
             dyc_ref, dya_ref, dyl_ref, dgc_ref, dga_ref, dgl_ref, dn_ref):
        @pl.when(pl.program_id(0) == 0)
        def _():
            dn_ref[...] = jnp.zeros_like(dn_ref)

        def group(yv, gate, wv, d):
            r = lax.rsqrt(jnp.mean(yv * yv, axis=-1, keepdims=True) + RMS_EPS)
            yh = yv * r
            sg = _sigmoid(gate)
            dn = d * (gate * sg)
            dgate = d * (yh * wv) * (sg * (1.0 + gate * (1.0 - sg)))
            dw = jnp.sum(dn * yh, axis=0, keepdims=True)
            dyn = dn * wv
            dyv = r * (dyn - yh * jnp.mean(dyn * yh, axis=-1, keepdims=True))
            return dyv, dgate, dw

        dyv, dg, dw = group(yc_ref[...], gc_ref[...], nc_ref[...], dy_ref[:, 0:c])
        dyc_ref[...] = dyv
        dgc_ref[...] = dg.astype(BF16)
        dn_ref[0:1, 0:c] += dw
        gate_a = jnp.concatenate([ga0_ref[...], ga1_ref[...]], axis=1)
        dyv, dg, dw = group(ya_ref[...], gate_a, na_ref[...], dy_ref[:, c:3 * c])
        dya_ref[...] = dyv
        dga_ref[...] = dg.astype(BF16)
        dn_ref[0:1, c:3 * c] += dw
        dyv, dg, dw = group(yl_ref[...], gl_ref[...], nl_ref[...], dy_ref[:, 3 * c:4 * c])
        dyl_ref[...] = dyv
        dgl_ref[...] = dg.astype(BF16)
        dn_ref[0:1, 3 * c:4 * c] += dw

    row = lambda wd: pl.BlockSpec((tr, wd), lambda i: (i, 0))
    vec = lambda wd: pl.BlockSpec((1, wd), lambda i: (0, 0))
    sh = lambda wd, dt: jax.ShapeDtypeStruct((s, wd), dt)
    return pl.pallas_call(
        body, name=name,
        out_shape=(sh(c, F32), sh(2 * c, F32), sh(c, F32), sh(c, BF16), sh(2 * c, BF16), sh(c, BF16),
                   jax.ShapeDtypeStruct((8, 4 * c), F32)),
        grid=(s // tr,),
        in_specs=[row(4 * c), row(c), row(2 * c), row(c)] + _gate_specs(c, tr) + [vec(c), vec(2 * c), vec(c)],
        out_specs=(row(c), row(2 * c), row(c), row(c), row(2 * c), row(c),
                   pl.BlockSpec((8, 4 * c), lambda i: (0, 0))),
        compiler_params=_params("arbitrary"),
    )(dy, y_conv, y_attn, y_lru, u, u, u, u, n_conv, n_attn, n_lru)


def _xattn_probs(qh, kh, scale):
    sc = _dot_nt(qh, kh) * scale
    p = jnp.exp(sc - jnp.max(sc, axis=-1, keepdims=True))
    return p / jnp.sum(p, axis=-1, keepdims=True)


def _xattn_fwd(q, kv, name):
    s, xw = q.shape
    m = kv.shape[0]
    nh = xw // HEAD
    tq = _tile(s, (256, 128, 64, 32, 16))
    scale = HEAD ** -0.5

    def body(q_ref, kv_ref, o_ref):
        for h in range(nh):
            qh = q_ref[:, h * HEAD:(h + 1) * HEAD].astype(BF16)
            kh = kv_ref[:, h * HEAD:(h + 1) * HEAD].astype(BF16)
            vh = kv_ref[:, xw + h * HEAD:xw + (h + 1) * HEAD].astype(BF16)
            p = _xattn_probs(qh, kh, scale)
            o_ref[:, h * HEAD:(h + 1) * HEAD] = _dot(p.astype(BF16), vh).astype(BF16)

    return pl.pallas_call(
        body, name=name, out_shape=jax.ShapeDtypeStruct((s, xw), BF16), grid=(s // tq,),
        in_specs=[pl.BlockSpec((tq, xw), lambda i: (i, 0)), pl.BlockSpec((m, 2 * xw), lambda i: (0, 0))],
        out_specs=pl.BlockSpec((tq, xw), lambda i: (i, 0)), compiler_params=_params("parallel"),
    )(q, kv)


def _xattn_bwd(q, kv, do, name):
    s, xw = q.shape
    m = kv.shape[0]
    nh = xw // HEAD
    tq = _tile(s, (256, 128, 64, 32, 16))
    scale = HEAD ** -0.5

    def body(q_ref, kv_ref, do_ref, dq_ref, dkv_ref):
        @pl.when(pl.program_id(0) == 0)
        def _():
            dkv_ref[...] = jnp.zeros_like(dkv_ref)

        for h in range(nh):
            ks = slice(h * HEAD, (h + 1) * HEAD)
            vs = slice(xw + h * HEAD, xw + (h + 1) * HEAD)
            qh = q_ref[:, ks].astype(BF16)
            kh = kv_ref[:, ks].astype(BF16)
            vh = kv_ref[:, vs].astype(BF16)
            doh = do_ref[:, ks].astype(BF16)
            p = _xattn_probs(qh, kh, scale)
            dkv_ref[:, vs] += _dot_tn(p.astype(BF16), doh)
            dp = _dot_nt(doh, vh)
            ds = (p * (dp - jnp.sum(dp * p, axis=-1, keepdims=True)) * scale).astype(BF16)
            dq_ref[:, ks] = _dot(ds, kh).astype(BF16)
            dkv_ref[:, ks] += _dot_tn(ds, qh)

    row = pl.BlockSpec((tq, xw), lambda i: (i, 0))
    full = pl.BlockSpec((m, 2 * xw), lambda i: (0, 0))
    return pl.pallas_call(
        body, name=name,
        out_shape=(jax.ShapeDtypeStruct((s, xw), BF16), jax.ShapeDtypeStruct((m, 2 * xw), F32)), grid=(s // tq,),
        in_specs=[row, full, row], out_specs=(row, full), compiler_params=_params("arbitrary"),
    )(q, kv, do)


def _adamw(w, g, m, v, name):
    rows, cols = w.shape
    tr = _tile(rows, (512, 256, 128, 64, 32, 16, 8)) if rows % 8 == 0 else rows
    bc1 = 1.0 - ADAM_B1 ** ADAM_STEP
    bc2 = 1.0 - ADAM_B2 ** ADAM_STEP

    def body(w_ref, g_ref, m_ref, v_ref, d_ref, nm_ref, nv_ref):
        gv = g_ref[...]
        nm = ADAM_B1 * m_ref[...] + (1.0 - ADAM_B1) * gv
        nv = ADAM_B2 * v_ref[...] + (1.0 - ADAM_B2) * (gv * gv)
        nm_ref[...] = nm
        nv_ref[...] = nv
        d_ref[...] = -ADAM_LR * ((nm / bc1) / (jnp.sqrt(nv / bc2) + ADAM_EPS) + ADAM_WD * w_ref[...])

    spec = pl.BlockSpec((tr, cols), lambda i: (i, 0))
    sh = jax.ShapeDtypeStruct((rows, cols), F32)
    return pl.pallas_call(
        body, name=name, out_shape=(sh, sh, sh), grid=(rows // tr,), in_specs=[spec] * 4,
        out_specs=(spec, spec, spec), compiler_params=_params("parallel"),
    )(w, g, m, v)


WEIGHTS = ['mix_norm_g', 'w_in', 'conv_dw_w', 'conv_dw_b', 'conv_ln_g', 'conv_ln_b', 'conv_pw_w', 'lru_conv_w',
           'lru_conv_b', 'lru_wa', 'lru_ba', 'lru_wx', 'lru_bx', 'lru_lambda', 'out_norm_conv', 'out_norm_attn',
           'out_norm_lru', 'w_out', 'xattn_norm_g', 'mem_norm_g', 'xattn_wq', 'xattn_wkv', 'xattn_wo',
           'final_norm_g']
BIG_SHARDED = {'w_in': 2, 'conv_pw_w': 1, 'w_out': 1, 'xattn_wq': 1, 'xattn_wkv': 1, 'xattn_wo': 2}
SMALL_SHARDED = {'conv_dw_w': 2, 'lru_conv_w': 2}


def _layer_fwd(x, mem, p, l):
    row = lambda name: p[name][l][None, :]
    c = p['conv_dw_b'].shape[1]
    heads = 2 * c // HEAD
    h = _rms_fwd(x, row('mix_norm_g'), "rms_mix")
    u = _mm(h, p['w_in'], bl=l, name="in_proj")
    wpad = jnp.pad(p['conv_dw_w'][l], ((0, 1), (0, 0)))
    cw = jnp.pad(p['lru_conv_w'][l], ((0, 4), (0, 0)))
    y_conv = _conv_fwd(u, wpad, row('conv_dw_b'), row('conv_ln_g'), row('conv_ln_b'), p['conv_pw_w'], l, "conv_fwd")
    q_blk = 3 * c // HEAD
    y_attn, tot = _sb_fwd(u, heads, q_blk, q_blk + heads, q_blk + 2 * heads, "sb_fwd")
    wa, wx = p['lru_wa'][l].astype(BF16), p['lru_wx'][l].astype(BF16)
    y_lru = _lru_fwd(u, 11, cw, row('lru_conv_b'), wa, row('lru_ba'), wx, row('lru_bx'), row('lru_lambda'), "lru_fwd")
    yc = _outgate_fwd(y_conv, y_attn, y_lru, u, row('out_norm_conv'), row('out_norm_attn'), row('out_norm_lru'),
                      "outgate_fwd")
    x1 = _mm(yc, p['w_out'], bl=l, add=x, name="out_proj")
    h2 = _rms_fwd(x1, row('xattn_norm_g'), "rms_xattn")
    memn = _rms_fwd(mem, row('mem_norm_g'), "rms_mem")
    q2 = _mm(h2, p['xattn_wq'], bl=l, name="xq_proj")
    kv = _mm(memn, p['xattn_wkv'], bl=l, name="xkv_proj")
    o2 = _xattn_fwd(q2, kv, "xattn_fwd")
    x2 = _mm(o2, p['xattn_wo'], bl=l, add=x1, name="xo_proj")
    saved = dict(x=x, h=h, u=u, wpad=wpad, cw=cw, wa=wa, wx=wx, y_conv=y_conv, y_attn=y_attn, tot=tot, y_lru=y_lru,
                 yc=yc, x1=x1, h2=h2, memn=memn, q2=q2, kv=kv, o2=o2)
    return x2, saved


def _layer_bwd(dx2, mem, p, l, sv):
    row = lambda name: p[name][l][None, :]
    c = p['conv_dw_b'].shape[1]
    heads = 2 * c // HEAD
    g = {}
    g['xattn_wo'] = _mm(sv['o2'], dx2, ta=True, out_dtype=BF16, name="d_wo")
    do2 = _mm(dx2, p['xattn_wo'], bl=l, tb=True, name="d_o2")
    dq2, dkv = _xattn_bwd(sv['q2'], sv['kv'], do2, "xattn_bwd")
    g['xattn_wq'] = _mm(sv['h2'], dq2, ta=True, out_dtype=BF16, name="d_wq")
    dh2 = _mm(dq2, p['xattn_wq'], bl=l, tb=True, name="d_h2")
    g['xattn_wkv'] = _mm(sv['memn'], dkv, ta=True, out_dtype=BF16, name="d_wkv")
    dmemn = _mm(dkv, p['xattn_wkv'], bl=l, tb=True, name="d_memn")
    _, g['mem_norm_g'] = _rms_bwd(mem, row('mem_norm_g'), dmemn, None, "rms_mem_bwd")
    dx1, g['xattn_norm_g'] = _rms_bwd(sv['x1'], row('xattn_norm_g'), dh2, dx2, "rms_xattn_bwd")
    g['w_out'] = _mm(sv['yc'], dx1, ta=True, out_dtype=BF16, name="d_wout")
    dyc = _mm(dx1, p['w_out'], bl=l, tb=True, name="d_yc")
    u = sv['u']
    d_yconv, d_yattn, d_ylru, dgc, dga, dgl, dn = _outgate_bwd(
        dyc, sv['y_conv'], sv['y_attn'], sv['y_lru'], u, row('out_norm_conv'), row('out_norm_attn'),
        row('out_norm_lru'), "outgate_bwd")
    g['out_norm_conv'], g['out_norm_attn'], g['out_norm_lru'] = dn[0, 0:c], dn[0, c:3 * c], dn[0, 3 * c:4 * c]
    dd, gl, dpw, cvec = _conv_bwd_post(u, d_yconv, sv['wpad'], row('conv_dw_b'), row('conv_ln_g'),
                                       row('conv_ln_b'), p['conv_pw_w'], l, "conv_bwd_post")
    g['conv_pw_w'] = dpw.astype(BF16)
    g['conv_ln_g'], g['conv_ln_b'], g['conv_dw_b'] = cvec[0], cvec[1], cvec[2]
    dval, dglu, ddw = _conv_bwd_dw(u, dd, gl, sv['wpad'], "conv_bwd_dw")
    g['conv_dw_w'] = ddw[0:31]
    q_blk = 3 * c // HEAD
    dq, dk, dv = _sb_bwd(u, sv['tot'], d_yattn, heads, q_blk, q_blk + heads, q_blk + 2 * heads, "sb_bwd")
    dxr, g['lru_wa'], g['lru_wx'], lvec = _lru_bwd(
        u, 11, sv['y_lru'], d_ylru, sv['cw'], row('lru_conv_b'), sv['wa'], row('lru_ba'), sv['wx'], row('lru_bx'),
        row('lru_lambda'), "lru_bwd")
    g['lru_ba'], g['lru_bx'], g['lru_lambda'], g['lru_conv_b'] = lvec[0], lvec[1], lvec[2], lvec[3]
    g['lru_conv_w'] = lvec[4:8]
    du = jnp.concatenate([dval, dglu, dgc, dq.astype(BF16), dk.astype(BF16), dv.astype(BF16), dga, dxr, dgl], axis=1)
    g['w_in'] = _mm(sv['h'], du, ta=True, out_dtype=BF16, name="d_win")
    dh = _mm(du, p['w_in'], bl=l, tb=True, name="d_h")
    dx0, g['mix_norm_g'] = _rms_bwd(sv['x'], row('mix_norm_g'), dh, dx1, "rms_mix_bwd")
    return dx0, g


def kernel(x, mem, mix_norm_g, w_in, conv_dw_w, conv_dw_b, conv_ln_g, conv_ln_b, conv_pw_w, lru_conv_w, lru_conv_b, lru_wa, lru_ba, lru_wx, lru_bx, lru_lambda, out_norm_conv, out_norm_attn, out_norm_lru, w_out, xattn_norm_g, mem_norm_g, xattn_wq, xattn_wkv, xattn_wo, final_norm_g, loss_target, m_mix_norm_g, m_w_in, m_conv_dw_w, m_conv_dw_b, m_conv_ln_g, m_conv_ln_b, m_conv_pw_w, m_lru_conv_w, m_lru_conv_b, m_lru_wa, m_lru_ba, m_lru_wx, m_lru_bx, m_lru_lambda, m_out_norm_conv, m_out_norm_attn, m_out_norm_lru, m_w_out, m_xattn_norm_g, m_mem_norm_g, m_xattn_wq, m_xattn_wkv, m_xattn_wo, m_final_norm_g, v_mix_norm_g, v_w_in, v_conv_dw_w, v_conv_dw_b, v_conv_ln_g, v_conv_ln_b, v_conv_pw_w, v_lru_conv_w, v_lru_conv_b, v_lru_wa, v_lru_ba, v_lru_wx, v_lru_bx, v_lru_lambda, v_out_norm_conv, v_out_norm_attn, v_out_norm_lru, v_w_out, v_xattn_norm_g, v_mem_norm_g, v_xattn_wq, v_xattn_wkv, v_xattn_wo, v_final_norm_g):
    args = locals()
    w = {n: args[n] for n in WEIGHTS}
    mom = {n: args["m_" + n] for n in WEIGHTS}
    var = {n: args["v_" + n] for n in WEIGHTS}
    depth = w_in.shape[0]
    c = conv_dw_b.shape[1]
    assert out_norm_attn.shape[1] == 2 * c and lru_lambda.shape[1] == c and w_in.shape[2] * N_DEV == 13 * c
    assert c % HEAD == 0 and x.shape[0] == 1 and mem.shape[0] == 1
    xs, mems, tgt = x[0], mem[0], loss_target[0]
    me = 4 * lax.axis_index("x") + 2 * lax.axis_index("y") + lax.axis_index("c")

    assert depth == 2 and (2 * w_in.shape[2]) % LANE == 0 and w_in.shape[2] % LANE in (0, LANE // 2)
    my_c = lax.axis_index("c")
    big = list(BIG_SHARDED)
    dev = [n for n in big if n != 'w_in']
    pair = _pair_gather(w_in.astype(BF16), "pair_w_in")
    chip_w_in = jnp.concatenate([pair[0], pair[1]], axis=2)
    gathered = _gather_weights(chip_w_in, 2, [w[n].astype(BF16) for n in dev], [BIG_SHARDED[n] for n in dev],
                               "gather_weights")
    full = dict(zip(['w_in'] + dev, gathered))
    small = list(SMALL_SHARDED)
    gathered = _all_gather(_pack([w[n] for n in small], F32), "gather_conv_taps")
    full.update({n: _join_blocks(blk, SMALL_SHARDED[n])
                 for n, blk in zip(small, _unpack(gathered, [w[n].shape for n in small], lead=N_DEV))})
    p = {**w, **full}

    saved = []
    act = xs
    for l in range(depth):
        act, sv = _layer_fwd(act, mems, p, l)
        saved.append(sv)
    loss_part, dact, d_final = _loss_bwd(act, final_norm_g[None, :], tgt, "loss_bwd")

    layer_grads = [None] * depth
    for l in reversed(range(depth)):
        dact, layer_grads[l] = _layer_bwd(dact, mems, p, l, saved[l])
    grad_x = dact[None]
    rest = [n for n in WEIGHTS if n not in BIG_SHARDED]
    partial = {n: jnp.stack([layer_grads[l][n] for l in range(depth)]) for n in rest if n != 'final_norm_g'}
    partial['final_norm_g'] = d_final[0]

    arrs, axes, widths, starts = [], [], [], []
    for n in big:
        axis = BIG_SHARDED[n] - 1
        blk = w[n].shape[axis + 1]
        pad = blk % LANE if axis == 1 else 0
        for l in range(depth):
            arrs.append(layer_grads[l][n])
            axes.append(axis)
            widths.append(blk + pad)
            starts.append(lambda px, py, pc, blk=blk, pad=pad: blk * (4 * px + 2 * py + pc) - pad * pc)
    received = _scatter_grads(arrs, axes, widths, starts, depth, "scatter_grads")
    grads = {}
    for n, got in zip(big, received):
        summed = _sum_blocks(got.reshape(N_DEV, -1, got.shape[-1]), "sum_" + n)
        width = w[n].shape[-1]
        if summed.shape[1] != width:
            summed = jnp.where(my_c == 0, summed[:, :width], summed[:, summed.shape[1] - width:])
        grads[n] = summed.reshape(w[n].shape)

    vec = _pack([partial[n] for n in rest] + [loss_part], F32)
    total = _sum_blocks(_all_gather(vec, "gather_small_grads"), "sum_small_grads")
    pieces = _unpack(total, [partial[n].shape for n in rest] + [(1, 1)])
    loss = pieces[-1][0, 0]
    for n, piece in zip(rest, pieces[:-1]):
        if n in SMALL_SHARDED:
            width = w[n].shape[2]
            piece = lax.dynamic_slice_in_dim(piece, me * width, width, axis=2)
        grads[n] = piece

    delta, new_m, new_v = {}, {}, {}
    for n in big:
        shape = w[n].shape
        two_d = lambda a: a.reshape(-1, shape[-1])
        d, nm, nv = _adamw(two_d(w[n]), two_d(grads[n]), two_d(mom[n]), two_d(var[n]), "adamw_" + n)
        delta[n], new_m[n], new_v[n] = d.reshape(shape), nm.reshape(shape), nv.reshape(shape)
    shapes = [w[n].shape for n in rest]
    packed = [_pack([src[n] for n in rest], F32) for src in (w, grads, mom, var)]
    outs = _adamw(*packed, "adamw_small")
    for dst, o in zip((delta, new_m, new_v), outs):
        dst.update(dict(zip(rest, _unpack(o, shapes))))

    return (loss, grad_x, *[grads[n] for n in WEIGHTS], *[delta[n] for n in WEIGHTS],
            *[new_m[n] for n in WEIGHTS], *[new_v[n] for n in WEIGHTS])
```

```python
import functools
import math

import jax
import jax.numpy as jnp
from jax import lax
from jax.experimental import pallas as pl
from jax.experimental.pallas import tpu as pltpu

F32 = jnp.float32
BF16 = jnp.bfloat16

N_DEV = 8
LANE = 128
HEAD = 128
VMEM_LIMIT = 56 * 1024 * 1024
PACK_COLS = 512
RMS_EPS = 1e-6
LN_EPS = 1e-5
LRU_C = 8.0
CONV_HALO = 32
LRU_HALO = 8

ADAM_LR, ADAM_B1, ADAM_B2, ADAM_EPS, ADAM_WD, ADAM_STEP = 0.001, 0.9, 0.999, 1e-08, 0.01, 10

MESH = pl.DeviceIdType.MESH


def _tile(n, cands):
    for c in cands:
        if n % c == 0:
            return c
    raise ValueError(f"no tile of {cands} divides {n}")


def _params(*sem):
    return pltpu.CompilerParams(dimension_semantics=sem, vmem_limit_bytes=VMEM_LIMIT)


def _dot(a, b):
    return lax.dot_general(a, b, (((1,), (0,)), ((), ())), preferred_element_type=F32)


def _dot_nt(a, b):
    return lax.dot_general(a, b, (((1,), (1,)), ((), ())), preferred_element_type=F32)


def _dot_tn(a, b):
    return lax.dot_general(a, b, (((0,), (0,)), ((), ())), preferred_element_type=F32)


def _sigmoid(x):
    return 1.0 / (1.0 + jnp.exp(-x))


def _expm1(x):
    series = x * (1.0 + x * (0.5 + x * (1.0 / 6.0 + x * (1.0 / 24.0))))
    return jnp.where(jnp.abs(x) < 0.05, series, jnp.exp(x) - 1.0)


def _my_place():
    return lax.axis_index("x"), lax.axis_index("y"), lax.axis_index("c")


def _flip(v, d):
    return 1 - v if d else v


def _window(ref, axis, start, size):
    return ref.at[tuple(pl.ds(start, size) if a == axis else pl.ds(0, ref.shape[a]) for a in range(len(ref.shape)))]


def _all_gather(x2d, name):
    rows, cols = x2d.shape

    def body(x_ref, out_ref, send_sems, recv_sems, local_sem):
        x, y, c = _my_place()
        me, sibling = (x, y, c), (x, y, 1 - c)
        chips = [(1 - x, y), (x, 1 - y), (1 - x, 1 - y)]

        def blk(px, py, pc):
            return out_ref.at[4 * px + 2 * py + pc]

        def copy(k, block, to, src=None):
            return pltpu.make_async_remote_copy(
                src_ref=blk(*block) if src is None else src, dst_ref=blk(*block),
                send_sem=send_sems.at[k], recv_sem=recv_sems.at[k], device_id=to, device_id_type=MESH)

        mine = pltpu.make_async_copy(x_ref, blk(*me), local_sem)
        mine.start()
        first = [copy(0, me, sibling, src=x_ref)]
        first += [copy(1 + j, me, (*chip, c), src=x_ref) for j, chip in enumerate(chips)]
        for cp in first:
            cp.start()
        passed = [copy(4 + j, (*chip, c), sibling) for j, chip in enumerate(chips)]
        for j, chip in enumerate(chips):
            copy(1 + j, (*chip, c), me).wait_recv()
            passed[j].start()
        copy(0, sibling, me).wait_recv()
        for j, chip in enumerate(chips):
            copy(4 + j, (*chip, 1 - c), me).wait_recv()
        for cp in first + passed:
            cp.wait_send()
        mine.wait()

    return pl.pallas_call(
        body, name=name,
        out_shape=jax.ShapeDtypeStruct((N_DEV, rows, cols), x2d.dtype),
        in_specs=[pl.BlockSpec(memory_space=pl.ANY)],
        out_specs=pl.BlockSpec(memory_space=pl.ANY),
        scratch_shapes=[pltpu.SemaphoreType.DMA((7,)), pltpu.SemaphoreType.DMA((7,)), pltpu.SemaphoreType.DMA],
    )(x2d)


def _pair_gather(blk, name):
    def body(x_ref, out_ref, send_sem, recv_sem, local_sem):
        x, y, c = _my_place()
        mine = pltpu.make_async_copy(x_ref, out_ref.at[c], local_sem)
        mine.start()
        send = pltpu.make_async_remote_copy(src_ref=x_ref, dst_ref=out_ref.at[c], send_sem=send_sem,
                                            recv_sem=recv_sem, device_id=(x, y, 1 - c), device_id_type=MESH)
        send.start()
        pltpu.make_async_remote_copy(src_ref=x_ref, dst_ref=out_ref.at[1 - c], send_sem=send_sem, recv_sem=recv_sem,
                                     device_id=(x, y, 1 - c), device_id_type=MESH).wait_recv()
        send.wait_send()
        mine.wait()

    return pl.pallas_call(
        body, name=name, out_shape=jax.ShapeDtypeStruct((2,) + blk.shape, blk.dtype),
        in_specs=[pl.BlockSpec(memory_space=pl.ANY)], out_specs=pl.BlockSpec(memory_space=pl.ANY),
        scratch_shapes=[pltpu.SemaphoreType.DMA, pltpu.SemaphoreType.DMA, pltpu.SemaphoreType.DMA],
    )(blk)


def _gather_weights(chip_block, chip_axis, dev_blocks, dev_axes, name):
    n_items = 1 + len(dev_blocks)
    chip_w = chip_block.shape[chip_axis]
    dev_w = [b.shape[a] for b, a in zip(dev_blocks, dev_axes)]

    def full_shape(b, a, n):
        return b.shape[:a] + (b.shape[a] * n,) + b.shape[a + 1:]

    out_shape = [jax.ShapeDtypeStruct(full_shape(chip_block, chip_axis, 4), chip_block.dtype)]
    out_shape += [jax.ShapeDtypeStruct(full_shape(b, a, N_DEV), b.dtype) for b, a in zip(dev_blocks, dev_axes)]

    def body(*refs):
        ins, outs = refs[:n_items], refs[n_items:2 * n_items]
        send_sems, recv_sems, local_sems = refs[2 * n_items:]
        x, y, c = _my_place()
        me, sibling = (x, y, c), (x, y, 1 - c)
        chips = [(1 - x, y), (x, 1 - y), (1 - x, 1 - y)]

        def dst(t, px, py, pc):
            if t == 0:
                return _window(outs[0].at[pc], chip_axis - 1, chip_w * (2 * px + py), chip_w)
            return _window(outs[t], dev_axes[t - 1], dev_w[t - 1] * (4 * px + 2 * py + pc), dev_w[t - 1])

        def own(t):
            return ins[0].at[c] if t == 0 else ins[t]

        def copy(t, k, block, to, src=None):
            return pltpu.make_async_remote_copy(
                src_ref=dst(t, *block) if src is None else src, dst_ref=dst(t, *block),
                send_sem=send_sems.at[7 * t + k], recv_sem=recv_sems.at[7 * t + k], device_id=to, device_id_type=MESH)

        local = [pltpu.make_async_copy(ins[0], _window(outs[0], chip_axis, chip_w * (2 * x + y), chip_w),
                                       local_sems.at[0])]
        local += [pltpu.make_async_copy(ins[t], dst(t, *me), local_sems.at[t]) for t in range(1, n_items)]
        for cp in local:
            cp.start()
        first = []
        for t in range(n_items):
            if t > 0:
                first.append(copy(t, 0, me, sibling, src=own(t)))
            first += [copy(t, 1 + j, me, (*chip, c), src=own(t)) for j, chip in enumerate(chips)]
        for cp in first:
            cp.start()
        passed = []
        for j, chip in enumerate(chips):
            for t in range(n_items):
                copy(t, 1 + j, (*chip, c), me).wait_recv()
                passed.append(copy(t, 4 + j, (*chip, c), sibling))
                passed[-1].start()
        for t in range(1, n_items):
            copy(t, 0, sibling, me).wait_recv()
        for j, chip in enumerate(chips):
            for t in range(n_items):
                copy(t, 4 + j, (*chip, 1 - c), me).wait_recv()
        for cp in first + passed:
            cp.wait_send()
        for cp in local:
            cp.wait()

    hbm = pl.BlockSpec(memory_space=pl.ANY)
    return pl.pallas_call(
        body, name=name, out_shape=out_shape, in_specs=[hbm] * n_items, out_specs=[hbm] * n_items,
        scratch_shapes=[pltpu.SemaphoreType.DMA((7 * n_items,)), pltpu.SemaphoreType.DMA((7 * n_items,)),
                        pltpu.SemaphoreType.DMA((n_items,))],
    )(chip_block, *dev_blocks)


def _scatter_grads(grads, axes, widths, starts, depth, name):
    n = len(grads)

    def win_shape(g, a, w):
        return g.shape[:a] + (w,) + g.shape[a + 1:]

    out_shape = [jax.ShapeDtypeStruct((N_DEV, depth) + win_shape(g, a, w), g.dtype)
                 for g, a, w in list(zip(grads, axes, widths))[::depth]]

    def body(*refs):
        ins, outs = refs[:n], refs[n:n + n // depth]
        send_sems, recv_sems, local_sems = refs[n + n // depth:]
        x, y, c = _my_place()
        me = 4 * x + 2 * y + c

        def win(t, px, py, pc):
            start = pl.multiple_of(starts[t](px, py, pc), math.gcd(widths[t], 1024))
            return _window(ins[t], axes[t], start, widths[t])

        def slot(t, j):
            return outs[t // depth].at[j, t % depth]

        local = [pltpu.make_async_copy(win(t, x, y, c), slot(t, me), local_sems.at[t]) for t in range(n)]
        for cp in local:
            cp.start()
        sends, recvs = [], []
        for k in range(1, N_DEV):
            px, py, pc = _flip(x, k & 4), _flip(y, k & 2), _flip(c, k & 1)
            peer = 4 * px + 2 * py + pc
            for t in range(n):
                sem = 7 * t + k - 1
                sends.append(pltpu.make_async_remote_copy(
                    src_ref=win(t, px, py, pc), dst_ref=slot(t, me), send_sem=send_sems.at[sem],
                    recv_sem=recv_sems.at[sem], device_id=(px, py, pc), device_id_type=MESH))
                recvs.append(pltpu.make_async_remote_copy(
                    src_ref=win(t, px, py, pc), dst_ref=slot(t, peer), send_sem=send_sems.at[sem],
                    recv_sem=recv_sems.at[sem], device_id=(px, py, pc), device_id_type=MESH))
        for cp in sends:
            cp.start()
        for cp in recvs:
            cp.wait_recv()
        for cp in sends:
            cp.wait_send()
        for cp in local:
            cp.wait()

    hbm = pl.BlockSpec(memory_space=pl.ANY)
    return pl.pallas_call(
        body, name=name, out_shape=out_shape, in_specs=[hbm] * n, out_specs=[hbm] * (n // depth),
        scratch_shapes=[pltpu.SemaphoreType.DMA((7 * n,)), pltpu.SemaphoreType.DMA((7 * n,)),
                        pltpu.SemaphoreType.DMA((n,))],
    )(*grads)


def _sum_blocks(x3d, name):
    n, rows, cols = x3d.shape
    tr = _tile(rows, (512, 256, 128, 64, 32, 16))

    def body(x_ref, o_ref):
        acc = x_ref[0].astype(F32)
        for j in range(1, n):
            acc = acc + x_ref[j].astype(F32)
        o_ref[...] = acc

    return pl.pallas_call(
        body, name=name, out_shape=jax.ShapeDtypeStruct((rows, cols), F32), grid=(rows // tr,),
        in_specs=[pl.BlockSpec((n, tr, cols), lambda i: (0, i, 0))],
        out_specs=pl.BlockSpec((tr, cols), lambda i: (i, 0)),
        compiler_params=_params("parallel"),
    )(x3d)


def _pack(arrs, dtype, lead=None):
    if lead is None:
        flat = jnp.concatenate([a.reshape(-1).astype(dtype) for a in arrs])
        n = flat.shape[0]
        total = -(-n // (16 * PACK_COLS)) * (16 * PACK_COLS)
        return jnp.pad(flat, (0, total - n)).reshape(-1, PACK_COLS)
    flat = jnp.concatenate([a.reshape(lead, -1).astype(dtype) for a in arrs], axis=1)
    n = flat.shape[1]
    total = -(-n // (16 * PACK_COLS)) * (16 * PACK_COLS)
    return jnp.pad(flat, ((0, 0), (0, total - n))).reshape(lead, -1, PACK_COLS)


def _unpack(packed, shapes, lead=None):
    out, off = [], 0
    if lead is None:
        flat = packed.reshape(-1)
        for s in shapes:
            n = math.prod(s)
            out.append(flat[off:off + n].reshape(s))
            off += n
        return out
    flat = packed.reshape(lead, -1)
    for s in shapes:
        n = math.prod(s)
        out.append(flat[:, off:off + n].reshape((lead,) + tuple(s)))
        off += n
    return out


def _join_blocks(g, axis):
    g = jnp.moveaxis(g, 0, axis)
    s = g.shape
    return g.reshape(s[:axis] + (s[axis] * s[axis + 1],) + s[axis + 2:])


def _mm_tiles(m, n, kdim, a_bytes):
    tk = kdim if kdim <= 2048 else _tile(kdim, (2048, 1664, 1024, 832, 512, 416, 256, 128))
    tm = _tile(m, (1024, 512, 256, 128, 64, 32, 16))
    tn = _tile(n, (1024, 512, 256, 128))

    def vmem(tm, tn):
        return 2 * tm * tk * a_bytes + 2 * tn * tk * 2 + 3 * tm * tn * 4

    while vmem(tm, tn) > VMEM_LIMIT * 3 // 4 and tn > 128 and tn % 256 == 0:
        tn //= 2
    while vmem(tm, tn) > VMEM_LIMIT * 3 // 4 and tm > 128 and tm % 256 == 0:
        tm //= 2
    return tm, tn, tk


def _mm(a, b, *, ta=False, tb=False, bl=None, out_dtype=F32, add=None, name):
    if ta:
        kdim, m = a.shape
    else:
        m, kdim = a.shape
    bshape = b.shape if bl is None else b.shape[1:]
    n = bshape[0] if tb else bshape[1]
    tm, tn, tk = _mm_tiles(m, n, kdim, a.dtype.itemsize)
    nk = kdim // tk
    a_spec = pl.BlockSpec((tk, tm), lambda i, j, k: (k, i)) if ta else pl.BlockSpec((tm, tk), lambda i, j, k: (i, k))
    b_blk, b_idx = ((tn, tk), lambda i, j, k: (j, k)) if tb else ((tk, tn), lambda i, j, k: (k, j))
    if bl is None:
        b_spec = pl.BlockSpec(b_blk, b_idx)
    else:
        b_spec = pl.BlockSpec((None,) + b_blk, lambda i, j, k: (bl,) + b_idx(i, j, k))
    o_spec = pl.BlockSpec((tm, tn), lambda i, j, k: (i, j))
    dims = (((0 if ta else 1,), (1 if tb else 0,)), ((), ()))

    def body(*refs):
        a_ref, b_ref = refs[:2]
        add_ref = refs[2] if add is not None else None
        o_ref = refs[3] if add is not None else refs[2]

        def finish(r):
            if add is not None:
                r = r + add_ref[...]
            o_ref[...] = r.astype(out_dtype)

        part = lax.dot_general(a_ref[...].astype(BF16), b_ref[...].astype(BF16), dims, preferred_element_type=F32)
        if nk == 1:
            finish(part)
            return
        acc_ref = refs[-1]
        k = pl.program_id(2)

        @pl.when(k == 0)
        def _():
            acc_ref[...] = part

        @pl.when(k > 0)
        def _():
            acc_ref[...] += part

        @pl.when(k == nk - 1)
        def _():
            finish(acc_ref[...])

    ins, specs = [a, b], [a_spec, b_spec]
    if add is not None:
        ins.append(add)
        specs.append(o_spec)
    return pl.pallas_call(
        body, name=name, out_shape=jax.ShapeDtypeStruct((m, n), out_dtype), grid=(m // tm, n // tn, nk),
        in_specs=specs, out_specs=o_spec, scratch_shapes=[pltpu.VMEM((tm, tn), F32)] if nk > 1 else [],
        compiler_params=_params("parallel", "parallel", "arbitrary"),
    )(*ins)


def _rms_fwd(x, g, name):
    s, d = x.shape
    tr = _tile(s, (256, 128, 64, 32, 16))

    def body(x_ref, g_ref, o_ref):
        xv = x_ref[...]
        r = lax.rsqrt(jnp.mean(xv * xv, axis=-1, keepdims=True) + RMS_EPS)
        o_ref[...] = (xv * r * g_ref[...]).astype(BF16)

    return pl.pallas_call(
        body, name=name, out_shape=jax.ShapeDtypeStruct((s, d), BF16), grid=(s // tr,),
        in_specs=[pl.BlockSpec((tr, d), lambda i: (i, 0)), pl.BlockSpec((1, d), lambda i: (0, 0))],
        out_specs=pl.BlockSpec((tr, d), lambda i: (i, 0)), compiler_params=_params("parallel"),
    )(x, g)


def _rms_bwd(x, g, dh, resid, name):
    s, d = x.shape
    tr = _tile(s, (256, 128, 64, 32, 16))

    def body(*refs):
        if resid is None:
            x_ref, g_ref, dh_ref, dx_ref, dg_ref = refs
        else:
            x_ref, g_ref, dh_ref, res_ref, dx_ref, dg_ref = refs

        @pl.when(pl.program_id(0) == 0)
        def _():
            dg_ref[...] = jnp.zeros_like(dg_ref)

        xv = x_ref[...]
        r = lax.rsqrt(jnp.mean(xv * xv, axis=-1, keepdims=True) + RMS_EPS)
        xh = xv * r
        dhv = dh_ref[...]
        dg_ref[0:1, :] += jnp.sum(dhv * xh, axis=0, keepdims=True)
        dyn = dhv * g_ref[...]
        dx = r * (dyn - xh * jnp.mean(dyn * xh, axis=-1, keepdims=True))
        if resid is not None:
            dx = dx + res_ref[...]
        dx_ref[...] = dx

    row = pl.BlockSpec((tr, d), lambda i: (i, 0))
    ins = [x, g, dh] + ([] if resid is None else [resid])
    specs = [row, pl.BlockSpec((1, d), lambda i: (0, 0)), row] + ([] if resid is None else [row])
    dx, dg = pl.pallas_call(
        body, name=name,
        out_shape=(jax.ShapeDtypeStruct((s, d), F32), jax.ShapeDtypeStruct((8, d), F32)), grid=(s // tr,),
        in_specs=specs, out_specs=(row, pl.BlockSpec((8, d), lambda i: (0, 0))),
        compiler_params=_params("arbitrary"),
    )(*ins)
    return dx, dg[0]


def _loss_bwd(x, g, tgt, name):
    s, d = x.shape
    tr = _tile(s, (256, 128, 64, 32, 16))

    def body(x_ref, g_ref, t_ref, dx_ref, dg_ref, loss_ref):
        @pl.when(pl.program_id(0) == 0)
        def _():
            dg_ref[...] = jnp.zeros_like(dg_ref)
            loss_ref[...] = jnp.zeros_like(loss_ref)

        xv = x_ref[...]
        r = lax.rsqrt(jnp.mean(xv * xv, axis=-1, keepdims=True) + RMS_EPS)
        xh = xv * r
        e = xh * g_ref[...] - t_ref[...]
        per_tok = jnp.mean(e * e, axis=-1, keepdims=True)
        loss_ref[...] += 0.5 * jnp.sum(per_tok, axis=0, keepdims=True)
        dy = e * (1.0 / d)
        dg_ref[0:1, :] += jnp.sum(dy * xh, axis=0, keepdims=True)
        dyn = dy * g_ref[...]
        dx_ref[...] = r * (dyn - xh * jnp.mean(dyn * xh, axis=-1, keepdims=True))

    row = pl.BlockSpec((tr, d), lambda i: (i, 0))
    dx, dg, loss = pl.pallas_call(
        body, name=name,
        out_shape=(jax.ShapeDtypeStruct((s, d), F32), jax.ShapeDtypeStruct((8, d), F32),
                   jax.ShapeDtypeStruct((8, LANE), F32)),
        grid=(s // tr,),
        in_specs=[row, pl.BlockSpec((1, d), lambda i: (0, 0)), row],
        out_specs=(row, pl.BlockSpec((8, d), lambda i: (0, 0)), pl.BlockSpec((8, LANE), lambda i: (0, 0))),
        compiler_params=_params("arbitrary"),
    )(x, g, tgt)
    return loss[0:1, 0:1], dx, dg[0:1]


def _conv_taps(gbuf, w_ref, tt, ntap, lo):
    acc = w_ref[0:1, :] * gbuf[pl.ds(lo, tt), :]
    for k in range(1, ntap):
        acc = acc + w_ref[k:k + 1, :] * gbuf[pl.ds(lo + k, tt), :]
    return acc


def _conv_time_tile(s):
    return _tile(s, (256, 128, 64, 32))


def _conv_fwd(u, wpad, dw_b, ln_g, ln_b, pw, l, name):
    s = u.shape[0]
    c = pw.shape[1]
    ntap = 31
    tt = _conv_time_tile(s)
    hb = tt // CONV_HALO

    def body(val_ref, glu_ref, valh_ref, gluh_ref, w_ref, b_ref, lg_ref, lb_ref, pw_ref, o_ref, gbuf):
        i = pl.program_id(0)
        glh = valh_ref[...] * _sigmoid(gluh_ref[...])
        gbuf[0:CONV_HALO, :] = jnp.where(i > 0, glh, 0.0)
        gbuf[CONV_HALO:CONV_HALO + tt, :] = val_ref[...] * _sigmoid(glu_ref[...])
        acc = _conv_taps(gbuf, w_ref, tt, ntap, CONV_HALO - (ntap - 1)) + b_ref[...]
        xc = acc - jnp.mean(acc, axis=-1, keepdims=True)
        rstd = lax.rsqrt(jnp.mean(xc * xc, axis=-1, keepdims=True) + LN_EPS)
        ln = xc * rstd * lg_ref[...] + lb_ref[...]
        sw = ln * _sigmoid(ln)
        o_ref[...] = _dot(sw.astype(BF16), pw_ref[...])

    vec = pl.BlockSpec((1, c), lambda i: (0, 0))
    return pl.pallas_call(
        body, name=name, out_shape=jax.ShapeDtypeStruct((s, c), F32), grid=(s // tt,),
        in_specs=[pl.BlockSpec((tt, c), lambda i: (i, 0)), pl.BlockSpec((tt, c), lambda i: (i, 1)),
                  pl.BlockSpec((CONV_HALO, c), lambda i: (jnp.maximum(i * hb - 1, 0), 0)),
                  pl.BlockSpec((CONV_HALO, c), lambda i: (jnp.maximum(i * hb - 1, 0), 1)),
                  pl.BlockSpec((32, c), lambda i: (0, 0)), vec, vec, vec,
                  pl.BlockSpec((None, c, c), lambda i: (l, 0, 0))],
        out_specs=pl.BlockSpec((tt, c), lambda i: (i, 0)),
        scratch_shapes=[pltpu.VMEM((CONV_HALO + tt, c), F32)],
        compiler_params=_params("parallel"),
    )(u, u, u, u, wpad, dw_b, ln_g, ln_b, pw)


def _conv_bwd_post(u, dyc, wpad, dw_b, ln_g, ln_b, pw, l, name):
    s = u.shape[0]
    c = pw.shape[1]
    ntap = 31
    tt = _conv_time_tile(s)
    hb = tt // CONV_HALO

    def body(val_ref, glu_ref, valh_ref, gluh_ref, dy_ref, w_ref, b_ref, lg_ref, lb_ref, pw_ref,
             dd_ref, gl_ref, dpw_ref, vec_ref, gbuf):
        i = pl.program_id(0)

        @pl.when(i == 0)
        def _():
            dpw_ref[...] = jnp.zeros_like(dpw_ref)
            vec_ref[...] = jnp.zeros_like(vec_ref)

        glh = valh_ref[...] * _sigmoid(gluh_ref[...])
        gbuf[0:CONV_HALO, :] = jnp.where(i > 0, glh, 0.0)
        gl = val_ref[...] * _sigmoid(glu_ref[...])
        gbuf[CONV_HALO:CONV_HALO + tt, :] = gl
        gl_ref[...] = gl
        acc = _conv_taps(gbuf, w_ref, tt, ntap, CONV_HALO - (ntap - 1)) + b_ref[...]
        xc = acc - jnp.mean(acc, axis=-1, keepdims=True)
        rstd = lax.rsqrt(jnp.mean(xc * xc, axis=-1, keepdims=True) + LN_EPS)
        xh = xc * rstd
        ln = xh * lg_ref[...] + lb_ref[...]
        sig = _sigmoid(ln)
        sw = ln * sig
        dyb = dy_ref[...].astype(BF16)
        dpw_ref[...] += _dot_tn(sw.astype(BF16), dyb)
        dsw = _dot_nt(dyb, pw_ref[...])
        dln = dsw * (sig * (1.0 + ln * (1.0 - sig)))
        vec_ref[0:1, :] += jnp.sum(dln * xh, axis=0, keepdims=True)
        vec_ref[1:2, :] += jnp.sum(dln, axis=0, keepdims=True)
        dxh = dln * lg_ref[...]
        dd = rstd * (dxh - jnp.mean(dxh, axis=-1, keepdims=True)
                     - xh * jnp.mean(dxh * xh, axis=-1, keepdims=True))
        vec_ref[2:3, :] += jnp.sum(dd, axis=0, keepdims=True)
        dd_ref[...] = dd

    vec = pl.BlockSpec((1, c), lambda i: (0, 0))
    tile = pl.BlockSpec((tt, c), lambda i: (i, 0))
    return pl.pallas_call(
        body, name=name,
        out_shape=(jax.ShapeDtypeStruct((s, c), F32), jax.ShapeDtypeStruct((s, c), F32),
                   jax.ShapeDtypeStruct((c, c), F32), jax.ShapeDtypeStruct((8, c), F32)),
        grid=(s // tt,),
        in_specs=[tile, pl.BlockSpec((tt, c), lambda i: (i, 1)),
                  pl.BlockSpec((CONV_HALO, c), lambda i: (jnp.maximum(i * hb - 1, 0), 0)),
                  pl.BlockSpec((CONV_HALO, c), lambda i: (jnp.maximum(i * hb - 1, 0), 1)),
                  tile, pl.BlockSpec((32, c), lambda i: (0, 0)), vec, vec, vec,
                  pl.BlockSpec((None, c, c), lambda i: (l, 0, 0))],
        out_specs=(tile, tile, pl.BlockSpec((c, c), lambda i: (0, 0)), pl.BlockSpec((8, c), lambda i: (0, 0))),
        scratch_shapes=[pltpu.VMEM((CONV_HALO + tt, c), F32)],
        compiler_params=_params("arbitrary"),
    )(u, u, u, u, dyc, wpad, dw_b, ln_g, ln_b, pw)


def _conv_bwd_dw(u, dd, gl, wpad, name):
    s, c = dd.shape
    ntap = 31
    tt = _conv_time_tile(s)
    hb = tt // CONV_HALO
    nt = s // tt
    last_halo = s // CONV_HALO - 1

    def body(val_ref, glu_ref, dd_ref, ddn_ref, gl_ref, glh_ref, w_ref, dval_ref, dglu_ref, dw_ref, dbuf, gbuf):
        i = pl.program_id(0)

        @pl.when(i == 0)
        def _():
            dw_ref[...] = jnp.zeros_like(dw_ref)

        d = dd_ref[...]
        dbuf[0:tt, :] = d
        dbuf[tt:tt + CONV_HALO, :] = jnp.where(i < nt - 1, ddn_ref[...], 0.0)
        gbuf[0:CONV_HALO, :] = jnp.where(i > 0, glh_ref[...], 0.0)
        gbuf[CONV_HALO:CONV_HALO + tt, :] = gl_ref[...]
        dgl = w_ref[0:1, :] * dbuf[pl.ds(ntap - 1, tt), :]
        for k in range(1, ntap):
            dgl = dgl + w_ref[k:k + 1, :] * dbuf[pl.ds(ntap - 1 - k, tt), :]
        for k in range(ntap):
            dw_ref[k:k + 1, :] += jnp.sum(d * gbuf[pl.ds(CONV_HALO - (ntap - 1) + k, tt), :], axis=0, keepdims=True)
        sg = _sigmoid(glu_ref[...])
        dval_ref[...] = (dgl * sg).astype(BF16)
        dglu_ref[...] = (dgl * val_ref[...] * sg * (1.0 - sg)).astype(BF16)

    tile = pl.BlockSpec((tt, c), lambda i: (i, 0))
    return pl.pallas_call(
        body, name=name,
        out_shape=(jax.ShapeDtypeStruct((s, c), BF16), jax.ShapeDtypeStruct((s, c), BF16),
                   jax.ShapeDtypeStruct((32, c), F32)),
        grid=(nt,),
        in_specs=[tile, pl.BlockSpec((tt, c), lambda i: (i, 1)), tile,
                  pl.BlockSpec((CONV_HALO, c), lambda i: (jnp.minimum((i + 1) * hb, last_halo), 0)),
                  tile, pl.BlockSpec((CONV_HALO, c), lambda i: (jnp.maximum(i * hb - 1, 0), 0)),
                  pl.BlockSpec((32, c), lambda i: (0, 0))],
        out_specs=(tile, tile, pl.BlockSpec((32, c), lambda i: (0, 0))),
        scratch_shapes=[pltpu.VMEM((tt + CONV_HALO, c), F32), pltpu.VMEM((CONV_HALO + tt, c), F32)],
        compiler_params=_params("arbitrary"),
    )(u, u, dd, dd, gl, gl, wpad)


SB_ROWS = 32


def _tri(n, cmp):
    r = lax.broadcasted_iota(jnp.int32, (n, n), 0)
    c = lax.broadcasted_iota(jnp.int32, (n, n), 1)
    return jnp.where(cmp(r, c), 1.0, 0.0).astype(BF16)


def _row_chunks(fn, n, *arrs):
    outs = [fn(*[a[r:r + SB_ROWS] for a in arrs]) for r in range(0, n, SB_ROWS)]
    return tuple(jnp.concatenate(list(o), axis=0) for o in zip(*outs))


def _hi_lo(v):
    hi = v.astype(BF16)
    return hi, (v - hi.astype(F32)).astype(BF16)


def _sb_sticks(z, causal, scale):
    z = z * scale
    l1p = jnp.log(1.0 + jnp.exp(-jnp.abs(z)))
    lb = jnp.minimum(z, 0.0) - l1p
    ell = lb - z
    if causal is not None:
        ell = jnp.where(causal, ell, 0.0)
    hi, lo = _hi_lo(ell)
    return lb, hi, lo, jnp.sum(ell, axis=1, keepdims=True)


def _sb_fwd(u, heads, q_blk, k_blk, v_blk, name):
    s = u.shape[0]
    tq = _tile(s, (256, 128))
    scale = HEAD ** -0.5

    def body(q_ref, k_ref, v_ref, o_ref, tot_ref):
        i = pl.program_id(1)
        qb = q_ref[...].astype(BF16)
        t_sfx = _tri(tq, lambda r, c: r > c)
        below_diag = lax.broadcasted_iota(jnp.int32, (tq, tq), 1) < lax.broadcasted_iota(jnp.int32, (tq, tq), 0)

        def blocks(j0, nb, c_a, acc, diag):
            mask = [below_diag] if diag else []
            rows = pl.ds(pl.multiple_of(j0 * tq, tq), nb * tq)
            kb = k_ref[rows, :].astype(BF16)
            vb = v_ref[rows, :].astype(BF16)
            z = _dot_nt(qb, kb)

            def sticks(zc, *m):
                out = []
                for b in range(nb):
                    out += _sb_sticks(zc[:, b * tq:(b + 1) * tq], m[0] if m else None, scale)
                return tuple(out)

            st = _row_chunks(sticks, tq, z, *mask)
            lb, hi, lo, rs = st[0::4], st[1::4], st[2::4], st[3::4]
            sfx = [_dot(hi[b], t_sfx) + _dot(lo[b], t_sfx) for b in range(nb)]
            before, run = [None] * nb, c_a
            for b in reversed(range(nb)):
                before[b], run = run, run + rs[b]

            def weights(*a):
                ws = []
                for b in range(nb):
                    lbc, sfxc, befc = a[3 * b:3 * b + 3]
                    w = jnp.exp(lbc + (befc + sfxc))
                    if diag:
                        w = jnp.where(a[-1], w, 0.0)
                    ws.append(w.astype(BF16))
                return (ws[0] if nb == 1 else jnp.concatenate(ws, axis=1),)

            flat = [v for b in range(nb) for v in (lb[b], sfx[b], before[b])]
            wb, = _row_chunks(weights, tq, *flat, *mask)
            return run, acc + _dot(wb, vb)

        carry = blocks(i, 1, jnp.zeros((tq, 1), F32), jnp.zeros((tq, HEAD), F32), True)
        carry = lax.fori_loop(0, i // 4, lambda t, cr: blocks(i - 4 - 4 * t, 4, *cr, False), carry)
        carry = lax.fori_loop(0, (i % 4) // 2, lambda _, cr: blocks(i % 2, 2, *cr, False), carry)
        c_a, acc = lax.fori_loop(0, i % 2, lambda _, cr: blocks(0, 1, *cr, False), carry)
        o_ref[...] = acc
        tot_ref[...] = jnp.broadcast_to(c_a, (tq, HEAD))

    full = lambda off: pl.BlockSpec((s, HEAD), lambda h, i: (0, off + h))
    out = pl.BlockSpec((tq, HEAD), lambda h, i: (i, h))
    return pl.pallas_call(
        body, name=name,
        out_shape=(jax.ShapeDtypeStruct((s, heads * HEAD), F32), jax.ShapeDtypeStruct((s, heads * HEAD), F32)),
        grid=(heads, s // tq),
        in_specs=[pl.BlockSpec((tq, HEAD), lambda h, i: (i, q_blk + h)), full(k_blk), full(v_blk)],
        out_specs=(out, out),
        compiler_params=_params("parallel", "parallel"),
    )(u, u, u)


def _sb_bwd(u, tot, dy, heads, q_blk, k_blk, v_blk, name):
    s = u.shape[0]
    tq = _tile(s, (256, 128))
    scale = HEAD ** -0.5

    def body(q_ref, k_ref, v_ref, tot_ref, dy_ref, dq_ref, dk_ref, dv_ref):
        i = pl.program_id(1)

        @pl.when(i == 0)
        def _():
            dk_ref[...] = jnp.zeros_like(dk_ref)
            dv_ref[...] = jnp.zeros_like(dv_ref)

        qb = q_ref[...].astype(BF16)
        dob = dy_ref[...].astype(BF16)
        total = tot_ref[:, 0:1]
        t_incl = _tri(tq, lambda r, c: r <= c)
        t_excl = _tri(tq, lambda r, c: r < c)
        below_diag = lax.broadcasted_iota(jnp.int32, (tq, tq), 1) < lax.broadcasted_iota(jnp.int32, (tq, tq), 0)

        def blocks(j0, nb, c_p, c_g, dq, diag):
            mask = [below_diag] if diag else []
            rows = pl.ds(pl.multiple_of(j0 * tq, tq), nb * tq)
            kb = k_ref[rows, :].astype(BF16)
            vb = v_ref[rows, :].astype(BF16)
            z = _dot_nt(qb, kb)
            dw = _dot_nt(dob, vb)
            cols = lambda a, b: a[:, b * tq:(b + 1) * tq]

            def sticks(zc, *m):
                out = []
                for b in range(nb):
                    out += _sb_sticks(cols(zc, b), m[0] if m else None, scale)
                return tuple(out)

            st = _row_chunks(sticks, tq, z, *mask)
            lb, hi, lo, rs_l = st[0::4], st[1::4], st[2::4], st[3::4]
            pfx = [_dot(hi[b], t_incl) + _dot(lo[b], t_incl) for b in range(nb)]
            p_before = [c_p]
            for b in range(nb):
                p_before.append(p_before[-1] + rs_l[b])

            def weights(totc, dwc, *a):
                out = []
                for b in range(nb):
                    lbc, pfxc, pbc = a[3 * b:3 * b + 3]
                    w = jnp.exp(lbc + (totc - (pbc + pfxc)))
                    if diag:
                        w = jnp.where(a[-1], w, 0.0)
                    g = w * cols(dwc, b)
                    out += [w.astype(BF16), g, *_hi_lo(g), jnp.sum(g, axis=1, keepdims=True)]
                return tuple(out)

            flat = [v for b in range(nb) for v in (lb[b], pfx[b], p_before[b])]
            wt = _row_chunks(weights, tq, total, dw, *flat, *mask)
            wb, g, ghi, glo, rs_g = wt[0::5], wt[1::5], wt[2::5], wt[3::5], wt[4::5]
            g_pre = [_dot(ghi[b], t_excl) + _dot(glo[b], t_excl) for b in range(nb)]
            g_before = [c_g]
            for b in range(nb):
                g_before.append(g_before[-1] + rs_g[b])

            def dscore(*a):
                dzs = []
                for b in range(nb):
                    lbc, gc, gprec, gbc = a[4 * b:4 * b + 4]
                    beta = jnp.exp(lbc)
                    dz = (gc * (1.0 - beta) - (gbc + gprec) * beta) * scale
                    if diag:
                        dz = jnp.where(a[-1], dz, 0.0)
                    dzs.append(dz.astype(BF16))
                return (dzs[0] if nb == 1 else jnp.concatenate(dzs, axis=1),)

            flat = [v for b in range(nb) for v in (lb[b], g[b], g_pre[b], g_before[b])]
            dzb, = _row_chunks(dscore, tq, *flat, *mask)
            wcat = wb[0] if nb == 1 else jnp.concatenate(wb, axis=1)
            dk_ref[rows, :] += _dot_tn(dzb, qb)
            dv_ref[rows, :] += _dot_tn(wcat, dob)
            return p_before[-1], g_before[-1], dq + _dot(dzb, kb)

        zero = jnp.zeros((tq, 1), F32)
        carry = lax.fori_loop(0, i // 2, lambda t, cr: blocks(2 * t, 2, *cr, False),
                              (zero, zero, jnp.zeros((tq, HEAD), F32)))
        carry = lax.fori_loop(0, i % 2, lambda _, cr: blocks(i - 1, 1, *cr, False), carry)
        _, _, dq = blocks(i, 1, *carry, True)
        dq_ref[...] = dq

    full = lambda off: pl.BlockSpec((s, HEAD), lambda h, i: (0, off + h))
    blk = pl.BlockSpec((tq, HEAD), lambda h, i: (i, h))
    acc = pl.BlockSpec((s, HEAD), lambda h, i: (0, h))
    shape = jax.ShapeDtypeStruct((s, heads * HEAD), F32)
    return pl.pallas_call(
        body, name=name, out_shape=(shape, shape, shape), grid=(heads, s // tq),
        in_specs=[pl.BlockSpec((tq, HEAD), lambda h, i: (i, q_blk + h)), full(k_blk), full(v_blk), blk, blk],
        out_specs=(blk, acc, acc),
        compiler_params=_params("parallel", "arbitrary"),
    )(u, u, u, tot, dy)


def _lru_time_tile(s):
    return _tile(s, (256, 128, 64, 32))


def _lru_gates(xc, wa_ref, ba_ref, wx_ref, bx_ref, lam_ref, nh):
    pr, pi = [], []
    for n in range(nh):
        xn = xc[:, n * HEAD:(n + 1) * HEAD].astype(BF16)
        pr.append(_dot(xn, wa_ref[n]))
        pi.append(_dot(xn, wx_ref[n]))
    r = _sigmoid((pr[0] if nh == 1 else jnp.concatenate(pr, axis=1)) + ba_ref[...])
    ig = _sigmoid((pi[0] if nh == 1 else jnp.concatenate(pi, axis=1)) + bx_ref[...])
    lam = lam_ref[...]
    sp = jnp.maximum(-lam, 0.0) + jnp.log(1.0 + jnp.exp(-jnp.abs(lam)))
    log_a = -LRU_C * r * sp
    a = jnp.exp(log_a)
    mult = jnp.sqrt(-_expm1(2.0 * log_a))
    return r, ig, a, mult, sp


def _lru_fwd(u, x_blk, cw, cb, wa, ba, wx, bx, lam, name):
    s = u.shape[0]
    w = lam.shape[1]
    nh = w // HEAD
    tt = _lru_time_tile(s)
    hb = tt // LRU_HALO

    def body(x_ref, xh_ref, cw_ref, cb_ref, wa_ref, ba_ref, wx_ref, bx_ref, lam_ref, y_ref,
             xbuf, abuf, bbuf, hstate, rowbuf):
        i = pl.program_id(0)

        @pl.when(i == 0)
        def _():
            hstate[...] = jnp.zeros_like(hstate)

        xbuf[0:LRU_HALO, :] = jnp.where(i > 0, xh_ref[...], 0.0)
        xbuf[LRU_HALO:LRU_HALO + tt, :] = x_ref[...]
        xc = _conv_taps(xbuf, cw_ref, tt, 4, LRU_HALO - 3) + cb_ref[...]
        _, ig, a, mult, _ = _lru_gates(xc, wa_ref, ba_ref, wx_ref, bx_ref, lam_ref, nh)
        abuf[...] = a
        bbuf[...] = mult * (ig * xc)

        def group(gi, h):
            rows = pl.ds(pl.multiple_of(gi * 8, 8), 8)
            a8 = abuf[rows, :]
            b8 = bbuf[rows, :]
            for j in range(8):
                h = a8[j:j + 1, :] * h + b8[j:j + 1, :]
                rowbuf[j:j + 1, :] = h
            y_ref[rows, :] = rowbuf[...]
            return h

        hstate[0:1, :] = lax.fori_loop(0, tt // 8, group, hstate[0:1, :])

    vec = pl.BlockSpec((1, w), lambda i: (0, 0))
    gate = pl.BlockSpec((nh, HEAD, HEAD), lambda i: (0, 0, 0))
    return pl.pallas_call(
        body, name=name, out_shape=jax.ShapeDtypeStruct((s, w), F32), grid=(s // tt,),
        in_specs=[pl.BlockSpec((tt, w), lambda i: (i, x_blk)),
                  pl.BlockSpec((LRU_HALO, w), lambda i: (jnp.maximum(i * hb - 1, 0), x_blk)),
                  pl.BlockSpec((8, w), lambda i: (0, 0)), vec, gate, vec, gate, vec, vec],
        out_specs=pl.BlockSpec((tt, w), lambda i: (i, 0)),
        scratch_shapes=[pltpu.VMEM((LRU_HALO + tt, w), F32), pltpu.VMEM((tt, w), F32), pltpu.VMEM((tt, w), F32),
                        pltpu.VMEM((8, w), F32), pltpu.VMEM((8, w), F32)],
        compiler_params=_params("arbitrary"),
    )(u, u, cw, cb, wa, ba, wx, bx, lam)


def _lru_bwd(u, x_blk, hseq, dy, cw, cb, wa, ba, wx, bx, lam, name):
    s = u.shape[0]
    w = lam.shape[1]
    nh = w // HEAD
    tt = _lru_time_tile(s)
    hb = tt // LRU_HALO
    nt = s // tt

    def body(x_ref, xh_ref, h_ref, hh_ref, dy_ref, cw_ref, cb_ref, wa_ref, ba_ref, wx_ref, bx_ref, lam_ref,
             dx_ref, dwa_ref, dwx_ref, vec_ref, xbuf, hbuf, abuf, lbuf, dbuf, cstate, dhalo, rowbuf):
        i = pl.program_id(0)
        rt = nt - 1 - i

        @pl.when(i == 0)
        def _():
            cstate[...] = jnp.zeros_like(cstate)
            dhalo[...] = jnp.zeros_like(dhalo)
            dwa_ref[...] = jnp.zeros_like(dwa_ref)
            dwx_ref[...] = jnp.zeros_like(dwx_ref)
            vec_ref[...] = jnp.zeros_like(vec_ref)

        xbuf[0:LRU_HALO, :] = jnp.where(rt > 0, xh_ref[...], 0.0)
        xbuf[LRU_HALO:LRU_HALO + tt, :] = x_ref[...]
        hbuf[0:LRU_HALO, :] = jnp.where(rt > 0, hh_ref[...], 0.0)
        hbuf[LRU_HALO:LRU_HALO + tt, :] = h_ref[...]
        xc = _conv_taps(xbuf, cw_ref, tt, 4, LRU_HALO - 3) + cb_ref[...]
        r, ig, a, mult, sp = _lru_gates(xc, wa_ref, ba_ref, wx_ref, bx_ref, lam_ref, nh)
        abuf[...] = a

        def group(gi, c):
            rows = pl.ds(pl.multiple_of((tt // 8 - 1 - gi) * 8, 8), 8)
            a8 = abuf[rows, :]
            d8 = dy_ref[rows, :]
            for j in range(7, -1, -1):
                lam_t = d8[j:j + 1, :] + c
                rowbuf[j:j + 1, :] = lam_t
                c = a8[j:j + 1, :] * lam_t
            lbuf[rows, :] = rowbuf[...]
            return c

        cstate[0:1, :] = lax.fori_loop(0, tt // 8, group, cstate[0:1, :])

        lam_t = lbuf[...]
        hprev = hbuf[pl.ds(LRU_HALO - 1, tt), :]
        ixc = ig * xc
        d_ixc = lam_t * mult
        d_ig = d_ixc * xc
        dxc = d_ixc * ig
        dlog_a = lam_t * hprev * a + lam_t * ixc * (-(a * a) / mult)
        dr = dlog_a * (-LRU_C * sp)
        lam_p = lam_ref[...]
        dsp = -_sigmoid(-lam_p)
        vec_ref[2:3, :] += jnp.sum(dlog_a * (-LRU_C * r), axis=0, keepdims=True) * dsp
        dpr = dr * r * (1.0 - r)
        dpi = d_ig * ig * (1.0 - ig)
        vec_ref[0:1, :] += jnp.sum(dpr, axis=0, keepdims=True)
        vec_ref[1:2, :] += jnp.sum(dpi, axis=0, keepdims=True)
        parts = []
        for n in range(nh):
            sl = slice(n * HEAD, (n + 1) * HEAD)
            xn = xc[:, sl].astype(BF16)
            dprn = dpr[:, sl].astype(BF16)
            dpin = dpi[:, sl].astype(BF16)
            dwa_ref[n] += _dot_tn(xn, dprn)
            dwx_ref[n] += _dot_tn(xn, dpin)
            parts.append(_dot_nt(dprn, wa_ref[n]) + _dot_nt(dpin, wx_ref[n]))
        dxc = dxc + (parts[0] if nh == 1 else jnp.concatenate(parts, axis=1))
        vec_ref[3:4, :] += jnp.sum(dxc, axis=0, keepdims=True)
        dbuf[0:tt, :] = dxc
        dbuf[tt:tt + LRU_HALO, :] = dhalo[...]
        dx = cw_ref[0:1, :] * dbuf[pl.ds(3, tt), :]
        for k in range(1, 4):
            dx = dx + cw_ref[k:k + 1, :] * dbuf[pl.ds(3 - k, tt), :]
        dx_ref[...] = dx.astype(BF16)
        for k in range(4):
            vec_ref[4 + k:5 + k, :] += jnp.sum(dxc * xbuf[pl.ds(LRU_HALO - 3 + k, tt), :], axis=0, keepdims=True)
        dhalo[...] = dbuf[0:LRU_HALO, :]

    vec = pl.BlockSpec((1, w), lambda i: (0, 0))
    gate = pl.BlockSpec((nh, HEAD, HEAD), lambda i: (0, 0, 0))
    rev = lambda i: nt - 1 - i
    tile = pl.BlockSpec((tt, w), lambda i: (rev(i), 0))
    halo = lambda col: pl.BlockSpec((LRU_HALO, w), lambda i: (jnp.maximum(rev(i) * hb - 1, 0), col))
    return pl.pallas_call(
        body, name=name,
        out_shape=(jax.ShapeDtypeStruct((s, w), BF16), jax.ShapeDtypeStruct((nh, HEAD, HEAD), F32),
                   jax.ShapeDtypeStruct((nh, HEAD, HEAD), F32), jax.ShapeDtypeStruct((8, w), F32)),
        grid=(nt,),
        in_specs=[pl.BlockSpec((tt, w), lambda i: (rev(i), x_blk)), halo(x_blk), tile, halo(0), tile,
                  pl.BlockSpec((8, w), lambda i: (0, 0)), vec, gate, vec, gate, vec, vec],
        out_specs=(tile, gate, gate, pl.BlockSpec((8, w), lambda i: (0, 0))),
        scratch_shapes=[pltpu.VMEM((LRU_HALO + tt, w), F32), pltpu.VMEM((LRU_HALO + tt, w), F32),
                        pltpu.VMEM((tt, w), F32), pltpu.VMEM((tt, w), F32), pltpu.VMEM((tt + LRU_HALO, w), F32),
                        pltpu.VMEM((8, w), F32), pltpu.VMEM((8, w), F32), pltpu.VMEM((8, w), F32)],
        compiler_params=_params("arbitrary"),
    )(u, u, hseq, hseq, dy, cw, cb, wa, ba, wx, bx, lam)


def _gate_specs(c, tr):
    return [pl.BlockSpec((tr, c), lambda i, b=b: (i, b)) for b in (2, 9, 10, 12)]


def _outgate_fwd(y_conv, y_attn, y_lru, u, n_conv, n_attn, n_lru, name):
    s, c = y_conv.shape
    tr = _tile(s, (256, 128, 64, 32, 16))

    def body(yc_ref, ya_ref, yl_ref, gc_ref, ga0_ref, ga1_ref, gl_ref, nc_ref, na_ref, nl_ref, o_ref):
        def rinv(v):
            return lax.rsqrt(jnp.mean(v * v, axis=-1, keepdims=True) + RMS_EPS)

        def silu(g):
            return g * _sigmoid(g)

        yc = yc_ref[...]
        o_ref[:, 0:c] = (yc * rinv(yc) * nc_ref[...] * silu(gc_ref[...])).astype(BF16)
        ya = ya_ref[...]
        ra = rinv(ya)
        o_ref[:, c:2 * c] = (ya[:, 0:c] * ra * na_ref[:, 0:c] * silu(ga0_ref[...])).astype(BF16)
        o_ref[:, 2 * c:3 * c] = (ya[:, c:2 * c] * ra * na_ref[:, c:2 * c] * silu(ga1_ref[...])).astype(BF16)
        yl = yl_ref[...]
        o_ref[:, 3 * c:4 * c] = (yl * rinv(yl) * nl_ref[...] * silu(gl_ref[...])).astype(BF16)

    row = lambda wd: pl.BlockSpec((tr, wd), lambda i: (i, 0))
    vec = lambda wd: pl.BlockSpec((1, wd), lambda i: (0, 0))
    return pl.pallas_call(
        body, name=name, out_shape=jax.ShapeDtypeStruct((s, 4 * c), BF16), grid=(s // tr,),
        in_specs=[row(c), row(2 * c), row(c)] + _gate_specs(c, tr) + [vec(c), vec(2 * c), vec(c)],
        out_specs=row(4 * c), compiler_params=_params("parallel"),
    )(y_conv, y_attn, y_lru, u, u, u, u, n_conv, n_attn, n_lru)


def _outgate_bwd(dy, y_conv, y_attn, y_lru, u, n_conv, n_attn, n_lru, name):
    s, c = y_conv.shape
    tr = _tile(s, (256, 128, 64, 32, 16))

    def body(dy_ref, yc_ref, ya_ref, yl_ref, gc_ref, ga0_ref, ga1_ref, gl_ref, nc_ref, na_ref, nl_ref,
             dyc_ref, dya_ref, dyl_ref, dgc_ref, dga_ref, dgl_ref, dn_ref):
        @pl.when(pl.program_id(0) == 0)
        def _():
            dn_ref[...] = jnp.zeros_like(dn_ref)

        def group(yv, gate, wv, d):
            r = lax.rsqrt(jnp.mean(yv * yv, axis=-1, keepdims=True) + RMS_EPS)
            yh = yv * r
            sg = _sigmoid(gate)
            dn = d * (gate * sg)
            dgate = d * (yh * wv) * (sg * (1.0 + gate * (1.0 - sg)))
            dw = jnp.sum(dn * yh, axis=0, keepdims=True)
            dyn = dn * wv
            dyv = r * (dyn - yh * jnp.mean(dyn * yh, axis=-1, keepdims=True))
            return dyv, dgate, dw

        dyv, dg, dw = group(yc_ref[...], gc_ref[...], nc_ref[...], dy_ref[:, 0:c])
        dyc_ref[...] = dyv
        dgc_ref[...] = dg.astype(BF16)
        dn_ref[0:1, 0:c] += dw
        gate_a = jnp.concatenate([ga0_ref[...], ga1_ref[...]], axis=1)
        dyv, dg, dw = group(ya_ref[...], gate_a, na_ref[...], dy_ref[:, c:3 * c])
        dya_ref[...] = dyv
        dga_ref[...] = dg.astype(BF16)
        dn_ref[0:1, c:3 * c] += dw
        dyv, dg, dw = group(yl_ref[...], gl_ref[...], nl_ref[...], dy_ref[:, 3 * c:4 * c])
        dyl_ref[...] = dyv
        dgl_ref[...] = dg.astype(BF16)
        dn_ref[0:1, 3 * c:4 * c] += dw

    row = lambda wd: pl.BlockSpec((tr, wd), lambda i: (i, 0))
    vec = lambda wd: pl.BlockSpec((1, wd), lambda i: (0, 0))
    sh = lambda wd, dt: jax.ShapeDtypeStruct((s, wd), dt)
    return pl.pallas_call(
        body, name=name,
        out_shape=(sh(c, F32), sh(2 * c, F32), sh(c, F32), sh(c, BF16), sh(2 * c, BF16), sh(c, BF16),
                   jax.ShapeDtypeStruct((8, 4 * c), F32)),
        grid=(s // tr,),
        in_specs=[row(4 * c), row(c), row(2 * c), row(c)] + _gate_specs(c, tr) + [vec(c), vec(2 * c), vec(c)],
        out_specs=(row(c), row(2 * c), row(c), row(c), row(2 * c), row(c),
                   pl.BlockSpec((8, 4 * c), lambda i: (0, 0))),
        compiler_params=_params("arbitrary"),
    )(dy, y_conv, y_attn, y_lru, u, u, u, u, n_conv, n_attn, n_lru)


def _xattn_probs(qh, kh, scale):
    sc = _dot_nt(qh, kh) * scale
    p = jnp.exp(sc - jnp.max(sc, axis=-1, keepdims=True))
    return p / jnp.sum(p, axis=-1, keepdims=True)


def _xattn_fwd(q, kv, name):
    s, xw = q.shape
    m = kv.shape[0]
    nh = xw // HEAD
    tq = _tile(s, (256, 128, 64, 32, 16))
    scale = HEAD ** -0.5

    def body(q_ref, kv_ref, o_ref):
        for h in range(nh):
            qh = q_ref[:, h * HEAD:(h + 1) * HEAD].astype(BF16)
            kh = kv_ref[:, h * HEAD:(h + 1) * HEAD].astype(BF16)
            vh = kv_ref[:, xw + h * HEAD:xw + (h + 1) * HEAD].astype(BF16)
            p = _xattn_probs(qh, kh, scale)
            o_ref[:, h * HEAD:(h + 1) * HEAD] = _dot(p.astype(BF16), vh).astype(BF16)

    return pl.pallas_call(
        body, name=name, out_shape=jax.ShapeDtypeStruct((s, xw), BF16), grid=(s // tq,),
        in_specs=[pl.BlockSpec((tq, xw), lambda i: (i, 0)), pl.BlockSpec((m, 2 * xw), lambda i: (0, 0))],
        out_specs=pl.BlockSpec((tq, xw), lambda i: (i, 0)), compiler_params=_params("parallel"),
    )(q, kv)


def _xattn_bwd(q, kv, do, name):
    s, xw = q.shape
    m = kv.shape[0]
    nh = xw // HEAD
    tq = _tile(s, (256, 128, 64, 32, 16))
    scale = HEAD ** -0.5

    def body(q_ref, kv_ref, do_ref, dq_ref, dkv_ref):
        @pl.when(pl.program_id(0) == 0)
        def _():
            dkv_ref[...] = jnp.zeros_like(dkv_ref)

        for h in range(nh):
            ks = slice(h * HEAD, (h + 1) * HEAD)
            vs = slice(xw + h * HEAD, xw + (h + 1) * HEAD)
            qh = q_ref[:, ks].astype(BF16)
            kh = kv_ref[:, ks].astype(BF16)
            vh = kv_ref[:, vs].astype(BF16)
            doh = do_ref[:, ks].astype(BF16)
            p = _xattn_probs(qh, kh, scale)
            dkv_ref[:, vs] += _dot_tn(p.astype(BF16), doh)
            dp = _dot_nt(doh, vh)
            ds = (p * (dp - jnp.sum(dp * p, axis=-1, keepdims=True)) * scale).astype(BF16)
            dq_ref[:, ks] = _dot(ds, kh).astype(BF16)
            dkv_ref[:, ks] += _dot_tn(ds, qh)

    row = pl.BlockSpec((tq, xw), lambda i: (i, 0))
    full = pl.BlockSpec((m, 2 * xw), lambda i: (0, 0))
    return pl.pallas_call(
        body, name=name,
        out_shape=(jax.ShapeDtypeStruct((s, xw), BF16), jax.ShapeDtypeStruct((m, 2 * xw), F32)), grid=(s // tq,),
        in_specs=[row, full, row], out_specs=(row, full), compiler_params=_params("arbitrary"),
    )(q, kv, do)


def _adamw(w, g, m, v, name):
    rows, cols = w.shape
    tr = _tile(rows, (512, 256, 128, 64, 32, 16, 8)) if rows % 8 == 0 else rows
    bc1 = 1.0 - ADAM_B1 ** ADAM_STEP
    bc2 = 1.0 - ADAM_B2 ** ADAM_STEP

    def body(w_ref, g_ref, m_ref, v_ref, d_ref, nm_ref, nv_ref):
        gv = g_ref[...]
        nm = ADAM_B1 * m_ref[...] + (1.0 - ADAM_B1) * gv
        nv = ADAM_B2 * v_ref[...] + (1.0 - ADAM_B2) * (gv * gv)
        nm_ref[...] = nm
        nv_ref[...] = nv
        d_ref[...] = -ADAM_LR * ((nm / bc1) / (jnp.sqrt(nv / bc2) + ADAM_EPS) + ADAM_WD * w_ref[...])

    spec = pl.BlockSpec((tr, cols), lambda i: (i, 0))
    sh = jax.ShapeDtypeStruct((rows, cols), F32)
    return pl.pallas_call(
        body, name=name, out_shape=(sh, sh, sh), grid=(rows // tr,), in_specs=[spec] * 4,
        out_specs=(spec, spec, spec), compiler_params=_params("parallel"),
    )(w, g, m, v)


WEIGHTS = ['mix_norm_g', 'w_in', 'conv_dw_w', 'conv_dw_b', 'conv_ln_g', 'conv_ln_b', 'conv_pw_w', 'lru_conv_w',
           'lru_conv_b', 'lru_wa', 'lru_ba', 'lru_wx', 'lru_bx', 'lru_lambda', 'out_norm_conv', 'out_norm_attn',
           'out_norm_lru', 'w_out', 'xattn_norm_g', 'mem_norm_g', 'xattn_wq', 'xattn_wkv', 'xattn_wo',
           'final_norm_g']
BIG_SHARDED = {'w_in': 2, 'conv_pw_w': 1, 'w_out': 1, 'xattn_wq': 1, 'xattn_wkv': 1, 'xattn_wo': 2}
SMALL_SHARDED = {'conv_dw_w': 2, 'lru_conv_w': 2}


def _layer_fwd(x, mem, p, l):
    row = lambda name: p[name][l][None, :]
    c = p['conv_dw_b'].shape[1]
    heads = 2 * c // HEAD
    h = _rms_fwd(x, row('mix_norm_g'), "rms_mix")
    u = _mm(h, p['w_in'], bl=l, name="in_proj")
    wpad = jnp.pad(p['conv_dw_w'][l], ((0, 1), (0, 0)))
    cw = jnp.pad(p['lru_conv_w'][l], ((0, 4), (0, 0)))
    y_conv = _conv_fwd(u, wpad, row('conv_dw_b'), row('conv_ln_g'), row('conv_ln_b'), p['conv_pw_w'], l, "conv_fwd")
    q_blk = 3 * c // HEAD
    y_attn, tot = _sb_fwd(u, heads, q_blk, q_blk + heads, q_blk + 2 * heads, "sb_fwd")
    wa, wx = p['lru_wa'][l].astype(BF16), p['lru_wx'][l].astype(BF16)
    y_lru = _lru_fwd(u, 11, cw, row('lru_conv_b'), wa, row('lru_ba'), wx, row('lru_bx'), row('lru_lambda'), "lru_fwd")
    yc = _outgate_fwd(y_conv, y_attn, y_lru, u, row('out_norm_conv'), row('out_norm_attn'), row('out_norm_lru'),
                      "outgate_fwd")
    x1 = _mm(yc, p['w_out'], bl=l, add=x, name="out_proj")
    h2 = _rms_fwd(x1, row('xattn_norm_g'), "rms_xattn")
    memn = _rms_fwd(mem, row('mem_norm_g'), "rms_mem")
    q2 = _mm(h2, p['xattn_wq'], bl=l, name="xq_proj")
    kv = _mm(memn, p['xattn_wkv'], bl=l, name="xkv_proj")
    o2 = _xattn_fwd(q2, kv, "xattn_fwd")
    x2 = _mm(o2, p['xattn_wo'], bl=l, add=x1, name="xo_proj")
    saved = dict(x=x, h=h, u=u, wpad=wpad, cw=cw, wa=wa, wx=wx, y_conv=y_conv, y_attn=y_attn, tot=tot, y_lru=y_lru,
                 yc=yc, x1=x1, h2=h2, memn=memn, q2=q2, kv=kv, o2=o2)
    return x2, saved


def _layer_bwd(dx2, mem, p, l, sv):
    row = lambda name: p[name][l][None, :]
    c = p['conv_dw_b'].shape[1]
    heads = 2 * c // HEAD
    g = {}
    g['xattn_wo'] = _mm(sv['o2'], dx2, ta=True, out_dtype=BF16, name="d_wo")
    do2 = _mm(dx2, p['xattn_wo'], bl=l, tb=True, name="d_o2")
    dq2, dkv = _xattn_bwd(sv['q2'], sv['kv'], do2, "xattn_bwd")
    g['xattn_wq'] = _mm(sv['h2'], dq2, ta=True, out_dtype=BF16, name="d_wq")
    dh2 = _mm(dq2, p['xattn_wq'], bl=l, tb=True, name="d_h2")
    g['xattn_wkv'] = _mm(sv['memn'], dkv, ta=True, out_dtype=BF16, name="d_wkv")
    dmemn = _mm(dkv, p['xattn_wkv'], bl=l, tb=True, name="d_memn")
    _, g['mem_norm_g'] = _rms_bwd(mem, row('mem_norm_g'), dmemn, None, "rms_mem_bwd")
    dx1, g['xattn_norm_g'] = _rms_bwd(sv['x1'], row('xattn_norm_g'), dh2, dx2, "rms_xattn_bwd")
    g['w_out'] = _mm(sv['yc'], dx1, ta=True, out_dtype=BF16, name="d_wout")
    dyc = _mm(dx1, p['w_out'], bl=l, tb=True, name="d_yc")
    u = sv['u']
    d_yconv, d_yattn, d_ylru, dgc, dga, dgl, dn = _outgate_bwd(
        dyc, sv['y_conv'], sv['y_attn'], sv['y_lru'], u, row('out_norm_conv'), row('out_norm_attn'),
        row('out_norm_lru'), "outgate_bwd")
    g['out_norm_conv'], g['out_norm_attn'], g['out_norm_lru'] = dn[0, 0:c], dn[0, c:3 * c], dn[0, 3 * c:4 * c]
    dd, gl, dpw, cvec = _conv_bwd_post(u, d_yconv, sv['wpad'], row('conv_dw_b'), row('conv_ln_g'),
                                       row('conv_ln_b'), p['conv_pw_w'], l, "conv_bwd_post")
    g['conv_pw_w'] = dpw.astype(BF16)
    g['conv_ln_g'], g['conv_ln_b'], g['conv_dw_b'] = cvec[0], cvec[1], cvec[2]
    dval, dglu, ddw = _conv_bwd_dw(u, dd, gl, sv['wpad'], "conv_bwd_dw")
    g['conv_dw_w'] = ddw[0:31]
    q_blk = 3 * c // HEAD
    dq, dk, dv = _sb_bwd(u, sv['tot'], d_yattn, heads, q_blk, q_blk + heads, q_blk + 2 * heads, "sb_bwd")
    dxr, g['lru_wa'], g['lru_wx'], lvec = _lru_bwd(
        u, 11, sv['y_lru'], d_ylru, sv['cw'], row('lru_conv_b'), sv['wa'], row('lru_ba'), sv['wx'], row('lru_bx'),
        row('lru_lambda'), "lru_bwd")
    g['lru_ba'], g['lru_bx'], g['lru_lambda'], g['lru_conv_b'] = lvec[0], lvec[1], lvec[2], lvec[3]
    g['lru_conv_w'] = lvec[4:8]
    du = jnp.concatenate([dval, dglu, dgc, dq.astype(BF16), dk.astype(BF16), dv.astype(BF16), dga, dxr, dgl], axis=1)
    g['w_in'] = _mm(sv['h'], du, ta=True, out_dtype=BF16, name="d_win")
    dh = _mm(du, p['w_in'], bl=l, tb=True, name="d_h")
    dx0, g['mix_norm_g'] = _rms_bwd(sv['x'], row('mix_norm_g'), dh, dx1, "rms_mix_bwd")
    return dx0, g


def kernel(x, mem, mix_norm_g, w_in, conv_dw_w, conv_dw_b, conv_ln_g, conv_ln_b, conv_pw_w, lru_conv_w, lru_conv_b, lru_wa, lru_ba, lru_wx, lru_bx, lru_lambda, out_norm_conv, out_norm_attn, out_norm_lru, w_out, xattn_norm_g, mem_norm_g, xattn_wq, xattn_wkv, xattn_wo, final_norm_g, loss_target, m_mix_norm_g, m_w_in, m_conv_dw_w, m_conv_dw_b, m_conv_ln_g, m_conv_ln_b, m_conv_pw_w, m_lru_conv_w, m_lru_conv_b, m_lru_wa, m_lru_ba, m_lru_wx, m_lru_bx, m_lru_lambda, m_out_norm_conv, m_out_norm_attn, m_out_norm_lru, m_w_out, m_xattn_norm_g, m_mem_norm_g, m_xattn_wq, m_xattn_wkv, m_xattn_wo, m_final_norm_g, v_mix_norm_g, v_w_in, v_conv_dw_w, v_conv_dw_b, v_conv_ln_g, v_conv_ln_b, v_conv_pw_w, v_lru_conv_w, v_lru_conv_b, v_lru_wa, v_lru_ba, v_lru_wx, v_lru_bx, v_lru_lambda, v_out_norm_conv, v_out_norm_attn, v_out_norm_lru, v_w_out, v_xattn_norm_g, v_mem_norm_g, v_xattn_wq, v_xattn_wkv, v_xattn_wo, v_final_norm_g):
    args = locals()
    w = {n: args[n] for n in WEIGHTS}
    mom = {n: args["m_" + n] for n in WEIGHTS}
    var = {n: args["v_" + n] for n in WEIGHTS}
    depth = w_in.shape[0]
    c = conv_dw_b.shape[1]
    assert out_norm_attn.shape[1] == 2 * c and lru_lambda.shape[1] == c and w_in.shape[2] * N_DEV == 13 * c
    assert c % HEAD == 0 and x.shape[0] == 1 and mem.shape[0] == 1
    xs, mems, tgt = x[0], mem[0], loss_target[0]
    me = 4 * lax.axis_index("x") + 2 * lax.axis_index("y") + lax.axis_index("c")

    assert depth == 2 and (2 * w_in.shape[2]) % LANE == 0 and w_in.shape[2] % LANE in (0, LANE // 2)
    my_c = lax.axis_index("c")
    big = list(BIG_SHARDED)
    dev = [n for n in big if n != 'w_in']
    pair = _pair_gather(w_in.astype(BF16), "pair_w_in")
    chip_w_in = jnp.concatenate([pair[0], pair[1]], axis=2)
    gathered = _gather_weights(chip_w_in, 2, [w[n].astype(BF16) for n in dev], [BIG_SHARDED[n] for n in dev],
                               "gather_weights")
    full = dict(zip(['w_in'] + dev, gathered))
    small = list(SMALL_SHARDED)
    gathered = _all_gather(_pack([w[n] for n in small], F32), "gather_conv_taps")
    full.update({n: _join_blocks(blk, SMALL_SHARDED[n])
                 for n, blk in zip(small, _unpack(gathered, [w[n].shape for n in small], lead=N_DEV))})
    p = {**w, **full}

    saved = []
    act = xs
    for l in range(depth):
        act, sv = _layer_fwd(act, mems, p, l)
        saved.append(sv)
    loss_part, dact, d_final = _loss_bwd(act, final_norm_g[None, :], tgt, "loss_bwd")

    layer_grads = [None] * depth
    for l in reversed(range(depth)):
        dact, layer_grads[l] = _layer_bwd(dact, mems, p, l, saved[l])
    grad_x = dact[None]
    rest = [n for n in WEIGHTS if n not in BIG_SHARDED]
    partial = {n: jnp.stack([layer_grads[l][n] for l in range(depth)]) for n in rest if n != 'final_norm_g'}
    partial['final_norm_g'] = d_final[0]

    arrs, axes, widths, starts = [], [], [], []
    for n in big:
        axis = BIG_SHARDED[n] - 1
        blk = w[n].shape[axis + 1]
        pad = blk % LANE if axis == 1 else 0
        for l in range(depth):
            arrs.append(layer_grads[l][n])
            axes.append(axis)
            widths.append(blk + pad)
            starts.append(lambda px, py, pc, blk=blk, pad=pad: blk * (4 * px + 2 * py + pc) - pad * pc)
    received = _scatter_grads(arrs, axes, widths, starts, depth, "scatter_grads")
    grads = {}
    for n, got in zip(big, received):
        summed = _sum_blocks(got.reshape(N_DEV, -1, got.shape[-1]), "sum_" + n)
        width = w[n].shape[-1]
        if summed.shape[1] != width:
            summed = jnp.where(my_c == 0, summed[:, :width], summed[:, summed.shape[1] - width:])
        grads[n] = summed.reshape(w[n].shape)

    vec = _pack([partial[n] for n in rest] + [loss_part], F32)
    total = _sum_blocks(_all_gather(vec, "gather_small_grads"), "sum_small_grads")
    pieces = _unpack(total, [partial[n].shape for n in rest] + [(1, 1)])
    loss = pieces[-1][0, 0]
    for n, piece in zip(rest, pieces[:-1]):
        if n in SMALL_SHARDED:
            width = w[n].shape[2]
            piece = lax.dynamic_slice_in_dim(piece, me * width, width, axis=2)
        grads[n] = piece

    delta, new_m, new_v = {}, {}, {}
    for n in big:
        shape = w[n].shape
        two_d = lambda a: a.reshape(-1, shape[-1])
        d, nm, nv = _adamw(two_d(w[n]), two_d(grads[n]), two_d(mom[n]), two_d(var[n]), "adamw_" + n)
        delta[n], new_m[n], new_v[n] = d.reshape(shape), nm.reshape(shape), nv.reshape(shape)
    shapes = [w[n].shape for n in rest]
    packed = [_pack([src[n] for n in rest], F32) for src in (w, grads, mom, var)]
    outs = _adamw(*packed, "adamw_small")
    for dst, o in zip((delta, new_m, new_v), outs):
        dst.update(dict(zip(rest, _unpack(o, shapes))))

    return (loss, grad_x, *[grads[n] for n in WEIGHTS], *[delta[n] for n in WEIGHTS],
            *[new_m[n] for n in WEIGHTS], *[new_v[n] for n in WEIGHTS])
```

```python
import functools
import math

import jax
import jax.numpy as jnp
from jax import lax
from jax.experimental import pallas as pl
from jax.experimental.pallas import tpu as pltpu

F32 = jnp.float32
BF16 = jnp.bfloat16

N_DEV = 8
LANE = 128
HEAD = 128
VMEM_LIMIT = 56 * 1024 * 1024
PACK_COLS = 512
RMS_EPS = 1e-6
LN_EPS = 1e-5
LRU_C = 8.0
CONV_HALO = 32
LRU_HALO = 8

ADAM_LR, ADAM_B1, ADAM_B2, ADAM_EPS, ADAM_WD, ADAM_STEP = 0.001, 0.9, 0.999, 1e-08, 0.01, 10

MESH = pl.DeviceIdType.MESH


def _tile(n, cands):
    for c in cands:
        if n % c == 0:
            return c
    raise ValueError(f"no tile of {cands} divides {n}")


def _params(*sem):
    return pltpu.CompilerParams(dimension_semantics=sem, vmem_limit_bytes=VMEM_LIMIT)


def _dot(a, b):
    return lax.dot_general(a, b, (((1,), (0,)), ((), ())), preferred_element_type=F32)


def _dot_nt(a, b):
    return lax.dot_general(a, b, (((1,), (1,)), ((), ())), preferred_element_type=F32)


def _dot_tn(a, b):
    return lax.dot_general(a, b, (((0,), (0,)), ((), ())), preferred_element_type=F32)


def _sigmoid(x):
    return 1.0 / (1.0 + jnp.exp(-x))


def _expm1(x):
    series = x * (1.0 + x * (0.5 + x * (1.0 / 6.0 + x * (1.0 / 24.0))))
    return jnp.where(jnp.abs(x) < 0.05, series, jnp.exp(x) - 1.0)


def _my_place():
    return lax.axis_index("x"), lax.axis_index("y"), lax.axis_index("c")


def _flip(v, d):
    return 1 - v if d else v


def _window(ref, axis, start, size):
    return ref.at[tuple(pl.ds(start, size) if a == axis else pl.ds(0, ref.shape[a]) for a in range(len(ref.shape)))]


def _all_gather(x2d, name):
    rows, cols = x2d.shape

    def body(x_ref, out_ref, send_sems, recv_sems, local_sem):
        x, y, c = _my_place()
        me, sibling = (x, y, c), (x, y, 1 - c)
        chips = [(1 - x, y), (x, 1 - y), (1 - x, 1 - y)]

        def blk(px, py, pc):
            return out_ref.at[4 * px + 2 * py + pc]

        def copy(k, block, to, src=None):
            return pltpu.make_async_remote_copy(
                src_ref=blk(*block) if src is None else src, dst_ref=blk(*block),
                send_sem=send_sems.at[k], recv_sem=recv_sems.at[k], device_id=to, device_id_type=MESH)

        mine = pltpu.make_async_copy(x_ref, blk(*me), local_sem)
        mine.start()
        first = [copy(0, me, sibling, src=x_ref)]
        first += [copy(1 + j, me, (*chip, c), src=x_ref) for j, chip in enumerate(chips)]
        for cp in first:
            cp.start()
        passed = [copy(4 + j, (*chip, c), sibling) for j, chip in enumerate(chips)]
        for j, chip in enumerate(chips):
            copy(1 + j, (*chip, c), me).wait_recv()
            passed[j].start()
        copy(0, sibling, me).wait_recv()
        for j, chip in enumerate(chips):
            copy(4 + j, (*chip, 1 - c), me).wait_recv()
        for cp in first + passed:
            cp.wait_send()
        mine.wait()

    return pl.pallas_call(
        body, name=name,
        out_shape=jax.ShapeDtypeStruct((N_DEV, rows, cols), x2d.dtype),
        in_specs=[pl.BlockSpec(memory_space=pl.ANY)],
        out_specs=pl.BlockSpec(memory_space=pl.ANY),
        scratch_shapes=[pltpu.SemaphoreType.DMA((7,)), pltpu.SemaphoreType.DMA((7,)), pltpu.SemaphoreType.DMA],
    )(x2d)


def _pair_gather(blk, name):
    def body(x_ref, out_ref, send_sem, recv_sem, local_sem):
        x, y, c = _my_place()
        mine = pltpu.make_async_copy(x_ref, out_ref.at[c], local_sem)
        mine.start()
        send = pltpu.make_async_remote_copy(src_ref=x_ref, dst_ref=out_ref.at[c], send_sem=send_sem,
                                            recv_sem=recv_sem, device_id=(x, y, 1 - c), device_id_type=MESH)
        send.start()
        pltpu.make_async_remote_copy(src_ref=x_ref, dst_ref=out_ref.at[1 - c], send_sem=send_sem, recv_sem=recv_sem,
                                     device_id=(x, y, 1 - c), device_id_type=MESH).wait_recv()
        send.wait_send()
        mine.wait()

    return pl.pallas_call(
        body, name=name, out_shape=jax.ShapeDtypeStruct((2,) + blk.shape, blk.dtype),
        in_specs=[pl.BlockSpec(memory_space=pl.ANY)], out_specs=pl.BlockSpec(memory_space=pl.ANY),
        scratch_shapes=[pltpu.SemaphoreType.DMA, pltpu.SemaphoreType.DMA, pltpu.SemaphoreType.DMA],
    )(blk)


def _gather_weights(chip_block, chip_axis, dev_blocks, dev_axes, name):
    n_items = 1 + len(dev_blocks)
    chip_w = chip_block.shape[chip_axis]
    dev_w = [b.shape[a] for b, a in zip(dev_blocks, dev_axes)]

    def full_shape(b, a, n):
        return b.shape[:a] + (b.shape[a] * n,) + b.shape[a + 1:]

    out_shape = [jax.ShapeDtypeStruct(full_shape(chip_block, chip_axis, 4), chip_block.dtype)]
    out_shape += [jax.ShapeDtypeStruct(full_shape(b, a, N_DEV), b.dtype) for b, a in zip(dev_blocks, dev_axes)]

    def body(*refs):
        ins, outs = refs[:n_items], refs[n_items:2 * n_items]
        send_sems, recv_sems, local_sems = refs[2 * n_items:]
        x, y, c = _my_place()
        me, sibling = (x, y, c), (x, y, 1 - c)
        chips = [(1 - x, y), (x, 1 - y), (1 - x, 1 - y)]

        def dst(t, px, py, pc):
            if t == 0:
                return _window(outs[0].at[pc], chip_axis - 1, chip_w * (2 * px + py), chip_w)
            return _window(outs[t], dev_axes[t - 1], dev_w[t - 1] * (4 * px + 2 * py + pc), dev_w[t - 1])

        def own(t):
            return ins[0].at[c] if t == 0 else ins[t]

        def copy(t, k, block, to, src=None):
            return pltpu.make_async_remote_copy(
                src_ref=dst(t, *block) if src is None else src, dst_ref=dst(t, *block),
                send_sem=send_sems.at[7 * t + k], recv_sem=recv_sems.at[7 * t + k], device_id=to, device_id_type=MESH)

        local = [pltpu.make_async_copy(ins[0], _window(outs[0], chip_axis, chip_w * (2 * x + y), chip_w),
                                       local_sems.at[0])]
        local += [pltpu.make_async_copy(ins[t], dst(t, *me), local_sems.at[t]) for t in range(1, n_items)]
        for cp in local:
            cp.start()
        first = []
        for t in range(n_items):
            if t > 0:
                first.append(copy(t, 0, me, sibling, src=own(t)))
            first += [copy(t, 1 + j, me, (*chip, c), src=own(t)) for j, chip in enumerate(chips)]
        for cp in first:
            cp.start()
        passed = []
        for j, chip in enumerate(chips):
            for t in range(n_items):
                copy(t, 1 + j, (*chip, c), me).wait_recv()
                passed.append(copy(t, 4 + j, (*chip, c), sibling))
                passed[-1].start()
        for t in range(1, n_items):
            copy(t, 0, sibling, me).wait_recv()
        for j, chip in enumerate(chips):
            for t in range(n_items):
                copy(t, 4 + j, (*chip, 1 - c), me).wait_recv()
        for cp in first + passed:
            cp.wait_send()
        for cp in local:
            cp.wait()

    hbm = pl.BlockSpec(memory_space=pl.ANY)
    return pl.pallas_call(
        body, name=name, out_shape=out_shape, in_specs=[hbm] * n_items, out_specs=[hbm] * n_items,
        scratch_shapes=[pltpu.SemaphoreType.DMA((7 * n_items,)), pltpu.SemaphoreType.DMA((7 * n_items,)),
                        pltpu.SemaphoreType.DMA((n_items,))],
    )(chip_block, *dev_blocks)


def _scatter_grads(grads, axes, widths, starts, depth, name):
    n = len(grads)

    def win_shape(g, a, w):
        return g.shape[:a] + (w,) + g.shape[a + 1:]

    out_shape = [jax.ShapeDtypeStruct((N_DEV, depth) + win_shape(g, a, w), g.dtype)
                 for g, a, w in list(zip(grads, axes, widths))[::depth]]

    def body(*refs):
        ins, outs = refs[:n], refs[n:n + n // depth]
        send_sems, recv_sems, local_sems = refs[n + n // depth:]
        x, y, c = _my_place()
        me = 4 * x + 2 * y + c

        def win(t, px, py, pc):
            start = pl.multiple_of(starts[t](px, py, pc), math.gcd(widths[t], 1024))
            return _window(ins[t], axes[t], start, widths[t])

        def slot(t, j):
            return outs[t // depth].at[j, t % depth]

        local = [pltpu.make_async_copy(win(t, x, y, c), slot(t, me), local_sems.at[t]) for t in range(n)]
        for cp in local:
            cp.start()
        sends, recvs = [], []
        for k in range(1, N_DEV):
            px, py, pc = _flip(x, k & 4), _flip(y, k & 2), _flip(c, k & 1)
            peer = 4 * px + 2 * py + pc
            for t in range(n):
                sem = 7 * t + k - 1
                sends.append(pltpu.make_async_remote_copy(
                    src_ref=win(t, px, py, pc), dst_ref=slot(t, me), send_sem=send_sems.at[sem],
                    recv_sem=recv_sems.at[sem], device_id=(px, py, pc), device_id_type=MESH))
                recvs.append(pltpu.make_async_remote_copy(
                    src_ref=win(t, px, py, pc), dst_ref=slot(t, peer), send_sem=send_sems.at[sem],
                    recv_sem=recv_sems.at[sem], device_id=(px, py, pc), device_id_type=MESH))
        for cp in sends:
            cp.start()
        for cp in recvs:
            cp.wait_recv()
        for cp in sends:
            cp.wait_send()
        for cp in local:
            cp.wait()

    hbm = pl.BlockSpec(memory_space=pl.ANY)
    return pl.pallas_call(
        body, name=name, out_shape=out_shape, in_specs=[hbm] * n, out_specs=[hbm] * (n // depth),
        scratch_shapes=[pltpu.SemaphoreType.DMA((7 * n,)), pltpu.SemaphoreType.DMA((7 * n,)),
                        pltpu.SemaphoreType.DMA((n,))],
    )(*grads)


HBM_SPEC = pl.BlockSpec(memory_space=pltpu.HBM)
SEM_SPEC = pl.BlockSpec(memory_space=pltpu.SEMAPHORE)
SPLIT_COPY = pltpu.SideEffectType.DATAFLOW_SIDE_EFFECTING


def _scatter_zone(g, axis, width, start_me, me):
    own = lax.dynamic_slice_in_dim(g, start_me, width, axis=axis)
    zone = lax.empty((N_DEV,) + own.shape, g.dtype)
    return lax.dynamic_update_slice_in_dim(zone, own[None], me, axis=0)


def _scatter_start(grads, zones, axes, widths, starts, name):
    n = len(grads)

    def body(*refs):
        ins, lands = refs[:n], refs[n:2 * n]
        send_sems, recv_sems, token = refs[2 * n:3 * n], refs[3 * n:4 * n], refs[-1]
        x, y, c = _my_place()
        me = 4 * x + 2 * y + c
        for k in range(1, N_DEV):
            px, py, pc = _flip(x, k & 4), _flip(y, k & 2), _flip(c, k & 1)
            for t in range(n):
                start = pl.multiple_of(starts[t](px, py, pc), math.gcd(widths[t], 1024))
                pltpu.make_async_remote_copy(
                    src_ref=_window(ins[t], axes[t], start, widths[t]), dst_ref=lands[t].at[me],
                    send_sem=send_sems[t], recv_sem=recv_sems[t], device_id=(px, py, pc), device_id_type=MESH).start()
        token[...] = jnp.zeros_like(token)

    thru = [pltpu.HBM(a.shape, a.dtype) for a in list(grads) + list(zones)]
    outs = pl.pallas_call(
        body, name=name,
        out_shape=[pltpu.SemaphoreType.DMA(())] * (2 * n) + thru + [jax.ShapeDtypeStruct((8, LANE), F32)],
        in_specs=[HBM_SPEC] * (2 * n),
        out_specs=[SEM_SPEC] * (2 * n) + [HBM_SPEC] * (2 * n) + [pl.BlockSpec(memory_space=pltpu.VMEM)],
        input_output_aliases={i: 2 * n + i for i in range(2 * n)},
        compiler_params=pltpu.CompilerParams(has_side_effects=SPLIT_COPY),
    )(*[pltpu.with_memory_space_constraint(a, pltpu.HBM) for a in list(grads) + list(zones)])
    return outs[:n], outs[n:2 * n], outs[2 * n:3 * n], outs[3 * n:4 * n], outs[-1]


def _scatter_wait(send_sems, recv_sems, grads, zones, after, name):
    n = len(zones)

    def body(*refs):
        lands, sends, recvs = refs[n:2 * n], refs[2 * n:3 * n], refs[3 * n:4 * n]
        x, y, c = _my_place()
        for t in range(n):
            seven = lands[t].at[pl.ds(0, N_DEV - 1)]
            done = pltpu.make_async_remote_copy(src_ref=seven, dst_ref=seven, send_sem=sends[t], recv_sem=recvs[t],
                                                device_id=(x, y, 1 - c), device_id_type=MESH)
            done.wait_send()
            done.wait_recv()

    thru = [pltpu.HBM(a.shape, a.dtype) for a in list(grads) + list(zones)]
    outs = pl.pallas_call(
        body, name=name, out_shape=thru,
        in_specs=[HBM_SPEC] * (2 * n) + [SEM_SPEC] * (2 * n) + [pl.BlockSpec(memory_space=pl.ANY)],
        out_specs=[HBM_SPEC] * (2 * n), input_output_aliases={i: i for i in range(2 * n)},
        compiler_params=pltpu.CompilerParams(has_side_effects=SPLIT_COPY),
    )(*grads, *zones, *send_sems, *recv_sems, after)
    return outs[n:]


def _sum_blocks(x3d, name):
    n, rows, cols = x3d.shape
    tr = _tile(rows, (512, 256, 128, 64, 32, 16))

    def body(x_ref, o_ref):
        acc = x_ref[0].astype(F32)
        for j in range(1, n):
            acc = acc + x_ref[j].astype(F32)
        o_ref[...] = acc

    return pl.pallas_call(
        body, name=name, out_shape=jax.ShapeDtypeStruct((rows, cols), F32), grid=(rows // tr,),
        in_specs=[pl.BlockSpec((n, tr, cols), lambda i: (0, i, 0))],
        out_specs=pl.BlockSpec((tr, cols), lambda i: (i, 0)),
        compiler_params=_params("parallel"),
    )(x3d)


def _pack(arrs, dtype, lead=None):
    if lead is None:
        flat = jnp.concatenate([a.reshape(-1).astype(dtype) for a in arrs])
        n = flat.shape[0]
        total = -(-n // (16 * PACK_COLS)) * (16 * PACK_COLS)
        return jnp.pad(flat, (0, total - n)).reshape(-1, PACK_COLS)
    flat = jnp.concatenate([a.reshape(lead, -1).astype(dtype) for a in arrs], axis=1)
    n = flat.shape[1]
    total = -(-n // (16 * PACK_COLS)) * (16 * PACK_COLS)
    return jnp.pad(flat, ((0, 0), (0, total - n))).reshape(lead, -1, PACK_COLS)


def _unpack(packed, shapes, lead=None):
    out, off = [], 0
    if lead is None:
        flat = packed.reshape(-1)
        for s in shapes:
            n = math.prod(s)
            out.append(flat[off:off + n].reshape(s))
            off += n
        return out
    flat = packed.reshape(lead, -1)
    for s in shapes:
        n = math.prod(s)
        out.append(flat[:, off:off + n].reshape((lead,) + tuple(s)))
        off += n
    return out


def _join_blocks(g, axis):
    g = jnp.moveaxis(g, 0, axis)
    s = g.shape
    return g.reshape(s[:axis] + (s[axis] * s[axis + 1],) + s[axis + 2:])


def _mm_tiles(m, n, kdim, a_bytes):
    tk = kdim if kdim <= 2048 else _tile(kdim, (2048, 1664, 1024, 832, 512, 416, 256, 128))
    tm = _tile(m, (1024, 512, 256, 128, 64, 32, 16))
    tn = _tile(n, (1024, 512, 256, 128))

    def vmem(tm, tn):
        return 2 * tm * tk * a_bytes + 2 * tn * tk * 2 + 3 * tm * tn * 4

    while vmem(tm, tn) > VMEM_LIMIT * 3 // 4 and tn > 128 and tn % 256 == 0:
        tn //= 2
    while vmem(tm, tn) > VMEM_LIMIT * 3 // 4 and tm > 128 and tm % 256 == 0:
        tm //= 2
    return tm, tn, tk


def _mm(a, b, *, ta=False, tb=False, bl=None, out_dtype=F32, add=None, name):
    if ta:
        kdim, m = a.shape
    else:
        m, kdim = a.shape
    bshape = b.shape if bl is None else b.shape[1:]
    n = bshape[0] if tb else bshape[1]
    tm, tn, tk = _mm_tiles(m, n, kdim, a.dtype.itemsize)
    nk = kdim // tk
    a_spec = pl.BlockSpec((tk, tm), lambda i, j, k: (k, i)) if ta else pl.BlockSpec((tm, tk), lambda i, j, k: (i, k))
    b_blk, b_idx = ((tn, tk), lambda i, j, k: (j, k)) if tb else ((tk, tn), lambda i, j, k: (k, j))
    if bl is None:
        b_spec = pl.BlockSpec(b_blk, b_idx)
    else:
        b_spec = pl.BlockSpec((None,) + b_blk, lambda i, j, k: (bl,) + b_idx(i, j, k))
    o_spec = pl.BlockSpec((tm, tn), lambda i, j, k: (i, j))
    dims = (((0 if ta else 1,), (1 if tb else 0,)), ((), ()))

    def body(*refs):
        a_ref, b_ref = refs[:2]
        add_ref = refs[2] if add is not None else None
        o_ref = refs[3] if add is not None else refs[2]

        def finish(r):
            if add is not None:
                r = r + add_ref[...]
            o_ref[...] = r.astype(out_dtype)

        part = lax.dot_general(a_ref[...].astype(BF16), b_ref[...].astype(BF16), dims, preferred_element_type=F32)
        if nk == 1:
            finish(part)
            return
        acc_ref = refs[-1]
        k = pl.program_id(2)

        @pl.when(k == 0)
        def _():
            acc_ref[...] = part

        @pl.when(k > 0)
        def _():
            acc_ref[...] += part

        @pl.when(k == nk - 1)
        def _():
            finish(acc_ref[...])

    ins, specs = [a, b], [a_spec, b_spec]
    if add is not None:
        ins.append(add)
        specs.append(o_spec)
    return pl.pallas_call(
        body, name=name, out_shape=jax.ShapeDtypeStruct((m, n), out_dtype), grid=(m // tm, n // tn, nk),
        in_specs=specs, out_specs=o_spec, scratch_shapes=[pltpu.VMEM((tm, tn), F32)] if nk > 1 else [],
        compiler_params=_params("parallel", "parallel", "arbitrary"),
    )(*ins)


def _rms_fwd(x, g, name):
    s, d = x.shape
    tr = _tile(s, (256, 128, 64, 32, 16))

    def body(x_ref, g_ref, o_ref):
        xv = x_ref[...]
        r = lax.rsqrt(jnp.mean(xv * xv, axis=-1, keepdims=True) + RMS_EPS)
        o_ref[...] = (xv * r * g_ref[...]).astype(BF16)

    return pl.pallas_call(
        body, name=name, out_shape=jax.ShapeDtypeStruct((s, d), BF16), grid=(s // tr,),
        in_specs=[pl.BlockSpec((tr, d), lambda i: (i, 0)), pl.BlockSpec((1, d), lambda i: (0, 0))],
        out_specs=pl.BlockSpec((tr, d), lambda i: (i, 0)), compiler_params=_params("parallel"),
    )(x, g)


def _rms_bwd(x, g, dh, resid, name):
    s, d = x.shape
    tr = _tile(s, (256, 128, 64, 32, 16))

    def body(*refs):
        if resid is None:
            x_ref, g_ref, dh_ref, dx_ref, dg_ref = refs
        else:
            x_ref, g_ref, dh_ref, res_ref, dx_ref, dg_ref = refs

        @pl.when(pl.program_id(0) == 0)
        def _():
            dg_ref[...] = jnp.zeros_like(dg_ref)

        xv = x_ref[...]
        r = lax.rsqrt(jnp.mean(xv * xv, axis=-1, keepdims=True) + RMS_EPS)
        xh = xv * r
        dhv = dh_ref[...]
        dg_ref[0:1, :] += jnp.sum(dhv * xh, axis=0, keepdims=True)
        dyn = dhv * g_ref[...]
        dx = r * (dyn - xh * jnp.mean(dyn * xh, axis=-1, keepdims=True))
        if resid is not None:
            dx = dx + res_ref[...]
        dx_ref[...] = dx

    row = pl.BlockSpec((tr, d), lambda i: (i, 0))
    ins = [x, g, dh] + ([] if resid is None else [resid])
    specs = [row, pl.BlockSpec((1, d), lambda i: (0, 0)), row] + ([] if resid is None else [row])
    dx, dg = pl.pallas_call(
        body, name=name,
        out_shape=(jax.ShapeDtypeStruct((s, d), F32), jax.ShapeDtypeStruct((8, d), F32)), grid=(s // tr,),
        in_specs=specs, out_specs=(row, pl.BlockSpec((8, d), lambda i: (0, 0))),
        compiler_params=_params("arbitrary"),
    )(*ins)
    return dx, dg[0]


def _loss_bwd(x, g, tgt, name):
    s, d = x.shape
    tr = _tile(s, (256, 128, 64, 32, 16))

    def body(x_ref, g_ref, t_ref, dx_ref, dg_ref, loss_ref):
        @pl.when(pl.program_id(0) == 0)
        def _():
            dg_ref[...] = jnp.zeros_like(dg_ref)
            loss_ref[...] = jnp.zeros_like(loss_ref)

        xv = x_ref[...]
        r = lax.rsqrt(jnp.mean(xv * xv, axis=-1, keepdims=True) + RMS_EPS)
        xh = xv * r
        e = xh * g_ref[...] - t_ref[...]
        per_tok = jnp.mean(e * e, axis=-1, keepdims=True)
        loss_ref[...] += 0.5 * jnp.sum(per_tok, axis=0, keepdims=True)
        dy = e * (1.0 / d)
        dg_ref[0:1, :] += jnp.sum(dy * xh, axis=0, keepdims=True)
        dyn = dy * g_ref[...]
        dx_ref[...] = r * (dyn - xh * jnp.mean(dyn * xh, axis=-1, keepdims=True))

    row = pl.BlockSpec((tr, d), lambda i: (i, 0))
    dx, dg, loss = pl.pallas_call(
        body, name=name,
        out_shape=(jax.ShapeDtypeStruct((s, d), F32), jax.ShapeDtypeStruct((8, d), F32),
                   jax.ShapeDtypeStruct((8, LANE), F32)),
        grid=(s // tr,),
        in_specs=[row, pl.BlockSpec((1, d), lambda i: (0, 0)), row],
        out_specs=(row, pl.BlockSpec((8, d), lambda i: (0, 0)), pl.BlockSpec((8, LANE), lambda i: (0, 0))),
        compiler_params=_params("arbitrary"),
    )(x, g, tgt)
    return loss[0:1, 0:1], dx, dg[0:1]


def _conv_taps(gbuf, w_ref, tt, ntap, lo):
    acc = w_ref[0:1, :] * gbuf[pl.ds(lo, tt), :]
    for k in range(1, ntap):
        acc = acc + w_ref[k:k + 1, :] * gbuf[pl.ds(lo + k, tt), :]
    return acc


def _conv_time_tile(s):
    return _tile(s, (256, 128, 64, 32))


def _conv_fwd(u, wpad, dw_b, ln_g, ln_b, pw, l, name):
    s = u.shape[0]
    c = pw.shape[1]
    ntap = 31
    tt = _conv_time_tile(s)
    hb = tt // CONV_HALO

    def body(val_ref, glu_ref, valh_ref, gluh_ref, w_ref, b_ref, lg_ref, lb_ref, pw_ref, o_ref, gbuf):
        i = pl.program_id(0)
        glh = valh_ref[...] * _sigmoid(gluh_ref[...])
        gbuf[0:CONV_HALO, :] = jnp.where(i > 0, glh, 0.0)
        gbuf[CONV_HALO:CONV_HALO + tt, :] = val_ref[...] * _sigmoid(glu_ref[...])
        acc = _conv_taps(gbuf, w_ref, tt, ntap, CONV_HALO - (ntap - 1)) + b_ref[...]
        xc = acc - jnp.mean(acc, axis=-1, keepdims=True)
        rstd = lax.rsqrt(jnp.mean(xc * xc, axis=-1, keepdims=True) + LN_EPS)
        ln = xc * rstd * lg_ref[...] + lb_ref[...]
        sw = ln * _sigmoid(ln)
        o_ref[...] = _dot(sw.astype(BF16), pw_ref[...])

    vec = pl.BlockSpec((1, c), lambda i: (0, 0))
    return pl.pallas_call(
        body, name=name, out_shape=jax.ShapeDtypeStruct((s, c), F32), grid=(s // tt,),
        in_specs=[pl.BlockSpec((tt, c), lambda i: (i, 0)), pl.BlockSpec((tt, c), lambda i: (i, 1)),
                  pl.BlockSpec((CONV_HALO, c), lambda i: (jnp.maximum(i * hb - 1, 0), 0)),
                  pl.BlockSpec((CONV_HALO, c), lambda i: (jnp.maximum(i * hb - 1, 0), 1)),
                  pl.BlockSpec((32, c), lambda i: (0, 0)), vec, vec, vec,
                  pl.BlockSpec((None, c, c), lambda i: (l, 0, 0))],
        out_specs=pl.BlockSpec((tt, c), lambda i: (i, 0)),
        scratch_shapes=[pltpu.VMEM((CONV_HALO + tt, c), F32)],
        compiler_params=_params("parallel"),
    )(u, u, u, u, wpad, dw_b, ln_g, ln_b, pw)


def _conv_bwd_post(u, dyc, wpad, dw_b, ln_g, ln_b, pw, l, name):
    s = u.shape[0]
    c = pw.shape[1]
    ntap = 31
    tt = _conv_time_tile(s)
    hb = tt // CONV_HALO

    def body(val_ref, glu_ref, valh_ref, gluh_ref, dy_ref, w_ref, b_ref, lg_ref, lb_ref, pw_ref,
             dd_ref, gl_ref, dpw_ref, vec_ref, gbuf):
        i = pl.program_id(0)

        @pl.when(i == 0)
        def _():
            dpw_ref[...] = jnp.zeros_like(dpw_ref)
            vec_ref[...] = jnp.zeros_like(vec_ref)

        glh = valh_ref[...] * _sigmoid(gluh_ref[...])
        gbuf[0:CONV_HALO, :] = jnp.where(i > 0, glh, 0.0)
        gl = val_ref[...] * _sigmoid(glu_ref[...])
        gbuf[CONV_HALO:CONV_HALO + tt, :] = gl
        gl_ref[...] = gl
        acc = _conv_taps(gbuf, w_ref, tt, ntap, CONV_HALO - (ntap - 1)) + b_ref[...]
        xc = acc - jnp.mean(acc, axis=-1, keepdims=True)
        rstd = lax.rsqrt(jnp.mean(xc * xc, axis=-1, keepdims=True) + LN_EPS)
        xh = xc * rstd
        ln = xh * lg_ref[...] + lb_ref[...]
        sig = _sigmoid(ln)
        sw = ln * sig
        dyb = dy_ref[...].astype(BF16)
        dpw_ref[...] += _dot_tn(sw.astype(BF16), dyb)
        dsw = _dot_nt(dyb, pw_ref[...])
        dln = dsw * (sig * (1.0 + ln * (1.0 - sig)))
        vec_ref[0:1, :] += jnp.sum(dln * xh, axis=0, keepdims=True)
        vec_ref[1:2, :] += jnp.sum(dln, axis=0, keepdims=True)
        dxh = dln * lg_ref[...]
        dd = rstd * (dxh - jnp.mean(dxh, axis=-1, keepdims=True)
                     - xh * jnp.mean(dxh * xh, axis=-1, keepdims=True))
        vec_ref[2:3, :] += jnp.sum(dd, axis=0, keepdims=True)
        dd_ref[...] = dd

    vec = pl.BlockSpec((1, c), lambda i: (0, 0))
    tile = pl.BlockSpec((tt, c), lambda i: (i, 0))
    return pl.pallas_call(
        body, name=name,
        out_shape=(jax.ShapeDtypeStruct((s, c), F32), jax.ShapeDtypeStruct((s, c), F32),
                   jax.ShapeDtypeStruct((c, c), F32), jax.ShapeDtypeStruct((8, c), F32)),
        grid=(s // tt,),
        in_specs=[tile, pl.BlockSpec((tt, c), lambda i: (i, 1)),
                  pl.BlockSpec((CONV_HALO, c), lambda i: (jnp.maximum(i * hb - 1, 0), 0)),
                  pl.BlockSpec((CONV_HALO, c), lambda i: (jnp.maximum(i * hb - 1, 0), 1)),
                  tile, pl.BlockSpec((32, c), lambda i: (0, 0)), vec, vec, vec,
                  pl.BlockSpec((None, c, c), lambda i: (l, 0, 0))],
        out_specs=(tile, tile, pl.BlockSpec((c, c), lambda i: (0, 0)), pl.BlockSpec((8, c), lambda i: (0, 0))),
        scratch_shapes=[pltpu.VMEM((CONV_HALO + tt, c), F32)],
        compiler_params=_params("arbitrary"),
    )(u, u, u, u, dyc, wpad, dw_b, ln_g, ln_b, pw)


def _conv_bwd_dw(u, dd, gl, wpad, name):
    s, c = dd.shape
    ntap = 31
    tt = _conv_time_tile(s)
    hb = tt // CONV_HALO
    nt = s // tt
    last_halo = s // CONV_HALO - 1

    def body(val_ref, glu_ref, dd_ref, ddn_ref, gl_ref, glh_ref, w_ref, dval_ref, dglu_ref, dw_ref, dbuf, gbuf):
        i = pl.program_id(0)

        @pl.when(i == 0)
        def _():
            dw_ref[...] = jnp.zeros_like(dw_ref)

        d = dd_ref[...]
        dbuf[0:tt, :] = d
        dbuf[tt:tt + CONV_HALO, :] = jnp.where(i < nt - 1, ddn_ref[...], 0.0)
        gbuf[0:CONV_HALO, :] = jnp.where(i > 0, glh_ref[...], 0.0)
        gbuf[CONV_HALO:CONV_HALO + tt, :] = gl_ref[...]
        dgl = w_ref[0:1, :] * dbuf[pl.ds(ntap - 1, tt), :]
        for k in range(1, ntap):
            dgl = dgl + w_ref[k:k + 1, :] * dbuf[pl.ds(ntap - 1 - k, tt), :]
        for k in range(ntap):
            dw_ref[k:k + 1, :] += jnp.sum(d * gbuf[pl.ds(CONV_HALO - (ntap - 1) + k, tt), :], axis=0, keepdims=True)
        sg = _sigmoid(glu_ref[...])
        dval_ref[...] = (dgl * sg).astype(BF16)
        dglu_ref[...] = (dgl * val_ref[...] * sg * (1.0 - sg)).astype(BF16)

    tile = pl.BlockSpec((tt, c), lambda i: (i, 0))
    return pl.pallas_call(
        body, name=name,
        out_shape=(jax.ShapeDtypeStruct((s, c), BF16), jax.ShapeDtypeStruct((s, c), BF16),
                   jax.ShapeDtypeStruct((32, c), F32)),
        grid=(nt,),
        in_specs=[tile, pl.BlockSpec((tt, c), lambda i: (i, 1)), tile,
                  pl.BlockSpec((CONV_HALO, c), lambda i: (jnp.minimum((i + 1) * hb, last_halo), 0)),
                  tile, pl.BlockSpec((CONV_HALO, c), lambda i: (jnp.maximum(i * hb - 1, 0), 0)),
                  pl.BlockSpec((32, c), lambda i: (0, 0))],
        out_specs=(tile, tile, pl.BlockSpec((32, c), lambda i: (0, 0))),
        scratch_shapes=[pltpu.VMEM((tt + CONV_HALO, c), F32), pltpu.VMEM((CONV_HALO + tt, c), F32)],
        compiler_params=_params("arbitrary"),
    )(u, u, dd, dd, gl, gl, wpad)


SB_ROWS = 32


def _tri(n, cmp):
    r = lax.broadcasted_iota(jnp.int32, (n, n), 0)
    c = lax.broadcasted_iota(jnp.int32, (n, n), 1)
    return jnp.where(cmp(r, c), 1.0, 0.0).astype(BF16)


def _row_chunks(fn, n, *arrs):
    outs = [fn(*[a[r:r + SB_ROWS] for a in arrs]) for r in range(0, n, SB_ROWS)]
    return tuple(jnp.concatenate(list(o), axis=0) for o in zip(*outs))


def _hi_lo(v):
    hi = v.astype(BF16)
    return hi, (v - hi.astype(F32)).astype(BF16)


def _sb_sticks(z, causal, scale):
    z = z * scale
    l1p = jnp.log(1.0 + jnp.exp(-jnp.abs(z)))
    lb = jnp.minimum(z, 0.0) - l1p
    ell = lb - z
    if causal is not None:
        ell = jnp.where(causal, ell, 0.0)
    hi, lo = _hi_lo(ell)
    return lb, hi, lo, jnp.sum(ell, axis=1, keepdims=True)


def _sb_fwd(u, heads, q_blk, k_blk, v_blk, name):
    s = u.shape[0]
    tq = _tile(s, (256, 128))
    scale = HEAD ** -0.5

    def body(q_ref, k_ref, v_ref, o_ref, tot_ref):
        i = pl.program_id(1)
        qb = q_ref[...].astype(BF16)
        t_sfx = _tri(tq, lambda r, c: r > c)
        below_diag = lax.broadcasted_iota(jnp.int32, (tq, tq), 1) < lax.broadcasted_iota(jnp.int32, (tq, tq), 0)

        def blocks(j0, nb, c_a, acc, diag):
            mask = [below_diag] if diag else []
            rows = pl.ds(pl.multiple_of(j0 * tq, tq), nb * tq)
            kb = k_ref[rows, :].astype(BF16)
            vb = v_ref[rows, :].astype(BF16)
            z = _dot_nt(qb, kb)

            def sticks(zc, *m):
                out = []
                for b in range(nb):
                    out += _sb_sticks(zc[:, b * tq:(b + 1) * tq], m[0] if m else None, scale)
                return tuple(out)

            st = _row_chunks(sticks, tq, z, *mask)
            lb, hi, lo, rs = st[0::4], st[1::4], st[2::4], st[3::4]
            sfx = [_dot(hi[b], t_sfx) + _dot(lo[b], t_sfx) for b in range(nb)]
            before, run = [None] * nb, c_a
            for b in reversed(range(nb)):
                before[b], run = run, run + rs[b]

            def weights(*a):
                ws = []
                for b in range(nb):
                    lbc, sfxc, befc = a[3 * b:3 * b + 3]
                    w = jnp.exp(lbc + (befc + sfxc))
                    if diag:
                        w = jnp.where(a[-1], w, 0.0)
                    ws.append(w.astype(BF16))
                return (ws[0] if nb == 1 else jnp.concatenate(ws, axis=1),)

            flat = [v for b in range(nb) for v in (lb[b], sfx[b], before[b])]
            wb, = _row_chunks(weights, tq, *flat, *mask)
            return run, acc + _dot(wb, vb)

        carry = blocks(i, 1, jnp.zeros((tq, 1), F32), jnp.zeros((tq, HEAD), F32), True)
        carry = lax.fori_loop(0, i // 4, lambda t, cr: blocks(i - 4 - 4 * t, 4, *cr, False), carry)
        carry = lax.fori_loop(0, (i % 4) // 2, lambda _, cr: blocks(i % 2, 2, *cr, False), carry)
        c_a, acc = lax.fori_loop(0, i % 2, lambda _, cr: blocks(0, 1, *cr, False), carry)
        o_ref[...] = acc
        tot_ref[...] = jnp.broadcast_to(c_a, (tq, HEAD))

    full = lambda off: pl.BlockSpec((s, HEAD), lambda h, i: (0, off + h))
    out = pl.BlockSpec((tq, HEAD), lambda h, i: (i, h))
    return pl.pallas_call(
        body, name=name,
        out_shape=(jax.ShapeDtypeStruct((s, heads * HEAD), F32), jax.ShapeDtypeStruct((s, heads * HEAD), F32)),
        grid=(heads, s // tq),
        in_specs=[pl.BlockSpec((tq, HEAD), lambda h, i: (i, q_blk + h)), full(k_blk), full(v_blk)],
        out_specs=(out, out),
        compiler_params=_params("parallel", "parallel"),
    )(u, u, u)


def _sb_bwd(u, tot, dy, heads, q_blk, k_blk, v_blk, name):
    s = u.shape[0]
    tq = _tile(s, (256, 128))
    scale = HEAD ** -0.5

    def body(q_ref, k_ref, v_ref, tot_ref, dy_ref, dq_ref, dk_ref, dv_ref):
        i = pl.program_id(1)

        @pl.when(i == 0)
        def _():
            dk_ref[...] = jnp.zeros_like(dk_ref)
            dv_ref[...] = jnp.zeros_like(dv_ref)

        qb = q_ref[...].astype(BF16)
        dob = dy_ref[...].astype(BF16)
        total = tot_ref[:, 0:1]
        t_incl = _tri(tq, lambda r, c: r <= c)
        t_excl = _tri(tq, lambda r, c: r < c)
        below_diag = lax.broadcasted_iota(jnp.int32, (tq, tq), 1) < lax.broadcasted_iota(jnp.int32, (tq, tq), 0)

        def blocks(j0, nb, c_p, c_g, dq, diag):
            mask = [below_diag] if diag else []
            rows = pl.ds(pl.multiple_of(j0 * tq, tq), nb * tq)
            kb = k_ref[rows, :].astype(BF16)
            vb = v_ref[rows, :].astype(BF16)
            z = _dot_nt(qb, kb)
            dw = _dot_nt(dob, vb)
            cols = lambda a, b: a[:, b * tq:(b + 1) * tq]

            def sticks(zc, *m):
                out = []
                for b in range(nb):
                    out += _sb_sticks(cols(zc, b), m[0] if m else None, scale)
                return tuple(out)

            st = _row_chunks(sticks, tq, z, *mask)
            lb, hi, lo, rs_l = st[0::4], st[1::4], st[2::4], st[3::4]
            pfx = [_dot(hi[b], t_incl) + _dot(lo[b], t_incl) for b in range(nb)]
            p_before = [c_p]
            for b in range(nb):
                p_before.append(p_before[-1] + rs_l[b])

            def weights(totc, dwc, *a):
                out = []
                for b in range(nb):
                    lbc, pfxc, pbc = a[3 * b:3 * b + 3]
                    w = jnp.exp(lbc + (totc - (pbc + pfxc)))
                    if diag:
                        w = jnp.where(a[-1], w, 0.0)
                    g = w * cols(dwc, b)
                    out += [w.astype(BF16), g, *_hi_lo(g), jnp.sum(g, axis=1, keepdims=True)]
                return tuple(out)

            flat = [v for b in range(nb) for v in (lb[b], pfx[b], p_before[b])]
            wt = _row_chunks(weights, tq, total, dw, *flat, *mask)
            wb, g, ghi, glo, rs_g = wt[0::5], wt[1::5], wt[2::5], wt[3::5], wt[4::5]
            g_pre = [_dot(ghi[b], t_excl) + _dot(glo[b], t_excl) for b in range(nb)]
            g_before = [c_g]
            for b in range(nb):
                g_before.append(g_before[-1] + rs_g[b])

            def dscore(*a):
                dzs = []
                for b in range(nb):
                    lbc, gc, gprec, gbc = a[4 * b:4 * b + 4]
                    beta = jnp.exp(lbc)
                    dz = (gc * (1.0 - beta) - (gbc + gprec) * beta) * scale
                    if diag:
                        dz = jnp.where(a[-1], dz, 0.0)
                    dzs.append(dz.astype(BF16))
                return (dzs[0] if nb == 1 else jnp.concatenate(dzs, axis=1),)

            flat = [v for b in range(nb) for v in (lb[b], g[b], g_pre[b], g_before[b])]
            dzb, = _row_chunks(dscore, tq, *flat, *mask)
            wcat = wb[0] if nb == 1 else jnp.concatenate(wb, axis=1)
            dk_ref[rows, :] += _dot_tn(dzb, qb)
            dv_ref[rows, :] += _dot_tn(wcat, dob)
            return p_before[-1], g_before[-1], dq + _dot(dzb, kb)

        zero = jnp.zeros((tq, 1), F32)
        carry = lax.fori_loop(0, i // 2, lambda t, cr: blocks(2 * t, 2, *cr, False),
                              (zero, zero, jnp.zeros((tq, HEAD), F32)))
        carry = lax.fori_loop(0, i % 2, lambda _, cr: blocks(i - 1, 1, *cr, False), carry)
        _, _, dq = blocks(i, 1, *carry, True)
        dq_ref[...] = dq

    full = lambda off: pl.BlockSpec((s, HEAD), lambda h, i: (0, off + h))
    blk = pl.BlockSpec((tq, HEAD), lambda h, i: (i, h))
    acc = pl.BlockSpec((s, HEAD), lambda h, i: (0, h))
    shape = jax.ShapeDtypeStruct((s, heads * HEAD), F32)
    return pl.pallas_call(
        body, name=name, out_shape=(shape, shape, shape), grid=(heads, s // tq),
        in_specs=[pl.BlockSpec((tq, HEAD), lambda h, i: (i, q_blk + h)), full(k_blk), full(v_blk), blk, blk],
        out_specs=(blk, acc, acc),
        compiler_params=_params("parallel", "arbitrary"),
    )(u, u, u, tot, dy)


def _lru_time_tile(s):
    return _tile(s, (256, 128, 64, 32))


def _lru_gates(xc, wa_ref, ba_ref, wx_ref, bx_ref, lam_ref, nh):
    pr, pi = [], []
    for n in range(nh):
        xn = xc[:, n * HEAD:(n + 1) * HEAD].astype(BF16)
        pr.append(_dot(xn, wa_ref[n]))
        pi.append(_dot(xn, wx_ref[n]))
    r = _sigmoid((pr[0] if nh == 1 else jnp.concatenate(pr, axis=1)) + ba_ref[...])
    ig = _sigmoid((pi[0] if nh == 1 else jnp.concatenate(pi, axis=1)) + bx_ref[...])
    lam = lam_ref[...]
    sp = jnp.maximum(-lam, 0.0) + jnp.log(1.0 + jnp.exp(-jnp.abs(lam)))
    log_a = -LRU_C * r * sp
    a = jnp.exp(log_a)
    mult = jnp.sqrt(-_expm1(2.0 * log_a))
    return r, ig, a, mult, sp


def _lru_fwd(u, x_blk, cw, cb, wa, ba, wx, bx, lam, name):
    s = u.shape[0]
    w = lam.shape[1]
    nh = w // HEAD
    tt = _lru_time_tile(s)
    hb = tt // LRU_HALO

    def body(x_ref, xh_ref, cw_ref, cb_ref, wa_ref, ba_ref, wx_ref, bx_ref, lam_ref, y_ref,
             xbuf, abuf, bbuf, hstate, rowbuf):
        i = pl.program_id(0)

        @pl.when(i == 0)
        def _():
            hstate[...] = jnp.zeros_like(hstate)

        xbuf[0:LRU_HALO, :] = jnp.where(i > 0, xh_ref[...], 0.0)
        xbuf[LRU_HALO:LRU_HALO + tt, :] = x_ref[...]
        xc = _conv_taps(xbuf, cw_ref, tt, 4, LRU_HALO - 3) + cb_ref[...]
        _, ig, a, mult, _ = _lru_gates(xc, wa_ref, ba_ref, wx_ref, bx_ref, lam_ref, nh)
        abuf[...] = a
        bbuf[...] = mult * (ig * xc)

        def group(gi, h):
            rows = pl.ds(pl.multiple_of(gi * 8, 8), 8)
            a8 = abuf[rows, :]
            b8 = bbuf[rows, :]
            for j in range(8):
                h = a8[j:j + 1, :] * h + b8[j:j + 1, :]
                rowbuf[j:j + 1, :] = h
            y_ref[rows, :] = rowbuf[...]
            return h

        hstate[0:1, :] = lax.fori_loop(0, tt // 8, group, hstate[0:1, :])

    vec = pl.BlockSpec((1, w), lambda i: (0, 0))
    gate = pl.BlockSpec((nh, HEAD, HEAD), lambda i: (0, 0, 0))
    return pl.pallas_call(
        body, name=name, out_shape=jax.ShapeDtypeStruct((s, w), F32), grid=(s // tt,),
        in_specs=[pl.BlockSpec((tt, w), lambda i: (i, x_blk)),
                  pl.BlockSpec((LRU_HALO, w), lambda i: (jnp.maximum(i * hb - 1, 0), x_blk)),
                  pl.BlockSpec((8, w), lambda i: (0, 0)), vec, gate, vec, gate, vec, vec],
        out_specs=pl.BlockSpec((tt, w), lambda i: (i, 0)),
        scratch_shapes=[pltpu.VMEM((LRU_HALO + tt, w), F32), pltpu.VMEM((tt, w), F32), pltpu.VMEM((tt, w), F32),
                        pltpu.VMEM((8, w), F32), pltpu.VMEM((8, w), F32)],
        compiler_params=_params("arbitrary"),
    )(u, u, cw, cb, wa, ba, wx, bx, lam)


def _lru_bwd(u, x_blk, hseq, dy, cw, cb, wa, ba, wx, bx, lam, name):
    s = u.shape[0]
    w = lam.shape[1]
    nh = w // HEAD
    tt = _lru_time_tile(s)
    hb = tt // LRU_HALO
    nt = s // tt

    def body(x_ref, xh_ref, h_ref, hh_ref, dy_ref, cw_ref, cb_ref, wa_ref, ba_ref, wx_ref, bx_ref, lam_ref,
             dx_ref, dwa_ref, dwx_ref, vec_ref, xbuf, hbuf, abuf, lbuf, dbuf, cstate, dhalo, rowbuf):
        i = pl.program_id(0)
        rt = nt - 1 - i

        @pl.when(i == 0)
        def _():
            cstate[...] = jnp.zeros_like(cstate)
            dhalo[...] = jnp.zeros_like(dhalo)
            dwa_ref[...] = jnp.zeros_like(dwa_ref)
            dwx_ref[...] = jnp.zeros_like(dwx_ref)
            vec_ref[...] = jnp.zeros_like(vec_ref)

        xbuf[0:LRU_HALO, :] = jnp.where(rt > 0, xh_ref[...], 0.0)
        xbuf[LRU_HALO:LRU_HALO + tt, :] = x_ref[...]
        hbuf[0:LRU_HALO, :] = jnp.where(rt > 0, hh_ref[...], 0.0)
        hbuf[LRU_HALO:LRU_HALO + tt, :] = h_ref[...]
        xc = _conv_taps(xbuf, cw_ref, tt, 4, LRU_HALO - 3) + cb_ref[...]
        r, ig, a, mult, sp = _lru_gates(xc, wa_ref, ba_ref, wx_ref, bx_ref, lam_ref, nh)
        abuf[...] = a

        def group(gi, c):
            rows = pl.ds(pl.multiple_of((tt // 8 - 1 - gi) * 8, 8), 8)
            a8 = abuf[rows, :]
            d8 = dy_ref[rows, :]
            for j in range(7, -1, -1):
                lam_t = d8[j:j + 1, :] + c
                rowbuf[j:j + 1, :] = lam_t
                c = a8[j:j + 1, :] * lam_t
            lbuf[rows, :] = rowbuf[...]
            return c

        cstate[0:1, :] = lax.fori_loop(0, tt // 8, group, cstate[0:1, :])

        lam_t = lbuf[...]
        hprev = hbuf[pl.ds(LRU_HALO - 1, tt), :]
        ixc = ig * xc
        d_ixc = lam_t * mult
        d_ig = d_ixc * xc
        dxc = d_ixc * ig
        dlog_a = lam_t * hprev * a + lam_t * ixc * (-(a * a) / mult)
        dr = dlog_a * (-LRU_C * sp)
        lam_p = lam_ref[...]
        dsp = -_sigmoid(-lam_p)
        vec_ref[2:3, :] += jnp.sum(dlog_a * (-LRU_C * r), axis=0, keepdims=True) * dsp
        dpr = dr * r * (1.0 - r)
        dpi = d_ig * ig * (1.0 - ig)
        vec_ref[0:1, :] += jnp.sum(dpr, axis=0, keepdims=True)
        vec_ref[1:2, :] += jnp.sum(dpi, axis=0, keepdims=True)
        parts = []
        for n in range(nh):
            sl = slice(n * HEAD, (n + 1) * HEAD)
            xn = xc[:, sl].astype(BF16)
            dprn = dpr[:, sl].astype(BF16)
            dpin = dpi[:, sl].astype(BF16)
            dwa_ref[n] += _dot_tn(xn, dprn)
            dwx_ref[n] += _dot_tn(xn, dpin)
            parts.append(_dot_nt(dprn, wa_ref[n]) + _dot_nt(dpin, wx_ref[n]))
        dxc = dxc + (parts[0] if nh == 1 else jnp.concatenate(parts, axis=1))
        vec_ref[3:4, :] += jnp.sum(dxc, axis=0, keepdims=True)
        dbuf[0:tt, :] = dxc
        dbuf[tt:tt + LRU_HALO, :] = dhalo[...]
        dx = cw_ref[0:1, :] * dbuf[pl.ds(3, tt), :]
        for k in range(1, 4):
            dx = dx + cw_ref[k:k + 1, :] * dbuf[pl.ds(3 - k, tt), :]
        dx_ref[...] = dx.astype(BF16)
        for k in range(4):
            vec_ref[4 + k:5 + k, :] += jnp.sum(dxc * xbuf[pl.ds(LRU_HALO - 3 + k, tt), :], axis=0, keepdims=True)
        dhalo[...] = dbuf[0:LRU_HALO, :]

    vec = pl.BlockSpec((1, w), lambda i: (0, 0))
    gate = pl.BlockSpec((nh, HEAD, HEAD), lambda i: (0, 0, 0))
    rev = lambda i: nt - 1 - i
    tile = pl.BlockSpec((tt, w), lambda i: (rev(i), 0))
    halo = lambda col: pl.BlockSpec((LRU_HALO, w), lambda i: (jnp.maximum(rev(i) * hb - 1, 0), col))
    return pl.pallas_call(
        body, name=name,
        out_shape=(jax.ShapeDtypeStruct((s, w), BF16), jax.ShapeDtypeStruct((nh, HEAD, HEAD), F32),
                   jax.ShapeDtypeStruct((nh, HEAD, HEAD), F32), jax.ShapeDtypeStruct((8, w), F32)),
        grid=(nt,),
        in_specs=[pl.BlockSpec((tt, w), lambda i: (rev(i), x_blk)), halo(x_blk), tile, halo(0), tile,
                  pl.BlockSpec((8, w), lambda i: (0, 0)), vec, gate, vec, gate, vec, vec],
        out_specs=(tile, gate, gate, pl.BlockSpec((8, w), lambda i: (0, 0))),
        scratch_shapes=[pltpu.VMEM((LRU_HALO + tt, w), F32), pltpu.VMEM((LRU_HALO + tt, w), F32),
                        pltpu.VMEM((tt, w), F32), pltpu.VMEM((tt, w), F32), pltpu.VMEM((tt + LRU_HALO, w), F32),
                        pltpu.VMEM((8, w), F32), pltpu.VMEM((8, w), F32), pltpu.VMEM((8, w), F32)],
        compiler_params=_params("arbitrary"),
    )(u, u, hseq, hseq, dy, cw, cb, wa, ba, wx, bx, lam)


def _gate_specs(c, tr):
    return [pl.BlockSpec((tr, c), lambda i, b=b: (i, b)) for b in (2, 9, 10, 12)]


def _outgate_fwd(y_conv, y_attn, y_lru, u, n_conv, n_attn, n_lru, name):
    s, c = y_conv.shape
    tr = _tile(s, (256, 128, 64, 32, 16))

    def body(yc_ref, ya_ref, yl_ref, gc_ref, ga0_ref, ga1_ref, gl_ref, nc_ref, na_ref, nl_ref, o_ref):
        def rinv(v):
            return lax.rsqrt(jnp.mean(v * v, axis=-1, keepdims=True) + RMS_EPS)

        def silu(g):
            return g * _sigmoid(g)

        yc = yc_ref[...]
        o_ref[:, 0:c] = (yc * rinv(yc) * nc_ref[...] * silu(gc_ref[...])).astype(BF16)
        ya = ya_ref[...]
        ra = rinv(ya)
        o_ref[:, c:2 * c] = (ya[:, 0:c] * ra * na_ref[:, 0:c] * silu(ga0_ref[...])).astype(BF16)
        o_ref[:, 2 * c:3 * c] = (ya[:, c:2 * c] * ra * na_ref[:, c:2 * c] * silu(ga1_ref[...])).astype(BF16)
        yl = yl_ref[...]
        o_ref[:, 3 * c:4 * c] = (yl * rinv(yl) * nl_ref[...] * silu(gl_ref[...])).astype(BF16)

    row = lambda wd: pl.BlockSpec((tr, wd), lambda i: (i, 0))
    vec = lambda wd: pl.BlockSpec((1, wd), lambda i: (0, 0))
    return pl.pallas_call(
        body, name=name, out_shape=jax.ShapeDtypeStruct((s, 4 * c), BF16), grid=(s // tr,),
        in_specs=[row(c), row(2 * c), row(c)] + _gate_specs(c, tr) + [vec(c), vec(2 * c), vec(c)],
        out_specs=row(4 * c), compiler_params=_params("parallel"),
    )(y_conv, y_attn, y_lru, u, u, u, u, n_conv, n_attn, n_lru)


def _outgate_bwd(dy, y_conv, y_attn, y_lru, u, n_conv, n_attn, n_lru, name):
    s, c = y_conv.shape
    tr = _tile(s, (256, 128, 64, 32, 16))

    def body(dy_ref, yc_ref, ya_ref, yl_ref, gc_ref, ga0_ref, ga1_ref, gl_ref, nc_ref, na_ref, nl_ref,
             dyc_ref, dya_ref, dyl_ref, dgc_ref, dga_ref, dgl_ref, dn_ref):
        @pl.when(pl.program_id(0) == 0)
        def _():
            dn_ref[...] = jnp.zeros_like(dn_ref)

        def group(yv, gate, wv, d):
            r = lax.rsqrt(jnp.mean(yv * yv, axis=-1, keepdims=True) + RMS_EPS)
            yh = yv * r
            sg = _sigmoid(gate)
            dn = d * (gate * sg)
            dgate = d * (yh * wv) * (sg * (1.0 + gate * (1.0 - sg)))
            dw = jnp.sum(dn * yh, axis=0, keepdims=True)
            dyn = dn * wv
            dyv = r * (dyn - yh * jnp.mean(dyn * yh, axis=-1, keepdims=True))
            return dyv, dgate, dw

        dyv, dg, dw = group(yc_ref[...], gc_ref[...], nc_ref[...], dy_ref[:, 0:c])
        dyc_ref[...] = dyv
        dgc_ref[...] = dg.astype(BF16)
        dn_ref[0:1, 0:c] += dw
        gate_a = jnp.concatenate([ga0_ref[...], ga1_ref[...]], axis=1)
        dyv, dg, dw = group(ya_ref[...], gate_a, na_ref[...], dy_ref[:, c:3 * c])
        dya_ref[...] = dyv
        dga_ref[...] = dg.astype(BF16)
        dn_ref[0:1, c:3 * c] += dw
        dyv, dg, dw = group(yl_ref[...], gl_ref[...], nl_ref[...], dy_ref[:, 3 * c:4 * c])
        dyl_ref[...] = dyv
        dgl_ref[...] = dg.astype(BF16)
        dn_ref[0:1, 3 * c:4 * c] += dw

    row = lambda wd: pl.BlockSpec((tr, wd), lambda i: (i, 0))
    vec = lambda wd: pl.BlockSpec((1, wd), lambda i: (0, 0))
    sh = lambda wd, dt: jax.ShapeDtypeStruct((s, wd), dt)
    return pl.pallas_call(
        body, name=name,
        out_shape=(sh(c, F32), sh(2 * c, F32), sh(c, F32), sh(c, BF16), sh(2 * c, BF16), sh(c, BF16),
                   jax.ShapeDtypeStruct((8, 4 * c), F32)),
        grid=(s // tr,),
        in_specs=[row(4 * c), row(c), row(2 * c), row(c)] + _gate_specs(c, tr) + [vec(c), vec(2 * c), vec(c)],
        out_specs=(row(c), row(2 * c), row(c), row(c), row(2 * c), row(c),
                   pl.BlockSpec((8, 4 * c), lambda i: (0, 0))),
        compiler_params=_params("arbitrary"),
    )(dy, y_conv, y_attn, y_lru, u, u, u, u, n_conv, n_attn, n_lru)


def _xattn_probs(qh, kh, scale):
    sc = _dot_nt(qh, kh) * scale
    p = jnp.exp(sc - jnp.max(sc, axis=-1, keepdims=True))
    return p / jnp.sum(p, axis=-1, keepdims=True)


def _xattn_fwd(q, kv, name):
    s, xw = q.shape
    m = kv.shape[0]
    nh = xw // HEAD
    tq = _tile(s, (256, 128, 64, 32, 16))
    scale = HEAD ** -0.5

    def body(q_ref, kv_ref, o_ref):
        for h in range(nh):
            qh = q_ref[:, h * HEAD:(h + 1) * HEAD].astype(BF16)
            kh = kv_ref[:, h * HEAD:(h + 1) * HEAD].astype(BF16)
            vh = kv_ref[:, xw + h * HEAD:xw + (h + 1) * HEAD].astype(BF16)
            p = _xattn_probs(qh, kh, scale)
            o_ref[:, h * HEAD:(h + 1) * HEAD] = _dot(p.astype(BF16), vh).astype(BF16)

    return pl.pallas_call(
        body, name=name, out_shape=jax.ShapeDtypeStruct((s, xw), BF16), grid=(s // tq,),
        in_specs=[pl.BlockSpec((tq, xw), lambda i: (i, 0)), pl.BlockSpec((m, 2 * xw), lambda i: (0, 0))],
        out_specs=pl.BlockSpec((tq, xw), lambda i: (i, 0)), compiler_params=_params("parallel"),
    )(q, kv)


def _xattn_bwd(q, kv, do, name):
    s, xw = q.shape
    m = kv.shape[0]
    nh = xw // HEAD
    tq = _tile(s, (256, 128, 64, 32, 16))
    scale = HEAD ** -0.5

    def body(q_ref, kv_ref, do_ref, dq_ref, dkv_ref):
        @pl.when(pl.program_id(0) == 0)
        def _():
            dkv_ref[...] = jnp.zeros_like(dkv_ref)

        for h in range(nh):
            ks = slice(h * HEAD, (h + 1) * HEAD)
            vs = slice(xw + h * HEAD, xw + (h + 1) * HEAD)
            qh = q_ref[:, ks].astype(BF16)
            kh = kv_ref[:, ks].astype(BF16)
            vh = kv_ref[:, vs].astype(BF16)
            doh = do_ref[:, ks].astype(BF16)
            p = _xattn_probs(qh, kh, scale)
            dkv_ref[:, vs] += _dot_tn(p.astype(BF16), doh)
            dp = _dot_nt(doh, vh)
            ds = (p * (dp - jnp.sum(dp * p, axis=-1, keepdims=True)) * scale).astype(BF16)
            dq_ref[:, ks] = _dot(ds, kh).astype(BF16)
            dkv_ref[:, ks] += _dot_tn(ds, qh)

    row = pl.BlockSpec((tq, xw), lambda i: (i, 0))
    full = pl.BlockSpec((m, 2 * xw), lambda i: (0, 0))
    return pl.pallas_call(
        body, name=name,
        out_shape=(jax.ShapeDtypeStruct((s, xw), BF16), jax.ShapeDtypeStruct((m, 2 * xw), F32)), grid=(s // tq,),
        in_specs=[row, full, row], out_specs=(row, full), compiler_params=_params("arbitrary"),
    )(q, kv, do)


def _adamw(w, g, m, v, name):
    rows, cols = w.shape
    tr = _tile(rows, (512, 256, 128, 64, 32, 16, 8)) if rows % 8 == 0 else rows
    bc1 = 1.0 - ADAM_B1 ** ADAM_STEP
    bc2 = 1.0 - ADAM_B2 ** ADAM_STEP

    def body(w_ref, g_ref, m_ref, v_ref, d_ref, nm_ref, nv_ref):
        gv = g_ref[...]
        nm = ADAM_B1 * m_ref[...] + (1.0 - ADAM_B1) * gv
        nv = ADAM_B2 * v_ref[...] + (1.0 - ADAM_B2) * (gv * gv)
        nm_ref[...] = nm
        nv_ref[...] = nv
        d_ref[...] = -ADAM_LR * ((nm / bc1) / (jnp.sqrt(nv / bc2) + ADAM_EPS) + ADAM_WD * w_ref[...])

    spec = pl.BlockSpec((tr, cols), lambda i: (i, 0))
    sh = jax.ShapeDtypeStruct((rows, cols), F32)
    return pl.pallas_call(
        body, name=name, out_shape=(sh, sh, sh), grid=(rows // tr,), in_specs=[spec] * 4,
        out_specs=(spec, spec, spec), compiler_params=_params("parallel"),
    )(w, g, m, v)


WEIGHTS = ['mix_norm_g', 'w_in', 'conv_dw_w', 'conv_dw_b', 'conv_ln_g', 'conv_ln_b', 'conv_pw_w', 'lru_conv_w',
           'lru_conv_b', 'lru_wa', 'lru_ba', 'lru_wx', 'lru_bx', 'lru_lambda', 'out_norm_conv', 'out_norm_attn',
           'out_norm_lru', 'w_out', 'xattn_norm_g', 'mem_norm_g', 'xattn_wq', 'xattn_wkv', 'xattn_wo',
           'final_norm_g']
BIG_SHARDED = {'w_in': 2, 'conv_pw_w': 1, 'w_out': 1, 'xattn_wq': 1, 'xattn_wkv': 1, 'xattn_wo': 2}
SMALL_SHARDED = {'conv_dw_w': 2, 'lru_conv_w': 2}


def _layer_fwd(x, mem, p, l):
    row = lambda name: p[name][l][None, :]
    c = p['conv_dw_b'].shape[1]
    heads = 2 * c // HEAD
    h = _rms_fwd(x, row('mix_norm_g'), "rms_mix")
    u = _mm(h, p['w_in'], bl=l, name="in_proj")
    wpad = jnp.pad(p['conv_dw_w'][l], ((0, 1), (0, 0)))
    cw = jnp.pad(p['lru_conv_w'][l], ((0, 4), (0, 0)))
    y_conv = _conv_fwd(u, wpad, row('conv_dw_b'), row('conv_ln_g'), row('conv_ln_b'), p['conv_pw_w'], l, "conv_fwd")
    q_blk = 3 * c // HEAD
    y_attn, tot = _sb_fwd(u, heads, q_blk, q_blk + heads, q_blk + 2 * heads, "sb_fwd")
    wa, wx = p['lru_wa'][l].astype(BF16), p['lru_wx'][l].astype(BF16)
    y_lru = _lru_fwd(u, 11, cw, row('lru_conv_b'), wa, row('lru_ba'), wx, row('lru_bx'), row('lru_lambda'), "lru_fwd")
    yc = _outgate_fwd(y_conv, y_attn, y_lru, u, row('out_norm_conv'), row('out_norm_attn'), row('out_norm_lru'),
                      "outgate_fwd")
    x1 = _mm(yc, p['w_out'], bl=l, add=x, name="out_proj")
    h2 = _rms_fwd(x1, row('xattn_norm_g'), "rms_xattn")
    memn = _rms_fwd(mem, row('mem_norm_g'), "rms_mem")
    q2 = _mm(h2, p['xattn_wq'], bl=l, name="xq_proj")
    kv = _mm(memn, p['xattn_wkv'], bl=l, name="xkv_proj")
    o2 = _xattn_fwd(q2, kv, "xattn_fwd")
    x2 = _mm(o2, p['xattn_wo'], bl=l, add=x1, name="xo_proj")
    saved = dict(x=x, h=h, u=u, wpad=wpad, cw=cw, wa=wa, wx=wx, y_conv=y_conv, y_attn=y_attn, tot=tot, y_lru=y_lru,
                 yc=yc, x1=x1, h2=h2, memn=memn, q2=q2, kv=kv, o2=o2)
    return x2, saved


def _layer_bwd(dx2, mem, p, l, sv):
    row = lambda name: p[name][l][None, :]
    c = p['conv_dw_b'].shape[1]
    heads = 2 * c // HEAD
    g = {}
    g['xattn_wo'] = _mm(sv['o2'], dx2, ta=True, out_dtype=BF16, name="d_wo")
    do2 = _mm(dx2, p['xattn_wo'], bl=l, tb=True, name="d_o2")
    dq2, dkv = _xattn_bwd(sv['q2'], sv['kv'], do2, "xattn_bwd")
    g['xattn_wq'] = _mm(sv['h2'], dq2, ta=True, out_dtype=BF16, name="d_wq")
    dh2 = _mm(dq2, p['xattn_wq'], bl=l, tb=True, name="d_h2")
    g['xattn_wkv'] = _mm(sv['memn'], dkv, ta=True, out_dtype=BF16, name="d_wkv")
    dmemn = _mm(dkv, p['xattn_wkv'], bl=l, tb=True, name="d_memn")
    _, g['mem_norm_g'] = _rms_bwd(mem, row('mem_norm_g'), dmemn, None, "rms_mem_bwd")
    dx1, g['xattn_norm_g'] = _rms_bwd(sv['x1'], row('xattn_norm_g'), dh2, dx2, "rms_xattn_bwd")
    g['w_out'] = _mm(sv['yc'], dx1, ta=True, out_dtype=BF16, name="d_wout")
    dyc = _mm(dx1, p['w_out'], bl=l, tb=True, name="d_yc")
    u = sv['u']
    d_yconv, d_yattn, d_ylru, dgc, dga, dgl, dn = _outgate_bwd(
        dyc, sv['y_conv'], sv['y_attn'], sv['y_lru'], u, row('out_norm_conv'), row('out_norm_attn'),
        row('out_norm_lru'), "outgate_bwd")
    g['out_norm_conv'], g['out_norm_attn'], g['out_norm_lru'] = dn[0, 0:c], dn[0, c:3 * c], dn[0, 3 * c:4 * c]
    dd, gl, dpw, cvec = _conv_bwd_post(u, d_yconv, sv['wpad'], row('conv_dw_b'), row('conv_ln_g'),
                                       row('conv_ln_b'), p['conv_pw_w'], l, "conv_bwd_post")
    g['conv_pw_w'] = dpw.astype(BF16)
    g['conv_ln_g'], g['conv_ln_b'], g['conv_dw_b'] = cvec[0], cvec[1], cvec[2]
    dval, dglu, ddw = _conv_bwd_dw(u, dd, gl, sv['wpad'], "conv_bwd_dw")
    g['conv_dw_w'] = ddw[0:31]
    q_blk = 3 * c // HEAD
    dq, dk, dv = _sb_bwd(u, sv['tot'], d_yattn, heads, q_blk, q_blk + heads, q_blk + 2 * heads, "sb_bwd")
    dxr, g['lru_wa'], g['lru_wx'], lvec = _lru_bwd(
        u, 11, sv['y_lru'], d_ylru, sv['cw'], row('lru_conv_b'), sv['wa'], row('lru_ba'), sv['wx'], row('lru_bx'),
        row('lru_lambda'), "lru_bwd")
    g['lru_ba'], g['lru_bx'], g['lru_lambda'], g['lru_conv_b'] = lvec[0], lvec[1], lvec[2], lvec[3]
    g['lru_conv_w'] = lvec[4:8]
    du = jnp.concatenate([dval, dglu, dgc, dq.astype(BF16), dk.astype(BF16), dv.astype(BF16), dga, dxr, dgl], axis=1)
    g['w_in'] = _mm(sv['h'], du, ta=True, out_dtype=BF16, name="d_win")
    dh = _mm(du, p['w_in'], bl=l, tb=True, name="d_h")
    dx0, g['mix_norm_g'] = _rms_bwd(sv['x'], row('mix_norm_g'), dh, dx1, "rms_mix_bwd")
    return dx0, g


def kernel(x, mem, mix_norm_g, w_in, conv_dw_w, conv_dw_b, conv_ln_g, conv_ln_b, conv_pw_w, lru_conv_w, lru_conv_b, lru_wa, lru_ba, lru_wx, lru_bx, lru_lambda, out_norm_conv, out_norm_attn, out_norm_lru, w_out, xattn_norm_g, mem_norm_g, xattn_wq, xattn_wkv, xattn_wo, final_norm_g, loss_target, m_mix_norm_g, m_w_in, m_conv_dw_w, m_conv_dw_b, m_conv_ln_g, m_conv_ln_b, m_conv_pw_w, m_lru_conv_w, m_lru_conv_b, m_lru_wa, m_lru_ba, m_lru_wx, m_lru_bx, m_lru_lambda, m_out_norm_conv, m_out_norm_attn, m_out_norm_lru, m_w_out, m_xattn_norm_g, m_mem_norm_g, m_xattn_wq, m_xattn_wkv, m_xattn_wo, m_final_norm_g, v_mix_norm_g, v_w_in, v_conv_dw_w, v_conv_dw_b, v_conv_ln_g, v_conv_ln_b, v_conv_pw_w, v_lru_conv_w, v_lru_conv_b, v_lru_wa, v_lru_ba, v_lru_wx, v_lru_bx, v_lru_lambda, v_out_norm_conv, v_out_norm_attn, v_out_norm_lru, v_w_out, v_xattn_norm_g, v_mem_norm_g, v_xattn_wq, v_xattn_wkv, v_xattn_wo, v_final_norm_g):
    args = locals()
    w = {n: args[n] for n in WEIGHTS}
    mom = {n: args["m_" + n] for n in WEIGHTS}
    var = {n: args["v_" + n] for n in WEIGHTS}
    depth = w_in.shape[0]
    c = conv_dw_b.shape[1]
    assert out_norm_attn.shape[1] == 2 * c and lru_lambda.shape[1] == c and w_in.shape[2] * N_DEV == 13 * c
    assert c % HEAD == 0 and x.shape[0] == 1 and mem.shape[0] == 1
    xs, mems, tgt = x[0], mem[0], loss_target[0]
    me = 4 * lax.axis_index("x") + 2 * lax.axis_index("y") + lax.axis_index("c")

    assert depth == 2 and (2 * w_in.shape[2]) % LANE == 0 and w_in.shape[2] % LANE in (0, LANE // 2)
    my_c = lax.axis_index("c")
    big = list(BIG_SHARDED)
    dev = [n for n in big if n != 'w_in']
    pair = _pair_gather(w_in.astype(BF16), "pair_w_in")
    chip_w_in = jnp.concatenate([pair[0], pair[1]], axis=2)
    gathered = _gather_weights(chip_w_in, 2, [w[n].astype(BF16) for n in dev], [BIG_SHARDED[n] for n in dev],
                               "gather_weights")
    full = dict(zip(['w_in'] + dev, gathered))
    small = list(SMALL_SHARDED)
    gathered = _all_gather(_pack([w[n] for n in small], F32), "gather_conv_taps")
    full.update({n: _join_blocks(blk, SMALL_SHARDED[n])
                 for n, blk in zip(small, _unpack(gathered, [w[n].shape for n in small], lead=N_DEV))})
    p = {**w, **full}

    saved = []
    act = xs
    for l in range(depth):
        act, sv = _layer_fwd(act, mems, p, l)
        saved.append(sv)
    loss_part, dact, d_final = _loss_bwd(act, final_norm_g[None, :], tgt, "loss_bwd")

    axes, widths, starts = [], [], []
    for n in big:
        axis = BIG_SHARDED[n] - 1
        blk = w[n].shape[axis + 1]
        pad = blk % LANE if axis == 1 else 0
        axes.append(axis)
        widths.append(blk + pad)
        starts.append(lambda px, py, pc, blk=blk, pad=pad: blk * (4 * px + 2 * py + pc) - pad * pc)
    layer_grads = [None] * depth
    dact, layer_grads[1] = _layer_bwd(dact, mems, p, 1, saved[1])
    arrs = [layer_grads[1][n] for n in big]
    zones = [_scatter_zone(g, a, wd, st(lax.axis_index("x"), lax.axis_index("y"), my_c), me)
             for g, a, wd, st in zip(arrs, axes, widths, starts)]
    send_sems, recv_sems, arrs, zones, token = _scatter_start(arrs, zones, axes, widths, starts, "scatter_start_l1")
    saved[0]['kv'] = saved[0]['kv'] + token[0, 0]
    dact, layer_grads[0] = _layer_bwd(dact, mems, p, 0, saved[0])
    grad_x = dact[None]
    received = [None, _scatter_wait(send_sems, recv_sems, arrs, zones, dact, "scatter_wait_l1")]
    received[0] = [got.reshape((N_DEV,) + got.shape[2:]) for got in
                   _scatter_grads([layer_grads[0][n] for n in big], axes, widths, starts, 1, "scatter_grads_l0")]
    rest = [n for n in WEIGHTS if n not in BIG_SHARDED]
    partial = {n: jnp.stack([layer_grads[l][n] for l in range(depth)]) for n in rest if n != 'final_norm_g'}
    partial['final_norm_g'] = d_final[0]
    grads = {}
    for t, n in enumerate(big):
        summed = jnp.stack([_sum_blocks(received[l][t], "sum_" + n) for l in range(depth)])
        width = w[n].shape[-1]
        if summed.shape[-1] != width:
            summed = jnp.where(my_c == 0, summed[..., :width], summed[..., summed.shape[-1] - width:])
        grads[n] = summed

    vec = _pack([partial[n] for n in rest] + [loss_part], F32)
    total = _sum_blocks(_all_gather(vec, "gather_small_grads"), "sum_small_grads")
    pieces = _unpack(total, [partial[n].shape for n in rest] + [(1, 1)])
    loss = pieces[-1][0, 0]
    for n, piece in zip(rest, pieces[:-1]):
        if n in SMALL_SHARDED:
            width = w[n].shape[2]
            piece = lax.dynamic_slice_in_dim(piece, me * width, width, axis=2)
        grads[n] = piece

    delta, new_m, new_v = {}, {}, {}
    for n in big:
        shape = w[n].shape
        two_d = lambda a: a.reshape(-1, shape[-1])
        d, nm, nv = _adamw(two_d(w[n]), two_d(grads[n]), two_d(mom[n]), two_d(var[n]), "adamw_" + n)
        delta[n], new_m[n], new_v[n] = d.reshape(shape), nm.reshape(shape), nv.reshape(shape)
    shapes = [w[n].shape for n in rest]
    packed = [_pack([src[n] for n in rest], F32) for src in (w, grads, mom, var)]
    outs = _adamw(*packed, "adamw_small")
    for dst, o in zip((delta, new_m, new_v), outs):
        dst.update(dict(zip(rest, _unpack(o, shapes))))

    return (loss, grad_x, *[grads[n] for n in WEIGHTS], *[delta[n] for n in WEIGHTS],
            *[new_m[n] for n in WEIGHTS], *[new_v[n] for n in WEIGHTS])
```

```python
import functools
import math

import jax
import jax.numpy as jnp
from jax import lax
from jax.experimental import pallas as pl
from jax.experimental.pallas import tpu as pltpu

F32 = jnp.float32
BF16 = jnp.bfloat16

N_DEV = 8
LANE = 128
HEAD = 128
VMEM_LIMIT = 56 * 1024 * 1024
PACK_COLS = 512
RMS_EPS = 1e-6
LN_EPS = 1e-5
LRU_C = 8.0
CONV_HALO = 32
LRU_HALO = 8

ADAM_LR, ADAM_B1, ADAM_B2, ADAM_EPS, ADAM_WD, ADAM_STEP = 0.001, 0.9, 0.999, 1e-08, 0.01, 10

MESH = pl.DeviceIdType.MESH


def _tile(n, cands):
    for c in cands:
        if n % c == 0:
            return c
    raise ValueError(f"no tile of {cands} divides {n}")


def _params(*sem):
    return pltpu.CompilerParams(dimension_semantics=sem, vmem_limit_bytes=VMEM_LIMIT)


def _dot(a, b):
    return lax.dot_general(a, b, (((1,), (0,)), ((), ())), preferred_element_type=F32)


def _dot_nt(a, b):
    return lax.dot_general(a, b, (((1,), (1,)), ((), ())), preferred_element_type=F32)


def _dot_tn(a, b):
    return lax.dot_general(a, b, (((0,), (0,)), ((), ())), preferred_element_type=F32)


def _sigmoid(x):
    return 1.0 / (1.0 + jnp.exp(-x))


def _expm1(x):
    series = x * (1.0 + x * (0.5 + x * (1.0 / 6.0 + x * (1.0 / 24.0))))
    return jnp.where(jnp.abs(x) < 0.05, series, jnp.exp(x) - 1.0)


def _my_place():
    return lax.axis_index("x"), lax.axis_index("y"), lax.axis_index("c")


def _flip(v, d):
    return 1 - v if d else v


def _window(ref, axis, start, size):
    return ref.at[tuple(pl.ds(start, size) if a == axis else pl.ds(0, ref.shape[a]) for a in range(len(ref.shape)))]


def _all_gather(x2d, name):
    rows, cols = x2d.shape

    def body(x_ref, out_ref, send_sems, recv_sems, local_sem):
        x, y, c = _my_place()
        me, sibling = (x, y, c), (x, y, 1 - c)
        chips = [(1 - x, y), (x, 1 - y), (1 - x, 1 - y)]

        def blk(px, py, pc):
            return out_ref.at[4 * px + 2 * py + pc]

        def copy(k, block, to, src=None):
            return pltpu.make_async_remote_copy(
                src_ref=blk(*block) if src is None else src, dst_ref=blk(*block),
                send_sem=send_sems.at[k], recv_sem=recv_sems.at[k], device_id=to, device_id_type=MESH)

        mine = pltpu.make_async_copy(x_ref, blk(*me), local_sem)
        mine.start()
        first = [copy(0, me, sibling, src=x_ref)]
        first += [copy(1 + j, me, (*chip, c), src=x_ref) for j, chip in enumerate(chips)]
        for cp in first:
            cp.start()
        passed = [copy(4 + j, (*chip, c), sibling) for j, chip in enumerate(chips)]
        for j, chip in enumerate(chips):
            copy(1 + j, (*chip, c), me).wait_recv()
            passed[j].start()
        copy(0, sibling, me).wait_recv()
        for j, chip in enumerate(chips):
            copy(4 + j, (*chip, 1 - c), me).wait_recv()
        for cp in first + passed:
            cp.wait_send()
        mine.wait()

    return pl.pallas_call(
        body, name=name,
        out_shape=jax.ShapeDtypeStruct((N_DEV, rows, cols), x2d.dtype),
        in_specs=[pl.BlockSpec(memory_space=pl.ANY)],
        out_specs=pl.BlockSpec(memory_space=pl.ANY),
        scratch_shapes=[pltpu.SemaphoreType.DMA((7,)), pltpu.SemaphoreType.DMA((7,)), pltpu.SemaphoreType.DMA],
    )(x2d)


def _pair_gather(blk, name):
    pieces = 8
    rows = blk.shape[1] // pieces

    def body(x_ref, out_ref, send_sem, recv_sem, local_sem):
        x, y, c = _my_place()
        mine = pltpu.make_async_copy(x_ref, out_ref.at[c], local_sem)
        mine.start()
        for i in range(pieces):
            pltpu.make_async_remote_copy(
                src_ref=_window(x_ref, 1, i * rows, rows), dst_ref=_window(out_ref.at[c], 1, i * rows, rows),
                send_sem=send_sem, recv_sem=recv_sem, device_id=(x, y, 1 - c), device_id_type=MESH).start()
        whole = pltpu.make_async_remote_copy(src_ref=x_ref, dst_ref=out_ref.at[1 - c], send_sem=send_sem,
                                             recv_sem=recv_sem, device_id=(x, y, 1 - c), device_id_type=MESH)
        whole.wait_recv()
        whole.wait_send()
        mine.wait()

    return pl.pallas_call(
        body, name=name, out_shape=jax.ShapeDtypeStruct((2,) + blk.shape, blk.dtype),
        in_specs=[pl.BlockSpec(memory_space=pl.ANY)], out_specs=pl.BlockSpec(memory_space=pl.ANY),
        scratch_shapes=[pltpu.SemaphoreType.DMA, pltpu.SemaphoreType.DMA, pltpu.SemaphoreType.DMA],
    )(blk)


def _gather_weights(chip_block, chip_axis, dev_blocks, dev_axes, name):
    n_items = 1 + len(dev_blocks)
    chip_w = chip_block.shape[chip_axis]
    dev_w = [b.shape[a] for b, a in zip(dev_blocks, dev_axes)]

    def full_shape(b, a, n):
        return b.shape[:a] + (b.shape[a] * n,) + b.shape[a + 1:]

    out_shape = [jax.ShapeDtypeStruct(full_shape(chip_block, chip_axis, 4), chip_block.dtype)]
    out_shape += [jax.ShapeDtypeStruct(full_shape(b, a, N_DEV), b.dtype) for b, a in zip(dev_blocks, dev_axes)]

    def body(*refs):
        ins, outs = refs[:n_items], refs[n_items:2 * n_items]
        send_sems, recv_sems, local_sems = refs[2 * n_items:]
        x, y, c = _my_place()
        me, sibling = (x, y, c), (x, y, 1 - c)
        chips = [(1 - x, y), (x, 1 - y), (1 - x, 1 - y)]

        def dst(t, px, py, pc):
            if t == 0:
                return _window(outs[0].at[pc], chip_axis - 1, chip_w * (2 * px + py), chip_w)
            return _window(outs[t], dev_axes[t - 1], dev_w[t - 1] * (4 * px + 2 * py + pc), dev_w[t - 1])

        def own(t):
            return ins[0].at[c] if t == 0 else ins[t]

        def copy(t, k, block, to, src=None):
            return pltpu.make_async_remote_copy(
                src_ref=dst(t, *block) if src is None else src, dst_ref=dst(t, *block),
                send_sem=send_sems.at[7 * t + k], recv_sem=recv_sems.at[7 * t + k], device_id=to, device_id_type=MESH)

        local = [pltpu.make_async_copy(ins[0], _window(outs[0], chip_axis, chip_w * (2 * x + y), chip_w),
                                       local_sems.at[0])]
        local += [pltpu.make_async_copy(ins[t], dst(t, *me), local_sems.at[t]) for t in range(1, n_items)]
        for cp in local:
            cp.start()
        first = []
        for t in range(n_items):
            if t > 0:
                first.append(copy(t, 0, me, sibling, src=own(t)))
            first += [copy(t, 1 + j, me, (*chip, c), src=own(t)) for j, chip in enumerate(chips)]
        for cp in first:
            cp.start()
        passed = []
        for j, chip in enumerate(chips):
            for t in range(n_items):
                copy(t, 1 + j, (*chip, c), me).wait_recv()
                passed.append(copy(t, 4 + j, (*chip, c), sibling))
                passed[-1].start()
        for t in range(1, n_items):
            copy(t, 0, sibling, me).wait_recv()
        for j, chip in enumerate(chips):
            for t in range(n_items):
                copy(t, 4 + j, (*chip, 1 - c), me).wait_recv()
        for cp in first + passed:
            cp.wait_send()
        for cp in local:
            cp.wait()

    hbm = pl.BlockSpec(memory_space=pl.ANY)
    return pl.pallas_call(
        body, name=name, out_shape=out_shape, in_specs=[hbm] * n_items, out_specs=[hbm] * n_items,
        scratch_shapes=[pltpu.SemaphoreType.DMA((7 * n_items,)), pltpu.SemaphoreType.DMA((7 * n_items,)),
                        pltpu.SemaphoreType.DMA((n_items,))],
    )(chip_block, *dev_blocks)


def _scatter_grads(grads, axes, widths, starts, depth, name):
    n = len(grads)

    def win_shape(g, a, w):
        return g.shape[:a] + (w,) + g.shape[a + 1:]

    out_shape = [jax.ShapeDtypeStruct((N_DEV, depth) + win_shape(g, a, w), g.dtype)
                 for g, a, w in list(zip(grads, axes, widths))[::depth]]

    def body(*refs):
        ins, outs = refs[:n], refs[n:n + n // depth]
        send_sems, recv_sems, local_sems = refs[n + n // depth:]
        x, y, c = _my_place()
        me = 4 * x + 2 * y + c

        def win(t, px, py, pc):
            start = pl.multiple_of(starts[t](px, py, pc), math.gcd(widths[t], 1024))
            return _window(ins[t], axes[t], start, widths[t])

        def slot(t, j):
            return outs[t // depth].at[j, t % depth]

        local = [pltpu.make_async_copy(win(t, x, y, c), slot(t, me), local_sems.at[t]) for t in range(n)]
        for cp in local:
            cp.start()
        sends, recvs = [], []
        for k in range(1, N_DEV):
            px, py, pc = _flip(x, k & 4), _flip(y, k & 2), _flip(c, k & 1)
            peer = 4 * px + 2 * py + pc
            for t in range(n):
                sem = 7 * t + k - 1
                sends.append(pltpu.make_async_remote_copy(
                    src_ref=win(t, px, py, pc), dst_ref=slot(t, me), send_sem=send_sems.at[sem],
                    recv_sem=recv_sems.at[sem], device_id=(px, py, pc), device_id_type=MESH))
                recvs.append(pltpu.make_async_remote_copy(
                    src_ref=win(t, px, py, pc), dst_ref=slot(t, peer), send_sem=send_sems.at[sem],
                    recv_sem=recv_sems.at[sem], device_id=(px, py, pc), device_id_type=MESH))
        for cp in sends:
            cp.start()
        for cp in recvs:
            cp.wait_recv()
        for cp in sends:
            cp.wait_send()
        for cp in local:
            cp.wait()

    hbm = pl.BlockSpec(memory_space=pl.ANY)
    return pl.pallas_call(
        body, name=name, out_shape=out_shape, in_specs=[hbm] * n, out_specs=[hbm] * (n // depth),
        scratch_shapes=[pltpu.SemaphoreType.DMA((7 * n,)), pltpu.SemaphoreType.DMA((7 * n,)),
                        pltpu.SemaphoreType.DMA((n,))],
    )(*grads)


HBM_SPEC = pl.BlockSpec(memory_space=pltpu.HBM)
SEM_SPEC = pl.BlockSpec(memory_space=pltpu.SEMAPHORE)
SPLIT_COPY = pltpu.SideEffectType.DATAFLOW_SIDE_EFFECTING


def _scatter_zone(g, axis, width, start_me, me):
    own = lax.dynamic_slice_in_dim(g, start_me, width, axis=axis)
    zone = lax.empty((N_DEV,) + own.shape, g.dtype)
    return lax.dynamic_update_slice_in_dim(zone, own[None], me, axis=0)


def _gather_zone(shape, block, axis, start):
    return lax.dynamic_update_slice_in_dim(lax.empty(shape, block.dtype), block, start, axis=axis)


class _Plan:
    def __init__(self, copies, total):
        self.copies, self.total = copies, total


def _scatter_plan(axis, width, start):
    def copies(src, zone, x, y, c):
        out = []
        for k in range(1, N_DEV):
            px, py, pc = _flip(x, k & 4), _flip(y, k & 2), _flip(c, k & 1)
            first = pl.multiple_of(start(px, py, pc), math.gcd(width, 1024))
            out.append((_window(src, axis, first, width), zone.at[4 * x + 2 * y + c], (px, py, pc)))
        return out

    return _Plan(copies, lambda zone: zone.at[pl.ds(0, N_DEV - 1)])


def _gather_plan(axis, width):
    def copies(src, zone, x, y, c):
        mine = lambda ref: _window(ref, axis, width * (4 * x + 2 * y + c), width)
        return [(src, mine(zone), (_flip(x, k & 4), _flip(y, k & 2), _flip(c, k & 1))) for k in range(1, N_DEV)]

    return _Plan(copies, lambda zone: _window(zone, axis, 0, (N_DEV - 1) * width))


def _chip_gather_plan(width):
    def copies(src, zone, x, y, c):
        half = src.shape[0] // 2
        aligned = lambda col: col if isinstance(col, int) else pl.multiple_of(col, LANE)
        mine = lambda ref, col: _window(_window(ref, 1, aligned(col), width), 0, pl.multiple_of(c * half, 8), half)
        out = []
        for px, py in ((1 - x, y), (x, 1 - y), (1 - x, 1 - y)):
            out += [(mine(src, 0), mine(zone, width * (2 * x + y)), (px, py, pc)) for pc in (0, 1)]
        return out

    return _Plan(copies, lambda zone: _window(zone, 1, 0, 3 * width))


def _split_start(srcs, zones, plans, dep, name):
    n = len(srcs)

    def body(*refs):
        ins, lands = refs[:n], refs[n:2 * n]
        send_sems, recv_sems, token = refs[2 * n + 1:3 * n + 1], refs[3 * n + 1:4 * n + 1], refs[-1]
        x, y, c = _my_place()
        for t in range(n):
            for src, dst, target in plans[t].copies(ins[t], lands[t], x, y, c):
                pltpu.make_async_remote_copy(src_ref=src, dst_ref=dst, send_sem=send_sems[t], recv_sem=recv_sems[t],
                                             device_id=target, device_id_type=MESH).start()
        token[...] = jnp.zeros_like(token)

    thru = [pltpu.HBM(a.shape, a.dtype) for a in list(srcs) + list(zones)]
    outs = pl.pallas_call(
        body, name=name,
        out_shape=[pltpu.SemaphoreType.DMA(())] * (2 * n) + thru + [jax.ShapeDtypeStruct((8, LANE), F32)],
        in_specs=[HBM_SPEC] * (2 * n) + [pl.BlockSpec(memory_space=pl.ANY)],
        out_specs=[SEM_SPEC] * (2 * n) + [HBM_SPEC] * (2 * n) + [pl.BlockSpec(memory_space=pltpu.VMEM)],
        input_output_aliases={i: 2 * n + i for i in range(2 * n)},
        compiler_params=pltpu.CompilerParams(has_side_effects=SPLIT_COPY),
    )(*[pltpu.with_memory_space_constraint(a, pltpu.HBM) for a in list(srcs) + list(zones)], dep)
    return (outs[:n], outs[n:2 * n], outs[2 * n:3 * n], outs[3 * n:4 * n], plans), outs[-1]


def _split_wait(pending, after, name):
    send_sems, recv_sems, srcs, zones, plans = pending
    n = len(zones)

    def body(*refs):
        lands, sends, recvs = refs[n:2 * n], refs[2 * n:3 * n], refs[3 * n:4 * n]
        x, y, c = _my_place()
        for t in range(n):
            total = plans[t].total(lands[t])
            done = pltpu.make_async_remote_copy(src_ref=total, dst_ref=total, send_sem=sends[t], recv_sem=recvs[t],
                                                device_id=(x, y, 1 - c), device_id_type=MESH)
            done.wait_send()
            done.wait_recv()

    thru = [pltpu.HBM(a.shape, a.dtype) for a in list(srcs) + list(zones)]
    outs = pl.pallas_call(
        body, name=name, out_shape=thru,
        in_specs=[HBM_SPEC] * (2 * n) + [SEM_SPEC] * (2 * n) + [pl.BlockSpec(memory_space=pl.ANY)],
        out_specs=[HBM_SPEC] * (2 * n), input_output_aliases={i: i for i in range(2 * n)},
        compiler_params=pltpu.CompilerParams(has_side_effects=SPLIT_COPY),
    )(*srcs, *zones, *send_sems, *recv_sems, after)
    return outs[n:]


def _sum_blocks(x3d, name):
    n, rows, cols = x3d.shape
    tr = _tile(rows, (512, 256, 128, 64, 32, 16))

    def body(x_ref, o_ref):
        acc = x_ref[0].astype(F32)
        for j in range(1, n):
            acc = acc + x_ref[j].astype(F32)
        o_ref[...] = acc

    return pl.pallas_call(
        body, name=name, out_shape=jax.ShapeDtypeStruct((rows, cols), F32), grid=(rows // tr,),
        in_specs=[pl.BlockSpec((n, tr, cols), lambda i: (0, i, 0))],
        out_specs=pl.BlockSpec((tr, cols), lambda i: (i, 0)),
        compiler_params=_params("parallel"),
    )(x3d)


def _pack(arrs, dtype, lead=None):
    if lead is None:
        flat = jnp.concatenate([a.reshape(-1).astype(dtype) for a in arrs])
        n = flat.shape[0]
        total = -(-n // (16 * PACK_COLS)) * (16 * PACK_COLS)
        return jnp.pad(flat, (0, total - n)).reshape(-1, PACK_COLS)
    flat = jnp.concatenate([a.reshape(lead, -1).astype(dtype) for a in arrs], axis=1)
    n = flat.shape[1]
    total = -(-n // (16 * PACK_COLS)) * (16 * PACK_COLS)
    return jnp.pad(flat, ((0, 0), (0, total - n))).reshape(lead, -1, PACK_COLS)


def _unpack(packed, shapes, lead=None):
    out, off = [], 0
    if lead is None:
        flat = packed.reshape(-1)
        for s in shapes:
            n = math.prod(s)
            out.append(flat[off:off + n].reshape(s))
            off += n
        return out
    flat = packed.reshape(lead, -1)
    for s in shapes:
        n = math.prod(s)
        out.append(flat[:, off:off + n].reshape((lead,) + tuple(s)))
        off += n
    return out


def _join_blocks(g, axis):
    g = jnp.moveaxis(g, 0, axis)
    s = g.shape
    return g.reshape(s[:axis] + (s[axis] * s[axis + 1],) + s[axis + 2:])


def _mm_tiles(m, n, kdim, a_bytes):
    tk = kdim if kdim <= 2048 else _tile(kdim, (2048, 1664, 1024, 832, 512, 416, 256, 128))
    tm = _tile(m, (1024, 512, 256, 128, 64, 32, 16))
    tn = _tile(n, (1024, 512, 256, 128))

    def vmem(tm, tn):
        return 2 * tm * tk * a_bytes + 2 * tn * tk * 2 + 3 * tm * tn * 4

    while vmem(tm, tn) > VMEM_LIMIT * 3 // 4 and tn > 128 and tn % 256 == 0:
        tn //= 2
    while vmem(tm, tn) > VMEM_LIMIT * 3 // 4 and tm > 128 and tm % 256 == 0:
        tm //= 2
    return tm, tn, tk


def _mm(a, b, *, ta=False, tb=False, bl=None, out_dtype=F32, add=None, dep=None, name):
    if ta:
        kdim, m = a.shape
    else:
        m, kdim = a.shape
    bshape = b.shape if bl is None else b.shape[1:]
    n = bshape[0] if tb else bshape[1]
    tm, tn, tk = _mm_tiles(m, n, kdim, a.dtype.itemsize)
    nk = kdim // tk
    a_spec = pl.BlockSpec((tk, tm), lambda i, j, k: (k, i)) if ta else pl.BlockSpec((tm, tk), lambda i, j, k: (i, k))
    b_blk, b_idx = ((tn, tk), lambda i, j, k: (j, k)) if tb else ((tk, tn), lambda i, j, k: (k, j))
    if bl is None:
        b_spec = pl.BlockSpec(b_blk, b_idx)
    else:
        b_spec = pl.BlockSpec((None,) + b_blk, lambda i, j, k: (bl,) + b_idx(i, j, k))
    o_spec = pl.BlockSpec((tm, tn), lambda i, j, k: (i, j))
    dims = (((0 if ta else 1,), (1 if tb else 0,)), ((), ()))

    n_in = 2 + (add is not None) + (dep is not None)

    def body(*refs):
        a_ref, b_ref = refs[:2]
        add_ref = refs[2] if add is not None else None
        o_ref = refs[n_in]

        def finish(r):
            if add is not None:
                r = r + add_ref[...]
            o_ref[...] = r.astype(out_dtype)

        part = lax.dot_general(a_ref[...].astype(BF16), b_ref[...].astype(BF16), dims, preferred_element_type=F32)
        if nk == 1:
            finish(part)
            return
        acc_ref = refs[-1]
        k = pl.program_id(2)

        @pl.when(k == 0)
        def _():
            acc_ref[...] = part

        @pl.when(k > 0)
        def _():
            acc_ref[...] += part

        @pl.when(k == nk - 1)
        def _():
            finish(acc_ref[...])

    ins, specs = [a, b], [a_spec, b_spec]
    if add is not None:
        ins.append(add)
        specs.append(o_spec)
    if dep is not None:
        ins.append(dep)
        specs.append(pl.BlockSpec(memory_space=pl.ANY))
    return pl.pallas_call(
        body, name=name, out_shape=jax.ShapeDtypeStruct((m, n), out_dtype), grid=(m // tm, n // tn, nk),
        in_specs=specs, out_specs=o_spec, scratch_shapes=[pltpu.VMEM((tm, tn), F32)] if nk > 1 else [],
        compiler_params=_params("parallel", "parallel", "arbitrary"),
    )(*ins)


def _rms_fwd(x, g, name):
    s, d = x.shape
    tr = _tile(s, (256, 128, 64, 32, 16))

    def body(x_ref, g_ref, o_ref):
        xv = x_ref[...]
        r = lax.rsqrt(jnp.mean(xv * xv, axis=-1, keepdims=True) + RMS_EPS)
        o_ref[...] = (xv * r * g_ref[...]).astype(BF16)

    return pl.pallas_call(
        body, name=name, out_shape=jax.ShapeDtypeStruct((s, d), BF16), grid=(s // tr,),
        in_specs=[pl.BlockSpec((tr, d), lambda i: (i, 0)), pl.BlockSpec((1, d), lambda i: (0, 0))],
        out_specs=pl.BlockSpec((tr, d), lambda i: (i, 0)), compiler_params=_params("parallel"),
    )(x, g)


def _rms_bwd(x, g, dh, resid, name):
    s, d = x.shape
    tr = _tile(s, (256, 128, 64, 32, 16))

    def body(*refs):
        if resid is None:
            x_ref, g_ref, dh_ref, dx_ref, dg_ref = refs
        else:
            x_ref, g_ref, dh_ref, res_ref, dx_ref, dg_ref = refs

        @pl.when(pl.program_id(0) == 0)
        def _():
            dg_ref[...] = jnp.zeros_like(dg_ref)

        xv = x_ref[...]
        r = lax.rsqrt(jnp.mean(xv * xv, axis=-1, keepdims=True) + RMS_EPS)
        xh = xv * r
        dhv = dh_ref[...]
        dg_ref[0:1, :] += jnp.sum(dhv * xh, axis=0, keepdims=True)
        dyn = dhv * g_ref[...]
        dx = r * (dyn - xh * jnp.mean(dyn * xh, axis=-1, keepdims=True))
        if resid is not None:
            dx = dx + res_ref[...]
        dx_ref[...] = dx

    row = pl.BlockSpec((tr, d), lambda i: (i, 0))
    ins = [x, g, dh] + ([] if resid is None else [resid])
    specs = [row, pl.BlockSpec((1, d), lambda i: (0, 0)), row] + ([] if resid is None else [row])
    dx, dg = pl.pallas_call(
        body, name=name,
        out_shape=(jax.ShapeDtypeStruct((s, d), F32), jax.ShapeDtypeStruct((8, d), F32)), grid=(s // tr,),
        in_specs=specs, out_specs=(row, pl.BlockSpec((8, d), lambda i: (0, 0))),
        compiler_params=_params("arbitrary"),
    )(*ins)
    return dx, dg[0]


def _loss_bwd(x, g, tgt, name):
    s, d = x.shape
    tr = _tile(s, (256, 128, 64, 32, 16))

    def body(x_ref, g_ref, t_ref, dx_ref, dg_ref, loss_ref):
        @pl.when(pl.program_id(0) == 0)
        def _():
            dg_ref[...] = jnp.zeros_like(dg_ref)
            loss_ref[...] = jnp.zeros_like(loss_ref)

        xv = x_ref[...]
        r = lax.rsqrt(jnp.mean(xv * xv, axis=-1, keepdims=True) + RMS_EPS)
        xh = xv * r
        e = xh * g_ref[...] - t_ref[...]
        per_tok = jnp.mean(e * e, axis=-1, keepdims=True)
        loss_ref[...] += 0.5 * jnp.sum(per_tok, axis=0, keepdims=True)
        dy = e * (1.0 / d)
        dg_ref[0:1, :] += jnp.sum(dy * xh, axis=0, keepdims=True)
        dyn = dy * g_ref[...]
        dx_ref[...] = r * (dyn - xh * jnp.mean(dyn * xh, axis=-1, keepdims=True))

    row = pl.BlockSpec((tr, d), lambda i: (i, 0))
    dx, dg, loss = pl.pallas_call(
        body, name=name,
        out_shape=(jax.ShapeDtypeStruct((s, d), F32), jax.ShapeDtypeStruct((8, d), F32),
                   jax.ShapeDtypeStruct((8, LANE), F32)),
        grid=(s // tr,),
        in_specs=[row, pl.BlockSpec((1, d), lambda i: (0, 0)), row],
        out_specs=(row, pl.BlockSpec((8, d), lambda i: (0, 0)), pl.BlockSpec((8, LANE), lambda i: (0, 0))),
        compiler_params=_params("arbitrary"),
    )(x, g, tgt)
    return loss[0:1, 0:1], dx, dg[0:1]


def _conv_taps(gbuf, w_ref, tt, ntap, lo):
    acc = w_ref[0:1, :] * gbuf[pl.ds(lo, tt), :]
    for k in range(1, ntap):
        acc = acc + w_ref[k:k + 1, :] * gbuf[pl.ds(lo + k, tt), :]
    return acc


def _conv_time_tile(s):
    return _tile(s, (256, 128, 64, 32))


def _conv_fwd(u, wpad, dw_b, ln_g, ln_b, pw, l, name):
    s = u.shape[0]
    c = pw.shape[1]
    ntap = 31
    tt = _conv_time_tile(s)
    hb = tt // CONV_HALO

    def body(val_ref, glu_ref, valh_ref, gluh_ref, w_ref, b_ref, lg_ref, lb_ref, pw_ref, o_ref, gbuf):
        i = pl.program_id(0)
        glh = valh_ref[...] * _sigmoid(gluh_ref[...])
        gbuf[0:CONV_HALO, :] = jnp.where(i > 0, glh, 0.0)
        gbuf[CONV_HALO:CONV_HALO + tt, :] = val_ref[...] * _sigmoid(glu_ref[...])
        acc = _conv_taps(gbuf, w_ref, tt, ntap, CONV_HALO - (ntap - 1)) + b_ref[...]
        xc = acc - jnp.mean(acc, axis=-1, keepdims=True)
        rstd = lax.rsqrt(jnp.mean(xc * xc, axis=-1, keepdims=True) + LN_EPS)
        ln = xc * rstd * lg_ref[...] + lb_ref[...]
        sw = ln * _sigmoid(ln)
        o_ref[...] = _dot(sw.astype(BF16), pw_ref[...])

    vec = pl.BlockSpec((1, c), lambda i: (0, 0))
    return pl.pallas_call(
        body, name=name, out_shape=jax.ShapeDtypeStruct((s, c), F32), grid=(s // tt,),
        in_specs=[pl.BlockSpec((tt, c), lambda i: (i, 0)), pl.BlockSpec((tt, c), lambda i: (i, 1)),
                  pl.BlockSpec((CONV_HALO, c), lambda i: (jnp.maximum(i * hb - 1, 0), 0)),
                  pl.BlockSpec((CONV_HALO, c), lambda i: (jnp.maximum(i * hb - 1, 0), 1)),
                  pl.BlockSpec((32, c), lambda i: (0, 0)), vec, vec, vec,
                  pl.BlockSpec((None, c, c), lambda i: (l, 0, 0))],
        out_specs=pl.BlockSpec((tt, c), lambda i: (i, 0)),
        scratch_shapes=[pltpu.VMEM((CONV_HALO + tt, c), F32)],
        compiler_params=_params("parallel"),
    )(u, u, u, u, wpad, dw_b, ln_g, ln_b, pw)


def _conv_bwd_post(u, dyc, wpad, dw_b, ln_g, ln_b, pw, l, name):
    s = u.shape[0]
    c = pw.shape[1]
    ntap = 31
    tt = _conv_time_tile(s)
    hb = tt // CONV_HALO

    def body(val_ref, glu_ref, valh_ref, gluh_ref, dy_ref, w_ref, b_ref, lg_ref, lb_ref, pw_ref,
             dd_ref, gl_ref, dpw_ref, vec_ref, gbuf):
        i = pl.program_id(0)

        @pl.when(i == 0)
        def _():
            dpw_ref[...] = jnp.zeros_like(dpw_ref)
            vec_ref[...] = jnp.zeros_like(vec_ref)

        glh = valh_ref[...] * _sigmoid(gluh_ref[...])
        gbuf[0:CONV_HALO, :] = jnp.where(i > 0, glh, 0.0)
        gl = val_ref[...] * _sigmoid(glu_ref[...])
        gbuf[CONV_HALO:CONV_HALO + tt, :] = gl
        gl_ref[...] = gl
        acc = _conv_taps(gbuf, w_ref, tt, ntap, CONV_HALO - (ntap - 1)) + b_ref[...]
        xc = acc - jnp.mean(acc, axis=-1, keepdims=True)
        rstd = lax.rsqrt(jnp.mean(xc * xc, axis=-1, keepdims=True) + LN_EPS)
        xh = xc * rstd
        ln = xh * lg_ref[...] + lb_ref[...]
        sig = _sigmoid(ln)
        sw = ln * sig
        dyb = dy_ref[...].astype(BF16)
        dpw_ref[...] += _dot_tn(sw.astype(BF16), dyb)
        dsw = _dot_nt(dyb, pw_ref[...])
        dln = dsw * (sig * (1.0 + ln * (1.0 - sig)))
        vec_ref[0:1, :] += jnp.sum(dln * xh, axis=0, keepdims=True)
        vec_ref[1:2, :] += jnp.sum(dln, axis=0, keepdims=True)
        dxh = dln * lg_ref[...]
        dd = rstd * (dxh - jnp.mean(dxh, axis=-1, keepdims=True)
                     - xh * jnp.mean(dxh * xh, axis=-1, keepdims=True))
        vec_ref[2:3, :] += jnp.sum(dd, axis=0, keepdims=True)
        dd_ref[...] = dd

    vec = pl.BlockSpec((1, c), lambda i: (0, 0))
    tile = pl.BlockSpec((tt, c), lambda i: (i, 0))
    return pl.pallas_call(
        body, name=name,
        out_shape=(jax.ShapeDtypeStruct((s, c), F32), jax.ShapeDtypeStruct((s, c), F32),
                   jax.ShapeDtypeStruct((c, c), F32), jax.ShapeDtypeStruct((8, c), F32)),
        grid=(s // tt,),
        in_specs=[tile, pl.BlockSpec((tt, c), lambda i: (i, 1)),
                  pl.BlockSpec((CONV_HALO, c), lambda i: (jnp.maximum(i * hb - 1, 0), 0)),
                  pl.BlockSpec((CONV_HALO, c), lambda i: (jnp.maximum(i * hb - 1, 0), 1)),
                  tile, pl.BlockSpec((32, c), lambda i: (0, 0)), vec, vec, vec,
                  pl.BlockSpec((None, c, c), lambda i: (l, 0, 0))],
        out_specs=(tile, tile, pl.BlockSpec((c, c), lambda i: (0, 0)), pl.BlockSpec((8, c), lambda i: (0, 0))),
        scratch_shapes=[pltpu.VMEM((CONV_HALO + tt, c), F32)],
        compiler_params=_params("arbitrary"),
    )(u, u, u, u, dyc, wpad, dw_b, ln_g, ln_b, pw)


def _conv_bwd_dw(u, dd, gl, wpad, name):
    s, c = dd.shape
    ntap = 31
    tt = _conv_time_tile(s)
    hb = tt // CONV_HALO
    nt = s // tt
    last_halo = s // CONV_HALO - 1

    def body(val_ref, glu_ref, dd_ref, ddn_ref, gl_ref, glh_ref, w_ref, dval_ref, dglu_ref, dw_ref, dbuf, gbuf):
        i = pl.program_id(0)

        @pl.when(i == 0)
        def _():
            dw_ref[...] = jnp.zeros_like(dw_ref)

        d = dd_ref[...]
        dbuf[0:tt, :] = d
        dbuf[tt:tt + CONV_HALO, :] = jnp.where(i < nt - 1, ddn_ref[...], 0.0)
        gbuf[0:CONV_HALO, :] = jnp.where(i > 0, glh_ref[...], 0.0)
        gbuf[CONV_HALO:CONV_HALO + tt, :] = gl_ref[...]
        dgl = w_ref[0:1, :] * dbuf[pl.ds(ntap - 1, tt), :]
        for k in range(1, ntap):
            dgl = dgl + w_ref[k:k + 1, :] * dbuf[pl.ds(ntap - 1 - k, tt), :]
        for k in range(ntap):
            dw_ref[k:k + 1, :] += jnp.sum(d * gbuf[pl.ds(CONV_HALO - (ntap - 1) + k, tt), :], axis=0, keepdims=True)
        sg = _sigmoid(glu_ref[...])
        dval_ref[...] = (dgl * sg).astype(BF16)
        dglu_ref[...] = (dgl * val_ref[...] * sg * (1.0 - sg)).astype(BF16)

    tile = pl.BlockSpec((tt, c), lambda i: (i, 0))
    return pl.pallas_call(
        body, name=name,
        out_shape=(jax.ShapeDtypeStruct((s, c), BF16), jax.ShapeDtypeStruct((s, c), BF16),
                   jax.ShapeDtypeStruct((32, c), F32)),
        grid=(nt,),
        in_specs=[tile, pl.BlockSpec((tt, c), lambda i: (i, 1)), tile,
                  pl.BlockSpec((CONV_HALO, c), lambda i: (jnp.minimum((i + 1) * hb, last_halo), 0)),
                  tile, pl.BlockSpec((CONV_HALO, c), lambda i: (jnp.maximum(i * hb - 1, 0), 0)),
                  pl.BlockSpec((32, c), lambda i: (0, 0))],
        out_specs=(tile, tile, pl.BlockSpec((32, c), lambda i: (0, 0))),
        scratch_shapes=[pltpu.VMEM((tt + CONV_HALO, c), F32), pltpu.VMEM((CONV_HALO + tt, c), F32)],
        compiler_params=_params("arbitrary"),
    )(u, u, dd, dd, gl, gl, wpad)


SB_ROWS = 32


def _tri(n, cmp):
    r = lax.broadcasted_iota(jnp.int32, (n, n), 0)
    c = lax.broadcasted_iota(jnp.int32, (n, n), 1)
    return jnp.where(cmp(r, c), 1.0, 0.0).astype(BF16)


def _row_chunks(fn, n, *arrs):
    outs = [fn(*[a[r:r + SB_ROWS] for a in arrs]) for r in range(0, n, SB_ROWS)]
    return tuple(jnp.concatenate(list(o), axis=0) for o in zip(*outs))


def _hi_lo(v):
    hi = v.astype(BF16)
    return hi, (v - hi.astype(F32)).astype(BF16)


def _sb_sticks(z, causal, scale):
    z = z * scale
    l1p = jnp.log(1.0 + jnp.exp(-jnp.abs(z)))
    lb = jnp.minimum(z, 0.0) - l1p
    ell = lb - z
    if causal is not None:
        ell = jnp.where(causal, ell, 0.0)
    hi, lo = _hi_lo(ell)
    return lb, hi, lo, jnp.sum(ell, axis=1, keepdims=True)


def _sb_fwd(u, heads, q_blk, k_blk, v_blk, name):
    s = u.shape[0]
    tq = _tile(s, (256, 128))
    scale = HEAD ** -0.5

    def body(q_ref, k_ref, v_ref, o_ref, tot_ref):
        i = pl.program_id(1)
        qb = q_ref[...].astype(BF16)
        t_sfx = _tri(tq, lambda r, c: r > c)
        below_diag = lax.broadcasted_iota(jnp.int32, (tq, tq), 1) < lax.broadcasted_iota(jnp.int32, (tq, tq), 0)

        def blocks(j0, nb, c_a, acc, diag):
            mask = [below_diag] if diag else []
            rows = pl.ds(pl.multiple_of(j0 * tq, tq), nb * tq)
            kb = k_ref[rows, :].astype(BF16)
            vb = v_ref[rows, :].astype(BF16)
            z = _dot_nt(qb, kb)

            def sticks(zc, *m):
                out = []
                for b in range(nb):
                    out += _sb_sticks(zc[:, b * tq:(b + 1) * tq], m[0] if m else None, scale)
                return tuple(out)

            st = _row_chunks(sticks, tq, z, *mask)
            lb, hi, lo, rs = st[0::4], st[1::4], st[2::4], st[3::4]
            sfx = [_dot(hi[b], t_sfx) + _dot(lo[b], t_sfx) for b in range(nb)]
            before, run = [None] * nb, c_a
            for b in reversed(range(nb)):
                before[b], run = run, run + rs[b]

            def weights(*a):
                ws = []
                for b in range(nb):
                    lbc, sfxc, befc = a[3 * b:3 * b + 3]
                    w = jnp.exp(lbc + (befc + sfxc))
                    if diag:
                        w = jnp.where(a[-1], w, 0.0)
                    ws.append(w.astype(BF16))
                return (ws[0] if nb == 1 else jnp.concatenate(ws, axis=1),)

            flat = [v for b in range(nb) for v in (lb[b], sfx[b], before[b])]
            wb, = _row_chunks(weights, tq, *flat, *mask)
            return run, acc + _dot(wb, vb)

        carry = blocks(i, 1, jnp.zeros((tq, 1), F32), jnp.zeros((tq, HEAD), F32), True)
        carry = lax.fori_loop(0, i // 4, lambda t, cr: blocks(i - 4 - 4 * t, 4, *cr, False), carry)
        carry = lax.fori_loop(0, (i % 4) // 2, lambda _, cr: blocks(i % 2, 2, *cr, False), carry)
        c_a, acc = lax.fori_loop(0, i % 2, lambda _, cr: blocks(0, 1, *cr, False), carry)
        o_ref[...] = acc
        tot_ref[...] = jnp.broadcast_to(c_a, (tq, HEAD))

    full = lambda off: pl.BlockSpec((s, HEAD), lambda h, i: (0, off + h))
    out = pl.BlockSpec((tq, HEAD), lambda h, i: (i, h))
    return pl.pallas_call(
        body, name=name,
        out_shape=(jax.ShapeDtypeStruct((s, heads * HEAD), F32), jax.ShapeDtypeStruct((s, heads * HEAD), F32)),
        grid=(heads, s // tq),
        in_specs=[pl.BlockSpec((tq, HEAD), lambda h, i: (i, q_blk + h)), full(k_blk), full(v_blk)],
        out_specs=(out, out),
        compiler_params=_params("parallel", "parallel"),
    )(u, u, u)


def _sb_bwd(u, tot, dy, heads, q_blk, k_blk, v_blk, name):
    s = u.shape[0]
    tq = _tile(s, (256, 128))
    scale = HEAD ** -0.5

    def body(q_ref, k_ref, v_ref, tot_ref, dy_ref, dq_ref, dk_ref, dv_ref):
        i = pl.program_id(1)

        @pl.when(i == 0)
        def _():
            dk_ref[...] = jnp.zeros_like(dk_ref)
            dv_ref[...] = jnp.zeros_like(dv_ref)

        qb = q_ref[...].astype(BF16)
        dob = dy_ref[...].astype(BF16)
        total = tot_ref[:, 0:1]
        t_incl = _tri(tq, lambda r, c: r <= c)
        t_excl = _tri(tq, lambda r, c: r < c)
        below_diag = lax.broadcasted_iota(jnp.int32, (tq, tq), 1) < lax.broadcasted_iota(jnp.int32, (tq, tq), 0)

        def blocks(j0, nb, c_p, c_g, dq, diag):
            mask = [below_diag] if diag else []
            rows = pl.ds(pl.multiple_of(j0 * tq, tq), nb * tq)
            kb = k_ref[rows, :].astype(BF16)
            vb = v_ref[rows, :].astype(BF16)
            z = _dot_nt(qb, kb)
            dw = _dot_nt(dob, vb)
            cols = lambda a, b: a[:, b * tq:(b + 1) * tq]

            def sticks(zc, *m):
                out = []
                for b in range(nb):
                    out += _sb_sticks(cols(zc, b), m[0] if m else None, scale)
                return tuple(out)

            st = _row_chunks(sticks, tq, z, *mask)
            lb, hi, lo, rs_l = st[0::4], st[1::4], st[2::4], st[3::4]
            pfx = [_dot(hi[b], t_incl) + _dot(lo[b], t_incl) for b in range(nb)]
            p_before = [c_p]
            for b in range(nb):
                p_before.append(p_before[-1] + rs_l[b])

            def weights(totc, dwc, *a):
                out = []
                for b in range(nb):
                    lbc, pfxc, pbc = a[3 * b:3 * b + 3]
                    w = jnp.exp(lbc + (totc - (pbc + pfxc)))
                    if diag:
                        w = jnp.where(a[-1], w, 0.0)
                    g = w * cols(dwc, b)
                    out += [w.astype(BF16), g, *_hi_lo(g), jnp.sum(g, axis=1, keepdims=True)]
                return tuple(out)

            flat = [v for b in range(nb) for v in (lb[b], pfx[b], p_before[b])]
            wt = _row_chunks(weights, tq, total, dw, *flat, *mask)
            wb, g, ghi, glo, rs_g = wt[0::5], wt[1::5], wt[2::5], wt[3::5], wt[4::5]
            g_pre = [_dot(ghi[b], t_excl) + _dot(glo[b], t_excl) for b in range(nb)]
            g_before = [c_g]
            for b in range(nb):
                g_before.append(g_before[-1] + rs_g[b])

            def dscore(*a):
                dzs = []
                for b in range(nb):
                    lbc, gc, gprec, gbc = a[4 * b:4 * b + 4]
                    beta = jnp.exp(lbc)
                    dz = (gc * (1.0 - beta) - (gbc + gprec) * beta) * scale
                    if diag:
                        dz = jnp.where(a[-1], dz, 0.0)
                    dzs.append(dz.astype(BF16))
                return (dzs[0] if nb == 1 else jnp.concatenate(dzs, axis=1),)

            flat = [v for b in range(nb) for v in (lb[b], g[b], g_pre[b], g_before[b])]
            dzb, = _row_chunks(dscore, tq, *flat, *mask)
            wcat = wb[0] if nb == 1 else jnp.concatenate(wb, axis=1)
            dk_ref[rows, :] += _dot_tn(dzb, qb)
            dv_ref[rows, :] += _dot_tn(wcat, dob)
            return p_before[-1], g_before[-1], dq + _dot(dzb, kb)

        zero = jnp.zeros((tq, 1), F32)
        carry = lax.fori_loop(0, i // 2, lambda t, cr: blocks(2 * t, 2, *cr, False),
                              (zero, zero, jnp.zeros((tq, HEAD), F32)))
        carry = lax.fori_loop(0, i % 2, lambda _, cr: blocks(i - 1, 1, *cr, False), carry)
        _, _, dq = blocks(i, 1, *carry, True)
        dq_ref[...] = dq

    full = lambda off: pl.BlockSpec((s, HEAD), lambda h, i: (0, off + h))
    blk = pl.BlockSpec((tq, HEAD), lambda h, i: (i, h))
    acc = pl.BlockSpec((s, HEAD), lambda h, i: (0, h))
    shape = jax.ShapeDtypeStruct((s, heads * HEAD), F32)
    return pl.pallas_call(
        body, name=name, out_shape=(shape, shape, shape), grid=(heads, s // tq),
        in_specs=[pl.BlockSpec((tq, HEAD), lambda h, i: (i, q_blk + h)), full(k_blk), full(v_blk), blk, blk],
        out_specs=(blk, acc, acc),
        compiler_params=_params("parallel", "arbitrary"),
    )(u, u, u, tot, dy)


def _lru_time_tile(s):
    return _tile(s, (256, 128, 64, 32))


def _lru_gates(xc, wa_ref, ba_ref, wx_ref, bx_ref, lam_ref, nh):
    pr, pi = [], []
    for n in range(nh):
        xn = xc[:, n * HEAD:(n + 1) * HEAD].astype(BF16)
        pr.append(_dot(xn, wa_ref[n]))
        pi.append(_dot(xn, wx_ref[n]))
    r = _sigmoid((pr[0] if nh == 1 else jnp.concatenate(pr, axis=1)) + ba_ref[...])
    ig = _sigmoid((pi[0] if nh == 1 else jnp.concatenate(pi, axis=1)) + bx_ref[...])
    lam = lam_ref[...]
    sp = jnp.maximum(-lam, 0.0) + jnp.log(1.0 + jnp.exp(-jnp.abs(lam)))
    log_a = -LRU_C * r * sp
    a = jnp.exp(log_a)
    mult = jnp.sqrt(-_expm1(2.0 * log_a))
    return r, ig, a, mult, sp


def _lru_fwd(u, x_blk, cw, cb, wa, ba, wx, bx, lam, name):
    s = u.shape[0]
    w = lam.shape[1]
    nh = w // HEAD
    tt = _lru_time_tile(s)
    hb = tt // LRU_HALO

    def body(x_ref, xh_ref, cw_ref, cb_ref, wa_ref, ba_ref, wx_ref, bx_ref, lam_ref, y_ref,
             xbuf, abuf, bbuf, hstate, rowbuf):
        i = pl.program_id(0)

        @pl.when(i == 0)
        def _():
            hstate[...] = jnp.zeros_like(hstate)

        xbuf[0:LRU_HALO, :] = jnp.where(i > 0, xh_ref[...], 0.0)
        xbuf[LRU_HALO:LRU_HALO + tt, :] = x_ref[...]
        xc = _conv_taps(xbuf, cw_ref, tt, 4, LRU_HALO - 3) + cb_ref[...]
        _, ig, a, mult, _ = _lru_gates(xc, wa_ref, ba_ref, wx_ref, bx_ref, lam_ref, nh)
        abuf[...] = a
        bbuf[...] = mult * (ig * xc)

        def group(gi, h):
            rows = pl.ds(pl.multiple_of(gi * 8, 8), 8)
            a8 = abuf[rows, :]
            b8 = bbuf[rows, :]
            for j in range(8):
                h = a8[j:j + 1, :] * h + b8[j:j + 1, :]
                rowbuf[j:j + 1, :] = h
            y_ref[rows, :] = rowbuf[...]
            return h

        hstate[0:1, :] = lax.fori_loop(0, tt // 8, group, hstate[0:1, :])

    vec = pl.BlockSpec((1, w), lambda i: (0, 0))
    gate = pl.BlockSpec((nh, HEAD, HEAD), lambda i: (0, 0, 0))
    return pl.pallas_call(
        body, name=name, out_shape=jax.ShapeDtypeStruct((s, w), F32), grid=(s // tt,),
        in_specs=[pl.BlockSpec((tt, w), lambda i: (i, x_blk)),
                  pl.BlockSpec((LRU_HALO, w), lambda i: (jnp.maximum(i * hb - 1, 0), x_blk)),
                  pl.BlockSpec((8, w), lambda i: (0, 0)), vec, gate, vec, gate, vec, vec],
        out_specs=pl.BlockSpec((tt, w), lambda i: (i, 0)),
        scratch_shapes=[pltpu.VMEM((LRU_HALO + tt, w), F32), pltpu.VMEM((tt, w), F32), pltpu.VMEM((tt, w), F32),
                        pltpu.VMEM((8, w), F32), pltpu.VMEM((8, w), F32)],
        compiler_params=_params("arbitrary"),
    )(u, u, cw, cb, wa, ba, wx, bx, lam)


def _lru_bwd(u, x_blk, hseq, dy, cw, cb, wa, ba, wx, bx, lam, name):
    s = u.shape[0]
    w = lam.shape[1]
    nh = w // HEAD
    tt = _lru_time_tile(s)
    hb = tt // LRU_HALO
    nt = s // tt

    def body(x_ref, xh_ref, h_ref, hh_ref, dy_ref, cw_ref, cb_ref, wa_ref, ba_ref, wx_ref, bx_ref, lam_ref,
             dx_ref, dwa_ref, dwx_ref, vec_ref, xbuf, hbuf, abuf, lbuf, dbuf, cstate, dhalo, rowbuf):
        i = pl.program_id(0)
        rt = nt - 1 - i

        @pl.when(i == 0)
        def _():
            cstate[...] = jnp.zeros_like(cstate)
            dhalo[...] = jnp.zeros_like(dhalo)
            dwa_ref[...] = jnp.zeros_like(dwa_ref)
            dwx_ref[...] = jnp.zeros_like(dwx_ref)
            vec_ref[...] = jnp.zeros_like(vec_ref)

        xbuf[0:LRU_HALO, :] = jnp.where(rt > 0, xh_ref[...], 0.0)
        xbuf[LRU_HALO:LRU_HALO + tt, :] = x_ref[...]
        hbuf[0:LRU_HALO, :] = jnp.where(rt > 0, hh_ref[...], 0.0)
        hbuf[LRU_HALO:LRU_HALO + tt, :] = h_ref[...]
        xc = _conv_taps(xbuf, cw_ref, tt, 4, LRU_HALO - 3) + cb_ref[...]
        r, ig, a, mult, sp = _lru_gates(xc, wa_ref, ba_ref, wx_ref, bx_ref, lam_ref, nh)
        abuf[...] = a

        def group(gi, c):
            rows = pl.ds(pl.multiple_of((tt // 8 - 1 - gi) * 8, 8), 8)
            a8 = abuf[rows, :]
            d8 = dy_ref[rows, :]
            for j in range(7, -1, -1):
                lam_t = d8[j:j + 1, :] + c
                rowbuf[j:j + 1, :] = lam_t
                c = a8[j:j + 1, :] * lam_t
            lbuf[rows, :] = rowbuf[...]
            return c

        cstate[0:1, :] = lax.fori_loop(0, tt // 8, group, cstate[0:1, :])

        lam_t = lbuf[...]
        hprev = hbuf[pl.ds(LRU_HALO - 1, tt), :]
        ixc = ig * xc
        d_ixc = lam_t * mult
        d_ig = d_ixc * xc
        dxc = d_ixc * ig
        dlog_a = lam_t * hprev * a + lam_t * ixc * (-(a * a) / mult)
        dr = dlog_a * (-LRU_C * sp)
        lam_p = lam_ref[...]
        dsp = -_sigmoid(-lam_p)
        vec_ref[2:3, :] += jnp.sum(dlog_a * (-LRU_C * r), axis=0, keepdims=True) * dsp
        dpr = dr * r * (1.0 - r)
        dpi = d_ig * ig * (1.0 - ig)
        vec_ref[0:1, :] += jnp.sum(dpr, axis=0, keepdims=True)
        vec_ref[1:2, :] += jnp.sum(dpi, axis=0, keepdims=True)
        parts = []
        for n in range(nh):
            sl = slice(n * HEAD, (n + 1) * HEAD)
            xn = xc[:, sl].astype(BF16)
            dprn = dpr[:, sl].astype(BF16)
            dpin = dpi[:, sl].astype(BF16)
            dwa_ref[n] += _dot_tn(xn, dprn)
            dwx_ref[n] += _dot_tn(xn, dpin)
            parts.append(_dot_nt(dprn, wa_ref[n]) + _dot_nt(dpin, wx_ref[n]))
        dxc = dxc + (parts[0] if nh == 1 else jnp.concatenate(parts, axis=1))
        vec_ref[3:4, :] += jnp.sum(dxc, axis=0, keepdims=True)
        dbuf[0:tt, :] = dxc
        dbuf[tt:tt + LRU_HALO, :] = dhalo[...]
        dx = cw_ref[0:1, :] * dbuf[pl.ds(3, tt), :]
        for k in range(1, 4):
            dx = dx + cw_ref[k:k + 1, :] * dbuf[pl.ds(3 - k, tt), :]
        dx_ref[...] = dx.astype(BF16)
        for k in range(4):
            vec_ref[4 + k:5 + k, :] += jnp.sum(dxc * xbuf[pl.ds(LRU_HALO - 3 + k, tt), :], axis=0, keepdims=True)
        dhalo[...] = dbuf[0:LRU_HALO, :]

    vec = pl.BlockSpec((1, w), lambda i: (0, 0))
    gate = pl.BlockSpec((nh, HEAD, HEAD), lambda i: (0, 0, 0))
    rev = lambda i: nt - 1 - i
    tile = pl.BlockSpec((tt, w), lambda i: (rev(i), 0))
    halo = lambda col: pl.BlockSpec((LRU_HALO, w), lambda i: (jnp.maximum(rev(i) * hb - 1, 0), col))
    return pl.pallas_call(
        body, name=name,
        out_shape=(jax.ShapeDtypeStruct((s, w), BF16), jax.ShapeDtypeStruct((nh, HEAD, HEAD), F32),
                   jax.ShapeDtypeStruct((nh, HEAD, HEAD), F32), jax.ShapeDtypeStruct((8, w), F32)),
        grid=(nt,),
        in_specs=[pl.BlockSpec((tt, w), lambda i: (rev(i), x_blk)), halo(x_blk), tile, halo(0), tile,
                  pl.BlockSpec((8, w), lambda i: (0, 0)), vec, gate, vec, gate, vec, vec],
        out_specs=(tile, gate, gate, pl.BlockSpec((8, w), lambda i: (0, 0))),
        scratch_shapes=[pltpu.VMEM((LRU_HALO + tt, w), F32), pltpu.VMEM((LRU_HALO + tt, w), F32),
                        pltpu.VMEM((tt, w), F32), pltpu.VMEM((tt, w), F32), pltpu.VMEM((tt + LRU_HALO, w), F32),
                        pltpu.VMEM((8, w), F32), pltpu.VMEM((8, w), F32), pltpu.VMEM((8, w), F32)],
        compiler_params=_params("arbitrary"),
    )(u, u, hseq, hseq, dy, cw, cb, wa, ba, wx, bx, lam)


def _gate_specs(c, tr):
    return [pl.BlockSpec((tr, c), lambda i, b=b: (i, b)) for b in (2, 9, 10, 12)]


def _outgate_fwd(y_conv, y_attn, y_lru, u, n_conv, n_attn, n_lru, name):
    s, c = y_conv.shape
    tr = _tile(s, (256, 128, 64, 32, 16))

    def body(yc_ref, ya_ref, yl_ref, gc_ref, ga0_ref, ga1_ref, gl_ref, nc_ref, na_ref, nl_ref, o_ref):
        def rinv(v):
            return lax.rsqrt(jnp.mean(v * v, axis=-1, keepdims=True) + RMS_EPS)

        def silu(g):
            return g * _sigmoid(g)

        yc = yc_ref[...]
        o_ref[:, 0:c] = (yc * rinv(yc) * nc_ref[...] * silu(gc_ref[...])).astype(BF16)
        ya = ya_ref[...]
        ra = rinv(ya)
        o_ref[:, c:2 * c] = (ya[:, 0:c] * ra * na_ref[:, 0:c] * silu(ga0_ref[...])).astype(BF16)
        o_ref[:, 2 * c:3 * c] = (ya[:, c:2 * c] * ra * na_ref[:, c:2 * c] * silu(ga1_ref[...])).astype(BF16)
        yl = yl_ref[...]
        o_ref[:, 3 * c:4 * c] = (yl * rinv(yl) * nl_ref[...] * silu(gl_ref[...])).astype(BF16)

    row = lambda wd: pl.BlockSpec((tr, wd), lambda i: (i, 0))
    vec = lambda wd: pl.BlockSpec((1, wd), lambda i: (0, 0))
    return pl.pallas_call(
        body, name=name, out_shape=jax.ShapeDtypeStruct((s, 4 * c), BF16), grid=(s // tr,),
        in_specs=[row(c), row(2 * c), row(c)] + _gate_specs(c, tr) + [vec(c), vec(2 * c), vec(c)],
        out_specs=row(4 * c), compiler_params=_params("parallel"),
    )(y_conv, y_attn, y_lru, u, u, u, u, n_conv, n_attn, n_lru)


def _outgate_bwd(dy, y_conv, y_attn, y_lru, u, n_conv, n_attn, n_lru, name):
    s, c = y_conv.shape
    tr = _tile(s, (256, 128, 64, 32, 16))

    def body(dy_ref, yc_ref, ya_ref, yl_ref, gc_ref, ga0_ref, ga1_ref, gl_ref, nc_ref, na_ref, nl_ref,
             dyc_ref, dya_ref, dyl_ref, dgc_ref, dga_ref, dgl_ref, dn_ref):
        @pl.when(pl.program_id(0) == 0)
        def _():
            dn_ref[...] = jnp.zeros_like(dn_ref)

        def group(yv, gate, wv, d):
            r = lax.rsqrt(jnp.mean(yv * yv, axis=-1, keepdims=True) + RMS_EPS)
            yh = yv * r
            sg = _sigmoid(gate)
            dn = d * (gate * sg)
            dgate = d * (yh * wv) * (sg * (1.0 + gate * (1.0 - sg)))
            dw = jnp.sum(dn * yh, axis=0, keepdims=True)
            dyn = dn * wv
            dyv = r * (dyn - yh * jnp.mean(dyn * yh, axis=-1, keepdims=True))
            return dyv, dgate, dw

        dyv, dg, dw = group(yc_ref[...], gc_ref[...], nc_ref[...], dy_ref[:, 0:c])
        dyc_ref[...] = dyv
        dgc_ref[...] = dg.astype(BF16)
        dn_ref[0:1, 0:c] += dw
        gate_a = jnp.concatenate([ga0_ref[...], ga1_ref[...]], axis=1)
        dyv, dg, dw = group(ya_ref[...], gate_a, na_ref[...], dy_ref[:, c:3 * c])
        dya_ref[...] = dyv
        dga_ref[...] = dg.astype(BF16)
        dn_ref[0:1, c:3 * c] += dw
        dyv, dg, dw = group(yl_ref[...], gl_ref[...], nl_ref[...], dy_ref[:, 3 * c:4 * c])
        dyl_ref[...] = dyv
        dgl_ref[...] = dg.astype(BF16)
        dn_ref[0:1, 3 * c:4 * c] += dw

    row = lambda wd: pl.BlockSpec((tr, wd), lambda i: (i, 0))
    vec = lambda wd: pl.BlockSpec((1, wd), lambda i: (0, 0))
    sh = lambda wd, dt: jax.ShapeDtypeStruct((s, wd), dt)
    return pl.pallas_call(
        body, name=name,
        out_shape=(sh(c, F32), sh(2 * c, F32), sh(c, F32), sh(c, BF16), sh(2 * c, BF16), sh(c, BF16),
                   jax.ShapeDtypeStruct((8, 4 * c), F32)),
        grid=(s // tr,),
        in_specs=[row(4 * c), row(c), row(2 * c), row(c)] + _gate_specs(c, tr) + [vec(c), vec(2 * c), vec(c)],
        out_specs=(row(c), row(2 * c), row(c), row(c), row(2 * c), row(c),
                   pl.BlockSpec((8, 4 * c), lambda i: (0, 0))),
        compiler_params=_params("arbitrary"),
    )(dy, y_conv, y_attn, y_lru, u, u, u, u, n_conv, n_attn, n_lru)


def _xattn_probs(qh, kh, scale):
    sc = _dot_nt(qh, kh) * scale
    p = jnp.exp(sc - jnp.max(sc, axis=-1, keepdims=True))
    return p / jnp.sum(p, axis=-1, keepdims=True)


def _xattn_fwd(q, kv, name):
    s, xw = q.shape
    m = kv.shape[0]
    nh = xw // HEAD
    tq = _tile(s, (256, 128, 64, 32, 16))
    scale = HEAD ** -0.5

    def body(q_ref, kv_ref, o_ref):
        for h in range(nh):
            qh = q_ref[:, h * HEAD:(h + 1) * HEAD].astype(BF16)
            kh = kv_ref[:, h * HEAD:(h + 1) * HEAD].astype(BF16)
            vh = kv_ref[:, xw + h * HEAD:xw + (h + 1) * HEAD].astype(BF16)
            p = _xattn_probs(qh, kh, scale)
            o_ref[:, h * HEAD:(h + 1) * HEAD] = _dot(p.astype(BF16), vh).astype(BF16)

    return pl.pallas_call(
        body, name=name, out_shape=jax.ShapeDtypeStruct((s, xw), BF16), grid=(s // tq,),
        in_specs=[pl.BlockSpec((tq, xw), lambda i: (i, 0)), pl.BlockSpec((m, 2 * xw), lambda i: (0, 0))],
        out_specs=pl.BlockSpec((tq, xw), lambda i: (i, 0)), compiler_params=_params("parallel"),
    )(q, kv)


def _xattn_bwd(q, kv, do, name):
    s, xw = q.shape
    m = kv.shape[0]
    nh = xw // HEAD
    tq = _tile(s, (256, 128, 64, 32, 16))
    scale = HEAD ** -0.5

    def body(q_ref, kv_ref, do_ref, dq_ref, dkv_ref):
        @pl.when(pl.program_id(0) == 0)
        def _():
            dkv_ref[...] = jnp.zeros_like(dkv_ref)

        for h in range(nh):
            ks = slice(h * HEAD, (h + 1) * HEAD)
            vs = slice(xw + h * HEAD, xw + (h + 1) * HEAD)
            qh = q_ref[:, ks].astype(BF16)
            kh = kv_ref[:, ks].astype(BF16)
            vh = kv_ref[:, vs].astype(BF16)
            doh = do_ref[:, ks].astype(BF16)
            p = _xattn_probs(qh, kh, scale)
            dkv_ref[:, vs] += _dot_tn(p.astype(BF16), doh)
            dp = _dot_nt(doh, vh)
            ds = (p * (dp - jnp.sum(dp * p, axis=-1, keepdims=True)) * scale).astype(BF16)
            dq_ref[:, ks] = _dot(ds, kh).astype(BF16)
            dkv_ref[:, ks] += _dot_tn(ds, qh)

    row = pl.BlockSpec((tq, xw), lambda i: (i, 0))
    full = pl.BlockSpec((m, 2 * xw), lambda i: (0, 0))
    return pl.pallas_call(
        body, name=name,
        out_shape=(jax.ShapeDtypeStruct((s, xw), BF16), jax.ShapeDtypeStruct((m, 2 * xw), F32)), grid=(s // tq,),
        in_specs=[row, full, row], out_specs=(row, full), compiler_params=_params("arbitrary"),
    )(q, kv, do)


def _adamw(w, g, m, v, name):
    rows, cols = w.shape
    tr = _tile(rows, (512, 256, 128, 64, 32, 16, 8)) if rows % 8 == 0 else rows
    bc1 = 1.0 - ADAM_B1 ** ADAM_STEP
    bc2 = 1.0 - ADAM_B2 ** ADAM_STEP

    def body(w_ref, g_ref, m_ref, v_ref, d_ref, nm_ref, nv_ref):
        gv = g_ref[...]
        nm = ADAM_B1 * m_ref[...] + (1.0 - ADAM_B1) * gv
        nv = ADAM_B2 * v_ref[...] + (1.0 - ADAM_B2) * (gv * gv)
        nm_ref[...] = nm
        nv_ref[...] = nv
        d_ref[...] = -ADAM_LR * ((nm / bc1) / (jnp.sqrt(nv / bc2) + ADAM_EPS) + ADAM_WD * w_ref[...])

    spec = pl.BlockSpec((tr, cols), lambda i: (i, 0))
    sh = jax.ShapeDtypeStruct((rows, cols), F32)
    return pl.pallas_call(
        body, name=name, out_shape=(sh, sh, sh), grid=(rows // tr,), in_specs=[spec] * 4,
        out_specs=(spec, spec, spec), compiler_params=_params("parallel"),
    )(w, g, m, v)


WEIGHTS = ['mix_norm_g', 'w_in', 'conv_dw_w', 'conv_dw_b', 'conv_ln_g', 'conv_ln_b', 'conv_pw_w', 'lru_conv_w',
           'lru_conv_b', 'lru_wa', 'lru_ba', 'lru_wx', 'lru_bx', 'lru_lambda', 'out_norm_conv', 'out_norm_attn',
           'out_norm_lru', 'w_out', 'xattn_norm_g', 'mem_norm_g', 'xattn_wq', 'xattn_wkv', 'xattn_wo',
           'final_norm_g']
BIG_SHARDED = {'w_in': 2, 'conv_pw_w': 1, 'w_out': 1, 'xattn_wq': 1, 'xattn_wkv': 1, 'xattn_wo': 2}
SMALL_SHARDED = {'conv_dw_w': 2, 'lru_conv_w': 2}


def _layer_fwd(x, mem, p, l, first, rest):
    row = lambda name: p[name][l][None, :]
    c = p['conv_dw_b'].shape[1]
    heads = 2 * c // HEAD
    h = _rms_fwd(x, row('mix_norm_g'), "rms_mix")
    w_in = first(h)
    u = _mm(h, w_in, name="in_proj")
    wpad = jnp.pad(p['conv_dw_w'][l], ((0, 1), (0, 0)))
    cw = jnp.pad(p['lru_conv_w'][l], ((0, 4), (0, 0)))
    q_blk = 3 * c // HEAD
    y_attn, tot = _sb_fwd(u, heads, q_blk, q_blk + heads, q_blk + 2 * heads, "sb_fwd")
    wa, wx = p['lru_wa'][l].astype(BF16), p['lru_wx'][l].astype(BF16)
    y_lru = _lru_fwd(u, 11, cw, row('lru_conv_b'), wa, row('lru_ba'), wx, row('lru_bx'), row('lru_lambda'), "lru_fwd")
    fw = dict(rest(y_lru), w_in=w_in)
    y_conv = _conv_fwd(u, wpad, row('conv_dw_b'), row('conv_ln_g'), row('conv_ln_b'), fw['conv_pw_w'][None], 0,
                       "conv_fwd")
    yc = _outgate_fwd(y_conv, y_attn, y_lru, u, row('out_norm_conv'), row('out_norm_attn'), row('out_norm_lru'),
                      "outgate_fwd")
    x1 = _mm(yc, fw['w_out'], add=x, name="out_proj")
    h2 = _rms_fwd(x1, row('xattn_norm_g'), "rms_xattn")
    memn = _rms_fwd(mem, row('mem_norm_g'), "rms_mem")
    q2 = _mm(h2, fw['xattn_wq'], name="xq_proj")
    kv = _mm(memn, fw['xattn_wkv'], name="xkv_proj")
    o2 = _xattn_fwd(q2, kv, "xattn_fwd")
    x2 = _mm(o2, fw['xattn_wo'], add=x1, name="xo_proj")
    saved = dict(x=x, h=h, u=u, wpad=wpad, cw=cw, wa=wa, wx=wx, y_conv=y_conv, y_attn=y_attn, tot=tot, y_lru=y_lru,
                 yc=yc, x1=x1, h2=h2, memn=memn, q2=q2, kv=kv, o2=o2, fw=fw)
    return x2, saved


def _layer_bwd(dx2, mem, p, l, sv, rest_ready, w_in_ready):
    row = lambda name: p[name][l][None, :]
    c = p['conv_dw_b'].shape[1]
    heads = 2 * c // HEAD
    fw = sv['fw']
    g, big = {}, {}
    big['xattn_wo'] = _mm(sv['o2'], dx2, ta=True, out_dtype=BF16, name="d_wo")
    do2 = _mm(dx2, fw['xattn_wo'], tb=True, name="d_o2")
    dq2, dkv = _xattn_bwd(sv['q2'], sv['kv'], do2, "xattn_bwd")
    big['xattn_wq'] = _mm(sv['h2'], dq2, ta=True, out_dtype=BF16, name="d_wq")
    dh2 = _mm(dq2, fw['xattn_wq'], tb=True, name="d_h2")
    big['xattn_wkv'] = _mm(sv['memn'], dkv, ta=True, out_dtype=BF16, name="d_wkv")
    dmemn = _mm(dkv, fw['xattn_wkv'], tb=True, name="d_memn")
    _, g['mem_norm_g'] = _rms_bwd(mem, row('mem_norm_g'), dmemn, None, "rms_mem_bwd")
    dx1, g['xattn_norm_g'] = _rms_bwd(sv['x1'], row('xattn_norm_g'), dh2, dx2, "rms_xattn_bwd")
    big['w_out'] = _mm(sv['yc'], dx1, ta=True, out_dtype=BF16, name="d_wout")
    dyc = _mm(dx1, fw['w_out'], tb=True, name="d_yc")
    u = sv['u']
    d_yconv, d_yattn, d_ylru, dgc, dga, dgl, dn = _outgate_bwd(
        dyc, sv['y_conv'], sv['y_attn'], sv['y_lru'], u, row('out_norm_conv'), row('out_norm_attn'),
        row('out_norm_lru'), "outgate_bwd")
    g['out_norm_conv'], g['out_norm_attn'], g['out_norm_lru'] = dn[0, 0:c], dn[0, c:3 * c], dn[0, 3 * c:4 * c]
    dd, gl, dpw, cvec = _conv_bwd_post(u, d_yconv, sv['wpad'], row('conv_dw_b'), row('conv_ln_g'),
                                       row('conv_ln_b'), fw['conv_pw_w'][None], 0, "conv_bwd_post")
    big['conv_pw_w'] = dpw.astype(BF16)
    g['conv_ln_g'], g['conv_ln_b'], g['conv_dw_b'] = cvec[0], cvec[1], cvec[2]
    tied = rest_ready(big)
    dval, dglu, ddw = _conv_bwd_dw(u, dd, gl, sv['wpad'] + tied[0:1, 0:1], "conv_bwd_dw")
    g['conv_dw_w'] = ddw[0:31]
    q_blk = 3 * c // HEAD
    dq, dk, dv = _sb_bwd(u, sv['tot'], d_yattn, heads, q_blk, q_blk + heads, q_blk + 2 * heads, "sb_bwd")
    dxr, g['lru_wa'], g['lru_wx'], lvec = _lru_bwd(
        u, 11, sv['y_lru'], d_ylru, sv['cw'], row('lru_conv_b'), sv['wa'], row('lru_ba'), sv['wx'], row('lru_bx'),
        row('lru_lambda'), "lru_bwd")
    g['lru_ba'], g['lru_bx'], g['lru_lambda'], g['lru_conv_b'] = lvec[0], lvec[1], lvec[2], lvec[3]
    g['lru_conv_w'] = lvec[4:8]
    du = jnp.concatenate([dval, dglu, dgc, dq.astype(BF16), dk.astype(BF16), dv.astype(BF16), dga, dxr, dgl], axis=1)
    tied = w_in_ready(_mm(sv['h'], du, ta=True, out_dtype=BF16, name="d_win"))
    dh = _mm(du, fw['w_in'], tb=True, dep=tied, name="d_h")
    dx0, g['mix_norm_g'] = _rms_bwd(sv['x'], row('mix_norm_g'), dh, dx1, "rms_mix_bwd")
    return dx0, g


def kernel(x, mem, mix_norm_g, w_in, conv_dw_w, conv_dw_b, conv_ln_g, conv_ln_b, conv_pw_w, lru_conv_w, lru_conv_b, lru_wa, lru_ba, lru_wx, lru_bx, lru_lambda, out_norm_conv, out_norm_attn, out_norm_lru, w_out, xattn_norm_g, mem_norm_g, xattn_wq, xattn_wkv, xattn_wo, final_norm_g, loss_target, m_mix_norm_g, m_w_in, m_conv_dw_w, m_conv_dw_b, m_conv_ln_g, m_conv_ln_b, m_conv_pw_w, m_lru_conv_w, m_lru_conv_b, m_lru_wa, m_lru_ba, m_lru_wx, m_lru_bx, m_lru_lambda, m_out_norm_conv, m_out_norm_attn, m_out_norm_lru, m_w_out, m_xattn_norm_g, m_mem_norm_g, m_xattn_wq, m_xattn_wkv, m_xattn_wo, m_final_norm_g, v_mix_norm_g, v_w_in, v_conv_dw_w, v_conv_dw_b, v_conv_ln_g, v_conv_ln_b, v_conv_pw_w, v_lru_conv_w, v_lru_conv_b, v_lru_wa, v_lru_ba, v_lru_wx, v_lru_bx, v_lru_lambda, v_out_norm_conv, v_out_norm_attn, v_out_norm_lru, v_w_out, v_xattn_norm_g, v_mem_norm_g, v_xattn_wq, v_xattn_wkv, v_xattn_wo, v_final_norm_g):
    args = locals()
    w = {n: args[n] for n in WEIGHTS}
    mom = {n: args["m_" + n] for n in WEIGHTS}
    var = {n: args["v_" + n] for n in WEIGHTS}
    depth = w_in.shape[0]
    c = conv_dw_b.shape[1]
    assert out_norm_attn.shape[1] == 2 * c and lru_lambda.shape[1] == c and w_in.shape[2] * N_DEV == 13 * c
    assert c % HEAD == 0 and x.shape[0] == 1 and mem.shape[0] == 1
    xs, mems, tgt = x[0], mem[0], loss_target[0]
    me = 4 * lax.axis_index("x") + 2 * lax.axis_index("y") + lax.axis_index("c")

    assert depth == 2 and (2 * w_in.shape[2]) % LANE == 0 and w_in.shape[2] % LANE in (0, LANE // 2)
    my_c = lax.axis_index("c")
    big = list(BIG_SHARDED)
    dev = [n for n in big if n != 'w_in']
    my_x, my_y = lax.axis_index("x"), lax.axis_index("y")
    pair = _pair_gather(w_in.astype(BF16), "pair_w_in")
    chip_w_in = jnp.concatenate([pair[0], pair[1]], axis=2)
    chip_width = chip_w_in.shape[2]

    def gather_group(names, l, dep, tag):
        srcs, zones, plans = [], [], []
        for n in names:
            if n == 'w_in':
                srcs.append(chip_w_in[l])
                zones.append(_gather_zone((w_in.shape[1], 4 * chip_width), srcs[-1], 1,
                                          chip_width * (2 * my_x + my_y)))
                plans.append(_chip_gather_plan(chip_width))
            else:
                axis = BIG_SHARDED[n] - 1
                blk = w[n][l].astype(BF16)
                width = blk.shape[axis]
                full_shape = blk.shape[:axis] + (N_DEV * width,) + blk.shape[axis + 1:]
                srcs.append(blk)
                zones.append(_gather_zone(full_shape, blk, axis, width * me))
                plans.append(_gather_plan(axis, width))
        return _split_start(srcs, zones, plans, dep, "gather_start_" + tag)

    g0a, tied = gather_group(['w_in'], 0, chip_w_in, "w_in0")
    g0b, tied = gather_group(dev, 0, tied, "rest0")
    g1, tied = gather_group(big, 1, tied, "layer1")
    small = list(SMALL_SHARDED)
    gathered = _all_gather(_pack([w[n] for n in small], F32) + tied[0:1, 0:1], "gather_conv_taps")
    p = dict(w)
    p.update({n: _join_blocks(blk, SMALL_SHARDED[n])
              for n, blk in zip(small, _unpack(gathered, [w[n].shape for n in small], lead=N_DEV))})
    layer1 = {}

    def first_of(l):
        def first(after):
            if l == 0:
                return _split_wait(g0a, after, "gather_wait_w_in0")[0]
            layer1.update(zip(big, _split_wait(g1, after, "gather_wait_layer1")))
            return layer1['w_in']
        return first

    def rest_of(l):
        def rest(after):
            if l == 0:
                return dict(zip(dev, _split_wait(g0b, after, "gather_wait_rest0")))
            return {n: layer1[n] for n in dev}
        return rest

    saved = []
    act = xs
    for l in range(depth):
        act, sv = _layer_fwd(act, mems, p, l, first_of(l), rest_of(l))
        saved.append(sv)
    loss_part, dact, d_final = _loss_bwd(act, final_norm_g[None, :], tgt, "loss_bwd")

    def grad_window(n):
        axis = BIG_SHARDED[n] - 1
        blk = w[n].shape[axis + 1]
        pad = blk % LANE if axis == 1 else 0
        return axis, blk + pad, lambda px, py, pc: blk * (4 * px + 2 * py + pc) - pad * pc

    scattering = {}

    def scatter_group(names, arrs, l, tag):
        zones, plans = [], []
        for n, g in zip(names, arrs):
            axis, width, start = grad_window(n)
            zones.append(_scatter_zone(g, axis, width, start(my_x, my_y, my_c), me))
            plans.append(_scatter_plan(axis, width, start))
        scattering[(l, tag)], token = _split_start(arrs, zones, plans, arrs[0], f"scatter_start_{tag}{l}")
        return token

    layer_grads = [None] * depth
    for l in reversed(range(depth)):
        dact, layer_grads[l] = _layer_bwd(
            dact, mems, p, l, saved[l],
            lambda big_grads, l=l: scatter_group(dev, [big_grads[n] for n in dev], l, "rest"),
            lambda g_w_in, l=l: scatter_group(['w_in'], [g_w_in], l, "w_in"))
    grad_x = dact[None]
    received = {}
    for l in reversed(range(depth)):
        received.update(zip([(n, l) for n in dev], _split_wait(scattering[(l, "rest")], dact, f"scatter_wait_rest{l}")))
        received[('w_in', l)], = _split_wait(scattering[(l, "w_in")], dact, f"scatter_wait_w_in{l}")
    rest = [n for n in WEIGHTS if n not in BIG_SHARDED]
    partial = {n: jnp.stack([layer_grads[l][n] for l in range(depth)]) for n in rest if n != 'final_norm_g'}
    partial['final_norm_g'] = d_final[0]
    grads = {}
    for n in big:
        summed = jnp.stack([_sum_blocks(received[(n, l)], "sum_" + n) for l in range(depth)])
        width = w[n].shape[-1]
        if summed.shape[-1] != width:
            summed = jnp.where(my_c == 0, summed[..., :width], summed[..., summed.shape[-1] - width:])
        grads[n] = summed

    vec = _pack([partial[n] for n in rest] + [loss_part], F32)
    total = _sum_blocks(_all_gather(vec, "gather_small_grads"), "sum_small_grads")
    pieces = _unpack(total, [partial[n].shape for n in rest] + [(1, 1)])
    loss = pieces[-1][0, 0]
    for n, piece in zip(rest, pieces[:-1]):
        if n in SMALL_SHARDED:
            width = w[n].shape[2]
            piece = lax.dynamic_slice_in_dim(piece, me * width, width, axis=2)
        grads[n] = piece

    delta, new_m, new_v = {}, {}, {}
    for n in big:
        shape = w[n].shape
        two_d = lambda a: a.reshape(-1, shape[-1])
        d, nm, nv = _adamw(two_d(w[n]), two_d(grads[n]), two_d(mom[n]), two_d(var[n]), "adamw_" + n)
        delta[n], new_m[n], new_v[n] = d.reshape(shape), nm.reshape(shape), nv.reshape(shape)
    shapes = [w[n].shape for n in rest]
    packed = [_pack([src[n] for n in rest], F32) for src in (w, grads, mom, var)]
    outs = _adamw(*packed, "adamw_small")
    for dst, o in zip((delta, new_m, new_v), outs):
        dst.update(dict(zip(rest, _unpack(o, shapes))))

    return (loss, grad_x, *[grads[n] for n in WEIGHTS], *[delta[n] for n in WEIGHTS],
            *[new_m[n] for n in WEIGHTS], *[new_v[n] for n in WEIGHTS])
```

```python
import functools
import math

import jax
import jax.numpy as jnp
from jax import lax
from jax.experimental import pallas as pl
from jax.experimental.pallas import tpu as pltpu

F32 = jnp.float32
BF16 = jnp.bfloat16

N_DEV = 8
LANE = 128
HEAD = 128
VMEM_LIMIT = 56 * 1024 * 1024
PACK_COLS = 512
RMS_EPS = 1e-6
LN_EPS = 1e-5
LRU_C = 8.0
CONV_HALO = 32
LRU_HALO = 8

ADAM_LR, ADAM_B1, ADAM_B2, ADAM_EPS, ADAM_WD, ADAM_STEP = 0.001, 0.9, 0.999, 1e-08, 0.01, 10

MESH = pl.DeviceIdType.MESH


def _tile(n, cands):
    for c in cands:
        if n % c == 0:
            return c
    raise ValueError(f"no tile of {cands} divides {n}")


def _params(*sem):
    return pltpu.CompilerParams(dimension_semantics=sem, vmem_limit_bytes=VMEM_LIMIT)


def _dot(a, b):
    return lax.dot_general(a, b, (((1,), (0,)), ((), ())), preferred_element_type=F32)


def _dot_nt(a, b):
    return lax.dot_general(a, b, (((1,), (1,)), ((), ())), preferred_element_type=F32)


def _dot_tn(a, b):
    return lax.dot_general(a, b, (((0,), (0,)), ((), ())), preferred_element_type=F32)


def _sigmoid(x):
    return 1.0 / (1.0 + jnp.exp(-x))


def _expm1(x):
    series = x * (1.0 + x * (0.5 + x * (1.0 / 6.0 + x * (1.0 / 24.0))))
    return jnp.where(jnp.abs(x) < 0.05, series, jnp.exp(x) - 1.0)


def _my_place():
    return lax.axis_index("x"), lax.axis_index("y"), lax.axis_index("c")


def _flip(v, d):
    return 1 - v if d else v


def _window(ref, axis, start, size):
    return ref.at[tuple(pl.ds(start, size) if a == axis else pl.ds(0, ref.shape[a]) for a in range(len(ref.shape)))]


def _all_gather(x2d, name):
    rows, cols = x2d.shape

    def body(x_ref, out_ref, send_sems, recv_sems, local_sem):
        x, y, c = _my_place()
        me, sibling = (x, y, c), (x, y, 1 - c)
        chips = [(1 - x, y), (x, 1 - y), (1 - x, 1 - y)]

        def blk(px, py, pc):
            return out_ref.at[4 * px + 2 * py + pc]

        def copy(k, block, to, src=None):
            return pltpu.make_async_remote_copy(
                src_ref=blk(*block) if src is None else src, dst_ref=blk(*block),
                send_sem=send_sems.at[k], recv_sem=recv_sems.at[k], device_id=to, device_id_type=MESH)

        mine = pltpu.make_async_copy(x_ref, blk(*me), local_sem)
        mine.start()
        first = [copy(0, me, sibling, src=x_ref)]
        first += [copy(1 + j, me, (*chip, c), src=x_ref) for j, chip in enumerate(chips)]
        for cp in first:
            cp.start()
        passed = [copy(4 + j, (*chip, c), sibling) for j, chip in enumerate(chips)]
        for j, chip in enumerate(chips):
            copy(1 + j, (*chip, c), me).wait_recv()
            passed[j].start()
        copy(0, sibling, me).wait_recv()
        for j, chip in enumerate(chips):
            copy(4 + j, (*chip, 1 - c), me).wait_recv()
        for cp in first + passed:
            cp.wait_send()
        mine.wait()

    return pl.pallas_call(
        body, name=name,
        out_shape=jax.ShapeDtypeStruct((N_DEV, rows, cols), x2d.dtype),
        in_specs=[pl.BlockSpec(memory_space=pl.ANY)],
        out_specs=pl.BlockSpec(memory_space=pl.ANY),
        scratch_shapes=[pltpu.SemaphoreType.DMA((7,)), pltpu.SemaphoreType.DMA((7,)), pltpu.SemaphoreType.DMA],
    )(x2d)


def _pair_gather(blk, name):
    pieces = 8
    rows = blk.shape[1] // pieces

    def body(x_ref, out_ref, send_sem, recv_sem, local_sem):
        x, y, c = _my_place()
        mine = pltpu.make_async_copy(x_ref, out_ref.at[c], local_sem)
        mine.start()
        for i in range(pieces):
            pltpu.make_async_remote_copy(
                src_ref=_window(x_ref, 1, i * rows, rows), dst_ref=_window(out_ref.at[c], 1, i * rows, rows),
                send_sem=send_sem, recv_sem=recv_sem, device_id=(x, y, 1 - c), device_id_type=MESH).start()
        whole = pltpu.make_async_remote_copy(src_ref=x_ref, dst_ref=out_ref.at[1 - c], send_sem=send_sem,
                                             recv_sem=recv_sem, device_id=(x, y, 1 - c), device_id_type=MESH)
        whole.wait_recv()
        whole.wait_send()
        mine.wait()

    return pl.pallas_call(
        body, name=name, out_shape=jax.ShapeDtypeStruct((2,) + blk.shape, blk.dtype),
        in_specs=[pl.BlockSpec(memory_space=pl.ANY)], out_specs=pl.BlockSpec(memory_space=pl.ANY),
        scratch_shapes=[pltpu.SemaphoreType.DMA, pltpu.SemaphoreType.DMA, pltpu.SemaphoreType.DMA],
    )(blk)


def _gather_weights(chip_block, chip_axis, dev_blocks, dev_axes, name):
    n_items = 1 + len(dev_blocks)
    chip_w = chip_block.shape[chip_axis]
    dev_w = [b.shape[a] for b, a in zip(dev_blocks, dev_axes)]

    def full_shape(b, a, n):
        return b.shape[:a] + (b.shape[a] * n,) + b.shape[a + 1:]

    out_shape = [jax.ShapeDtypeStruct(full_shape(chip_block, chip_axis, 4), chip_block.dtype)]
    out_shape += [jax.ShapeDtypeStruct(full_shape(b, a, N_DEV), b.dtype) for b, a in zip(dev_blocks, dev_axes)]

    def body(*refs):
        ins, outs = refs[:n_items], refs[n_items:2 * n_items]
        send_sems, recv_sems, local_sems = refs[2 * n_items:]
        x, y, c = _my_place()
        me, sibling = (x, y, c), (x, y, 1 - c)
        chips = [(1 - x, y), (x, 1 - y), (1 - x, 1 - y)]

        def dst(t, px, py, pc):
            if t == 0:
                return _window(outs[0].at[pc], chip_axis - 1, chip_w * (2 * px + py), chip_w)
            return _window(outs[t], dev_axes[t - 1], dev_w[t - 1] * (4 * px + 2 * py + pc), dev_w[t - 1])

        def own(t):
            return ins[0].at[c] if t == 0 else ins[t]

        def copy(t, k, block, to, src=None):
            return pltpu.make_async_remote_copy(
                src_ref=dst(t, *block) if src is None else src, dst_ref=dst(t, *block),
                send_sem=send_sems.at[7 * t + k], recv_sem=recv_sems.at[7 * t + k], device_id=to, device_id_type=MESH)

        local = [pltpu.make_async_copy(ins[0], _window(outs[0], chip_axis, chip_w * (2 * x + y), chip_w),
                                       local_sems.at[0])]
        local += [pltpu.make_async_copy(ins[t], dst(t, *me), local_sems.at[t]) for t in range(1, n_items)]
        for cp in local:
            cp.start()
        first = []
        for t in range(n_items):
            if t > 0:
                first.append(copy(t, 0, me, sibling, src=own(t)))
            first += [copy(t, 1 + j, me, (*chip, c), src=own(t)) for j, chip in enumerate(chips)]
        for cp in first:
            cp.start()
        passed = []
        for j, chip in enumerate(chips):
            for t in range(n_items):
                copy(t, 1 + j, (*chip, c), me).wait_recv()
                passed.append(copy(t, 4 + j, (*chip, c), sibling))
                passed[-1].start()
        for t in range(1, n_items):
            copy(t, 0, sibling, me).wait_recv()
        for j, chip in enumerate(chips):
            for t in range(n_items):
                copy(t, 4 + j, (*chip, 1 - c), me).wait_recv()
        for cp in first + passed:
            cp.wait_send()
        for cp in local:
            cp.wait()

    hbm = pl.BlockSpec(memory_space=pl.ANY)
    return pl.pallas_call(
        body, name=name, out_shape=out_shape, in_specs=[hbm] * n_items, out_specs=[hbm] * n_items,
        scratch_shapes=[pltpu.SemaphoreType.DMA((7 * n_items,)), pltpu.SemaphoreType.DMA((7 * n_items,)),
                        pltpu.SemaphoreType.DMA((n_items,))],
    )(chip_block, *dev_blocks)


def _scatter_grads(grads, axes, widths, starts, depth, name):
    n = len(grads)

    def win_shape(g, a, w):
        return g.shape[:a] + (w,) + g.shape[a + 1:]

    out_shape = [jax.ShapeDtypeStruct((N_DEV, depth) + win_shape(g, a, w), g.dtype)
                 for g, a, w in list(zip(grads, axes, widths))[::depth]]

    def body(*refs):
        ins, outs = refs[:n], refs[n:n + n // depth]
        send_sems, recv_sems, local_sems = refs[n + n // depth:]
        x, y, c = _my_place()
        me = 4 * x + 2 * y + c

        def win(t, px, py, pc):
            start = pl.multiple_of(starts[t](px, py, pc), math.gcd(widths[t], 1024))
            return _window(ins[t], axes[t], start, widths[t])

        def slot(t, j):
            return outs[t // depth].at[j, t % depth]

        local = [pltpu.make_async_copy(win(t, x, y, c), slot(t, me), local_sems.at[t]) for t in range(n)]
        for cp in local:
            cp.start()
        sends, recvs = [], []
        for k in range(1, N_DEV):
            px, py, pc = _flip(x, k & 4), _flip(y, k & 2), _flip(c, k & 1)
            peer = 4 * px + 2 * py + pc
            for t in range(n):
                sem = 7 * t + k - 1
                sends.append(pltpu.make_async_remote_copy(
                    src_ref=win(t, px, py, pc), dst_ref=slot(t, me), send_sem=send_sems.at[sem],
                    recv_sem=recv_sems.at[sem], device_id=(px, py, pc), device_id_type=MESH))
                recvs.append(pltpu.make_async_remote_copy(
                    src_ref=win(t, px, py, pc), dst_ref=slot(t, peer), send_sem=send_sems.at[sem],
                    recv_sem=recv_sems.at[sem], device_id=(px, py, pc), device_id_type=MESH))
        for cp in sends:
            cp.start()
        for cp in recvs:
            cp.wait_recv()
        for cp in sends:
            cp.wait_send()
        for cp in local:
            cp.wait()

    hbm = pl.BlockSpec(memory_space=pl.ANY)
    return pl.pallas_call(
        body, name=name, out_shape=out_shape, in_specs=[hbm] * n, out_specs=[hbm] * (n // depth),
        scratch_shapes=[pltpu.SemaphoreType.DMA((7 * n,)), pltpu.SemaphoreType.DMA((7 * n,)),
                        pltpu.SemaphoreType.DMA((n,))],
    )(*grads)


HBM_SPEC = pl.BlockSpec(memory_space=pltpu.HBM)
SEM_SPEC = pl.BlockSpec(memory_space=pltpu.SEMAPHORE)
SPLIT_COPY = pltpu.SideEffectType.DATAFLOW_SIDE_EFFECTING


class _Plan:
    def __init__(self, copies, own, total):
        self.copies, self.own, self.total = copies, own, total


def _scatter_plan(axis, width, start):
    def win(src, px, py, pc):
        return _window(src, axis, pl.multiple_of(start(px, py, pc), math.gcd(width, 1024)), width)

    def copies(src, zone, x, y, c):
        out = []
        for k in range(1, N_DEV):
            px, py, pc = _flip(x, k & 4), _flip(y, k & 2), _flip(c, k & 1)
            out.append((win(src, px, py, pc), zone.at[4 * x + 2 * y + c], (px, py, pc)))
        return out

    return _Plan(copies, lambda src, zone, x, y, c: (win(src, x, y, c), zone.at[4 * x + 2 * y + c]),
                 lambda zone: zone.at[pl.ds(0, N_DEV - 1)])


def _gather_plan(axis, width, stage):
    flips = range(1, N_DEV) if stage == "direct" else (1, 2, 4, 6)

    def mine(zone, x, y, c):
        return _window(zone, axis, width * (4 * x + 2 * y + c), width)

    def copies(src, zone, x, y, c):
        return [(src, mine(zone, x, y, c), (_flip(x, k & 4), _flip(y, k & 2), _flip(c, k & 1))) for k in flips]

    return _Plan(copies, lambda src, zone, x, y, c: (src, mine(zone, x, y, c)),
                 lambda zone: _window(zone, axis, 0, len(flips) * width))


def _forward_plan(axis, width):
    def copies(src, zone, x, y, c):
        out = []
        for px, py in ((1 - x, y), (x, 1 - y), (1 - x, 1 - y)):
            win = _window(zone, axis, width * (4 * px + 2 * py + c), width)
            out.append((win, win, (x, y, 1 - c)))
        return out

    return _Plan(copies, lambda src, zone, x, y, c: None, lambda zone: _window(zone, axis, 0, 3 * width))


def _split_start(srcs, zone_shapes, plans, dep, name):
    n = len(srcs)
    zones = [lax.empty(shape, a.dtype) for shape, a in zip(zone_shapes, srcs) if shape is not None]
    m = len(zones)
    zone_of = [None if shape is None else sum(s is not None for s in zone_shapes[:t])
               for t, shape in enumerate(zone_shapes)]

    def body(*refs):
        ins, fresh = refs[:n], refs[n:n + m]
        sems, token = refs[n + m + 1:n + m + 1 + 3 * n], refs[-1]
        send_sems, recv_sems, local_sems = sems[:n], sems[n:2 * n], sems[2 * n:]
        x, y, c = _my_place()
        for t in range(n):
            land = ins[t] if zone_of[t] is None else fresh[zone_of[t]]
            own = plans[t].own(ins[t], land, x, y, c)
            if own is not None:
                pltpu.make_async_copy(*own, local_sems[t]).start()
            for src, dst, target in plans[t].copies(ins[t], land, x, y, c):
                pltpu.make_async_remote_copy(src_ref=src, dst_ref=dst, send_sem=send_sems[t], recv_sem=recv_sems[t],
                                             device_id=target, device_id_type=MESH).start()
        token[...] = jnp.zeros_like(token)

    thru = [pltpu.HBM(a.shape, a.dtype) for a in list(srcs) + zones]
    outs = pl.pallas_call(
        body, name=name,
        out_shape=[pltpu.SemaphoreType.DMA(())] * (3 * n) + thru + [jax.ShapeDtypeStruct((8, LANE), F32)],
        in_specs=[HBM_SPEC] * (n + m) + [pl.BlockSpec(memory_space=pl.ANY)],
        out_specs=[SEM_SPEC] * (3 * n) + [HBM_SPEC] * (n + m) + [pl.BlockSpec(memory_space=pltpu.VMEM)],
        input_output_aliases={i: 3 * n + i for i in range(n + m)},
        compiler_params=pltpu.CompilerParams(has_side_effects=SPLIT_COPY),
    )(*[pltpu.with_memory_space_constraint(a, pltpu.HBM) for a in list(srcs) + zones], dep)
    return (outs[:3 * n], outs[3 * n:4 * n], outs[4 * n:4 * n + m], plans, zone_of), outs[-1]


def _split_wait(pending, after, name):
    sems, srcs, zones, plans, zone_of = pending
    n, m = len(srcs), len(zones)

    def body(*refs):
        ins, fresh, sems = refs[:n], refs[n:n + m], refs[n + m:n + m + 3 * n]
        send_sems, recv_sems, local_sems = sems[:n], sems[n:2 * n], sems[2 * n:]
        x, y, c = _my_place()
        for t in range(n):
            land = ins[t] if zone_of[t] is None else fresh[zone_of[t]]
            total = plans[t].total(land)
            done = pltpu.make_async_remote_copy(src_ref=total, dst_ref=total, send_sem=send_sems[t],
                                                recv_sem=recv_sems[t], device_id=(x, y, 1 - c), device_id_type=MESH)
            done.wait_send()
            done.wait_recv()
            own = plans[t].own(ins[t], land, x, y, c)
            if own is not None:
                pltpu.make_async_copy(*own, local_sems[t]).wait()

    thru = [pltpu.HBM(a.shape, a.dtype) for a in list(srcs) + list(zones)]
    outs = pl.pallas_call(
        body, name=name, out_shape=thru,
        in_specs=[HBM_SPEC] * (n + m) + [SEM_SPEC] * (3 * n) + [pl.BlockSpec(memory_space=pl.ANY)],
        out_specs=[HBM_SPEC] * (n + m), input_output_aliases={i: i for i in range(n + m)},
        compiler_params=pltpu.CompilerParams(has_side_effects=SPLIT_COPY),
    )(*srcs, *zones, *sems, after)
    return [outs[t] if zone_of[t] is None else outs[n + zone_of[t]] for t in range(n)]


def _join_cols(blocks, width, name):
    nb, rows, padded = blocks.shape
    tr = _tile(rows, (256, 128, 64, 32, 16))

    def body(x_ref, o_ref):
        for b in range(nb):
            o_ref[:, b * width:(b + 1) * width] = x_ref[b, :, 0:width]

    return pl.pallas_call(
        body, name=name, out_shape=jax.ShapeDtypeStruct((rows, nb * width), blocks.dtype), grid=(rows // tr,),
        in_specs=[pl.BlockSpec((nb, tr, padded), lambda i: (0, i, 0))],
        out_specs=pl.BlockSpec((tr, nb * width), lambda i: (i, 0)), compiler_params=_params("parallel"),
    )(blocks)


def _sum_blocks(x3d, name):
    n, rows, cols = x3d.shape
    tr = _tile(rows, (512, 256, 128, 64, 32, 16))

    def body(x_ref, o_ref):
        acc = x_ref[0].astype(F32)
        for j in range(1, n):
            acc = acc + x_ref[j].astype(F32)
        o_ref[...] = acc

    return pl.pallas_call(
        body, name=name, out_shape=jax.ShapeDtypeStruct((rows, cols), F32), grid=(rows // tr,),
        in_specs=[pl.BlockSpec((n, tr, cols), lambda i: (0, i, 0))],
        out_specs=pl.BlockSpec((tr, cols), lambda i: (i, 0)),
        compiler_params=_params("parallel"),
    )(x3d)


def _pack(arrs, dtype, lead=None):
    if lead is None:
        flat = jnp.concatenate([a.reshape(-1).astype(dtype) for a in arrs])
        n = flat.shape[0]
        total = -(-n // (16 * PACK_COLS)) * (16 * PACK_COLS)
        return jnp.pad(flat, (0, total - n)).reshape(-1, PACK_COLS)
    flat = jnp.concatenate([a.reshape(lead, -1).astype(dtype) for a in arrs], axis=1)
    n = flat.shape[1]
    total = -(-n // (16 * PACK_COLS)) * (16 * PACK_COLS)
    return jnp.pad(flat, ((0, 0), (0, total - n))).reshape(lead, -1, PACK_COLS)


def _unpack(packed, shapes, lead=None):
    out, off = [], 0
    if lead is None:
        flat = packed.reshape(-1)
        for s in shapes:
            n = math.prod(s)
            out.append(flat[off:off + n].reshape(s))
            off += n
        return out
    flat = packed.reshape(lead, -1)
    for s in shapes:
        n = math.prod(s)
        out.append(flat[:, off:off + n].reshape((lead,) + tuple(s)))
        off += n
    return out


def _join_blocks(g, axis):
    g = jnp.moveaxis(g, 0, axis)
    s = g.shape
    return g.reshape(s[:axis] + (s[axis] * s[axis + 1],) + s[axis + 2:])


def _mm_tiles(m, n, kdim, a_bytes):
    tk = kdim if kdim <= 2048 else _tile(kdim, (2048, 1664, 1024, 832, 512, 416, 256, 128))
    tm = _tile(m, (1024, 512, 256, 128, 64, 32, 16))
    tn = _tile(n, (1024, 512, 256, 128))

    def vmem(tm, tn):
        return 2 * tm * tk * a_bytes + 2 * tn * tk * 2 + 3 * tm * tn * 4

    while vmem(tm, tn) > VMEM_LIMIT * 3 // 4 and tn > 128 and tn % 256 == 0:
        tn //= 2
    while vmem(tm, tn) > VMEM_LIMIT * 3 // 4 and tm > 128 and tm % 256 == 0:
        tm //= 2
    return tm, tn, tk


def _mm(a, b, *, ta=False, tb=False, bl=None, out_dtype=F32, add=None, dep=None, name):
    if ta:
        kdim, m = a.shape
    else:
        m, kdim = a.shape
    bshape = b.shape if bl is None else b.shape[1:]
    n = bshape[0] if tb else bshape[1]
    tm, tn, tk = _mm_tiles(m, n, kdim, a.dtype.itemsize)
    nk = kdim // tk
    a_spec = pl.BlockSpec((tk, tm), lambda i, j, k: (k, i)) if ta else pl.BlockSpec((tm, tk), lambda i, j, k: (i, k))
    b_blk, b_idx = ((tn, tk), lambda i, j, k: (j, k)) if tb else ((tk, tn), lambda i, j, k: (k, j))
    if bl is None:
        b_spec = pl.BlockSpec(b_blk, b_idx)
    else:
        b_spec = pl.BlockSpec((None,) + b_blk, lambda i, j, k: (bl,) + b_idx(i, j, k))
    o_spec = pl.BlockSpec((tm, tn), lambda i, j, k: (i, j))
    dims = (((0 if ta else 1,), (1 if tb else 0,)), ((), ()))

    n_in = 2 + (add is not None) + (dep is not None)

    def body(*refs):
        a_ref, b_ref = refs[:2]
        add_ref = refs[2] if add is not None else None
        o_ref = refs[n_in]

        def finish(r):
            if add is not None:
                r = r + add_ref[...]
            o_ref[...] = r.astype(out_dtype)

        part = lax.dot_general(a_ref[...].astype(BF16), b_ref[...].astype(BF16), dims, preferred_element_type=F32)
        if nk == 1:
            finish(part)
            return
        acc_ref = refs[-1]
        k = pl.program_id(2)

        @pl.when(k == 0)
        def _():
            acc_ref[...] = part

        @pl.when(k > 0)
        def _():
            acc_ref[...] += part

        @pl.when(k == nk - 1)
        def _():
            finish(acc_ref[...])

    ins, specs = [a, b], [a_spec, b_spec]
    if add is not None:
        ins.append(add)
        specs.append(o_spec)
    if dep is not None:
        ins.append(dep)
        specs.append(pl.BlockSpec(memory_space=pl.ANY))
    return pl.pallas_call(
        body, name=name, out_shape=jax.ShapeDtypeStruct((m, n), out_dtype), grid=(m // tm, n // tn, nk),
        in_specs=specs, out_specs=o_spec, scratch_shapes=[pltpu.VMEM((tm, tn), F32)] if nk > 1 else [],
        compiler_params=_params("parallel", "parallel", "arbitrary"),
    )(*ins)


def _rms_fwd(x, g, name):
    s, d = x.shape
    tr = _tile(s, (256, 128, 64, 32, 16))

    def body(x_ref, g_ref, o_ref):
        xv = x_ref[...]
        r = lax.rsqrt(jnp.mean(xv * xv, axis=-1, keepdims=True) + RMS_EPS)
        o_ref[...] = (xv * r * g_ref[...]).astype(BF16)

    return pl.pallas_call(
        body, name=name, out_shape=jax.ShapeDtypeStruct((s, d), BF16), grid=(s // tr,),
        in_specs=[pl.BlockSpec((tr, d), lambda i: (i, 0)), pl.BlockSpec((1, d), lambda i: (0, 0))],
        out_specs=pl.BlockSpec((tr, d), lambda i: (i, 0)), compiler_params=_params("parallel"),
    )(x, g)


def _rms_bwd(x, g, dh, resid, name):
    s, d = x.shape
    tr = _tile(s, (256, 128, 64, 32, 16))

    def body(*refs):
        if resid is None:
            x_ref, g_ref, dh_ref, dx_ref, dg_ref = refs
        else:
            x_ref, g_ref, dh_ref, res_ref, dx_ref, dg_ref = refs

        @pl.when(pl.program_id(0) == 0)
        def _():
            dg_ref[...] = jnp.zeros_like(dg_ref)

        xv = x_ref[...]
        r = lax.rsqrt(jnp.mean(xv * xv, axis=-1, keepdims=True) + RMS_EPS)
        xh = xv * r
        dhv = dh_ref[...]
        dg_ref[0:1, :] += jnp.sum(dhv * xh, axis=0, keepdims=True)
        dyn = dhv * g_ref[...]
        dx = r * (dyn - xh * jnp.mean(dyn * xh, axis=-1, keepdims=True))
        if resid is not None:
            dx = dx + res_ref[...]
        dx_ref[...] = dx

    row = pl.BlockSpec((tr, d), lambda i: (i, 0))
    ins = [x, g, dh] + ([] if resid is None else [resid])
    specs = [row, pl.BlockSpec((1, d), lambda i: (0, 0)), row] + ([] if resid is None else [row])
    dx, dg = pl.pallas_call(
        body, name=name,
        out_shape=(jax.ShapeDtypeStruct((s, d), F32), jax.ShapeDtypeStruct((8, d), F32)), grid=(s // tr,),
        in_specs=specs, out_specs=(row, pl.BlockSpec((8, d), lambda i: (0, 0))),
        compiler_params=_params("arbitrary"),
    )(*ins)
    return dx, dg[0]


def _loss_bwd(x, g, tgt, name):
    s, d = x.shape
    tr = _tile(s, (256, 128, 64, 32, 16))

    def body(x_ref, g_ref, t_ref, dx_ref, dg_ref, loss_ref):
        @pl.when(pl.program_id(0) == 0)
        def _():
            dg_ref[...] = jnp.zeros_like(dg_ref)
            loss_ref[...] = jnp.zeros_like(loss_ref)

        xv = x_ref[...]
        r = lax.rsqrt(jnp.mean(xv * xv, axis=-1, keepdims=True) + RMS_EPS)
        xh = xv * r
        e = xh * g_ref[...] - t_ref[...]
        per_tok = jnp.mean(e * e, axis=-1, keepdims=True)
        loss_ref[...] += 0.5 * jnp.sum(per_tok, axis=0, keepdims=True)
        dy = e * (1.0 / d)
        dg_ref[0:1, :] += jnp.sum(dy * xh, axis=0, keepdims=True)
        dyn = dy * g_ref[...]
        dx_ref[...] = r * (dyn - xh * jnp.mean(dyn * xh, axis=-1, keepdims=True))

    row = pl.BlockSpec((tr, d), lambda i: (i, 0))
    dx, dg, loss = pl.pallas_call(
        body, name=name,
        out_shape=(jax.ShapeDtypeStruct((s, d), F32), jax.ShapeDtypeStruct((8, d), F32),
                   jax.ShapeDtypeStruct((8, LANE), F32)),
        grid=(s // tr,),
        in_specs=[row, pl.BlockSpec((1, d), lambda i: (0, 0)), row],
        out_specs=(row, pl.BlockSpec((8, d), lambda i: (0, 0)), pl.BlockSpec((8, LANE), lambda i: (0, 0))),
        compiler_params=_params("arbitrary"),
    )(x, g, tgt)
    return loss[0:1, 0:1], dx, dg[0:1]


def _conv_taps(gbuf, w_ref, tt, ntap, lo):
    acc = w_ref[0:1, :] * gbuf[pl.ds(lo, tt), :]
    for k in range(1, ntap):
        acc = acc + w_ref[k:k + 1, :] * gbuf[pl.ds(lo + k, tt), :]
    return acc


def _conv_time_tile(s):
    return _tile(s, (256, 128, 64, 32))


def _conv_fwd(u, wpad, dw_b, ln_g, ln_b, pw, l, name):
    s = u.shape[0]
    c = pw.shape[1]
    ntap = 31
    tt = _conv_time_tile(s)
    hb = tt // CONV_HALO

    def body(val_ref, glu_ref, valh_ref, gluh_ref, w_ref, b_ref, lg_ref, lb_ref, pw_ref, o_ref, gbuf):
        i = pl.program_id(0)
        glh = valh_ref[...] * _sigmoid(gluh_ref[...])
        gbuf[0:CONV_HALO, :] = jnp.where(i > 0, glh, 0.0)
        gbuf[CONV_HALO:CONV_HALO + tt, :] = val_ref[...] * _sigmoid(glu_ref[...])
        acc = _conv_taps(gbuf, w_ref, tt, ntap, CONV_HALO - (ntap - 1)) + b_ref[...]
        xc = acc - jnp.mean(acc, axis=-1, keepdims=True)
        rstd = lax.rsqrt(jnp.mean(xc * xc, axis=-1, keepdims=True) + LN_EPS)
        ln = xc * rstd * lg_ref[...] + lb_ref[...]
        sw = ln * _sigmoid(ln)
        o_ref[...] = _dot(sw.astype(BF16), pw_ref[...])

    vec = pl.BlockSpec((1, c), lambda i: (0, 0))
    return pl.pallas_call(
        body, name=name, out_shape=jax.ShapeDtypeStruct((s, c), F32), grid=(s // tt,),
        in_specs=[pl.BlockSpec((tt, c), lambda i: (i, 0)), pl.BlockSpec((tt, c), lambda i: (i, 1)),
                  pl.BlockSpec((CONV_HALO, c), lambda i: (jnp.maximum(i * hb - 1, 0), 0)),
                  pl.BlockSpec((CONV_HALO, c), lambda i: (jnp.maximum(i * hb - 1, 0), 1)),
                  pl.BlockSpec((32, c), lambda i: (0, 0)), vec, vec, vec,
                  pl.BlockSpec((None, c, c), lambda i: (l, 0, 0))],
        out_specs=pl.BlockSpec((tt, c), lambda i: (i, 0)),
        scratch_shapes=[pltpu.VMEM((CONV_HALO + tt, c), F32)],
        compiler_params=_params("parallel"),
    )(u, u, u, u, wpad, dw_b, ln_g, ln_b, pw)


def _conv_bwd_post(u, dyc, wpad, dw_b, ln_g, ln_b, pw, l, name):
    s = u.shape[0]
    c = pw.shape[1]
    ntap = 31
    tt = _conv_time_tile(s)
    hb = tt // CONV_HALO

    def body(val_ref, glu_ref, valh_ref, gluh_ref, dy_ref, w_ref, b_ref, lg_ref, lb_ref, pw_ref,
             dd_ref, gl_ref, dpw_ref, vec_ref, gbuf):
        i = pl.program_id(0)

        @pl.when(i == 0)
        def _():
            dpw_ref[...] = jnp.zeros_like(dpw_ref)
            vec_ref[...] = jnp.zeros_like(vec_ref)

        glh = valh_ref[...] * _sigmoid(gluh_ref[...])
        gbuf[0:CONV_HALO, :] = jnp.where(i > 0, glh, 0.0)
        gl = val_ref[...] * _sigmoid(glu_ref[...])
        gbuf[CONV_HALO:CONV_HALO + tt, :] = gl
        gl_ref[...] = gl
        acc = _conv_taps(gbuf, w_ref, tt, ntap, CONV_HALO - (ntap - 1)) + b_ref[...]
        xc = acc - jnp.mean(acc, axis=-1, keepdims=True)
        rstd = lax.rsqrt(jnp.mean(xc * xc, axis=-1, keepdims=True) + LN_EPS)
        xh = xc * rstd
        ln = xh * lg_ref[...] + lb_ref[...]
        sig = _sigmoid(ln)
        sw = ln * sig
        dyb = dy_ref[...].astype(BF16)
        dpw_ref[...] += _dot_tn(sw.astype(BF16), dyb)
        dsw = _dot_nt(dyb, pw_ref[...])
        dln = dsw * (sig * (1.0 + ln * (1.0 - sig)))
        vec_ref[0:1, :] += jnp.sum(dln * xh, axis=0, keepdims=True)
        vec_ref[1:2, :] += jnp.sum(dln, axis=0, keepdims=True)
        dxh = dln * lg_ref[...]
        dd = rstd * (dxh - jnp.mean(dxh, axis=-1, keepdims=True)
                     - xh * jnp.mean(dxh * xh, axis=-1, keepdims=True))
        vec_ref[2:3, :] += jnp.sum(dd, axis=0, keepdims=True)
        dd_ref[...] = dd

    vec = pl.BlockSpec((1, c), lambda i: (0, 0))
    tile = pl.BlockSpec((tt, c), lambda i: (i, 0))
    return pl.pallas_call(
        body, name=name,
        out_shape=(jax.ShapeDtypeStruct((s, c), F32), jax.ShapeDtypeStruct((s, c), F32),
                   jax.ShapeDtypeStruct((c, c), F32), jax.ShapeDtypeStruct((8, c), F32)),
        grid=(s // tt,),
        in_specs=[tile, pl.BlockSpec((tt, c), lambda i: (i, 1)),
                  pl.BlockSpec((CONV_HALO, c), lambda i: (jnp.maximum(i * hb - 1, 0), 0)),
                  pl.BlockSpec((CONV_HALO, c), lambda i: (jnp.maximum(i * hb - 1, 0), 1)),
                  tile, pl.BlockSpec((32, c), lambda i: (0, 0)), vec, vec, vec,
                  pl.BlockSpec((None, c, c), lambda i: (l, 0, 0))],
        out_specs=(tile, tile, pl.BlockSpec((c, c), lambda i: (0, 0)), pl.BlockSpec((8, c), lambda i: (0, 0))),
        scratch_shapes=[pltpu.VMEM((CONV_HALO + tt, c), F32)],
        compiler_params=_params("arbitrary"),
    )(u, u, u, u, dyc, wpad, dw_b, ln_g, ln_b, pw)


def _conv_bwd_dw(u, dd, gl, wpad, name):
    s, c = dd.shape
    ntap = 31
    tt = _conv_time_tile(s)
    hb = tt // CONV_HALO
    nt = s // tt
    last_halo = s // CONV_HALO - 1

    def body(val_ref, glu_ref, dd_ref, ddn_ref, gl_ref, glh_ref, w_ref, dval_ref, dglu_ref, dw_ref, dbuf, gbuf):
        i = pl.program_id(0)

        @pl.when(i == 0)
        def _():
            dw_ref[...] = jnp.zeros_like(dw_ref)

        d = dd_ref[...]
        dbuf[0:tt, :] = d
        dbuf[tt:tt + CONV_HALO, :] = jnp.where(i < nt - 1, ddn_ref[...], 0.0)
        gbuf[0:CONV_HALO, :] = jnp.where(i > 0, glh_ref[...], 0.0)
        gbuf[CONV_HALO:CONV_HALO + tt, :] = gl_ref[...]
        dgl = w_ref[0:1, :] * dbuf[pl.ds(ntap - 1, tt), :]
        for k in range(1, ntap):
            dgl = dgl + w_ref[k:k + 1, :] * dbuf[pl.ds(ntap - 1 - k, tt), :]
        for k in range(ntap):
            dw_ref[k:k + 1, :] += jnp.sum(d * gbuf[pl.ds(CONV_HALO - (ntap - 1) + k, tt), :], axis=0, keepdims=True)
        sg = _sigmoid(glu_ref[...])
        dval_ref[...] = (dgl * sg).astype(BF16)
        dglu_ref[...] = (dgl * val_ref[...] * sg * (1.0 - sg)).astype(BF16)

    tile = pl.BlockSpec((tt, c), lambda i: (i, 0))
    return pl.pallas_call(
        body, name=name,
        out_shape=(jax.ShapeDtypeStruct((s, c), BF16), jax.ShapeDtypeStruct((s, c), BF16),
                   jax.ShapeDtypeStruct((32, c), F32)),
        grid=(nt,),
        in_specs=[tile, pl.BlockSpec((tt, c), lambda i: (i, 1)), tile,
                  pl.BlockSpec((CONV_HALO, c), lambda i: (jnp.minimum((i + 1) * hb, last_halo), 0)),
                  tile, pl.BlockSpec((CONV_HALO, c), lambda i: (jnp.maximum(i * hb - 1, 0), 0)),
                  pl.BlockSpec((32, c), lambda i: (0, 0))],
        out_specs=(tile, tile, pl.BlockSpec((32, c), lambda i: (0, 0))),
        scratch_shapes=[pltpu.VMEM((tt + CONV_HALO, c), F32), pltpu.VMEM((CONV_HALO + tt, c), F32)],
        compiler_params=_params("arbitrary"),
    )(u, u, dd, dd, gl, gl, wpad)


SB_ROWS = 32


def _tri(n, cmp):
    r = lax.broadcasted_iota(jnp.int32, (n, n), 0)
    c = lax.broadcasted_iota(jnp.int32, (n, n), 1)
    return jnp.where(cmp(r, c), 1.0, 0.0).astype(BF16)


def _row_chunks(fn, n, *arrs):
    outs = [fn(*[a[r:r + SB_ROWS] for a in arrs]) for r in range(0, n, SB_ROWS)]
    return tuple(jnp.concatenate(list(o), axis=0) for o in zip(*outs))


def _hi_lo(v):
    hi = v.astype(BF16)
    return hi, (v - hi.astype(F32)).astype(BF16)


def _sb_sticks(z, causal, scale):
    z = z * scale
    l1p = jnp.log(1.0 + jnp.exp(-jnp.abs(z)))
    lb = jnp.minimum(z, 0.0) - l1p
    ell = lb - z
    if causal is not None:
        ell = jnp.where(causal, ell, 0.0)
    hi, lo = _hi_lo(ell)
    return lb, hi, lo, jnp.sum(ell, axis=1, keepdims=True)


def _sb_fwd(u, heads, q_blk, k_blk, v_blk, name):
    s = u.shape[0]
    tq = _tile(s, (256, 128))
    scale = HEAD ** -0.5

    def body(q_ref, k_ref, v_ref, o_ref, tot_ref):
        i = pl.program_id(1)
        qb = q_ref[...].astype(BF16)
        t_sfx = _tri(tq, lambda r, c: r > c)
        below_diag = lax.broadcasted_iota(jnp.int32, (tq, tq), 1) < lax.broadcasted_iota(jnp.int32, (tq, tq), 0)

        def blocks(j0, nb, c_a, acc, diag):
            mask = [below_diag] if diag else []
            rows = pl.ds(pl.multiple_of(j0 * tq, tq), nb * tq)
            kb = k_ref[rows, :].astype(BF16)
            vb = v_ref[rows, :].astype(BF16)
            z = _dot_nt(qb, kb)

            def sticks(zc, *m):
                out = []
                for b in range(nb):
                    out += _sb_sticks(zc[:, b * tq:(b + 1) * tq], m[0] if m else None, scale)
                return tuple(out)

            st = _row_chunks(sticks, tq, z, *mask)
            lb, hi, lo, rs = st[0::4], st[1::4], st[2::4], st[3::4]
            sfx = [_dot(hi[b], t_sfx) + _dot(lo[b], t_sfx) for b in range(nb)]
            before, run = [None] * nb, c_a
            for b in reversed(range(nb)):
                before[b], run = run, run + rs[b]

            def weights(*a):
                ws = []
                for b in range(nb):
                    lbc, sfxc, befc = a[3 * b:3 * b + 3]
                    w = jnp.exp(lbc + (befc + sfxc))
                    if diag:
                        w = jnp.where(a[-1], w, 0.0)
                    ws.append(w.astype(BF16))
                return (ws[0] if nb == 1 else jnp.concatenate(ws, axis=1),)

            flat = [v for b in range(nb) for v in (lb[b], sfx[b], before[b])]
            wb, = _row_chunks(weights, tq, *flat, *mask)
            return run, acc + _dot(wb, vb)

        carry = blocks(i, 1, jnp.zeros((tq, 1), F32), jnp.zeros((tq, HEAD), F32), True)
        carry = lax.fori_loop(0, i // 4, lambda t, cr: blocks(i - 4 - 4 * t, 4, *cr, False), carry)
        carry = lax.fori_loop(0, (i % 4) // 2, lambda _, cr: blocks(i % 2, 2, *cr, False), carry)
        c_a, acc = lax.fori_loop(0, i % 2, lambda _, cr: blocks(0, 1, *cr, False), carry)
        o_ref[...] = acc
        tot_ref[...] = jnp.broadcast_to(c_a, (tq, HEAD))

    full = lambda off: pl.BlockSpec((s, HEAD), lambda h, i: (0, off + h))
    out = pl.BlockSpec((tq, HEAD), lambda h, i: (i, h))
    return pl.pallas_call(
        body, name=name,
        out_shape=(jax.ShapeDtypeStruct((s, heads * HEAD), F32), jax.ShapeDtypeStruct((s, heads * HEAD), F32)),
        grid=(heads, s // tq),
        in_specs=[pl.BlockSpec((tq, HEAD), lambda h, i: (i, q_blk + h)), full(k_blk), full(v_blk)],
        out_specs=(out, out),
        compiler_params=_params("parallel", "parallel"),
    )(u, u, u)


def _sb_bwd(u, tot, dy, heads, q_blk, k_blk, v_blk, name):
    s = u.shape[0]
    tq = _tile(s, (256, 128))
    scale = HEAD ** -0.5

    def body(q_ref, k_ref, v_ref, tot_ref, dy_ref, dq_ref, dk_ref, dv_ref):
        i = pl.program_id(1)

        @pl.when(i == 0)
        def _():
            dk_ref[...] = jnp.zeros_like(dk_ref)
            dv_ref[...] = jnp.zeros_like(dv_ref)

        qb = q_ref[...].astype(BF16)
        dob = dy_ref[...].astype(BF16)
        total = tot_ref[:, 0:1]
        t_incl = _tri(tq, lambda r, c: r <= c)
        t_excl = _tri(tq, lambda r, c: r < c)
        below_diag = lax.broadcasted_iota(jnp.int32, (tq, tq), 1) < lax.broadcasted_iota(jnp.int32, (tq, tq), 0)

        def blocks(j0, nb, c_p, c_g, dq, diag):
            mask = [below_diag] if diag else []
            rows = pl.ds(pl.multiple_of(j0 * tq, tq), nb * tq)
            kb = k_ref[rows, :].astype(BF16)
            vb = v_ref[rows, :].astype(BF16)
            z = _dot_nt(qb, kb)
            dw = _dot_nt(dob, vb)
            cols = lambda a, b: a[:, b * tq:(b + 1) * tq]

            def sticks(zc, *m):
                out = []
                for b in range(nb):
                    out += _sb_sticks(cols(zc, b), m[0] if m else None, scale)
                return tuple(out)

            st = _row_chunks(sticks, tq, z, *mask)
            lb, hi, lo, rs_l = st[0::4], st[1::4], st[2::4], st[3::4]
            pfx = [_dot(hi[b], t_incl) + _dot(lo[b], t_incl) for b in range(nb)]
            p_before = [c_p]
            for b in range(nb):
                p_before.append(p_before[-1] + rs_l[b])

            def weights(totc, dwc, *a):
                out = []
                for b in range(nb):
                    lbc, pfxc, pbc = a[3 * b:3 * b + 3]
                    w = jnp.exp(lbc + (totc - (pbc + pfxc)))
                    if diag:
                        w = jnp.where(a[-1], w, 0.0)
                    g = w * cols(dwc, b)
                    out += [w.astype(BF16), g, *_hi_lo(g), jnp.sum(g, axis=1, keepdims=True)]
                return tuple(out)

            flat = [v for b in range(nb) for v in (lb[b], pfx[b], p_before[b])]
            wt = _row_chunks(weights, tq, total, dw, *flat, *mask)
            wb, g, ghi, glo, rs_g = wt[0::5], wt[1::5], wt[2::5], wt[3::5], wt[4::5]
            g_pre = [_dot(ghi[b], t_excl) + _dot(glo[b], t_excl) for b in range(nb)]
            g_before = [c_g]
            for b in range(nb):
                g_before.append(g_before[-1] + rs_g[b])

            def dscore(*a):
                dzs = []
                for b in range(nb):
                    lbc, gc, gprec, gbc = a[4 * b:4 * b + 4]
                    beta = jnp.exp(lbc)
                    dz = (gc * (1.0 - beta) - (gbc + gprec) * beta) * scale
                    if diag:
                        dz = jnp.where(a[-1], dz, 0.0)
                    dzs.append(dz.astype(BF16))
                return (dzs[0] if nb == 1 else jnp.concatenate(dzs, axis=1),)

            flat = [v for b in range(nb) for v in (lb[b], g[b], g_pre[b], g_before[b])]
            dzb, = _row_chunks(dscore, tq, *flat, *mask)
            wcat = wb[0] if nb == 1 else jnp.concatenate(wb, axis=1)
            dk_ref[rows, :] += _dot_tn(dzb, qb)
            dv_ref[rows, :] += _dot_tn(wcat, dob)
            return p_before[-1], g_before[-1], dq + _dot(dzb, kb)

        zero = jnp.zeros((tq, 1), F32)
        carry = lax.fori_loop(0, i // 2, lambda t, cr: blocks(2 * t, 2, *cr, False),
                              (zero, zero, jnp.zeros((tq, HEAD), F32)))
        carry = lax.fori_loop(0, i % 2, lambda _, cr: blocks(i - 1, 1, *cr, False), carry)
        _, _, dq = blocks(i, 1, *carry, True)
        dq_ref[...] = dq

    full = lambda off: pl.BlockSpec((s, HEAD), lambda h, i: (0, off + h))
    blk = pl.BlockSpec((tq, HEAD), lambda h, i: (i, h))
    acc = pl.BlockSpec((s, HEAD), lambda h, i: (0, h))
    shape = jax.ShapeDtypeStruct((s, heads * HEAD), F32)
    return pl.pallas_call(
        body, name=name, out_shape=(shape, shape, shape), grid=(heads, s // tq),
        in_specs=[pl.BlockSpec((tq, HEAD), lambda h, i: (i, q_blk + h)), full(k_blk), full(v_blk), blk, blk],
        out_specs=(blk, acc, acc),
        compiler_params=_params("parallel", "arbitrary"),
    )(u, u, u, tot, dy)


def _lru_time_tile(s):
    return _tile(s, (256, 128, 64, 32))


def _lru_gates(xc, wa_ref, ba_ref, wx_ref, bx_ref, lam_ref, nh):
    pr, pi = [], []
    for n in range(nh):
        xn = xc[:, n * HEAD:(n + 1) * HEAD].astype(BF16)
        pr.append(_dot(xn, wa_ref[n]))
        pi.append(_dot(xn, wx_ref[n]))
    r = _sigmoid((pr[0] if nh == 1 else jnp.concatenate(pr, axis=1)) + ba_ref[...])
    ig = _sigmoid((pi[0] if nh == 1 else jnp.concatenate(pi, axis=1)) + bx_ref[...])
    lam = lam_ref[...]
    sp = jnp.maximum(-lam, 0.0) + jnp.log(1.0 + jnp.exp(-jnp.abs(lam)))
    log_a = -LRU_C * r * sp
    a = jnp.exp(log_a)
    mult = jnp.sqrt(-_expm1(2.0 * log_a))
    return r, ig, a, mult, sp


def _lru_fwd(u, x_blk, cw, cb, wa, ba, wx, bx, lam, name):
    s = u.shape[0]
    w = lam.shape[1]
    nh = w // HEAD
    tt = _lru_time_tile(s)
    hb = tt // LRU_HALO

    def body(x_ref, xh_ref, cw_ref, cb_ref, wa_ref, ba_ref, wx_ref, bx_ref, lam_ref, y_ref,
             xbuf, abuf, bbuf, hstate, rowbuf):
        i = pl.program_id(0)

        @pl.when(i == 0)
        def _():
            hstate[...] = jnp.zeros_like(hstate)

        xbuf[0:LRU_HALO, :] = jnp.where(i > 0, xh_ref[...], 0.0)
        xbuf[LRU_HALO:LRU_HALO + tt, :] = x_ref[...]
        xc = _conv_taps(xbuf, cw_ref, tt, 4, LRU_HALO - 3) + cb_ref[...]
        _, ig, a, mult, _ = _lru_gates(xc, wa_ref, ba_ref, wx_ref, bx_ref, lam_ref, nh)
        abuf[...] = a
        bbuf[...] = mult * (ig * xc)

        def group(gi, h):
            rows = pl.ds(pl.multiple_of(gi * 8, 8), 8)
            a8 = abuf[rows, :]
            b8 = bbuf[rows, :]
            for j in range(8):
                h = a8[j:j + 1, :] * h + b8[j:j + 1, :]
                rowbuf[j:j + 1, :] = h
            y_ref[rows, :] = rowbuf[...]
            return h

        hstate[0:1, :] = lax.fori_loop(0, tt // 8, group, hstate[0:1, :])

    vec = pl.BlockSpec((1, w), lambda i: (0, 0))
    gate = pl.BlockSpec((nh, HEAD, HEAD), lambda i: (0, 0, 0))
    return pl.pallas_call(
        body, name=name, out_shape=jax.ShapeDtypeStruct((s, w), F32), grid=(s // tt,),
        in_specs=[pl.BlockSpec((tt, w), lambda i: (i, x_blk)),
                  pl.BlockSpec((LRU_HALO, w), lambda i: (jnp.maximum(i * hb - 1, 0), x_blk)),
                  pl.BlockSpec((8, w), lambda i: (0, 0)), vec, gate, vec, gate, vec, vec],
        out_specs=pl.BlockSpec((tt, w), lambda i: (i, 0)),
        scratch_shapes=[pltpu.VMEM((LRU_HALO + tt, w), F32), pltpu.VMEM((tt, w), F32), pltpu.VMEM((tt, w), F32),
                        pltpu.VMEM((8, w), F32), pltpu.VMEM((8, w), F32)],
        compiler_params=_params("arbitrary"),
    )(u, u, cw, cb, wa, ba, wx, bx, lam)


def _lru_bwd(u, x_blk, hseq, dy, cw, cb, wa, ba, wx, bx, lam, name):
    s = u.shape[0]
    w = lam.shape[1]
    nh = w // HEAD
    tt = _lru_time_tile(s)
    hb = tt // LRU_HALO
    nt = s // tt

    def body(x_ref, xh_ref, h_ref, hh_ref, dy_ref, cw_ref, cb_ref, wa_ref, ba_ref, wx_ref, bx_ref, lam_ref,
             dx_ref, dwa_ref, dwx_ref, vec_ref, xbuf, hbuf, abuf, lbuf, dbuf, cstate, dhalo, rowbuf):
        i = pl.program_id(0)
        rt = nt - 1 - i

        @pl.when(i == 0)
        def _():
            cstate[...] = jnp.zeros_like(cstate)
            dhalo[...] = jnp.zeros_like(dhalo)
            dwa_ref[...] = jnp.zeros_like(dwa_ref)
            dwx_ref[...] = jnp.zeros_like(dwx_ref)
            vec_ref[...] = jnp.zeros_like(vec_ref)

        xbuf[0:LRU_HALO, :] = jnp.where(rt > 0, xh_ref[...], 0.0)
        xbuf[LRU_HALO:LRU_HALO + tt, :] = x_ref[...]
        hbuf[0:LRU_HALO, :] = jnp.where(rt > 0, hh_ref[...], 0.0)
        hbuf[LRU_HALO:LRU_HALO + tt, :] = h_ref[...]
        xc = _conv_taps(xbuf, cw_ref, tt, 4, LRU_HALO - 3) + cb_ref[...]
        r, ig, a, mult, sp = _lru_gates(xc, wa_ref, ba_ref, wx_ref, bx_ref, lam_ref, nh)
        abuf[...] = a

        def group(gi, c):
            rows = pl.ds(pl.multiple_of((tt // 8 - 1 - gi) * 8, 8), 8)
            a8 = abuf[rows, :]
            d8 = dy_ref[rows, :]
            for j in range(7, -1, -1):
                lam_t = d8[j:j + 1, :] + c
                rowbuf[j:j + 1, :] = lam_t
                c = a8[j:j + 1, :] * lam_t
            lbuf[rows, :] = rowbuf[...]
            return c

        cstate[0:1, :] = lax.fori_loop(0, tt // 8, group, cstate[0:1, :])

        lam_t = lbuf[...]
        hprev = hbuf[pl.ds(LRU_HALO - 1, tt), :]
        ixc = ig * xc
        d_ixc = lam_t * mult
        d_ig = d_ixc * xc
        dxc = d_ixc * ig
        dlog_a = lam_t * hprev * a + lam_t * ixc * (-(a * a) / mult)
        dr = dlog_a * (-LRU_C * sp)
        lam_p = lam_ref[...]
        dsp = -_sigmoid(-lam_p)
        vec_ref[2:3, :] += jnp.sum(dlog_a * (-LRU_C * r), axis=0, keepdims=True) * dsp
        dpr = dr * r * (1.0 - r)
        dpi = d_ig * ig * (1.0 - ig)
        vec_ref[0:1, :] += jnp.sum(dpr, axis=0, keepdims=True)
        vec_ref[1:2, :] += jnp.sum(dpi, axis=0, keepdims=True)
        parts = []
        for n in range(nh):
            sl = slice(n * HEAD, (n + 1) * HEAD)
            xn = xc[:, sl].astype(BF16)
            dprn = dpr[:, sl].astype(BF16)
            dpin = dpi[:, sl].astype(BF16)
            dwa_ref[n] += _dot_tn(xn, dprn)
            dwx_ref[n] += _dot_tn(xn, dpin)
            parts.append(_dot_nt(dprn, wa_ref[n]) + _dot_nt(dpin, wx_ref[n]))
        dxc = dxc + (parts[0] if nh == 1 else jnp.concatenate(parts, axis=1))
        vec_ref[3:4, :] += jnp.sum(dxc, axis=0, keepdims=True)
        dbuf[0:tt, :] = dxc
        dbuf[tt:tt + LRU_HALO, :] = dhalo[...]
        dx = cw_ref[0:1, :] * dbuf[pl.ds(3, tt), :]
        for k in range(1, 4):
            dx = dx + cw_ref[k:k + 1, :] * dbuf[pl.ds(3 - k, tt), :]
        dx_ref[...] = dx.astype(BF16)
        for k in range(4):
            vec_ref[4 + k:5 + k, :] += jnp.sum(dxc * xbuf[pl.ds(LRU_HALO - 3 + k, tt), :], axis=0, keepdims=True)
        dhalo[...] = dbuf[0:LRU_HALO, :]

    vec = pl.BlockSpec((1, w), lambda i: (0, 0))
    gate = pl.BlockSpec((nh, HEAD, HEAD), lambda i: (0, 0, 0))
    rev = lambda i: nt - 1 - i
    tile = pl.BlockSpec((tt, w), lambda i: (rev(i), 0))
    halo = lambda col: pl.BlockSpec((LRU_HALO, w), lambda i: (jnp.maximum(rev(i) * hb - 1, 0), col))
    return pl.pallas_call(
        body, name=name,
        out_shape=(jax.ShapeDtypeStruct((s, w), BF16), jax.ShapeDtypeStruct((nh, HEAD, HEAD), F32),
                   jax.ShapeDtypeStruct((nh, HEAD, HEAD), F32), jax.ShapeDtypeStruct((8, w), F32)),
        grid=(nt,),
        in_specs=[pl.BlockSpec((tt, w), lambda i: (rev(i), x_blk)), halo(x_blk), tile, halo(0), tile,
                  pl.BlockSpec((8, w), lambda i: (0, 0)), vec, gate, vec, gate, vec, vec],
        out_specs=(tile, gate, gate, pl.BlockSpec((8, w), lambda i: (0, 0))),
        scratch_shapes=[pltpu.VMEM((LRU_HALO + tt, w), F32), pltpu.VMEM((LRU_HALO + tt, w), F32),
                        pltpu.VMEM((tt, w), F32), pltpu.VMEM((tt, w), F32), pltpu.VMEM((tt + LRU_HALO, w), F32),
                        pltpu.VMEM((8, w), F32), pltpu.VMEM((8, w), F32), pltpu.VMEM((8, w), F32)],
        compiler_params=_params("arbitrary"),
    )(u, u, hseq, hseq, dy, cw, cb, wa, ba, wx, bx, lam)


def _gate_specs(c, tr):
    return [pl.BlockSpec((tr, c), lambda i, b=b: (i, b)) for b in (2, 9, 10, 12)]


def _outgate_fwd(y_conv, y_attn, y_lru, u, n_conv, n_attn, n_lru, name):
    s, c = y_conv.shape
    tr = _tile(s, (256, 128, 64, 32, 16))

    def body(yc_ref, ya_ref, yl_ref, gc_ref, ga0_ref, ga1_ref, gl_ref, nc_ref, na_ref, nl_ref, o_ref):
        def rinv(v):
            return lax.rsqrt(jnp.mean(v * v, axis=-1, keepdims=True) + RMS_EPS)

        def silu(g):
            return g * _sigmoid(g)

        yc = yc_ref[...]
        o_ref[:, 0:c] = (yc * rinv(yc) * nc_ref[...] * silu(gc_ref[...])).astype(BF16)
        ya = ya_ref[...]
        ra = rinv(ya)
        o_ref[:, c:2 * c] = (ya[:, 0:c] * ra * na_ref[:, 0:c] * silu(ga0_ref[...])).astype(BF16)
        o_ref[:, 2 * c:3 * c] = (ya[:, c:2 * c] * ra * na_ref[:, c:2 * c] * silu(ga1_ref[...])).astype(BF16)
        yl = yl_ref[...]
        o_ref[:, 3 * c:4 * c] = (yl * rinv(yl) * nl_ref[...] * silu(gl_ref[...])).astype(BF16)

    row = lambda wd: pl.BlockSpec((tr, wd), lambda i: (i, 0))
    vec = lambda wd: pl.BlockSpec((1, wd), lambda i: (0, 0))
    return pl.pallas_call(
        body, name=name, out_shape=jax.ShapeDtypeStruct((s, 4 * c), BF16), grid=(s // tr,),
        in_specs=[row(c), row(2 * c), row(c)] + _gate_specs(c, tr) + [vec(c), vec(2 * c), vec(c)],
        out_specs=row(4 * c), compiler_params=_params("parallel"),
    )(y_conv, y_attn, y_lru, u, u, u, u, n_conv, n_attn, n_lru)


def _outgate_bwd(dy, y_conv, y_attn, y_lru, u, n_conv, n_attn, n_lru, name):
    s, c = y_conv.shape
    tr = _tile(s, (256, 128, 64, 32, 16))

    def body(dy_ref, yc_ref, ya_ref, yl_ref, gc_ref, ga0_ref, ga1_ref, gl_ref, nc_ref, na_ref, nl_ref,
             dyc_ref, dya_ref, dyl_ref, dgc_ref, dga_ref, dgl_ref, dn_ref):
        @pl.when(pl.program_id(0) == 0)
        def _():
            dn_ref[...] = jnp.zeros_like(dn_ref)

        def group(yv, gate, wv, d):
            r = lax.rsqrt(jnp.mean(yv * yv, axis=-1, keepdims=True) + RMS_EPS)
            yh = yv * r
            sg = _sigmoid(gate)
            dn = d * (gate * sg)
            dgate = d * (yh * wv) * (sg * (1.0 + gate * (1.0 - sg)))
            dw = jnp.sum(dn * yh, axis=0, keepdims=True)
            dyn = dn * wv
            dyv = r * (dyn - yh * jnp.mean(dyn * yh, axis=-1, keepdims=True))
            return dyv, dgate, dw

        dyv, dg, dw = group(yc_ref[...], gc_ref[...], nc_ref[...], dy_ref[:, 0:c])
        dyc_ref[...] = dyv
        dgc_ref[...] = dg.astype(BF16)
        dn_ref[0:1, 0:c] += dw
        gate_a = jnp.concatenate([ga0_ref[...], ga1_ref[...]], axis=1)
        dyv, dg, dw = group(ya_ref[...], gate_a, na_ref[...], dy_ref[:, c:3 * c])
        dya_ref[...] = dyv
        dga_ref[...] = dg.astype(BF16)
        dn_ref[0:1, c:3 * c] += dw
        dyv, dg, dw = group(yl_ref[...], gl_ref[...], nl_ref[...], dy_ref[:, 3 * c:4 * c])
        dyl_ref[...] = dyv
        dgl_ref[...] = dg.astype(BF16)
        dn_ref[0:1, 3 * c:4 * c] += dw

    row = lambda wd: pl.BlockSpec((tr, wd), lambda i: (i, 0))
    vec = lambda wd: pl.BlockSpec((1, wd), lambda i: (0, 0))
    sh = lambda wd, dt: jax.ShapeDtypeStruct((s, wd), dt)
    return pl.pallas_call(
        body, name=name,
        out_shape=(sh(c, F32), sh(2 * c, F32), sh(c, F32), sh(c, BF16), sh(2 * c, BF16), sh(c, BF16),
                   jax.ShapeDtypeStruct((8, 4 * c), F32)),
        grid=(s // tr,),
        in_specs=[row(4 * c), row(c), row(2 * c), row(c)] + _gate_specs(c, tr) + [vec(c), vec(2 * c), vec(c)],
        out_specs=(row(c), row(2 * c), row(c), row(c), row(2 * c), row(c),
                   pl.BlockSpec((8, 4 * c), lambda i: (0, 0))),
        compiler_params=_params("arbitrary"),
    )(dy, y_conv, y_attn, y_lru, u, u, u, u, n_conv, n_attn, n_lru)


def _xattn_probs(qh, kh, scale):
    sc = _dot_nt(qh, kh) * scale
    p = jnp.exp(sc - jnp.max(sc, axis=-1, keepdims=True))
    return p / jnp.sum(p, axis=-1, keepdims=True)


def _xattn_fwd(q, kv, name):
    s, xw = q.shape
    m = kv.shape[0]
    nh = xw // HEAD
    tq = _tile(s, (256, 128, 64, 32, 16))
    scale = HEAD ** -0.5

    def body(q_ref, kv_ref, o_ref):
        for h in range(nh):
            qh = q_ref[:, h * HEAD:(h + 1) * HEAD].astype(BF16)
            kh = kv_ref[:, h * HEAD:(h + 1) * HEAD].astype(BF16)
            vh = kv_ref[:, xw + h * HEAD:xw + (h + 1) * HEAD].astype(BF16)
            p = _xattn_probs(qh, kh, scale)
            o_ref[:, h * HEAD:(h + 1) * HEAD] = _dot(p.astype(BF16), vh).astype(BF16)

    return pl.pallas_call(
        body, name=name, out_shape=jax.ShapeDtypeStruct((s, xw), BF16), grid=(s // tq,),
        in_specs=[pl.BlockSpec((tq, xw), lambda i: (i, 0)), pl.BlockSpec((m, 2 * xw), lambda i: (0, 0))],
        out_specs=pl.BlockSpec((tq, xw), lambda i: (i, 0)), compiler_params=_params("parallel"),
    )(q, kv)


def _xattn_bwd(q, kv, do, name):
    s, xw = q.shape
    m = kv.shape[0]
    nh = xw // HEAD
    tq = _tile(s, (256, 128, 64, 32, 16))
    scale = HEAD ** -0.5

    def body(q_ref, kv_ref, do_ref, dq_ref, dkv_ref):
        @pl.when(pl.program_id(0) == 0)
        def _():
            dkv_ref[...] = jnp.zeros_like(dkv_ref)

        for h in range(nh):
            ks = slice(h * HEAD, (h + 1) * HEAD)
            vs = slice(xw + h * HEAD, xw + (h + 1) * HEAD)
            qh = q_ref[:, ks].astype(BF16)
            kh = kv_ref[:, ks].astype(BF16)
            vh = kv_ref[:, vs].astype(BF16)
            doh = do_ref[:, ks].astype(BF16)
            p = _xattn_probs(qh, kh, scale)
            dkv_ref[:, vs] += _dot_tn(p.astype(BF16), doh)
            dp = _dot_nt(doh, vh)
            ds = (p * (dp - jnp.sum(dp * p, axis=-1, keepdims=True)) * scale).astype(BF16)
            dq_ref[:, ks] = _dot(ds, kh).astype(BF16)
            dkv_ref[:, ks] += _dot_tn(ds, qh)

    row = pl.BlockSpec((tq, xw), lambda i: (i, 0))
    full = pl.BlockSpec((m, 2 * xw), lambda i: (0, 0))
    return pl.pallas_call(
        body, name=name,
        out_shape=(jax.ShapeDtypeStruct((s, xw), BF16), jax.ShapeDtypeStruct((m, 2 * xw), F32)), grid=(s // tq,),
        in_specs=[row, full, row], out_specs=(row, full), compiler_params=_params("arbitrary"),
    )(q, kv, do)


def _adamw(w, g, m, v, name):
    rows, cols = w.shape
    tr = _tile(rows, (512, 256, 128, 64, 32, 16, 8)) if rows % 8 == 0 else rows
    bc1 = 1.0 - ADAM_B1 ** ADAM_STEP
    bc2 = 1.0 - ADAM_B2 ** ADAM_STEP

    def body(w_ref, g_ref, m_ref, v_ref, d_ref, nm_ref, nv_ref):
        gv = g_ref[...]
        nm = ADAM_B1 * m_ref[...] + (1.0 - ADAM_B1) * gv
        nv = ADAM_B2 * v_ref[...] + (1.0 - ADAM_B2) * (gv * gv)
        nm_ref[...] = nm
        nv_ref[...] = nv
        d_ref[...] = -ADAM_LR * ((nm / bc1) / (jnp.sqrt(nv / bc2) + ADAM_EPS) + ADAM_WD * w_ref[...])

    spec = pl.BlockSpec((tr, cols), lambda i: (i, 0))
    sh = jax.ShapeDtypeStruct((rows, cols), F32)
    return pl.pallas_call(
        body, name=name, out_shape=(sh, sh, sh), grid=(rows // tr,), in_specs=[spec] * 4,
        out_specs=(spec, spec, spec), compiler_params=_params("parallel"),
    )(w, g, m, v)


WEIGHTS = ['mix_norm_g', 'w_in', 'conv_dw_w', 'conv_dw_b', 'conv_ln_g', 'conv_ln_b', 'conv_pw_w', 'lru_conv_w',
           'lru_conv_b', 'lru_wa', 'lru_ba', 'lru_wx', 'lru_bx', 'lru_lambda', 'out_norm_conv', 'out_norm_attn',
           'out_norm_lru', 'w_out', 'xattn_norm_g', 'mem_norm_g', 'xattn_wq', 'xattn_wkv', 'xattn_wo',
           'final_norm_g']
BIG_SHARDED = {'w_in': 2, 'conv_pw_w': 1, 'w_out': 1, 'xattn_wq': 1, 'xattn_wkv': 1, 'xattn_wo': 2}
SMALL_SHARDED = {'conv_dw_w': 2, 'lru_conv_w': 2}


def _layer_fwd(x, mem, p, l, first, rest):
    row = lambda name: p[name][l][None, :]
    c = p['conv_dw_b'].shape[1]
    heads = 2 * c // HEAD
    h = _rms_fwd(x, row('mix_norm_g'), "rms_mix")
    w_in, tied = first(h)
    u = _mm(h, w_in, dep=tied, name="in_proj")
    q_blk = 3 * c // HEAD
    y_attn, tot = _sb_fwd(u, heads, q_blk, q_blk + heads, q_blk + 2 * heads, "sb_fwd")
    fw = dict(rest(y_attn), w_in=w_in)
    wpad = jnp.pad(fw['conv_dw_w'][l], ((0, 1), (0, 0)))
    cw = jnp.pad(fw['lru_conv_w'][l], ((0, 4), (0, 0)))
    wa, wx = p['lru_wa'][l].astype(BF16), p['lru_wx'][l].astype(BF16)
    y_lru = _lru_fwd(u, 11, cw, row('lru_conv_b'), wa, row('lru_ba'), wx, row('lru_bx'), row('lru_lambda'), "lru_fwd")
    y_conv = _conv_fwd(u, wpad, row('conv_dw_b'), row('conv_ln_g'), row('conv_ln_b'), fw['conv_pw_w'][None], 0,
                       "conv_fwd")
    yc = _outgate_fwd(y_conv, y_attn, y_lru, u, row('out_norm_conv'), row('out_norm_attn'), row('out_norm_lru'),
                      "outgate_fwd")
    x1 = _mm(yc, fw['w_out'], add=x, name="out_proj")
    h2 = _rms_fwd(x1, row('xattn_norm_g'), "rms_xattn")
    memn = _rms_fwd(mem, row('mem_norm_g'), "rms_mem")
    q2 = _mm(h2, fw['xattn_wq'], name="xq_proj")
    kv = _mm(memn, fw['xattn_wkv'], name="xkv_proj")
    o2 = _xattn_fwd(q2, kv, "xattn_fwd")
    x2 = _mm(o2, fw['xattn_wo'], add=x1, name="xo_proj")
    saved = dict(x=x, h=h, u=u, wpad=wpad, cw=cw, wa=wa, wx=wx, y_conv=y_conv, y_attn=y_attn, tot=tot, y_lru=y_lru,
                 yc=yc, x1=x1, h2=h2, memn=memn, q2=q2, kv=kv, o2=o2, fw=fw)
    return x2, saved


def _layer_bwd(dx2, mem, p, l, sv, rest_ready, w_in_ready):
    row = lambda name: p[name][l][None, :]
    c = p['conv_dw_b'].shape[1]
    heads = 2 * c // HEAD
    fw = sv['fw']
    g, big = {}, {}
    big['xattn_wo'] = _mm(sv['o2'], dx2, ta=True, out_dtype=BF16, name="d_wo")
    do2 = _mm(dx2, fw['xattn_wo'], tb=True, name="d_o2")
    dq2, dkv = _xattn_bwd(sv['q2'], sv['kv'], do2, "xattn_bwd")
    big['xattn_wq'] = _mm(sv['h2'], dq2, ta=True, out_dtype=BF16, name="d_wq")
    dh2 = _mm(dq2, fw['xattn_wq'], tb=True, name="d_h2")
    big['xattn_wkv'] = _mm(sv['memn'], dkv, ta=True, out_dtype=BF16, name="d_wkv")
    dmemn = _mm(dkv, fw['xattn_wkv'], tb=True, name="d_memn")
    _, g['mem_norm_g'] = _rms_bwd(mem, row('mem_norm_g'), dmemn, None, "rms_mem_bwd")
    dx1, g['xattn_norm_g'] = _rms_bwd(sv['x1'], row('xattn_norm_g'), dh2, dx2, "rms_xattn_bwd")
    big['w_out'] = _mm(sv['yc'], dx1, ta=True, out_dtype=BF16, name="d_wout")
    dyc = _mm(dx1, fw['w_out'], tb=True, name="d_yc")
    u = sv['u']
    d_yconv, d_yattn, d_ylru, dgc, dga, dgl, dn = _outgate_bwd(
        dyc, sv['y_conv'], sv['y_attn'], sv['y_lru'], u, row('out_norm_conv'), row('out_norm_attn'),
        row('out_norm_lru'), "outgate_bwd")
    g['out_norm_conv'], g['out_norm_attn'], g['out_norm_lru'] = dn[0, 0:c], dn[0, c:3 * c], dn[0, 3 * c:4 * c]
    dd, gl, dpw, cvec = _conv_bwd_post(u, d_yconv, sv['wpad'], row('conv_dw_b'), row('conv_ln_g'),
                                       row('conv_ln_b'), fw['conv_pw_w'][None], 0, "conv_bwd_post")
    big['conv_pw_w'] = dpw.astype(BF16)
    g['conv_ln_g'], g['conv_ln_b'], g['conv_dw_b'] = cvec[0], cvec[1], cvec[2]
    tied = rest_ready(big)
    dval, dglu, ddw = _conv_bwd_dw(u, dd, gl, sv['wpad'] + tied[0:1, 0:1], "conv_bwd_dw")
    g['conv_dw_w'] = ddw[0:31]
    q_blk = 3 * c // HEAD
    dq, dk, dv = _sb_bwd(u, sv['tot'], d_yattn, heads, q_blk, q_blk + heads, q_blk + 2 * heads, "sb_bwd")
    dxr, g['lru_wa'], g['lru_wx'], lvec = _lru_bwd(
        u, 11, sv['y_lru'], d_ylru, sv['cw'], row('lru_conv_b'), sv['wa'], row('lru_ba'), sv['wx'], row('lru_bx'),
        row('lru_lambda'), "lru_bwd")
    g['lru_ba'], g['lru_bx'], g['lru_lambda'], g['lru_conv_b'] = lvec[0], lvec[1], lvec[2], lvec[3]
    g['lru_conv_w'] = lvec[4:8]
    du = jnp.concatenate([dval, dglu, dgc, dq.astype(BF16), dk.astype(BF16), dv.astype(BF16), dga, dxr, dgl], axis=1)
    tied = w_in_ready(_mm(sv['h'], du, ta=True, out_dtype=BF16, name="d_win"))
    dh = _mm(du, fw['w_in'], tb=True, dep=tied, name="d_h")
    dx0, g['mix_norm_g'] = _rms_bwd(sv['x'], row('mix_norm_g'), dh, dx1, "rms_mix_bwd")
    return dx0, g


def kernel(x, mem, mix_norm_g, w_in, conv_dw_w, conv_dw_b, conv_ln_g, conv_ln_b, conv_pw_w, lru_conv_w, lru_conv_b, lru_wa, lru_ba, lru_wx, lru_bx, lru_lambda, out_norm_conv, out_norm_attn, out_norm_lru, w_out, xattn_norm_g, mem_norm_g, xattn_wq, xattn_wkv, xattn_wo, final_norm_g, loss_target, m_mix_norm_g, m_w_in, m_conv_dw_w, m_conv_dw_b, m_conv_ln_g, m_conv_ln_b, m_conv_pw_w, m_lru_conv_w, m_lru_conv_b, m_lru_wa, m_lru_ba, m_lru_wx, m_lru_bx, m_lru_lambda, m_out_norm_conv, m_out_norm_attn, m_out_norm_lru, m_w_out, m_xattn_norm_g, m_mem_norm_g, m_xattn_wq, m_xattn_wkv, m_xattn_wo, m_final_norm_g, v_mix_norm_g, v_w_in, v_conv_dw_w, v_conv_dw_b, v_conv_ln_g, v_conv_ln_b, v_conv_pw_w, v_lru_conv_w, v_lru_conv_b, v_lru_wa, v_lru_ba, v_lru_wx, v_lru_bx, v_lru_lambda, v_out_norm_conv, v_out_norm_attn, v_out_norm_lru, v_w_out, v_xattn_norm_g, v_mem_norm_g, v_xattn_wq, v_xattn_wkv, v_xattn_wo, v_final_norm_g):
    args = locals()
    w = {n: args[n] for n in WEIGHTS}
    mom = {n: args["m_" + n] for n in WEIGHTS}
    var = {n: args["v_" + n] for n in WEIGHTS}
    depth = w_in.shape[0]
    c = conv_dw_b.shape[1]
    assert out_norm_attn.shape[1] == 2 * c and lru_lambda.shape[1] == c and w_in.shape[2] * N_DEV == 13 * c
    assert c % HEAD == 0 and x.shape[0] == 1 and mem.shape[0] == 1
    xs, mems, tgt = x[0], mem[0], loss_target[0]
    me = 4 * lax.axis_index("x") + 2 * lax.axis_index("y") + lax.axis_index("c")

    assert depth == 2 and (2 * w_in.shape[2]) % LANE == 0 and w_in.shape[2] % LANE in (0, LANE // 2)
    my_c = lax.axis_index("c")
    big = list(BIG_SHARDED)
    dev = [n for n in big if n != 'w_in']
    my_x, my_y = lax.axis_index("x"), lax.axis_index("y")
    small = list(SMALL_SHARDED)
    p = dict(w)
    blk_width = w_in.shape[2]
    w_in_blk = jnp.pad(w_in.astype(BF16), ((0, 0), (0, 0), (0, -blk_width % LANE)))
    w_in0_blocks = _all_gather(w_in_blk[0], "gather_w_in0")
    w_in0 = _join_cols(w_in0_blocks, blk_width, "join_w_in")
    taps = _pack([w[n] for n in small], F32)[None]

    def gather_group(names, l, stage, dep, tag, with_taps=False):
        srcs, shapes, plans = [], [], []
        if with_taps:
            srcs, shapes, plans = [taps], [(N_DEV,) + taps.shape[1:]], [_gather_plan(0, 1, stage)]
        for n in names:
            if n == 'w_in':
                srcs.append(w_in_blk[l][None])
                shapes.append((N_DEV,) + w_in_blk.shape[1:])
                plans.append(_gather_plan(0, 1, stage))
            else:
                axis = BIG_SHARDED[n] - 1
                blk = w[n][l].astype(BF16)
                width = blk.shape[axis]
                srcs.append(blk)
                shapes.append(blk.shape[:axis] + (N_DEV * width,) + blk.shape[axis + 1:])
                plans.append(_gather_plan(axis, width, stage))
        return _split_start(srcs, shapes, plans, dep, "gather_start_" + tag)

    def forward_group(names, landed, dep, tag):
        plans = [_forward_plan(0, 1) if n == 'w_in' else _forward_plan(BIG_SHARDED[n] - 1, w[n].shape[BIG_SHARDED[n]])
                 for n in names]
        return _split_start(landed, [None] * len(names), plans, dep, "gather_start_" + tag)

    g0, tied = gather_group(dev, 0, "direct", w_in0_blocks, "rest0", with_taps=True)
    g1, tied = gather_group(big, 1, "first", tied, "layer1")
    layer1, full_taps = {}, {}

    def first_of(l):
        def first(after):
            if l == 0:
                return w_in0, tied
            layer1.update(zip(big, _split_wait(layer1.pop('forwarding'), after, "gather_wait_layer1_passed")))
            layer1['w_in'] = _join_cols(layer1['w_in'], blk_width, "join_w_in")
            return layer1['w_in'], None
        return first

    def rest_of(l):
        def rest(after):
            if l == 0:
                landed = _split_wait(g0, after, "gather_wait_rest0")
                full_taps.update({n: _join_blocks(blk, SMALL_SHARDED[n]) for n, blk in
                                  zip(small, _unpack(landed[0], [w[n].shape for n in small], lead=N_DEV))})
                got = dict(zip(dev, landed[1:]), **full_taps)
                landed = _split_wait(g1, after, "gather_wait_layer1")
                layer1['forwarding'], token = forward_group(big, landed, landed[0], "layer1_passed")
                got['conv_pw_w'] = got['conv_pw_w'] + token[0:1, 0:1].astype(BF16)
                return got
            return dict({n: layer1[n] for n in dev}, **full_taps)
        return rest

    saved = []
    act = xs
    for l in range(depth):
        act, sv = _layer_fwd(act, mems, p, l, first_of(l), rest_of(l))
        saved.append(sv)
    loss_part, dact, d_final = _loss_bwd(act, final_norm_g[None, :], tgt, "loss_bwd")

    def grad_window(n):
        axis = BIG_SHARDED[n] - 1
        blk = w[n].shape[axis + 1]
        pad = blk % LANE if axis == 1 else 0
        return axis, blk + pad, lambda px, py, pc: blk * (4 * px + 2 * py + pc) - pad * pc

    scattering = {}

    def scatter_group(names, arrs, l, tag):
        shapes, plans = [], []
        for n, g in zip(names, arrs):
            axis, width, start = grad_window(n)
            shapes.append((N_DEV,) + g.shape[:axis] + (width,) + g.shape[axis + 1:])
            plans.append(_scatter_plan(axis, width, start))
        scattering[(l, tag)], token = _split_start(arrs, shapes, plans, arrs[0], f"scatter_start_{tag}{l}")
        return token

    layer_grads = [None] * depth
    for l in reversed(range(depth)):
        dact, layer_grads[l] = _layer_bwd(
            dact, mems, p, l, saved[l],
            lambda big_grads, l=l: scatter_group(dev, [big_grads[n] for n in dev], l, "rest"),
            lambda g_w_in, l=l: scatter_group(['w_in'], [g_w_in], l, "w_in"))
    grad_x = dact[None]
    rest = [n for n in WEIGHTS if n not in BIG_SHARDED]
    partial = {n: jnp.stack([layer_grads[l][n] for l in range(depth)]) for n in rest if n != 'final_norm_g'}
    partial['final_norm_g'] = d_final[0]

    vec = _pack([partial[n] for n in rest] + [loss_part], F32)
    small_grads, tied = _split_start([vec[None]], [(N_DEV,) + vec.shape], [_gather_plan(0, 1, "direct")], vec,
                                     "small_grads_start")

    grads, delta, new_m, new_v = {}, {}, {}, {}

    def update(n, received):
        summed = jnp.stack([_sum_blocks(received[l], "sum_" + n) for l in range(depth)])
        width = w[n].shape[-1]
        if summed.shape[-1] != width:
            summed = jnp.where(my_c == 0, summed[..., :width], summed[..., summed.shape[-1] - width:])
        grads[n] = summed
        shape = w[n].shape
        two_d = lambda a: a.reshape(-1, shape[-1])
        d, nm, nv = _adamw(two_d(w[n]), two_d(summed), two_d(mom[n]), two_d(var[n]), "adamw_" + n)
        delta[n], new_m[n], new_v[n] = d.reshape(shape), nm.reshape(shape), nv.reshape(shape)

    landed = [_split_wait(scattering[(l, "rest")], tied, f"scatter_wait_rest{l}") for l in range(depth)]
    for t, n in enumerate(dev):
        update(n, [landed[l][t] for l in range(depth)])
    after = delta[dev[-1]]
    update('w_in', [_split_wait(scattering[(l, "w_in")], after, f"scatter_wait_w_in{l}")[0] for l in range(depth)])
    total = _sum_blocks(_split_wait(small_grads, delta['w_in'], "small_grads_wait")[0], "sum_small_grads")
    pieces = _unpack(total, [partial[n].shape for n in rest] + [(1, 1)])
    loss = pieces[-1][0, 0]
    for n, piece in zip(rest, pieces[:-1]):
        if n in SMALL_SHARDED:
            width = w[n].shape[2]
            piece = lax.dynamic_slice_in_dim(piece, me * width, width, axis=2)
        grads[n] = piece
    shapes = [w[n].shape for n in rest]
    packed = [_pack([src[n] for n in rest], F32) for src in (w, grads, mom, var)]
    outs = _adamw(*packed, "adamw_small")
    for dst, o in zip((delta, new_m, new_v), outs):
        dst.update(dict(zip(rest, _unpack(o, shapes))))

    return (loss, grad_x, *[grads[n] for n in WEIGHTS], *[delta[n] for n in WEIGHTS],
            *[new_m[n] for n in WEIGHTS], *[new_v[n] for n in WEIGHTS])
```

```python
import functools
import math

import jax
import jax.numpy as jnp
from jax import lax
from jax.experimental import pallas as pl
from jax.experimental.pallas import tpu as pltpu

F32 = jnp.float32
BF16 = jnp.bfloat16

N_DEV = 8
LANE = 128
HEAD = 128
VMEM_LIMIT = 56 * 1024 * 1024
PACK_COLS = 512
RMS_EPS = 1e-6
LN_EPS = 1e-5
LRU_C = 8.0
CONV_HALO = 32
LRU_HALO = 8

ADAM_LR, ADAM_B1, ADAM_B2, ADAM_EPS, ADAM_WD, ADAM_STEP = 0.001, 0.9, 0.999, 1e-08, 0.01, 10

MESH = pl.DeviceIdType.MESH


def _tile(n, cands):
    for c in cands:
        if n % c == 0:
            return c
    raise ValueError(f"no tile of {cands} divides {n}")


def _params(*sem):
    return pltpu.CompilerParams(dimension_semantics=sem, vmem_limit_bytes=VMEM_LIMIT)


def _dot(a, b):
    return lax.dot_general(a, b, (((1,), (0,)), ((), ())), preferred_element_type=F32)


def _dot_nt(a, b):
    return lax.dot_general(a, b, (((1,), (1,)), ((), ())), preferred_element_type=F32)


def _dot_tn(a, b):
    return lax.dot_general(a, b, (((0,), (0,)), ((), ())), preferred_element_type=F32)


def _sigmoid(x):
    return 1.0 / (1.0 + jnp.exp(-x))


def _expm1(x):
    series = x * (1.0 + x * (0.5 + x * (1.0 / 6.0 + x * (1.0 / 24.0))))
    return jnp.where(jnp.abs(x) < 0.05, series, jnp.exp(x) - 1.0)


def _my_place():
    return lax.axis_index("x"), lax.axis_index("y"), lax.axis_index("c")


def _flip(v, d):
    return 1 - v if d else v


def _window(ref, axis, start, size):
    return ref.at[tuple(pl.ds(start, size) if a == axis else pl.ds(0, ref.shape[a]) for a in range(len(ref.shape)))]


def _all_gather(x2d, name):
    rows, cols = x2d.shape

    def body(x_ref, out_ref, send_sems, recv_sems, local_sem):
        x, y, c = _my_place()
        me, sibling = (x, y, c), (x, y, 1 - c)
        chips = [(1 - x, y), (x, 1 - y), (1 - x, 1 - y)]

        def blk(px, py, pc):
            return out_ref.at[4 * px + 2 * py + pc]

        def copy(k, block, to, src=None):
            return pltpu.make_async_remote_copy(
                src_ref=blk(*block) if src is None else src, dst_ref=blk(*block),
                send_sem=send_sems.at[k], recv_sem=recv_sems.at[k], device_id=to, device_id_type=MESH)

        mine = pltpu.make_async_copy(x_ref, blk(*me), local_sem)
        mine.start()
        first = [copy(0, me, sibling, src=x_ref)]
        first += [copy(1 + j, me, (*chip, c), src=x_ref) for j, chip in enumerate(chips)]
        for cp in first:
            cp.start()
        passed = [copy(4 + j, (*chip, c), sibling) for j, chip in enumerate(chips)]
        for j, chip in enumerate(chips):
            copy(1 + j, (*chip, c), me).wait_recv()
            passed[j].start()
        copy(0, sibling, me).wait_recv()
        for j, chip in enumerate(chips):
            copy(4 + j, (*chip, 1 - c), me).wait_recv()
        for cp in first + passed:
            cp.wait_send()
        mine.wait()

    return pl.pallas_call(
        body, name=name,
        out_shape=jax.ShapeDtypeStruct((N_DEV, rows, cols), x2d.dtype),
        in_specs=[pl.BlockSpec(memory_space=pl.ANY)],
        out_specs=pl.BlockSpec(memory_space=pl.ANY),
        scratch_shapes=[pltpu.SemaphoreType.DMA((7,)), pltpu.SemaphoreType.DMA((7,)), pltpu.SemaphoreType.DMA],
    )(x2d)


HBM_SPEC = pl.BlockSpec(memory_space=pltpu.HBM)
SEM_SPEC = pl.BlockSpec(memory_space=pltpu.SEMAPHORE)
SPLIT_COPY = pltpu.SideEffectType.DATAFLOW_SIDE_EFFECTING


class _Plan:
    def __init__(self, copies, own, total):
        self.copies, self.own, self.total = copies, own, total


def _scatter_plan(axis, width, start):
    def win(src, px, py, pc):
        return _window(src, axis, pl.multiple_of(start(px, py, pc), math.gcd(width, 1024)), width)

    def copies(src, zone, x, y, c):
        out = []
        for k in range(1, N_DEV):
            px, py, pc = _flip(x, k & 4), _flip(y, k & 2), _flip(c, k & 1)
            out.append((win(src, px, py, pc), zone.at[4 * x + 2 * y + c], (px, py, pc)))
        return out

    return _Plan(copies, lambda src, zone, x, y, c: (win(src, x, y, c), zone.at[4 * x + 2 * y + c]),
                 lambda zone: zone.at[pl.ds(0, N_DEV - 1)])


def _gather_plan(axis, width, stage):
    flips = range(1, N_DEV) if stage == "direct" else (1, 2, 4, 6)

    def mine(zone, x, y, c):
        return _window(zone, axis, width * (4 * x + 2 * y + c), width)

    def copies(src, zone, x, y, c):
        return [(src, mine(zone, x, y, c), (_flip(x, k & 4), _flip(y, k & 2), _flip(c, k & 1))) for k in flips]

    return _Plan(copies, lambda src, zone, x, y, c: (src, mine(zone, x, y, c)),
                 lambda zone: _window(zone, axis, 0, len(flips) * width))


def _forward_plan(axis, width):
    def copies(src, zone, x, y, c):
        out = []
        for px, py in ((1 - x, y), (x, 1 - y), (1 - x, 1 - y)):
            win = _window(zone, axis, width * (4 * px + 2 * py + c), width)
            out.append((win, win, (x, y, 1 - c)))
        return out

    return _Plan(copies, lambda src, zone, x, y, c: None, lambda zone: _window(zone, axis, 0, 3 * width))


def _split_start(srcs, zone_shapes, plans, dep, name):
    n = len(srcs)
    zones = [lax.empty(shape, a.dtype) for shape, a in zip(zone_shapes, srcs) if shape is not None]
    m = len(zones)
    zone_of = [None if shape is None else sum(s is not None for s in zone_shapes[:t])
               for t, shape in enumerate(zone_shapes)]

    def body(*refs):
        ins, fresh = refs[:n], refs[n:n + m]
        sems, token = refs[n + m + 1:n + m + 1 + 3 * n], refs[-1]
        send_sems, recv_sems, local_sems = sems[:n], sems[n:2 * n], sems[2 * n:]
        x, y, c = _my_place()
        for t in range(n):
            land = ins[t] if zone_of[t] is None else fresh[zone_of[t]]
            own = plans[t].own(ins[t], land, x, y, c)
            if own is not None:
                pltpu.make_async_copy(*own, local_sems[t]).start()
            for src, dst, target in plans[t].copies(ins[t], land, x, y, c):
                pltpu.make_async_remote_copy(src_ref=src, dst_ref=dst, send_sem=send_sems[t], recv_sem=recv_sems[t],
                                             device_id=target, device_id_type=MESH).start()
        token[...] = jnp.zeros_like(token)

    thru = [pltpu.HBM(a.shape, a.dtype) for a in list(srcs) + zones]
    outs = pl.pallas_call(
        body, name=name,
        out_shape=[pltpu.SemaphoreType.DMA(())] * (3 * n) + thru + [jax.ShapeDtypeStruct((8, LANE), F32)],
        in_specs=[HBM_SPEC] * (n + m) + [pl.BlockSpec(memory_space=pl.ANY)],
        out_specs=[SEM_SPEC] * (3 * n) + [HBM_SPEC] * (n + m) + [pl.BlockSpec(memory_space=pltpu.VMEM)],
        input_output_aliases={i: 3 * n + i for i in range(n + m)},
        compiler_params=pltpu.CompilerParams(has_side_effects=SPLIT_COPY),
    )(*[pltpu.with_memory_space_constraint(a, pltpu.HBM) for a in list(srcs) + zones], dep)
    return (outs[:3 * n], outs[3 * n:4 * n], outs[4 * n:4 * n + m], plans, zone_of), outs[-1]


def _split_wait(pending, after, name):
    sems, srcs, zones, plans, zone_of = pending
    n, m = len(srcs), len(zones)

    def body(*refs):
        ins, fresh, sems = refs[:n], refs[n:n + m], refs[n + m:n + m + 3 * n]
        send_sems, recv_sems, local_sems = sems[:n], sems[n:2 * n], sems[2 * n:]
        x, y, c = _my_place()
        for t in range(n):
            land = ins[t] if zone_of[t] is None else fresh[zone_of[t]]
            total = plans[t].total(land)
            done = pltpu.make_async_remote_copy(src_ref=total, dst_ref=total, send_sem=send_sems[t],
                                                recv_sem=recv_sems[t], device_id=(x, y, 1 - c), device_id_type=MESH)
            done.wait_send()
            done.wait_recv()
            own = plans[t].own(ins[t], land, x, y, c)
            if own is not None:
                pltpu.make_async_copy(*own, local_sems[t]).wait()

    thru = [pltpu.HBM(a.shape, a.dtype) for a in list(srcs) + list(zones)]
    outs = pl.pallas_call(
        body, name=name, out_shape=thru,
        in_specs=[HBM_SPEC] * (n + m) + [SEM_SPEC] * (3 * n) + [pl.BlockSpec(memory_space=pl.ANY)],
        out_specs=[HBM_SPEC] * (n + m), input_output_aliases={i: i for i in range(n + m)},
        compiler_params=pltpu.CompilerParams(has_side_effects=SPLIT_COPY),
    )(*srcs, *zones, *sems, after)
    return [outs[t] if zone_of[t] is None else outs[n + zone_of[t]] for t in range(n)]


def _join_cols(blocks, width, name):
    nb, rows, padded = blocks.shape
    tr = _tile(rows, (256, 128, 64, 32, 16))

    def body(x_ref, o_ref):
        for b in range(nb):
            o_ref[:, b * width:(b + 1) * width] = x_ref[b, :, 0:width]

    return pl.pallas_call(
        body, name=name, out_shape=jax.ShapeDtypeStruct((rows, nb * width), blocks.dtype), grid=(rows // tr,),
        in_specs=[pl.BlockSpec((nb, tr, padded), lambda i: (0, i, 0))],
        out_specs=pl.BlockSpec((tr, nb * width), lambda i: (i, 0)), compiler_params=_params("parallel"),
    )(blocks)


def _sum_blocks(x3d, name):
    n, rows, cols = x3d.shape
    tr = _tile(rows, (512, 256, 128, 64, 32, 16))

    def body(x_ref, o_ref):
        acc = x_ref[0].astype(F32)
        for j in range(1, n):
            acc = acc + x_ref[j].astype(F32)
        o_ref[...] = acc

    return pl.pallas_call(
        body, name=name, out_shape=jax.ShapeDtypeStruct((rows, cols), F32), grid=(rows // tr,),
        in_specs=[pl.BlockSpec((n, tr, cols), lambda i: (0, i, 0))],
        out_specs=pl.BlockSpec((tr, cols), lambda i: (i, 0)),
        compiler_params=_params("parallel"),
    )(x3d)


def _pack(arrs, dtype, lead=None):
    if lead is None:
        flat = jnp.concatenate([a.reshape(-1).astype(dtype) for a in arrs])
        n = flat.shape[0]
        total = -(-n // (16 * PACK_COLS)) * (16 * PACK_COLS)
        return jnp.pad(flat, (0, total - n)).reshape(-1, PACK_COLS)
    flat = jnp.concatenate([a.reshape(lead, -1).astype(dtype) for a in arrs], axis=1)
    n = flat.shape[1]
    total = -(-n // (16 * PACK_COLS)) * (16 * PACK_COLS)
    return jnp.pad(flat, ((0, 0), (0, total - n))).reshape(lead, -1, PACK_COLS)


def _unpack(packed, shapes, lead=None):
    out, off = [], 0
    if lead is None:
        flat = packed.reshape(-1)
        for s in shapes:
            n = math.prod(s)
            out.append(flat[off:off + n].reshape(s))
            off += n
        return out
    flat = packed.reshape(lead, -1)
    for s in shapes:
        n = math.prod(s)
        out.append(flat[:, off:off + n].reshape((lead,) + tuple(s)))
        off += n
    return out


def _join_blocks(g, axis):
    g = jnp.moveaxis(g, 0, axis)
    s = g.shape
    return g.reshape(s[:axis] + (s[axis] * s[axis + 1],) + s[axis + 2:])


def _mm_tiles(m, n, kdim, a_bytes):
    tk = kdim if kdim <= 2048 else _tile(kdim, (2048, 1664, 1024, 832, 512, 416, 256, 128))
    tm = _tile(m, (1024, 512, 256, 128, 64, 32, 16))
    tn = _tile(n, (1024, 512, 256, 128))

    def vmem(tm, tn):
        return 2 * tm * tk * a_bytes + 2 * tn * tk * 2 + 3 * tm * tn * 4

    while vmem(tm, tn) > VMEM_LIMIT * 3 // 4 and tn > 128 and tn % 256 == 0:
        tn //= 2
    while vmem(tm, tn) > VMEM_LIMIT * 3 // 4 and tm > 128 and tm % 256 == 0:
        tm //= 2
    return tm, tn, tk


def _mm(a, b, *, ta=False, tb=False, bl=None, out_dtype=F32, add=None, dep=None, name):
    if ta:
        kdim, m = a.shape
    else:
        m, kdim = a.shape
    bshape = b.shape if bl is None else b.shape[1:]
    n = bshape[0] if tb else bshape[1]
    tm, tn, tk = _mm_tiles(m, n, kdim, a.dtype.itemsize)
    nk = kdim // tk
    a_spec = pl.BlockSpec((tk, tm), lambda i, j, k: (k, i)) if ta else pl.BlockSpec((tm, tk), lambda i, j, k: (i, k))
    b_blk, b_idx = ((tn, tk), lambda i, j, k: (j, k)) if tb else ((tk, tn), lambda i, j, k: (k, j))
    if bl is None:
        b_spec = pl.BlockSpec(b_blk, b_idx)
    else:
        b_spec = pl.BlockSpec((None,) + b_blk, lambda i, j, k: (bl,) + b_idx(i, j, k))
    o_spec = pl.BlockSpec((tm, tn), lambda i, j, k: (i, j))
    dims = (((0 if ta else 1,), (1 if tb else 0,)), ((), ()))

    n_in = 2 + (add is not None) + (dep is not None)

    def body(*refs):
        a_ref, b_ref = refs[:2]
        add_ref = refs[2] if add is not None else None
        o_ref = refs[n_in]

        def finish(r):
            if add is not None:
                r = r + add_ref[...]
            o_ref[...] = r.astype(out_dtype)

        part = lax.dot_general(a_ref[...].astype(BF16), b_ref[...].astype(BF16), dims, preferred_element_type=F32)
        if nk == 1:
            finish(part)
            return
        acc_ref = refs[-1]
        k = pl.program_id(2)

        @pl.when(k == 0)
        def _():
            acc_ref[...] = part

        @pl.when(k > 0)
        def _():
            acc_ref[...] += part

        @pl.when(k == nk - 1)
        def _():
            finish(acc_ref[...])

    ins, specs = [a, b], [a_spec, b_spec]
    if add is not None:
        ins.append(add)
        specs.append(o_spec)
    if dep is not None:
        ins.append(dep)
        specs.append(pl.BlockSpec(memory_space=pl.ANY))
    return pl.pallas_call(
        body, name=name, out_shape=jax.ShapeDtypeStruct((m, n), out_dtype), grid=(m // tm, n // tn, nk),
        in_specs=specs, out_specs=o_spec, scratch_shapes=[pltpu.VMEM((tm, tn), F32)] if nk > 1 else [],
        compiler_params=_params("parallel", "parallel", "arbitrary"),
    )(*ins)


def _rms_fwd(x, g, name):
    s, d = x.shape
    tr = _tile(s, (256, 128, 64, 32, 16))

    def body(x_ref, g_ref, o_ref):
        xv = x_ref[...]
        r = lax.rsqrt(jnp.mean(xv * xv, axis=-1, keepdims=True) + RMS_EPS)
        o_ref[...] = (xv * r * g_ref[...]).astype(BF16)

    return pl.pallas_call(
        body, name=name, out_shape=jax.ShapeDtypeStruct((s, d), BF16), grid=(s // tr,),
        in_specs=[pl.BlockSpec((tr, d), lambda i: (i, 0)), pl.BlockSpec((1, d), lambda i: (0, 0))],
        out_specs=pl.BlockSpec((tr, d), lambda i: (i, 0)), compiler_params=_params("parallel"),
    )(x, g)


def _rms_bwd(x, g, dh, resid, name):
    s, d = x.shape
    tr = _tile(s, (256, 128, 64, 32, 16))

    def body(*refs):
        if resid is None:
            x_ref, g_ref, dh_ref, dx_ref, dg_ref = refs
        else:
            x_ref, g_ref, dh_ref, res_ref, dx_ref, dg_ref = refs

        @pl.when(pl.program_id(0) == 0)
        def _():
            dg_ref[...] = jnp.zeros_like(dg_ref)

        xv = x_ref[...]
        r = lax.rsqrt(jnp.mean(xv * xv, axis=-1, keepdims=True) + RMS_EPS)
        xh = xv * r
        dhv = dh_ref[...]
        dg_ref[0:1, :] += jnp.sum(dhv * xh, axis=0, keepdims=True)
        dyn = dhv * g_ref[...]
        dx = r * (dyn - xh * jnp.mean(dyn * xh, axis=-1, keepdims=True))
        if resid is not None:
            dx = dx + res_ref[...]
        dx_ref[...] = dx

    row = pl.BlockSpec((tr, d), lambda i: (i, 0))
    ins = [x, g, dh] + ([] if resid is None else [resid])
    specs = [row, pl.BlockSpec((1, d), lambda i: (0, 0)), row] + ([] if resid is None else [row])
    dx, dg = pl.pallas_call(
        body, name=name,
        out_shape=(jax.ShapeDtypeStruct((s, d), F32), jax.ShapeDtypeStruct((8, d), F32)), grid=(s // tr,),
        in_specs=specs, out_specs=(row, pl.BlockSpec((8, d), lambda i: (0, 0))),
        compiler_params=_params("arbitrary"),
    )(*ins)
    return dx, dg[0]


def _loss_bwd(x, g, tgt, name):
    s, d = x.shape
    tr = _tile(s, (256, 128, 64, 32, 16))

    def body(x_ref, g_ref, t_ref, dx_ref, dg_ref, loss_ref):
        @pl.when(pl.program_id(0) == 0)
        def _():
            dg_ref[...] = jnp.zeros_like(dg_ref)
            loss_ref[...] = jnp.zeros_like(loss_ref)

        xv = x_ref[...]
        r = lax.rsqrt(jnp.mean(xv * xv, axis=-1, keepdims=True) + RMS_EPS)
        xh = xv * r
        e = xh * g_ref[...] - t_ref[...]
        per_tok = jnp.mean(e * e, axis=-1, keepdims=True)
        loss_ref[...] += 0.5 * jnp.sum(per_tok, axis=0, keepdims=True)
        dy = e * (1.0 / d)
        dg_ref[0:1, :] += jnp.sum(dy * xh, axis=0, keepdims=True)
        dyn = dy * g_ref[...]
        dx_ref[...] = r * (dyn - xh * jnp.mean(dyn * xh, axis=-1, keepdims=True))

    row = pl.BlockSpec((tr, d), lambda i: (i, 0))
    dx, dg, loss = pl.pallas_call(
        body, name=name,
        out_shape=(jax.ShapeDtypeStruct((s, d), F32), jax.ShapeDtypeStruct((8, d), F32),
                   jax.ShapeDtypeStruct((8, LANE), F32)),
        grid=(s // tr,),
        in_specs=[row, pl.BlockSpec((1, d), lambda i: (0, 0)), row],
        out_specs=(row, pl.BlockSpec((8, d), lambda i: (0, 0)), pl.BlockSpec((8, LANE), lambda i: (0, 0))),
        compiler_params=_params("arbitrary"),
    )(x, g, tgt)
    return loss[0:1, 0:1], dx, dg[0:1]


def _conv_taps(gbuf, w_ref, tt, ntap, lo):
    acc = w_ref[0:1, :] * gbuf[pl.ds(lo, tt), :]
    for k in range(1, ntap):
        acc = acc + w_ref[k:k + 1, :] * gbuf[pl.ds(lo + k, tt), :]
    return acc


def _conv_time_tile(s):
    return _tile(s, (256, 128, 64, 32))


def _conv_fwd(u, wpad, dw_b, ln_g, ln_b, pw, l, name):
    s = u.shape[0]
    c = pw.shape[1]
    ntap = 31
    tt = _conv_time_tile(s)
    hb = tt // CONV_HALO

    def body(val_ref, glu_ref, valh_ref, gluh_ref, w_ref, b_ref, lg_ref, lb_ref, pw_ref, o_ref, gbuf):
        i = pl.program_id(0)
        glh = valh_ref[...] * _sigmoid(gluh_ref[...])
        gbuf[0:CONV_HALO, :] = jnp.where(i > 0, glh, 0.0)
        gbuf[CONV_HALO:CONV_HALO + tt, :] = val_ref[...] * _sigmoid(glu_ref[...])
        acc = _conv_taps(gbuf, w_ref, tt, ntap, CONV_HALO - (ntap - 1)) + b_ref[...]
        xc = acc - jnp.mean(acc, axis=-1, keepdims=True)
        rstd = lax.rsqrt(jnp.mean(xc * xc, axis=-1, keepdims=True) + LN_EPS)
        ln = xc * rstd * lg_ref[...] + lb_ref[...]
        sw = ln * _sigmoid(ln)
        o_ref[...] = _dot(sw.astype(BF16), pw_ref[...])

    vec = pl.BlockSpec((1, c), lambda i: (0, 0))
    return pl.pallas_call(
        body, name=name, out_shape=jax.ShapeDtypeStruct((s, c), F32), grid=(s // tt,),
        in_specs=[pl.BlockSpec((tt, c), lambda i: (i, 0)), pl.BlockSpec((tt, c), lambda i: (i, 1)),
                  pl.BlockSpec((CONV_HALO, c), lambda i: (jnp.maximum(i * hb - 1, 0), 0)),
                  pl.BlockSpec((CONV_HALO, c), lambda i: (jnp.maximum(i * hb - 1, 0), 1)),
                  pl.BlockSpec((32, c), lambda i: (0, 0)), vec, vec, vec,
                  pl.BlockSpec((None, c, c), lambda i: (l, 0, 0))],
        out_specs=pl.BlockSpec((tt, c), lambda i: (i, 0)),
        scratch_shapes=[pltpu.VMEM((CONV_HALO + tt, c), F32)],
        compiler_params=_params("parallel"),
    )(u, u, u, u, wpad, dw_b, ln_g, ln_b, pw)


def _conv_bwd_post(u, dyc, wpad, dw_b, ln_g, ln_b, pw, l, name):
    s = u.shape[0]
    c = pw.shape[1]
    ntap = 31
    tt = _conv_time_tile(s)
    hb = tt // CONV_HALO

    def body(val_ref, glu_ref, valh_ref, gluh_ref, dy_ref, w_ref, b_ref, lg_ref, lb_ref, pw_ref,
             dd_ref, gl_ref, dpw_ref, vec_ref, gbuf):
        i = pl.program_id(0)

        @pl.when(i == 0)
        def _():
            dpw_ref[...] = jnp.zeros_like(dpw_ref)
            vec_ref[...] = jnp.zeros_like(vec_ref)

        glh = valh_ref[...] * _sigmoid(gluh_ref[...])
        gbuf[0:CONV_HALO, :] = jnp.where(i > 0, glh, 0.0)
        gl = val_ref[...] * _sigmoid(glu_ref[...])
        gbuf[CONV_HALO:CONV_HALO + tt, :] = gl
        gl_ref[...] = gl
        acc = _conv_taps(gbuf, w_ref, tt, ntap, CONV_HALO - (ntap - 1)) + b_ref[...]
        xc = acc - jnp.mean(acc, axis=-1, keepdims=True)
        rstd = lax.rsqrt(jnp.mean(xc * xc, axis=-1, keepdims=True) + LN_EPS)
        xh = xc * rstd
        ln = xh * lg_ref[...] + lb_ref[...]
        sig = _sigmoid(ln)
        sw = ln * sig
        dyb = dy_ref[...].astype(BF16)
        dpw_ref[...] += _dot_tn(sw.astype(BF16), dyb)
        dsw = _dot_nt(dyb, pw_ref[...])
        dln = dsw * (sig * (1.0 + ln * (1.0 - sig)))
        vec_ref[0:1, :] += jnp.sum(dln * xh, axis=0, keepdims=True)
        vec_ref[1:2, :] += jnp.sum(dln, axis=0, keepdims=True)
        dxh = dln * lg_ref[...]
        dd = rstd * (dxh - jnp.mean(dxh, axis=-1, keepdims=True)
                     - xh * jnp.mean(dxh * xh, axis=-1, keepdims=True))
        vec_ref[2:3, :] += jnp.sum(dd, axis=0, keepdims=True)
        dd_ref[...] = dd

    vec = pl.BlockSpec((1, c), lambda i: (0, 0))
    tile = pl.BlockSpec((tt, c), lambda i: (i, 0))
    return pl.pallas_call(
        body, name=name,
        out_shape=(jax.ShapeDtypeStruct((s, c), F32), jax.ShapeDtypeStruct((s, c), F32),
                   jax.ShapeDtypeStruct((c, c), F32), jax.ShapeDtypeStruct((8, c), F32)),
        grid=(s // tt,),
        in_specs=[tile, pl.BlockSpec((tt, c), lambda i: (i, 1)),
                  pl.BlockSpec((CONV_HALO, c), lambda i: (jnp.maximum(i * hb - 1, 0), 0)),
                  pl.BlockSpec((CONV_HALO, c), lambda i: (jnp.maximum(i * hb - 1, 0), 1)),
                  tile, pl.BlockSpec((32, c), lambda i: (0, 0)), vec, vec, vec,
                  pl.BlockSpec((None, c, c), lambda i: (l, 0, 0))],
        out_specs=(tile, tile, pl.BlockSpec((c, c), lambda i: (0, 0)), pl.BlockSpec((8, c), lambda i: (0, 0))),
        scratch_shapes=[pltpu.VMEM((CONV_HALO + tt, c), F32)],
        compiler_params=_params("arbitrary"),
    )(u, u, u, u, dyc, wpad, dw_b, ln_g, ln_b, pw)


def _conv_bwd_dw(u, dd, gl, wpad, name):
    s, c = dd.shape
    ntap = 31
    tt = _conv_time_tile(s)
    hb = tt // CONV_HALO
    nt = s // tt
    last_halo = s // CONV_HALO - 1

    def body(val_ref, glu_ref, dd_ref, ddn_ref, gl_ref, glh_ref, w_ref, dval_ref, dglu_ref, dw_ref, dbuf, gbuf):
        i = pl.program_id(0)

        @pl.when(i == 0)
        def _():
            dw_ref[...] = jnp.zeros_like(dw_ref)

        d = dd_ref[...]
        dbuf[0:tt, :] = d
        dbuf[tt:tt + CONV_HALO, :] = jnp.where(i < nt - 1, ddn_ref[...], 0.0)
        gbuf[0:CONV_HALO, :] = jnp.where(i > 0, glh_ref[...], 0.0)
        gbuf[CONV_HALO:CONV_HALO + tt, :] = gl_ref[...]
        dgl = w_ref[0:1, :] * dbuf[pl.ds(ntap - 1, tt), :]
        for k in range(1, ntap):
            dgl = dgl + w_ref[k:k + 1, :] * dbuf[pl.ds(ntap - 1 - k, tt), :]
        for k in range(ntap):
            dw_ref[k:k + 1, :] += jnp.sum(d * gbuf[pl.ds(CONV_HALO - (ntap - 1) + k, tt), :], axis=0, keepdims=True)
        sg = _sigmoid(glu_ref[...])
        dval_ref[...] = (dgl * sg).astype(BF16)
        dglu_ref[...] = (dgl * val_ref[...] * sg * (1.0 - sg)).astype(BF16)

    tile = pl.BlockSpec((tt, c), lambda i: (i, 0))
    return pl.pallas_call(
        body, name=name,
        out_shape=(jax.ShapeDtypeStruct((s, c), BF16), jax.ShapeDtypeStruct((s, c), BF16),
                   jax.ShapeDtypeStruct((32, c), F32)),
        grid=(nt,),
        in_specs=[tile, pl.BlockSpec((tt, c), lambda i: (i, 1)), tile,
                  pl.BlockSpec((CONV_HALO, c), lambda i: (jnp.minimum((i + 1) * hb, last_halo), 0)),
                  tile, pl.BlockSpec((CONV_HALO, c), lambda i: (jnp.maximum(i * hb - 1, 0), 0)),
                  pl.BlockSpec((32, c), lambda i: (0, 0))],
        out_specs=(tile, tile, pl.BlockSpec((32, c), lambda i: (0, 0))),
        scratch_shapes=[pltpu.VMEM((tt + CONV_HALO, c), F32), pltpu.VMEM((CONV_HALO + tt, c), F32)],
        compiler_params=_params("arbitrary"),
    )(u, u, dd, dd, gl, gl, wpad)


SB_ROWS = 64


def _tri(n, cmp):
    r = lax.broadcasted_iota(jnp.int32, (n, n), 0)
    c = lax.broadcasted_iota(jnp.int32, (n, n), 1)
    return jnp.where(cmp(r, c), 1.0, 0.0).astype(BF16)


def _row_chunks(fn, n, *arrs):
    outs = [fn(*[a[r:r + SB_ROWS] for a in arrs]) for r in range(0, n, SB_ROWS)]
    return tuple(jnp.concatenate(list(o), axis=0) for o in zip(*outs))


def _hi_lo(v):
    hi = v.astype(BF16)
    return hi, (v - hi.astype(F32)).astype(BF16)


def _sb_sticks(z, causal):
    l1p = jnp.log(1.0 + jnp.exp(-jnp.abs(z)))
    lb = jnp.minimum(z, 0.0) - l1p
    ell = lb - z
    if causal is not None:
        ell = jnp.where(causal, ell, 0.0)
    hi, lo = _hi_lo(ell)
    return lb, hi, lo, jnp.sum(ell, axis=1, keepdims=True)


def _sb_fwd(u, heads, q_blk, k_blk, v_blk, name):
    s = u.shape[0]
    tq = _tile(s, (256, 128))
    scale = HEAD ** -0.5

    def body(q_ref, k_ref, v_ref, tri_ref, o_ref, tot_ref, kb_ref, vb_ref):
        i = pl.program_id(1)

        @pl.when(i == 0)
        def _():
            kb_ref[...] = k_ref[...].astype(BF16)
            vb_ref[...] = v_ref[...].astype(BF16)

        qb = (q_ref[...] * scale).astype(BF16)
        below_diag = lax.broadcasted_iota(jnp.int32, (tq, tq), 1) < lax.broadcasted_iota(jnp.int32, (tq, tq), 0)

        def blocks(j0, nb, c_a, acc, diag):
            mask = [below_diag] if diag else []
            rows = pl.ds(pl.multiple_of(j0 * tq, tq), nb * tq)
            kb = kb_ref[rows, :]
            vb = vb_ref[rows, :]
            z = _dot_nt(qb, kb)
            t_sfx = tri_ref[...]

            def sticks(zc, *m):
                out = []
                for b in range(nb):
                    out += _sb_sticks(zc[:, b * tq:(b + 1) * tq], m[0] if m else None)
                return tuple(out)

            st = _row_chunks(sticks, tq, z, *mask)
            lb, hi, lo, rs = st[0::4], st[1::4], st[2::4], st[3::4]
            sfx = [_dot(hi[b], t_sfx) + _dot(lo[b], t_sfx) for b in range(nb)]
            before, run = [None] * nb, c_a
            for b in reversed(range(nb)):
                before[b], run = run, run + rs[b]

            def weights(*a):
                ws = []
                for b in range(nb):
                    lbc, sfxc, befc = a[3 * b:3 * b + 3]
                    w = jnp.exp(lbc + (befc + sfxc))
                    if diag:
                        w = jnp.where(a[-1], w, 0.0)
                    ws.append(w.astype(BF16))
                return (ws[0] if nb == 1 else jnp.concatenate(ws, axis=1),)

            flat = [v for b in range(nb) for v in (lb[b], sfx[b], before[b])]
            wb, = _row_chunks(weights, tq, *flat, *mask)
            return run, acc + _dot(wb, vb)

        carry = blocks(i, 1, jnp.zeros((tq, 1), F32), jnp.zeros((tq, HEAD), F32), True)
        carry = lax.fori_loop(0, i // 4, lambda t, cr: blocks(i - 4 - 4 * t, 4, *cr, False), carry)
        carry = lax.fori_loop(0, (i % 4) // 2, lambda _, cr: blocks(i % 2, 2, *cr, False), carry)
        c_a, acc = lax.fori_loop(0, i % 2, lambda _, cr: blocks(0, 1, *cr, False), carry)
        o_ref[...] = acc
        tot_ref[...] = jnp.broadcast_to(c_a, (tq, HEAD))

    full = lambda off: pl.BlockSpec((s, HEAD), lambda h, i: (0, off + h))
    out = pl.BlockSpec((tq, HEAD), lambda h, i: (i, h))
    return pl.pallas_call(
        body, name=name,
        out_shape=(jax.ShapeDtypeStruct((s, heads * HEAD), F32), jax.ShapeDtypeStruct((s, heads * HEAD), F32)),
        grid=(heads, s // tq),
        in_specs=[pl.BlockSpec((tq, HEAD), lambda h, i: (i, q_blk + h)), full(k_blk), full(v_blk),
                  pl.BlockSpec((tq, tq), lambda h, i: (0, 0))],
        out_specs=(out, out),
        scratch_shapes=[pltpu.VMEM((s, HEAD), BF16), pltpu.VMEM((s, HEAD), BF16)],
        compiler_params=_params("parallel", "arbitrary"),
    )(u, u, u, _tri(tq, lambda r, c: r > c))


def _sb_bwd(u, tot, dy, heads, q_blk, k_blk, v_blk, name):
    s = u.shape[0]
    tq = _tile(s, (256, 128))
    scale = HEAD ** -0.5

    def body(q_ref, k_ref, v_ref, tot_ref, dy_ref, incl_ref, excl_ref, dq_ref, dk_ref, dv_ref, kb_ref, vb_ref):
        i = pl.program_id(1)

        @pl.when(i == 0)
        def _():
            dk_ref[...] = jnp.zeros_like(dk_ref)
            dv_ref[...] = jnp.zeros_like(dv_ref)
            kb_ref[...] = k_ref[...].astype(BF16)
            vb_ref[...] = v_ref[...].astype(BF16)

        qb = (q_ref[...] * scale).astype(BF16)
        dob = dy_ref[...].astype(BF16)
        total = tot_ref[:, 0:1]
        below_diag = lax.broadcasted_iota(jnp.int32, (tq, tq), 1) < lax.broadcasted_iota(jnp.int32, (tq, tq), 0)

        def blocks(j0, nb, c_p, c_g, dq, diag):
            mask = [below_diag] if diag else []
            rows = pl.ds(pl.multiple_of(j0 * tq, tq), nb * tq)
            kb = kb_ref[rows, :]
            vb = vb_ref[rows, :]
            z = _dot_nt(qb, kb)
            dw = _dot_nt(dob, vb)
            t_incl, t_excl = incl_ref[...], excl_ref[...]
            cols = lambda a, b: a[:, b * tq:(b + 1) * tq]

            def sticks(zc, *m):
                out = []
                for b in range(nb):
                    out += _sb_sticks(cols(zc, b), m[0] if m else None)
                return tuple(out)

            st = _row_chunks(sticks, tq, z, *mask)
            lb, hi, lo, rs_l = st[0::4], st[1::4], st[2::4], st[3::4]
            pfx = [_dot(hi[b], t_incl) + _dot(lo[b], t_incl) for b in range(nb)]
            p_before = [c_p]
            for b in range(nb):
                p_before.append(p_before[-1] + rs_l[b])

            def weights(totc, dwc, *a):
                out = []
                for b in range(nb):
                    lbc, pfxc, pbc = a[3 * b:3 * b + 3]
                    w = jnp.exp(lbc + (totc - (pbc + pfxc)))
                    if diag:
                        w = jnp.where(a[-1], w, 0.0)
                    g = w * cols(dwc, b)
                    out += [w.astype(BF16), g, g.astype(BF16), jnp.sum(g, axis=1, keepdims=True)]
                return tuple(out)

            flat = [v for b in range(nb) for v in (lb[b], pfx[b], p_before[b])]
            wt = _row_chunks(weights, tq, total, dw, *flat, *mask)
            wb, g, gb, rs_g = wt[0::4], wt[1::4], wt[2::4], wt[3::4]
            g_pre = [_dot(gb[b], t_excl) for b in range(nb)]
            g_before = [c_g]
            for b in range(nb):
                g_before.append(g_before[-1] + rs_g[b])

            def dscore(*a):
                dzs = []
                for b in range(nb):
                    lbc, gc, gprec, gbc = a[4 * b:4 * b + 4]
                    beta = jnp.exp(lbc)
                    dz = gc * (1.0 - beta) - (gbc + gprec) * beta
                    if diag:
                        dz = jnp.where(a[-1], dz, 0.0)
                    dzs.append(dz.astype(BF16))
                return (dzs[0] if nb == 1 else jnp.concatenate(dzs, axis=1),)

            flat = [v for b in range(nb) for v in (lb[b], g[b], g_pre[b], g_before[b])]
            dzb, = _row_chunks(dscore, tq, *flat, *mask)
            wcat = wb[0] if nb == 1 else jnp.concatenate(wb, axis=1)
            dk_ref[rows, :] += _dot_tn(dzb, qb)
            dv_ref[rows, :] += _dot_tn(wcat, dob)
            return p_before[-1], g_before[-1], dq + _dot(dzb, kb)

        zero = jnp.zeros((tq, 1), F32)
        carry = lax.fori_loop(0, i // 2, lambda t, cr: blocks(2 * t, 2, *cr, False),
                              (zero, zero, jnp.zeros((tq, HEAD), F32)))
        carry = lax.fori_loop(0, i % 2, lambda _, cr: blocks(i - 1, 1, *cr, False), carry)
        _, _, dq = blocks(i, 1, *carry, True)
        dq_ref[...] = dq * scale

    full = lambda off: pl.BlockSpec((s, HEAD), lambda h, i: (0, off + h))
    blk = pl.BlockSpec((tq, HEAD), lambda h, i: (i, h))
    acc = pl.BlockSpec((s, HEAD), lambda h, i: (0, h))
    tri = pl.BlockSpec((tq, tq), lambda h, i: (0, 0))
    shape = jax.ShapeDtypeStruct((s, heads * HEAD), F32)
    return pl.pallas_call(
        body, name=name, out_shape=(shape, shape, shape), grid=(heads, s // tq),
        in_specs=[pl.BlockSpec((tq, HEAD), lambda h, i: (i, q_blk + h)), full(k_blk), full(v_blk), blk, blk, tri, tri],
        out_specs=(blk, acc, acc),
        scratch_shapes=[pltpu.VMEM((s, HEAD), BF16), pltpu.VMEM((s, HEAD), BF16)],
        compiler_params=_params("parallel", "arbitrary"),
    )(u, u, u, tot, dy, _tri(tq, lambda r, c: r <= c), _tri(tq, lambda r, c: r < c))


def _lru_time_tile(s):
    return _tile(s, (256, 128, 64, 32))


def _lru_gates(xc, wa_ref, ba_ref, wx_ref, bx_ref, lam_ref, nh):
    pr, pi = [], []
    for n in range(nh):
        xn = xc[:, n * HEAD:(n + 1) * HEAD].astype(BF16)
        pr.append(_dot(xn, wa_ref[n]))
        pi.append(_dot(xn, wx_ref[n]))
    r = _sigmoid((pr[0] if nh == 1 else jnp.concatenate(pr, axis=1)) + ba_ref[...])
    ig = _sigmoid((pi[0] if nh == 1 else jnp.concatenate(pi, axis=1)) + bx_ref[...])
    lam = lam_ref[...]
    sp = jnp.maximum(-lam, 0.0) + jnp.log(1.0 + jnp.exp(-jnp.abs(lam)))
    log_a = -LRU_C * r * sp
    a = jnp.exp(log_a)
    mult = jnp.sqrt(-_expm1(2.0 * log_a))
    return r, ig, a, mult, sp


def _lru_fwd(u, x_blk, cw, cb, wa, ba, wx, bx, lam, name):
    s = u.shape[0]
    w = lam.shape[1]
    nh = w // HEAD
    tt = _lru_time_tile(s)
    hb = tt // LRU_HALO

    def body(x_ref, xh_ref, cw_ref, cb_ref, wa_ref, ba_ref, wx_ref, bx_ref, lam_ref, y_ref,
             xbuf, abuf, bbuf, hstate, rowbuf):
        i = pl.program_id(0)

        @pl.when(i == 0)
        def _():
            hstate[...] = jnp.zeros_like(hstate)

        xbuf[0:LRU_HALO, :] = jnp.where(i > 0, xh_ref[...], 0.0)
        xbuf[LRU_HALO:LRU_HALO + tt, :] = x_ref[...]
        xc = _conv_taps(xbuf, cw_ref, tt, 4, LRU_HALO - 3) + cb_ref[...]
        _, ig, a, mult, _ = _lru_gates(xc, wa_ref, ba_ref, wx_ref, bx_ref, lam_ref, nh)
        abuf[...] = a
        bbuf[...] = mult * (ig * xc)

        def group(gi, h):
            rows = pl.ds(pl.multiple_of(gi * 8, 8), 8)
            a8 = abuf[rows, :]
            b8 = bbuf[rows, :]
            for j in range(8):
                h = a8[j:j + 1, :] * h + b8[j:j + 1, :]
                rowbuf[j:j + 1, :] = h
            y_ref[rows, :] = rowbuf[...]
            return h

        hstate[0:1, :] = lax.fori_loop(0, tt // 8, group, hstate[0:1, :])

    vec = pl.BlockSpec((1, w), lambda i: (0, 0))
    gate = pl.BlockSpec((nh, HEAD, HEAD), lambda i: (0, 0, 0))
    return pl.pallas_call(
        body, name=name, out_shape=jax.ShapeDtypeStruct((s, w), F32), grid=(s // tt,),
        in_specs=[pl.BlockSpec((tt, w), lambda i: (i, x_blk)),
                  pl.BlockSpec((LRU_HALO, w), lambda i: (jnp.maximum(i * hb - 1, 0), x_blk)),
                  pl.BlockSpec((8, w), lambda i: (0, 0)), vec, gate, vec, gate, vec, vec],
        out_specs=pl.BlockSpec((tt, w), lambda i: (i, 0)),
        scratch_shapes=[pltpu.VMEM((LRU_HALO + tt, w), F32), pltpu.VMEM((tt, w), F32), pltpu.VMEM((tt, w), F32),
                        pltpu.VMEM((8, w), F32), pltpu.VMEM((8, w), F32)],
        compiler_params=_params("arbitrary"),
    )(u, u, cw, cb, wa, ba, wx, bx, lam)


def _lru_bwd(u, x_blk, hseq, dy, cw, cb, wa, ba, wx, bx, lam, name):
    s = u.shape[0]
    w = lam.shape[1]
    nh = w // HEAD
    tt = _lru_time_tile(s)
    hb = tt // LRU_HALO
    nt = s // tt

    def body(x_ref, xh_ref, h_ref, hh_ref, dy_ref, cw_ref, cb_ref, wa_ref, ba_ref, wx_ref, bx_ref, lam_ref,
             dx_ref, dwa_ref, dwx_ref, vec_ref, xbuf, hbuf, abuf, lbuf, dbuf, cstate, dhalo, rowbuf):
        i = pl.program_id(0)
        rt = nt - 1 - i

        @pl.when(i == 0)
        def _():
            cstate[...] = jnp.zeros_like(cstate)
            dhalo[...] = jnp.zeros_like(dhalo)
            dwa_ref[...] = jnp.zeros_like(dwa_ref)
            dwx_ref[...] = jnp.zeros_like(dwx_ref)
            vec_ref[...] = jnp.zeros_like(vec_ref)

        xbuf[0:LRU_HALO, :] = jnp.where(rt > 0, xh_ref[...], 0.0)
        xbuf[LRU_HALO:LRU_HALO + tt, :] = x_ref[...]
        hbuf[0:LRU_HALO, :] = jnp.where(rt > 0, hh_ref[...], 0.0)
        hbuf[LRU_HALO:LRU_HALO + tt, :] = h_ref[...]
        xc = _conv_taps(xbuf, cw_ref, tt, 4, LRU_HALO - 3) + cb_ref[...]
        r, ig, a, mult, sp = _lru_gates(xc, wa_ref, ba_ref, wx_ref, bx_ref, lam_ref, nh)
        abuf[...] = a

        def group(gi, c):
            rows = pl.ds(pl.multiple_of((tt // 8 - 1 - gi) * 8, 8), 8)
            a8 = abuf[rows, :]
            d8 = dy_ref[rows, :]
            for j in range(7, -1, -1):
                lam_t = d8[j:j + 1, :] + c
                rowbuf[j:j + 1, :] = lam_t
                c = a8[j:j + 1, :] * lam_t
            lbuf[rows, :] = rowbuf[...]
            return c

        cstate[0:1, :] = lax.fori_loop(0, tt // 8, group, cstate[0:1, :])

        lam_t = lbuf[...]
        hprev = hbuf[pl.ds(LRU_HALO - 1, tt), :]
        ixc = ig * xc
        d_ixc = lam_t * mult
        d_ig = d_ixc * xc
        dxc = d_ixc * ig
        dlog_a = lam_t * hprev * a + lam_t * ixc * (-(a * a) / mult)
        dr = dlog_a * (-LRU_C * sp)
        lam_p = lam_ref[...]
        dsp = -_sigmoid(-lam_p)
        vec_ref[2:3, :] += jnp.sum(dlog_a * (-LRU_C * r), axis=0, keepdims=True) * dsp
        dpr = dr * r * (1.0 - r)
        dpi = d_ig * ig * (1.0 - ig)
        vec_ref[0:1, :] += jnp.sum(dpr, axis=0, keepdims=True)
        vec_ref[1:2, :] += jnp.sum(dpi, axis=0, keepdims=True)
        parts = []
        for n in range(nh):
            sl = slice(n * HEAD, (n + 1) * HEAD)
            xn = xc[:, sl].astype(BF16)
            dprn = dpr[:, sl].astype(BF16)
            dpin = dpi[:, sl].astype(BF16)
            dwa_ref[n] += _dot_tn(xn, dprn)
            dwx_ref[n] += _dot_tn(xn, dpin)
            parts.append(_dot_nt(dprn, wa_ref[n]) + _dot_nt(dpin, wx_ref[n]))
        dxc = dxc + (parts[0] if nh == 1 else jnp.concatenate(parts, axis=1))
        vec_ref[3:4, :] += jnp.sum(dxc, axis=0, keepdims=True)
        dbuf[0:tt, :] = dxc
        dbuf[tt:tt + LRU_HALO, :] = dhalo[...]
        dx = cw_ref[0:1, :] * dbuf[pl.ds(3, tt), :]
        for k in range(1, 4):
            dx = dx + cw_ref[k:k + 1, :] * dbuf[pl.ds(3 - k, tt), :]
        dx_ref[...] = dx.astype(BF16)
        for k in range(4):
            vec_ref[4 + k:5 + k, :] += jnp.sum(dxc * xbuf[pl.ds(LRU_HALO - 3 + k, tt), :], axis=0, keepdims=True)
        dhalo[...] = dbuf[0:LRU_HALO, :]

    vec = pl.BlockSpec((1, w), lambda i: (0, 0))
    gate = pl.BlockSpec((nh, HEAD, HEAD), lambda i: (0, 0, 0))
    rev = lambda i: nt - 1 - i
    tile = pl.BlockSpec((tt, w), lambda i: (rev(i), 0))
    halo = lambda col: pl.BlockSpec((LRU_HALO, w), lambda i: (jnp.maximum(rev(i) * hb - 1, 0), col))
    return pl.pallas_call(
        body, name=name,
        out_shape=(jax.ShapeDtypeStruct((s, w), BF16), jax.ShapeDtypeStruct((nh, HEAD, HEAD), F32),
                   jax.ShapeDtypeStruct((nh, HEAD, HEAD), F32), jax.ShapeDtypeStruct((8, w), F32)),
        grid=(nt,),
        in_specs=[pl.BlockSpec((tt, w), lambda i: (rev(i), x_blk)), halo(x_blk), tile, halo(0), tile,
                  pl.BlockSpec((8, w), lambda i: (0, 0)), vec, gate, vec, gate, vec, vec],
        out_specs=(tile, gate, gate, pl.BlockSpec((8, w), lambda i: (0, 0))),
        scratch_shapes=[pltpu.VMEM((LRU_HALO + tt, w), F32), pltpu.VMEM((LRU_HALO + tt, w), F32),
                        pltpu.VMEM((tt, w), F32), pltpu.VMEM((tt, w), F32), pltpu.VMEM((tt + LRU_HALO, w), F32),
                        pltpu.VMEM((8, w), F32), pltpu.VMEM((8, w), F32), pltpu.VMEM((8, w), F32)],
        compiler_params=_params("arbitrary"),
    )(u, u, hseq, hseq, dy, cw, cb, wa, ba, wx, bx, lam)


def _gate_specs(c, tr):
    return [pl.BlockSpec((tr, c), lambda i, b=b: (i, b)) for b in (2, 9, 10, 12)]


def _outgate_fwd(y_conv, y_attn, y_lru, u, n_conv, n_attn, n_lru, name):
    s, c = y_conv.shape
    tr = _tile(s, (256, 128, 64, 32, 16))

    def body(yc_ref, ya_ref, yl_ref, gc_ref, ga0_ref, ga1_ref, gl_ref, nc_ref, na_ref, nl_ref, o_ref):
        def rinv(v):
            return lax.rsqrt(jnp.mean(v * v, axis=-1, keepdims=True) + RMS_EPS)

        def silu(g):
            return g * _sigmoid(g)

        yc = yc_ref[...]
        o_ref[:, 0:c] = (yc * rinv(yc) * nc_ref[...] * silu(gc_ref[...])).astype(BF16)
        ya = ya_ref[...]
        ra = rinv(ya)
        o_ref[:, c:2 * c] = (ya[:, 0:c] * ra * na_ref[:, 0:c] * silu(ga0_ref[...])).astype(BF16)
        o_ref[:, 2 * c:3 * c] = (ya[:, c:2 * c] * ra * na_ref[:, c:2 * c] * silu(ga1_ref[...])).astype(BF16)
        yl = yl_ref[...]
        o_ref[:, 3 * c:4 * c] = (yl * rinv(yl) * nl_ref[...] * silu(gl_ref[...])).astype(BF16)

    row = lambda wd: pl.BlockSpec((tr, wd), lambda i: (i, 0))
    vec = lambda wd: pl.BlockSpec((1, wd), lambda i: (0, 0))
    return pl.pallas_call(
        body, name=name, out_shape=jax.ShapeDtypeStruct((s, 4 * c), BF16), grid=(s // tr,),
        in_specs=[row(c), row(2 * c), row(c)] + _gate_specs(c, tr) + [vec(c), vec(2 * c), vec(c)],
        out_specs=row(4 * c), compiler_params=_params("parallel"),
    )(y_conv, y_attn, y_lru, u, u, u, u, n_conv, n_attn, n_lru)


def _outgate_bwd(dy, y_conv, y_attn, y_lru, u, n_conv, n_attn, n_lru, name):
    s, c = y_conv.shape
    tr = _tile(s, (256, 128, 64, 32, 16))

    def body(dy_ref, yc_ref, ya_ref, yl_ref, gc_ref, ga0_ref, ga1_ref, gl_ref, nc_ref, na_ref, nl_ref,
             dyc_ref, dya_ref, dyl_ref, dgc_ref, dga_ref, dgl_ref, dn_ref):
        @pl.when(pl.program_id(0) == 0)
        def _():
            dn_ref[...] = jnp.zeros_like(dn_ref)

        def group(yv, gate, wv, d):
            r = lax.rsqrt(jnp.mean(yv * yv, axis=-1, keepdims=True) + RMS_EPS)
            yh = yv * r
            sg = _sigmoid(gate)
            dn = d * (gate * sg)
            dgate = d * (yh * wv) * (sg * (1.0 + gate * (1.0 - sg)))
            dw = jnp.sum(dn * yh, axis=0, keepdims=True)
            dyn = dn * wv
            dyv = r * (dyn - yh * jnp.mean(dyn * yh, axis=-1, keepdims=True))
            return dyv, dgate, dw

        dyv, dg, dw = group(yc_ref[...], gc_ref[...], nc_ref[...], dy_ref[:, 0:c])
        dyc_ref[...] = dyv
        dgc_ref[...] = dg.astype(BF16)
        dn_ref[0:1, 0:c] += dw
        gate_a = jnp.concatenate([ga0_ref[...], ga1_ref[...]], axis=1)
        dyv, dg, dw = group(ya_ref[...], gate_a, na_ref[...], dy_ref[:, c:3 * c])
        dya_ref[...] = dyv
        dga_ref[...] = dg.astype(BF16)
        dn_ref[0:1, c:3 * c] += dw
        dyv, dg, dw = group(yl_ref[...], gl_ref[...], nl_ref[...], dy_ref[:, 3 * c:4 * c])
        dyl_ref[...] = dyv
        dgl_ref[...] = dg.astype(BF16)
        dn_ref[0:1, 3 * c:4 * c] += dw

    row = lambda wd: pl.BlockSpec((tr, wd), lambda i: (i, 0))
    vec = lambda wd: pl.BlockSpec((1, wd), lambda i: (0, 0))
    sh = lambda wd, dt: jax.ShapeDtypeStruct((s, wd), dt)
    return pl.pallas_call(
        body, name=name,
        out_shape=(sh(c, F32), sh(2 * c, F32), sh(c, F32), sh(c, BF16), sh(2 * c, BF16), sh(c, BF16),
                   jax.ShapeDtypeStruct((8, 4 * c), F32)),
        grid=(s // tr,),
        in_specs=[row(4 * c), row(c), row(2 * c), row(c)] + _gate_specs(c, tr) + [vec(c), vec(2 * c), vec(c)],
        out_specs=(row(c), row(2 * c), row(c), row(c), row(2 * c), row(c),
                   pl.BlockSpec((8, 4 * c), lambda i: (0, 0))),
        compiler_params=_params("arbitrary"),
    )(dy, y_conv, y_attn, y_lru, u, u, u, u, n_conv, n_attn, n_lru)


def _xattn_probs(qh, kh, scale):
    sc = _dot_nt(qh, kh) * scale
    p = jnp.exp(sc - jnp.max(sc, axis=-1, keepdims=True))
    return p / jnp.sum(p, axis=-1, keepdims=True)


def _xattn_fwd(q, kv, name):
    s, xw = q.shape
    m = kv.shape[0]
    nh = xw // HEAD
    tq = _tile(s, (256, 128, 64, 32, 16))
    scale = HEAD ** -0.5

    def body(q_ref, kv_ref, o_ref):
        for h in range(nh):
            qh = q_ref[:, h * HEAD:(h + 1) * HEAD].astype(BF16)
            kh = kv_ref[:, h * HEAD:(h + 1) * HEAD].astype(BF16)
            vh = kv_ref[:, xw + h * HEAD:xw + (h + 1) * HEAD].astype(BF16)
            p = _xattn_probs(qh, kh, scale)
            o_ref[:, h * HEAD:(h + 1) * HEAD] = _dot(p.astype(BF16), vh).astype(BF16)

    return pl.pallas_call(
        body, name=name, out_shape=jax.ShapeDtypeStruct((s, xw), BF16), grid=(s // tq,),
        in_specs=[pl.BlockSpec((tq, xw), lambda i: (i, 0)), pl.BlockSpec((m, 2 * xw), lambda i: (0, 0))],
        out_specs=pl.BlockSpec((tq, xw), lambda i: (i, 0)), compiler_params=_params("parallel"),
    )(q, kv)


def _xattn_bwd(q, kv, do, name):
    s, xw = q.shape
    m = kv.shape[0]
    nh = xw // HEAD
    tq = _tile(s, (256, 128, 64, 32, 16))
    scale = HEAD ** -0.5

    def body(q_ref, kv_ref, do_ref, dq_ref, dkv_ref):
        @pl.when(pl.program_id(0) == 0)
        def _():
            dkv_ref[...] = jnp.zeros_like(dkv_ref)

        for h in range(nh):
            ks = slice(h * HEAD, (h + 1) * HEAD)
            vs = slice(xw + h * HEAD, xw + (h + 1) * HEAD)
            qh = q_ref[:, ks].astype(BF16)
            kh = kv_ref[:, ks].astype(BF16)
            vh = kv_ref[:, vs].astype(BF16)
            doh = do_ref[:, ks].astype(BF16)
            p = _xattn_probs(qh, kh, scale)
            dkv_ref[:, vs] += _dot_tn(p.astype(BF16), doh)
            dp = _dot_nt(doh, vh)
            ds = (p * (dp - jnp.sum(dp * p, axis=-1, keepdims=True)) * scale).astype(BF16)
            dq_ref[:, ks] = _dot(ds, kh).astype(BF16)
            dkv_ref[:, ks] += _dot_tn(ds, qh)

    row = pl.BlockSpec((tq, xw), lambda i: (i, 0))
    full = pl.BlockSpec((m, 2 * xw), lambda i: (0, 0))
    return pl.pallas_call(
        body, name=name,
        out_shape=(jax.ShapeDtypeStruct((s, xw), BF16), jax.ShapeDtypeStruct((m, 2 * xw), F32)), grid=(s // tq,),
        in_specs=[row, full, row], out_specs=(row, full), compiler_params=_params("arbitrary"),
    )(q, kv, do)


def _adamw(w, g, m, v, name):
    rows, cols = w.shape
    tr = _tile(rows, (512, 256, 128, 64, 32, 16, 8)) if rows % 8 == 0 else rows
    bc1 = 1.0 - ADAM_B1 ** ADAM_STEP
    bc2 = 1.0 - ADAM_B2 ** ADAM_STEP

    def body(w_ref, g_ref, m_ref, v_ref, d_ref, nm_ref, nv_ref):
        gv = g_ref[...]
        nm = ADAM_B1 * m_ref[...] + (1.0 - ADAM_B1) * gv
        nv = ADAM_B2 * v_ref[...] + (1.0 - ADAM_B2) * (gv * gv)
        nm_ref[...] = nm
        nv_ref[...] = nv
        d_ref[...] = -ADAM_LR * ((nm / bc1) / (jnp.sqrt(nv / bc2) + ADAM_EPS) + ADAM_WD * w_ref[...])

    spec = pl.BlockSpec((tr, cols), lambda i: (i, 0))
    sh = jax.ShapeDtypeStruct((rows, cols), F32)
    return pl.pallas_call(
        body, name=name, out_shape=(sh, sh, sh), grid=(rows // tr,), in_specs=[spec] * 4,
        out_specs=(spec, spec, spec), compiler_params=_params("parallel"),
    )(w, g, m, v)


WEIGHTS = ['mix_norm_g', 'w_in', 'conv_dw_w', 'conv_dw_b', 'conv_ln_g', 'conv_ln_b', 'conv_pw_w', 'lru_conv_w',
           'lru_conv_b', 'lru_wa', 'lru_ba', 'lru_wx', 'lru_bx', 'lru_lambda', 'out_norm_conv', 'out_norm_attn',
           'out_norm_lru', 'w_out', 'xattn_norm_g', 'mem_norm_g', 'xattn_wq', 'xattn_wkv', 'xattn_wo',
           'final_norm_g']
BIG_SHARDED = {'w_in': 2, 'conv_pw_w': 1, 'w_out': 1, 'xattn_wq': 1, 'xattn_wkv': 1, 'xattn_wo': 2}
SMALL_SHARDED = {'conv_dw_w': 2, 'lru_conv_w': 2}


def _layer_fwd(x, mem, p, l, first, rest):
    row = lambda name: p[name][l][None, :]
    c = p['conv_dw_b'].shape[1]
    heads = 2 * c // HEAD
    h = _rms_fwd(x, row('mix_norm_g'), "rms_mix")
    w_in, tied = first(h)
    u = _mm(h, w_in, dep=tied, name="in_proj")
    q_blk = 3 * c // HEAD
    y_attn, tot = _sb_fwd(u, heads, q_blk, q_blk + heads, q_blk + 2 * heads, "sb_fwd")
    fw = dict(rest(y_attn), w_in=w_in)
    wpad = jnp.pad(fw['conv_dw_w'][l], ((0, 1), (0, 0)))
    cw = jnp.pad(fw['lru_conv_w'][l], ((0, 4), (0, 0)))
    wa, wx = p['lru_wa'][l].astype(BF16), p['lru_wx'][l].astype(BF16)
    y_lru = _lru_fwd(u, 11, cw, row('lru_conv_b'), wa, row('lru_ba'), wx, row('lru_bx'), row('lru_lambda'), "lru_fwd")
    y_conv = _conv_fwd(u, wpad, row('conv_dw_b'), row('conv_ln_g'), row('conv_ln_b'), fw['conv_pw_w'][None], 0,
                       "conv_fwd")
    yc = _outgate_fwd(y_conv, y_attn, y_lru, u, row('out_norm_conv'), row('out_norm_attn'), row('out_norm_lru'),
                      "outgate_fwd")
    x1 = _mm(yc, fw['w_out'], add=x, name="out_proj")
    h2 = _rms_fwd(x1, row('xattn_norm_g'), "rms_xattn")
    memn = _rms_fwd(mem, row('mem_norm_g'), "rms_mem")
    q2 = _mm(h2, fw['xattn_wq'], name="xq_proj")
    kv = _mm(memn, fw['xattn_wkv'], name="xkv_proj")
    o2 = _xattn_fwd(q2, kv, "xattn_fwd")
    x2 = _mm(o2, fw['xattn_wo'], add=x1, name="xo_proj")
    saved = dict(x=x, h=h, u=u, wpad=wpad, cw=cw, wa=wa, wx=wx, y_conv=y_conv, y_attn=y_attn, tot=tot, y_lru=y_lru,
                 yc=yc, x1=x1, h2=h2, memn=memn, q2=q2, kv=kv, o2=o2, fw=fw)
    return x2, saved


def _layer_bwd(dx2, mem, p, l, sv, rest_ready, w_in_ready):
    row = lambda name: p[name][l][None, :]
    c = p['conv_dw_b'].shape[1]
    heads = 2 * c // HEAD
    fw = sv['fw']
    g, big = {}, {}
    big['xattn_wo'] = _mm(sv['o2'], dx2, ta=True, out_dtype=BF16, name="d_wo")
    do2 = _mm(dx2, fw['xattn_wo'], tb=True, name="d_o2")
    dq2, dkv = _xattn_bwd(sv['q2'], sv['kv'], do2, "xattn_bwd")
    big['xattn_wq'] = _mm(sv['h2'], dq2, ta=True, out_dtype=BF16, name="d_wq")
    dh2 = _mm(dq2, fw['xattn_wq'], tb=True, name="d_h2")
    big['xattn_wkv'] = _mm(sv['memn'], dkv, ta=True, out_dtype=BF16, name="d_wkv")
    dmemn = _mm(dkv, fw['xattn_wkv'], tb=True, name="d_memn")
    _, g['mem_norm_g'] = _rms_bwd(mem, row('mem_norm_g'), dmemn, None, "rms_mem_bwd")
    dx1, g['xattn_norm_g'] = _rms_bwd(sv['x1'], row('xattn_norm_g'), dh2, dx2, "rms_xattn_bwd")
    big['w_out'] = _mm(sv['yc'], dx1, ta=True, out_dtype=BF16, name="d_wout")
    dyc = _mm(dx1, fw['w_out'], tb=True, name="d_yc")
    u = sv['u']
    d_yconv, d_yattn, d_ylru, dgc, dga, dgl, dn = _outgate_bwd(
        dyc, sv['y_conv'], sv['y_attn'], sv['y_lru'], u, row('out_norm_conv'), row('out_norm_attn'),
        row('out_norm_lru'), "outgate_bwd")
    g['out_norm_conv'], g['out_norm_attn'], g['out_norm_lru'] = dn[0, 0:c], dn[0, c:3 * c], dn[0, 3 * c:4 * c]
    dd, gl, dpw, cvec = _conv_bwd_post(u, d_yconv, sv['wpad'], row('conv_dw_b'), row('conv_ln_g'),
                                       row('conv_ln_b'), fw['conv_pw_w'][None], 0, "conv_bwd_post")
    big['conv_pw_w'] = dpw.astype(BF16)
    g['conv_ln_g'], g['conv_ln_b'], g['conv_dw_b'] = cvec[0], cvec[1], cvec[2]
    tied = rest_ready(big)
    dval, dglu, ddw = _conv_bwd_dw(u, dd, gl, sv['wpad'] + tied[0:1, 0:1], "conv_bwd_dw")
    g['conv_dw_w'] = ddw[0:31]
    q_blk = 3 * c // HEAD
    dq, dk, dv = _sb_bwd(u, sv['tot'], d_yattn, heads, q_blk, q_blk + heads, q_blk + 2 * heads, "sb_bwd")
    dxr, g['lru_wa'], g['lru_wx'], lvec = _lru_bwd(
        u, 11, sv['y_lru'], d_ylru, sv['cw'], row('lru_conv_b'), sv['wa'], row('lru_ba'), sv['wx'], row('lru_bx'),
        row('lru_lambda'), "lru_bwd")
    g['lru_ba'], g['lru_bx'], g['lru_lambda'], g['lru_conv_b'] = lvec[0], lvec[1], lvec[2], lvec[3]
    g['lru_conv_w'] = lvec[4:8]
    du = jnp.concatenate([dval, dglu, dgc, dq.astype(BF16), dk.astype(BF16), dv.astype(BF16), dga, dxr, dgl], axis=1)
    tied = w_in_ready(_mm(sv['h'], du, ta=True, out_dtype=BF16, name="d_win"))
    dh = _mm(du, fw['w_in'], tb=True, dep=tied, name="d_h")
    dx0, g['mix_norm_g'] = _rms_bwd(sv['x'], row('mix_norm_g'), dh, dx1, "rms_mix_bwd")
    return dx0, g


def kernel(x, mem, mix_norm_g, w_in, conv_dw_w, conv_dw_b, conv_ln_g, conv_ln_b, conv_pw_w, lru_conv_w, lru_conv_b, lru_wa, lru_ba, lru_wx, lru_bx, lru_lambda, out_norm_conv, out_norm_attn, out_norm_lru, w_out, xattn_norm_g, mem_norm_g, xattn_wq, xattn_wkv, xattn_wo, final_norm_g, loss_target, m_mix_norm_g, m_w_in, m_conv_dw_w, m_conv_dw_b, m_conv_ln_g, m_conv_ln_b, m_conv_pw_w, m_lru_conv_w, m_lru_conv_b, m_lru_wa, m_lru_ba, m_lru_wx, m_lru_bx, m_lru_lambda, m_out_norm_conv, m_out_norm_attn, m_out_norm_lru, m_w_out, m_xattn_norm_g, m_mem_norm_g, m_xattn_wq, m_xattn_wkv, m_xattn_wo, m_final_norm_g, v_mix_norm_g, v_w_in, v_conv_dw_w, v_conv_dw_b, v_conv_ln_g, v_conv_ln_b, v_conv_pw_w, v_lru_conv_w, v_lru_conv_b, v_lru_wa, v_lru_ba, v_lru_wx, v_lru_bx, v_lru_lambda, v_out_norm_conv, v_out_norm_attn, v_out_norm_lru, v_w_out, v_xattn_norm_g, v_mem_norm_g, v_xattn_wq, v_xattn_wkv, v_xattn_wo, v_final_norm_g):
    args = locals()
    w = {n: args[n] for n in WEIGHTS}
    mom = {n: args["m_" + n] for n in WEIGHTS}
    var = {n: args["v_" + n] for n in WEIGHTS}
    depth = w_in.shape[0]
    c = conv_dw_b.shape[1]
    assert out_norm_attn.shape[1] == 2 * c and lru_lambda.shape[1] == c and w_in.shape[2] * N_DEV == 13 * c
    assert c % HEAD == 0 and x.shape[0] == 1 and mem.shape[0] == 1
    xs, mems, tgt = x[0], mem[0], loss_target[0]
    me = 4 * lax.axis_index("x") + 2 * lax.axis_index("y") + lax.axis_index("c")

    assert depth == 2 and (2 * w_in.shape[2]) % LANE == 0 and w_in.shape[2] % LANE in (0, LANE // 2)
    my_c = lax.axis_index("c")
    big = list(BIG_SHARDED)
    dev = [n for n in big if n != 'w_in']
    my_x, my_y = lax.axis_index("x"), lax.axis_index("y")
    small = list(SMALL_SHARDED)
    p = dict(w)
    blk_width = w_in.shape[2]
    w_in_blk = jnp.pad(w_in.astype(BF16), ((0, 0), (0, 0), (0, -blk_width % LANE)))
    w_in0_blocks = _all_gather(w_in_blk[0], "gather_w_in0")
    w_in0 = _join_cols(w_in0_blocks, blk_width, "join_w_in")
    taps = _pack([w[n] for n in small], F32)[None]

    def gather_group(names, l, stage, dep, tag, with_taps=False):
        srcs, shapes, plans = [], [], []
        if with_taps:
            srcs, shapes, plans = [taps], [(N_DEV,) + taps.shape[1:]], [_gather_plan(0, 1, stage)]
        for n in names:
            if n == 'w_in':
                srcs.append(w_in_blk[l][None])
                shapes.append((N_DEV,) + w_in_blk.shape[1:])
                plans.append(_gather_plan(0, 1, stage))
            else:
                axis = BIG_SHARDED[n] - 1
                blk = w[n][l].astype(BF16)
                width = blk.shape[axis]
                srcs.append(blk)
                shapes.append(blk.shape[:axis] + (N_DEV * width,) + blk.shape[axis + 1:])
                plans.append(_gather_plan(axis, width, stage))
        return _split_start(srcs, shapes, plans, dep, "gather_start_" + tag)

    def forward_group(names, landed, dep, tag):
        plans = [_forward_plan(0, 1) if n == 'w_in' else _forward_plan(BIG_SHARDED[n] - 1, w[n].shape[BIG_SHARDED[n]])
                 for n in names]
        return _split_start(landed, [None] * len(names), plans, dep, "gather_start_" + tag)

    g0, tied = gather_group(dev, 0, "direct", w_in0_blocks, "rest0", with_taps=True)
    g1, tied = gather_group(big, 1, "first", tied, "layer1")
    layer1, full_taps = {}, {}

    def first_of(l):
        def first(after):
            if l == 0:
                return w_in0, tied
            layer1.update(zip(big, _split_wait(layer1.pop('forwarding'), after, "gather_wait_layer1_passed")))
            layer1['w_in'] = _join_cols(layer1['w_in'], blk_width, "join_w_in")
            return layer1['w_in'], None
        return first

    def rest_of(l):
        def rest(after):
            if l == 0:
                landed = _split_wait(g0, after, "gather_wait_rest0")
                full_taps.update({n: _join_blocks(blk, SMALL_SHARDED[n]) for n, blk in
                                  zip(small, _unpack(landed[0], [w[n].shape for n in small], lead=N_DEV))})
                got = dict(zip(dev, landed[1:]), **full_taps)
                landed = _split_wait(g1, after, "gather_wait_layer1")
                layer1['forwarding'], token = forward_group(big, landed, landed[0], "layer1_passed")
                got['conv_pw_w'] = got['conv_pw_w'] + token[0:1, 0:1].astype(BF16)
                return got
            return dict({n: layer1[n] for n in dev}, **full_taps)
        return rest

    saved = []
    act = xs
    for l in range(depth):
        act, sv = _layer_fwd(act, mems, p, l, first_of(l), rest_of(l))
        saved.append(sv)
    loss_part, dact, d_final = _loss_bwd(act, final_norm_g[None, :], tgt, "loss_bwd")

    def grad_window(n):
        axis = BIG_SHARDED[n] - 1
        blk = w[n].shape[axis + 1]
        pad = blk % LANE if axis == 1 else 0
        return axis, blk + pad, lambda px, py, pc: blk * (4 * px + 2 * py + pc) - pad * pc

    scattering = {}

    def scatter_group(names, arrs, l, tag):
        shapes, plans = [], []
        for n, g in zip(names, arrs):
            axis, width, start = grad_window(n)
            shapes.append((N_DEV,) + g.shape[:axis] + (width,) + g.shape[axis + 1:])
            plans.append(_scatter_plan(axis, width, start))
        scattering[(l, tag)], token = _split_start(arrs, shapes, plans, arrs[0], f"scatter_start_{tag}{l}")
        return token

    layer_grads = [None] * depth
    for l in reversed(range(depth)):
        dact, layer_grads[l] = _layer_bwd(
            dact, mems, p, l, saved[l],
            lambda big_grads, l=l: scatter_group(dev, [big_grads[n] for n in dev], l, "rest"),
            lambda g_w_in, l=l: scatter_group(['w_in'], [g_w_in], l, "w_in"))
    grad_x = dact[None]
    rest = [n for n in WEIGHTS if n not in BIG_SHARDED]
    partial = {n: jnp.stack([layer_grads[l][n] for l in range(depth)]) for n in rest if n != 'final_norm_g'}
    partial['final_norm_g'] = d_final[0]

    vec = _pack([partial[n] for n in rest] + [loss_part], F32)
    small_grads, tied = _split_start([vec[None]], [(N_DEV,) + vec.shape], [_gather_plan(0, 1, "direct")], vec,
                                     "small_grads_start")

    grads, delta, new_m, new_v = {}, {}, {}, {}

    def update(n, received):
        summed = jnp.stack([_sum_blocks(received[l], "sum_" + n) for l in range(depth)])
        width = w[n].shape[-1]
        if summed.shape[-1] != width:
            summed = jnp.where(my_c == 0, summed[..., :width], summed[..., summed.shape[-1] - width:])
        grads[n] = summed
        shape = w[n].shape
        two_d = lambda a: a.reshape(-1, shape[-1])
        d, nm, nv = _adamw(two_d(w[n]), two_d(summed), two_d(mom[n]), two_d(var[n]), "adamw_" + n)
        delta[n], new_m[n], new_v[n] = d.reshape(shape), nm.reshape(shape), nv.reshape(shape)

    landed = [_split_wait(scattering[(l, "rest")], tied, f"scatter_wait_rest{l}") for l in range(depth)]
    for t, n in enumerate(dev):
        update(n, [landed[l][t] for l in range(depth)])
    after = delta[dev[-1]]
    update('w_in', [_split_wait(scattering[(l, "w_in")], after, f"scatter_wait_w_in{l}")[0] for l in range(depth)])
    total = _sum_blocks(_split_wait(small_grads, delta['w_in'], "small_grads_wait")[0], "sum_small_grads")
    pieces = _unpack(total, [partial[n].shape for n in rest] + [(1, 1)])
    loss = pieces[-1][0, 0]
    for n, piece in zip(rest, pieces[:-1]):
        if n in SMALL_SHARDED:
            width = w[n].shape[2]
            piece = lax.dynamic_slice_in_dim(piece, me * width, width, axis=2)
        grads[n] = piece
    shapes = [w[n].shape for n in rest]
    packed = [_pack([src[n] for n in rest], F32) for src in (w, grads, mom, var)]
    outs = _adamw(*packed, "adamw_small")
    for dst, o in zip((delta, new_m, new_v), outs):
        dst.update(dict(zip(rest, _unpack(o, shapes))))

    return (loss, grad_x, *[grads[n] for n in WEIGHTS], *[delta[n] for n in WEIGHTS],
            *[new_m[n] for n in WEIGHTS], *[new_v[n] for n in WEIGHTS])
```

```python
import functools
import math

import jax
import jax.numpy as jnp
from jax import lax
from jax.experimental import pallas as pl
from jax.experimental.pallas import tpu as pltpu

F32 = jnp.float32
BF16 = jnp.bfloat16

N_DEV = 8
LANE = 128
HEAD = 128
VMEM_LIMIT = 56 * 1024 * 1024
PACK_COLS = 512
RMS_EPS = 1e-6
LN_EPS = 1e-5
LRU_C = 8.0
CONV_HALO = 32
LRU_HALO = 8

ADAM_LR, ADAM_B1, ADAM_B2, ADAM_EPS, ADAM_WD, ADAM_STEP = 0.001, 0.9, 0.999, 1e-08, 0.01, 10

MESH = pl.DeviceIdType.MESH


def _tile(n, cands):
    for c in cands:
        if n % c == 0:
            return c
    raise ValueError(f"no tile of {cands} divides {n}")


def _params(*sem):
    return pltpu.CompilerParams(dimension_semantics=sem, vmem_limit_bytes=VMEM_LIMIT)


def _dot(a, b):
    return lax.dot_general(a, b, (((1,), (0,)), ((), ())), preferred_element_type=F32)


def _dot_nt(a, b):
    return lax.dot_general(a, b, (((1,), (1,)), ((), ())), preferred_element_type=F32)


def _dot_tn(a, b):
    return lax.dot_general(a, b, (((0,), (0,)), ((), ())), preferred_element_type=F32)


def _sigmoid(x):
    return 1.0 / (1.0 + jnp.exp(-x))


def _expm1(x):
    series = x * (1.0 + x * (0.5 + x * (1.0 / 6.0 + x * (1.0 / 24.0))))
    return jnp.where(jnp.abs(x) < 0.05, series, jnp.exp(x) - 1.0)


def _my_place():
    return lax.axis_index("x"), lax.axis_index("y"), lax.axis_index("c")


def _flip(v, d):
    return 1 - v if d else v


def _window(ref, axis, start, size):
    return ref.at[tuple(pl.ds(start, size) if a == axis else pl.ds(0, ref.shape[a]) for a in range(len(ref.shape)))]


def _all_gather(x2d, name):
    rows, cols = x2d.shape

    def body(x_ref, out_ref, send_sems, recv_sems, local_sem):
        x, y, c = _my_place()
        me, sibling = (x, y, c), (x, y, 1 - c)
        chips = [(1 - x, y), (x, 1 - y), (1 - x, 1 - y)]

        def blk(px, py, pc):
            return out_ref.at[4 * px + 2 * py + pc]

        def copy(k, block, to, src=None):
            return pltpu.make_async_remote_copy(
                src_ref=blk(*block) if src is None else src, dst_ref=blk(*block),
                send_sem=send_sems.at[k], recv_sem=recv_sems.at[k], device_id=to, device_id_type=MESH)

        mine = pltpu.make_async_copy(x_ref, blk(*me), local_sem)
        mine.start()
        first = [copy(0, me, sibling, src=x_ref)]
        first += [copy(1 + j, me, (*chip, c), src=x_ref) for j, chip in enumerate(chips)]
        for cp in first:
            cp.start()
        passed = [copy(4 + j, (*chip, c), sibling) for j, chip in enumerate(chips)]
        for j, chip in enumerate(chips):
            copy(1 + j, (*chip, c), me).wait_recv()
            passed[j].start()
        copy(0, sibling, me).wait_recv()
        for j, chip in enumerate(chips):
            copy(4 + j, (*chip, 1 - c), me).wait_recv()
        for cp in first + passed:
            cp.wait_send()
        mine.wait()

    return pl.pallas_call(
        body, name=name,
        out_shape=jax.ShapeDtypeStruct((N_DEV, rows, cols), x2d.dtype),
        in_specs=[pl.BlockSpec(memory_space=pl.ANY)],
        out_specs=pl.BlockSpec(memory_space=pl.ANY),
        scratch_shapes=[pltpu.SemaphoreType.DMA((7,)), pltpu.SemaphoreType.DMA((7,)), pltpu.SemaphoreType.DMA],
    )(x2d)


HBM_SPEC = pl.BlockSpec(memory_space=pltpu.HBM)
SEM_SPEC = pl.BlockSpec(memory_space=pltpu.SEMAPHORE)
SPLIT_COPY = pltpu.SideEffectType.DATAFLOW_SIDE_EFFECTING


class _Plan:
    def __init__(self, copies, own, total):
        self.copies, self.own, self.total = copies, own, total


def _scatter_plan(axis, width, start):
    def win(src, px, py, pc):
        return _window(src, axis, pl.multiple_of(start(px, py, pc), math.gcd(width, 1024)), width)

    def copies(src, zone, x, y, c):
        out = []
        for k in range(1, N_DEV):
            px, py, pc = _flip(x, k & 4), _flip(y, k & 2), _flip(c, k & 1)
            out.append((win(src, px, py, pc), zone.at[4 * x + 2 * y + c], (px, py, pc)))
        return out

    return _Plan(copies, lambda src, zone, x, y, c: (win(src, x, y, c), zone.at[4 * x + 2 * y + c]),
                 lambda zone: zone.at[pl.ds(0, N_DEV - 1)])


def _gather_plan(axis, width, stage):
    flips = range(1, N_DEV) if stage == "direct" else (1, 2, 4, 6)

    def mine(zone, x, y, c):
        return _window(zone, axis, width * (4 * x + 2 * y + c), width)

    def copies(src, zone, x, y, c):
        return [(src, mine(zone, x, y, c), (_flip(x, k & 4), _flip(y, k & 2), _flip(c, k & 1))) for k in flips]

    return _Plan(copies, lambda src, zone, x, y, c: (src, mine(zone, x, y, c)),
                 lambda zone: _window(zone, axis, 0, len(flips) * width))


def _forward_plan(axis, width):
    def copies(src, zone, x, y, c):
        out = []
        for px, py in ((1 - x, y), (x, 1 - y), (1 - x, 1 - y)):
            win = _window(zone, axis, width * (4 * px + 2 * py + c), width)
            out.append((win, win, (x, y, 1 - c)))
        return out

    return _Plan(copies, lambda src, zone, x, y, c: None, lambda zone: _window(zone, axis, 0, 3 * width))


def _split_start(srcs, zone_shapes, plans, dep, name):
    n = len(srcs)
    zones = [lax.empty(shape, a.dtype) for shape, a in zip(zone_shapes, srcs) if shape is not None]
    m = len(zones)
    zone_of = [None if shape is None else sum(s is not None for s in zone_shapes[:t])
               for t, shape in enumerate(zone_shapes)]

    def body(*refs):
        ins, fresh = refs[:n], refs[n:n + m]
        sems, token = refs[n + m + 1:n + m + 1 + 3 * n], refs[-1]
        send_sems, recv_sems, local_sems = sems[:n], sems[n:2 * n], sems[2 * n:]
        x, y, c = _my_place()
        for t in range(n):
            land = ins[t] if zone_of[t] is None else fresh[zone_of[t]]
            own = plans[t].own(ins[t], land, x, y, c)
            if own is not None:
                pltpu.make_async_copy(*own, local_sems[t]).start()
            for src, dst, target in plans[t].copies(ins[t], land, x, y, c):
                pltpu.make_async_remote_copy(src_ref=src, dst_ref=dst, send_sem=send_sems[t], recv_sem=recv_sems[t],
                                             device_id=target, device_id_type=MESH).start()
        token[...] = jnp.zeros_like(token)

    thru = [pltpu.HBM(a.shape, a.dtype) for a in list(srcs) + zones]
    outs = pl.pallas_call(
        body, name=name,
        out_shape=[pltpu.SemaphoreType.DMA(())] * (3 * n) + thru + [jax.ShapeDtypeStruct((8, LANE), F32)],
        in_specs=[HBM_SPEC] * (n + m) + [pl.BlockSpec(memory_space=pl.ANY)],
        out_specs=[SEM_SPEC] * (3 * n) + [HBM_SPEC] * (n + m) + [pl.BlockSpec(memory_space=pltpu.VMEM)],
        input_output_aliases={i: 3 * n + i for i in range(n + m)},
        compiler_params=pltpu.CompilerParams(has_side_effects=SPLIT_COPY),
    )(*[pltpu.with_memory_space_constraint(a, pltpu.HBM) for a in list(srcs) + zones], dep)
    return (outs[:3 * n], outs[3 * n:4 * n], outs[4 * n:4 * n + m], plans, zone_of), outs[-1]


def _split_wait(pending, after, name):
    sems, srcs, zones, plans, zone_of = pending
    n, m = len(srcs), len(zones)

    def body(*refs):
        ins, fresh, sems = refs[:n], refs[n:n + m], refs[n + m:n + m + 3 * n]
        send_sems, recv_sems, local_sems = sems[:n], sems[n:2 * n], sems[2 * n:]
        x, y, c = _my_place()
        for t in range(n):
            land = ins[t] if zone_of[t] is None else fresh[zone_of[t]]
            total = plans[t].total(land)
            done = pltpu.make_async_remote_copy(src_ref=total, dst_ref=total, send_sem=send_sems[t],
                                                recv_sem=recv_sems[t], device_id=(x, y, 1 - c), device_id_type=MESH)
            done.wait_send()
            done.wait_recv()
            own = plans[t].own(ins[t], land, x, y, c)
            if own is not None:
                pltpu.make_async_copy(*own, local_sems[t]).wait()

    thru = [pltpu.HBM(a.shape, a.dtype) for a in list(srcs) + list(zones)]
    outs = pl.pallas_call(
        body, name=name, out_shape=thru,
        in_specs=[HBM_SPEC] * (n + m) + [SEM_SPEC] * (3 * n) + [pl.BlockSpec(memory_space=pl.ANY)],
        out_specs=[HBM_SPEC] * (n + m), input_output_aliases={i: i for i in range(n + m)},
        compiler_params=pltpu.CompilerParams(has_side_effects=SPLIT_COPY),
    )(*srcs, *zones, *sems, after)
    return [outs[t] if zone_of[t] is None else outs[n + zone_of[t]] for t in range(n)]


def _join_cols(blocks, width, name):
    nb, rows, padded = blocks.shape
    tr = _tile(rows, (256, 128, 64, 32, 16))

    def body(x_ref, o_ref):
        for b in range(nb):
            o_ref[:, b * width:(b + 1) * width] = x_ref[b, :, 0:width]

    return pl.pallas_call(
        body, name=name, out_shape=jax.ShapeDtypeStruct((rows, nb * width), blocks.dtype), grid=(rows // tr,),
        in_specs=[pl.BlockSpec((nb, tr, padded), lambda i: (0, i, 0))],
        out_specs=pl.BlockSpec((tr, nb * width), lambda i: (i, 0)), compiler_params=_params("parallel"),
    )(blocks)


def _sum_blocks(x3d, name):
    n, rows, cols = x3d.shape
    tr = _tile(rows, (512, 256, 128, 64, 32, 16))

    def body(x_ref, o_ref):
        acc = x_ref[0].astype(F32)
        for j in range(1, n):
            acc = acc + x_ref[j].astype(F32)
        o_ref[...] = acc

    return pl.pallas_call(
        body, name=name, out_shape=jax.ShapeDtypeStruct((rows, cols), F32), grid=(rows // tr,),
        in_specs=[pl.BlockSpec((n, tr, cols), lambda i: (0, i, 0))],
        out_specs=pl.BlockSpec((tr, cols), lambda i: (i, 0)),
        compiler_params=_params("parallel"),
    )(x3d)


def _pack(arrs, dtype, lead=None):
    if lead is None:
        flat = jnp.concatenate([a.reshape(-1).astype(dtype) for a in arrs])
        n = flat.shape[0]
        total = -(-n // (16 * PACK_COLS)) * (16 * PACK_COLS)
        return jnp.pad(flat, (0, total - n)).reshape(-1, PACK_COLS)
    flat = jnp.concatenate([a.reshape(lead, -1).astype(dtype) for a in arrs], axis=1)
    n = flat.shape[1]
    total = -(-n // (16 * PACK_COLS)) * (16 * PACK_COLS)
    return jnp.pad(flat, ((0, 0), (0, total - n))).reshape(lead, -1, PACK_COLS)


def _unpack(packed, shapes, lead=None):
    out, off = [], 0
    if lead is None:
        flat = packed.reshape(-1)
        for s in shapes:
            n = math.prod(s)
            out.append(flat[off:off + n].reshape(s))
            off += n
        return out
    flat = packed.reshape(lead, -1)
    for s in shapes:
        n = math.prod(s)
        out.append(flat[:, off:off + n].reshape((lead,) + tuple(s)))
        off += n
    return out


def _join_blocks(g, axis):
    g = jnp.moveaxis(g, 0, axis)
    s = g.shape
    return g.reshape(s[:axis] + (s[axis] * s[axis + 1],) + s[axis + 2:])


def _mm_tiles(m, n, kdim, a_bytes):
    tk = kdim if kdim <= 2048 else _tile(kdim, (2048, 1664, 1024, 832, 512, 416, 256, 128))
    tm = _tile(m, (1024, 512, 256, 128, 64, 32, 16))
    tn = _tile(n, (1024, 512, 256, 128))

    def vmem(tm, tn):
        return 2 * tm * tk * a_bytes + 2 * tn * tk * 2 + 3 * tm * tn * 4

    while vmem(tm, tn) > VMEM_LIMIT * 3 // 4 and tn > 128 and tn % 256 == 0:
        tn //= 2
    while vmem(tm, tn) > VMEM_LIMIT * 3 // 4 and tm > 128 and tm % 256 == 0:
        tm //= 2
    return tm, tn, tk


def _mm(a, b, *, ta=False, tb=False, bl=None, out_dtype=F32, add=None, dep=None, name):
    if ta:
        kdim, m = a.shape
    else:
        m, kdim = a.shape
    bshape = b.shape if bl is None else b.shape[1:]
    n = bshape[0] if tb else bshape[1]
    tm, tn, tk = _mm_tiles(m, n, kdim, a.dtype.itemsize)
    nk = kdim // tk
    a_spec = pl.BlockSpec((tk, tm), lambda i, j, k: (k, i)) if ta else pl.BlockSpec((tm, tk), lambda i, j, k: (i, k))
    b_blk, b_idx = ((tn, tk), lambda i, j, k: (j, k)) if tb else ((tk, tn), lambda i, j, k: (k, j))
    if bl is None:
        b_spec = pl.BlockSpec(b_blk, b_idx)
    else:
        b_spec = pl.BlockSpec((None,) + b_blk, lambda i, j, k: (bl,) + b_idx(i, j, k))
    o_spec = pl.BlockSpec((tm, tn), lambda i, j, k: (i, j))
    dims = (((0 if ta else 1,), (1 if tb else 0,)), ((), ()))

    n_in = 2 + (add is not None) + (dep is not None)

    def body(*refs):
        a_ref, b_ref = refs[:2]
        add_ref = refs[2] if add is not None else None
        o_ref = refs[n_in]

        def finish(r):
            if add is not None:
                r = r + add_ref[...]
            o_ref[...] = r.astype(out_dtype)

        part = lax.dot_general(a_ref[...].astype(BF16), b_ref[...].astype(BF16), dims, preferred_element_type=F32)
        if nk == 1:
            finish(part)
            return
        acc_ref = refs[-1]
        k = pl.program_id(2)

        @pl.when(k == 0)
        def _():
            acc_ref[...] = part

        @pl.when(k > 0)
        def _():
            acc_ref[...] += part

        @pl.when(k == nk - 1)
        def _():
            finish(acc_ref[...])

    ins, specs = [a, b], [a_spec, b_spec]
    if add is not None:
        ins.append(add)
        specs.append(o_spec)
    if dep is not None:
        ins.append(dep)
        specs.append(pl.BlockSpec(memory_space=pl.ANY))
    return pl.pallas_call(
        body, name=name, out_shape=jax.ShapeDtypeStruct((m, n), out_dtype), grid=(m // tm, n // tn, nk),
        in_specs=specs, out_specs=o_spec, scratch_shapes=[pltpu.VMEM((tm, tn), F32)] if nk > 1 else [],
        compiler_params=_params("parallel", "parallel", "arbitrary"),
    )(*ins)


def _rms_fwd(x, g, name):
    s, d = x.shape
    tr = _tile(s, (256, 128, 64, 32, 16))

    def body(x_ref, g_ref, o_ref):
        xv = x_ref[...]
        r = lax.rsqrt(jnp.mean(xv * xv, axis=-1, keepdims=True) + RMS_EPS)
        o_ref[...] = (xv * r * g_ref[...]).astype(BF16)

    return pl.pallas_call(
        body, name=name, out_shape=jax.ShapeDtypeStruct((s, d), BF16), grid=(s // tr,),
        in_specs=[pl.BlockSpec((tr, d), lambda i: (i, 0)), pl.BlockSpec((1, d), lambda i: (0, 0))],
        out_specs=pl.BlockSpec((tr, d), lambda i: (i, 0)), compiler_params=_params("parallel"),
    )(x, g)


def _rms_bwd(x, g, dh, resid, name):
    s, d = x.shape
    tr = _tile(s, (256, 128, 64, 32, 16))

    def body(*refs):
        if resid is None:
            x_ref, g_ref, dh_ref, dx_ref, dg_ref = refs
        else:
            x_ref, g_ref, dh_ref, res_ref, dx_ref, dg_ref = refs

        @pl.when(pl.program_id(0) == 0)
        def _():
            dg_ref[...] = jnp.zeros_like(dg_ref)

        xv = x_ref[...]
        r = lax.rsqrt(jnp.mean(xv * xv, axis=-1, keepdims=True) + RMS_EPS)
        xh = xv * r
        dhv = dh_ref[...]
        dg_ref[0:1, :] += jnp.sum(dhv * xh, axis=0, keepdims=True)
        dyn = dhv * g_ref[...]
        dx = r * (dyn - xh * jnp.mean(dyn * xh, axis=-1, keepdims=True))
        if resid is not None:
            dx = dx + res_ref[...]
        dx_ref[...] = dx

    row = pl.BlockSpec((tr, d), lambda i: (i, 0))
    ins = [x, g, dh] + ([] if resid is None else [resid])
    specs = [row, pl.BlockSpec((1, d), lambda i: (0, 0)), row] + ([] if resid is None else [row])
    dx, dg = pl.pallas_call(
        body, name=name,
        out_shape=(jax.ShapeDtypeStruct((s, d), F32), jax.ShapeDtypeStruct((8, d), F32)), grid=(s // tr,),
        in_specs=specs, out_specs=(row, pl.BlockSpec((8, d), lambda i: (0, 0))),
        compiler_params=_params("arbitrary"),
    )(*ins)
    return dx, dg[0]


def _loss_bwd(x, g, tgt, name):
    s, d = x.shape
    tr = _tile(s, (256, 128, 64, 32, 16))

    def body(x_ref, g_ref, t_ref, dx_ref, dg_ref, loss_ref):
        @pl.when(pl.program_id(0) == 0)
        def _():
            dg_ref[...] = jnp.zeros_like(dg_ref)
            loss_ref[...] = jnp.zeros_like(loss_ref)

        xv = x_ref[...]
        r = lax.rsqrt(jnp.mean(xv * xv, axis=-1, keepdims=True) + RMS_EPS)
        xh = xv * r
        e = xh * g_ref[...] - t_ref[...]
        per_tok = jnp.mean(e * e, axis=-1, keepdims=True)
        loss_ref[...] += 0.5 * jnp.sum(per_tok, axis=0, keepdims=True)
        dy = e * (1.0 / d)
        dg_ref[0:1, :] += jnp.sum(dy * xh, axis=0, keepdims=True)
        dyn = dy * g_ref[...]
        dx_ref[...] = r * (dyn - xh * jnp.mean(dyn * xh, axis=-1, keepdims=True))

    row = pl.BlockSpec((tr, d), lambda i: (i, 0))
    dx, dg, loss = pl.pallas_call(
        body, name=name,
        out_shape=(jax.ShapeDtypeStruct((s, d), F32), jax.ShapeDtypeStruct((8, d), F32),
                   jax.ShapeDtypeStruct((8, LANE), F32)),
        grid=(s // tr,),
        in_specs=[row, pl.BlockSpec((1, d), lambda i: (0, 0)), row],
        out_specs=(row, pl.BlockSpec((8, d), lambda i: (0, 0)), pl.BlockSpec((8, LANE), lambda i: (0, 0))),
        compiler_params=_params("arbitrary"),
    )(x, g, tgt)
    return loss[0:1, 0:1], dx, dg[0:1]


def _conv_taps(gbuf, w_ref, tt, ntap, lo):
    acc = w_ref[0:1, :] * gbuf[pl.ds(lo, tt), :]
    for k in range(1, ntap):
        acc = acc + w_ref[k:k + 1, :] * gbuf[pl.ds(lo + k, tt), :]
    return acc


def _conv_time_tile(s):
    return _tile(s, (256, 128, 64, 32))


def _conv_fwd(u, wpad, dw_b, ln_g, ln_b, pw, l, name):
    s = u.shape[0]
    c = pw.shape[1]
    ntap = 31
    tt = _conv_time_tile(s)
    hb = tt // CONV_HALO

    def body(val_ref, glu_ref, valh_ref, gluh_ref, w_ref, b_ref, lg_ref, lb_ref, pw_ref, o_ref, gbuf):
        i = pl.program_id(0)
        glh = valh_ref[...] * _sigmoid(gluh_ref[...])
        gbuf[0:CONV_HALO, :] = jnp.where(i > 0, glh, 0.0)
        gbuf[CONV_HALO:CONV_HALO + tt, :] = val_ref[...] * _sigmoid(glu_ref[...])
        acc = _conv_taps(gbuf, w_ref, tt, ntap, CONV_HALO - (ntap - 1)) + b_ref[...]
        xc = acc - jnp.mean(acc, axis=-1, keepdims=True)
        rstd = lax.rsqrt(jnp.mean(xc * xc, axis=-1, keepdims=True) + LN_EPS)
        ln = xc * rstd * lg_ref[...] + lb_ref[...]
        sw = ln * _sigmoid(ln)
        o_ref[...] = _dot(sw.astype(BF16), pw_ref[...])

    vec = pl.BlockSpec((1, c), lambda i: (0, 0))
    return pl.pallas_call(
        body, name=name, out_shape=jax.ShapeDtypeStruct((s, c), F32), grid=(s // tt,),
        in_specs=[pl.BlockSpec((tt, c), lambda i: (i, 0)), pl.BlockSpec((tt, c), lambda i: (i, 1)),
                  pl.BlockSpec((CONV_HALO, c), lambda i: (jnp.maximum(i * hb - 1, 0), 0)),
                  pl.BlockSpec((CONV_HALO, c), lambda i: (jnp.maximum(i * hb - 1, 0), 1)),
                  pl.BlockSpec((32, c), lambda i: (0, 0)), vec, vec, vec,
                  pl.BlockSpec((None, c, c), lambda i: (l, 0, 0))],
        out_specs=pl.BlockSpec((tt, c), lambda i: (i, 0)),
        scratch_shapes=[pltpu.VMEM((CONV_HALO + tt, c), F32)],
        compiler_params=_params("parallel"),
    )(u, u, u, u, wpad, dw_b, ln_g, ln_b, pw)


def _conv_bwd_post(u, dyc, wpad, dw_b, ln_g, ln_b, pw, l, name):
    s = u.shape[0]
    c = pw.shape[1]
    ntap = 31
    tt = _conv_time_tile(s)
    hb = tt // CONV_HALO

    def body(val_ref, glu_ref, valh_ref, gluh_ref, dy_ref, w_ref, b_ref, lg_ref, lb_ref, pw_ref,
             dd_ref, gl_ref, dpw_ref, vec_ref, gbuf):
        i = pl.program_id(0)

        @pl.when(i == 0)
        def _():
            dpw_ref[...] = jnp.zeros_like(dpw_ref)
            vec_ref[...] = jnp.zeros_like(vec_ref)

        glh = valh_ref[...] * _sigmoid(gluh_ref[...])
        gbuf[0:CONV_HALO, :] = jnp.where(i > 0, glh, 0.0)
        gl = val_ref[...] * _sigmoid(glu_ref[...])
        gbuf[CONV_HALO:CONV_HALO + tt, :] = gl
        gl_ref[...] = gl
        acc = _conv_taps(gbuf, w_ref, tt, ntap, CONV_HALO - (ntap - 1)) + b_ref[...]
        xc = acc - jnp.mean(acc, axis=-1, keepdims=True)
        rstd = lax.rsqrt(jnp.mean(xc * xc, axis=-1, keepdims=True) + LN_EPS)
        xh = xc * rstd
        ln = xh * lg_ref[...] + lb_ref[...]
        sig = _sigmoid(ln)
        sw = ln * sig
        dyb = dy_ref[...].astype(BF16)
        dpw_ref[...] += _dot_tn(sw.astype(BF16), dyb)
        dsw = _dot_nt(dyb, pw_ref[...])
        dln = dsw * (sig * (1.0 + ln * (1.0 - sig)))
        vec_ref[0:1, :] += jnp.sum(dln * xh, axis=0, keepdims=True)
        vec_ref[1:2, :] += jnp.sum(dln, axis=0, keepdims=True)
        dxh = dln * lg_ref[...]
        dd = rstd * (dxh - jnp.mean(dxh, axis=-1, keepdims=True)
                     - xh * jnp.mean(dxh * xh, axis=-1, keepdims=True))
        vec_ref[2:3, :] += jnp.sum(dd, axis=0, keepdims=True)
        dd_ref[...] = dd

    vec = pl.BlockSpec((1, c), lambda i: (0, 0))
    tile = pl.BlockSpec((tt, c), lambda i: (i, 0))
    return pl.pallas_call(
        body, name=name,
        out_shape=(jax.ShapeDtypeStruct((s, c), F32), jax.ShapeDtypeStruct((s, c), F32),
                   jax.ShapeDtypeStruct((c, c), F32), jax.ShapeDtypeStruct((8, c), F32)),
        grid=(s // tt,),
        in_specs=[tile, pl.BlockSpec((tt, c), lambda i: (i, 1)),
                  pl.BlockSpec((CONV_HALO, c), lambda i: (jnp.maximum(i * hb - 1, 0), 0)),
                  pl.BlockSpec((CONV_HALO, c), lambda i: (jnp.maximum(i * hb - 1, 0), 1)),
                  tile, pl.BlockSpec((32, c), lambda i: (0, 0)), vec, vec, vec,
                  pl.BlockSpec((None, c, c), lambda i: (l, 0, 0))],
        out_specs=(tile, tile, pl.BlockSpec((c, c), lambda i: (0, 0)), pl.BlockSpec((8, c), lambda i: (0, 0))),
        scratch_shapes=[pltpu.VMEM((CONV_HALO + tt, c), F32)],
        compiler_params=_params("arbitrary"),
    )(u, u, u, u, dyc, wpad, dw_b, ln_g, ln_b, pw)


def _conv_bwd_dw(u, dd, gl, wpad, name):
    s, c = dd.shape
    ntap = 31
    tt = _conv_time_tile(s)
    hb = tt // CONV_HALO
    nt = s // tt
    last_halo = s // CONV_HALO - 1

    def body(val_ref, glu_ref, dd_ref, ddn_ref, gl_ref, glh_ref, w_ref, dval_ref, dglu_ref, dw_ref, dbuf, gbuf):
        i = pl.program_id(0)

        @pl.when(i == 0)
        def _():
            dw_ref[...] = jnp.zeros_like(dw_ref)

        d = dd_ref[...]
        dbuf[0:tt, :] = d
        dbuf[tt:tt + CONV_HALO, :] = jnp.where(i < nt - 1, ddn_ref[...], 0.0)
        gbuf[0:CONV_HALO, :] = jnp.where(i > 0, glh_ref[...], 0.0)
        gbuf[CONV_HALO:CONV_HALO + tt, :] = gl_ref[...]
        dgl = w_ref[0:1, :] * dbuf[pl.ds(ntap - 1, tt), :]
        for k in range(1, ntap):
            dgl = dgl + w_ref[k:k + 1, :] * dbuf[pl.ds(ntap - 1 - k, tt), :]
        for k in range(ntap):
            dw_ref[k:k + 1, :] += jnp.sum(d * gbuf[pl.ds(CONV_HALO - (ntap - 1) + k, tt), :], axis=0, keepdims=True)
        sg = _sigmoid(glu_ref[...])
        dval_ref[...] = (dgl * sg).astype(BF16)
        dglu_ref[...] = (dgl * val_ref[...] * sg * (1.0 - sg)).astype(BF16)

    tile = pl.BlockSpec((tt, c), lambda i: (i, 0))
    return pl.pallas_call(
        body, name=name,
        out_shape=(jax.ShapeDtypeStruct((s, c), BF16), jax.ShapeDtypeStruct((s, c), BF16),
                   jax.ShapeDtypeStruct((32, c), F32)),
        grid=(nt,),
        in_specs=[tile, pl.BlockSpec((tt, c), lambda i: (i, 1)), tile,
                  pl.BlockSpec((CONV_HALO, c), lambda i: (jnp.minimum((i + 1) * hb, last_halo), 0)),
                  tile, pl.BlockSpec((CONV_HALO, c), lambda i: (jnp.maximum(i * hb - 1, 0), 0)),
                  pl.BlockSpec((32, c), lambda i: (0, 0))],
        out_specs=(tile, tile, pl.BlockSpec((32, c), lambda i: (0, 0))),
        scratch_shapes=[pltpu.VMEM((tt + CONV_HALO, c), F32), pltpu.VMEM((CONV_HALO + tt, c), F32)],
        compiler_params=_params("arbitrary"),
    )(u, u, dd, dd, gl, gl, wpad)


SB_ROWS = 64


def _tri(n, cmp):
    r = lax.broadcasted_iota(jnp.int32, (n, n), 0)
    c = lax.broadcasted_iota(jnp.int32, (n, n), 1)
    return jnp.where(cmp(r, c), 1.0, 0.0).astype(BF16)


def _row_chunks(fn, n, *arrs):
    outs = [fn(*[a[r:r + SB_ROWS] for a in arrs]) for r in range(0, n, SB_ROWS)]
    return tuple(jnp.concatenate(list(o), axis=0) for o in zip(*outs))


def _hi_lo(v):
    hi = v.astype(BF16)
    return hi, (v - hi.astype(F32)).astype(BF16)


def _sb_sticks(z, causal):
    l1p = jnp.log(1.0 + jnp.exp(-jnp.abs(z)))
    lb = jnp.minimum(z, 0.0) - l1p
    ell = lb - z
    if causal is not None:
        ell = jnp.where(causal, ell, 0.0)
    hi, lo = _hi_lo(ell)
    return lb, hi, lo, jnp.sum(ell, axis=1, keepdims=True)


def _sb_fwd(u, heads, q_blk, k_blk, v_blk, name):
    s = u.shape[0]
    tq = _tile(s, (256, 128))
    scale = HEAD ** -0.5

    def body(q_ref, k_ref, v_ref, tri_ref, o_ref, tot_ref, kb_ref, vb_ref):
        i = pl.program_id(1)

        @pl.when(i == 0)
        def _():
            kb_ref[...] = k_ref[...].astype(BF16)
            vb_ref[...] = v_ref[...].astype(BF16)

        qb = (q_ref[...] * scale).astype(BF16)
        below_diag = lax.broadcasted_iota(jnp.int32, (tq, tq), 1) < lax.broadcasted_iota(jnp.int32, (tq, tq), 0)

        def blocks(j0, nb, c_a, acc, diag):
            mask = [below_diag] if diag else []
            rows = pl.ds(pl.multiple_of(j0 * tq, tq), nb * tq)
            kb = kb_ref[rows, :]
            vb = vb_ref[rows, :]
            z = _dot_nt(qb, kb)
            t_sfx = tri_ref[...]

            def sticks(zc, *m):
                out = []
                for b in range(nb):
                    out += _sb_sticks(zc[:, b * tq:(b + 1) * tq], m[0] if m else None)
                return tuple(out)

            st = _row_chunks(sticks, tq, z, *mask)
            lb, hi, lo, rs = st[0::4], st[1::4], st[2::4], st[3::4]
            sfx = [_dot(hi[b], t_sfx) + _dot(lo[b], t_sfx) for b in range(nb)]
            before, run = [None] * nb, c_a
            for b in reversed(range(nb)):
                before[b], run = run, run + rs[b]

            def weights(*a):
                ws = []
                for b in range(nb):
                    lbc, sfxc, befc = a[3 * b:3 * b + 3]
                    w = jnp.exp(lbc + (befc + sfxc))
                    if diag:
                        w = jnp.where(a[-1], w, 0.0)
                    ws.append(w.astype(BF16))
                return (ws[0] if nb == 1 else jnp.concatenate(ws, axis=1),)

            flat = [v for b in range(nb) for v in (lb[b], sfx[b], before[b])]
            wb, = _row_chunks(weights, tq, *flat, *mask)
            return run, acc + _dot(wb, vb)

        carry = blocks(i, 1, jnp.zeros((tq, 1), F32), jnp.zeros((tq, HEAD), F32), True)
        carry = lax.fori_loop(0, i // 4, lambda t, cr: blocks(i - 4 - 4 * t, 4, *cr, False), carry)
        carry = lax.fori_loop(0, (i % 4) // 2, lambda _, cr: blocks(i % 2, 2, *cr, False), carry)
        c_a, acc = lax.fori_loop(0, i % 2, lambda _, cr: blocks(0, 1, *cr, False), carry)
        o_ref[...] = acc
        tot_ref[...] = jnp.broadcast_to(c_a, (tq, HEAD))

    full = lambda off: pl.BlockSpec((s, HEAD), lambda h, i: (0, off + h))
    out = pl.BlockSpec((tq, HEAD), lambda h, i: (i, h))
    return pl.pallas_call(
        body, name=name,
        out_shape=(jax.ShapeDtypeStruct((s, heads * HEAD), F32), jax.ShapeDtypeStruct((s, heads * HEAD), F32)),
        grid=(heads, s // tq),
        in_specs=[pl.BlockSpec((tq, HEAD), lambda h, i: (i, q_blk + h)), full(k_blk), full(v_blk),
                  pl.BlockSpec((tq, tq), lambda h, i: (0, 0))],
        out_specs=(out, out),
        scratch_shapes=[pltpu.VMEM((s, HEAD), BF16), pltpu.VMEM((s, HEAD), BF16)],
        compiler_params=_params("parallel", "arbitrary"),
    )(u, u, u, _tri(tq, lambda r, c: r > c))


def _sb_bwd(u, tot, dy, heads, q_blk, k_blk, v_blk, dep, name):
    s = u.shape[0]
    tq = _tile(s, (256, 128))
    scale = HEAD ** -0.5

    def body(q_ref, k_ref, v_ref, tot_ref, dy_ref, incl_ref, excl_ref, dep_ref, dq_ref, dk_ref, dv_ref,
             kb_ref, vb_ref):
        i = pl.program_id(1)

        @pl.when(i == 0)
        def _():
            dk_ref[...] = jnp.zeros_like(dk_ref)
            dv_ref[...] = jnp.zeros_like(dv_ref)
            kb_ref[...] = k_ref[...].astype(BF16)
            vb_ref[...] = v_ref[...].astype(BF16)

        qb = (q_ref[...] * scale).astype(BF16)
        dob = dy_ref[...].astype(BF16)
        total = tot_ref[:, 0:1]
        below_diag = lax.broadcasted_iota(jnp.int32, (tq, tq), 1) < lax.broadcasted_iota(jnp.int32, (tq, tq), 0)

        def blocks(j0, nb, c_p, c_g, dq, diag):
            mask = [below_diag] if diag else []
            rows = pl.ds(pl.multiple_of(j0 * tq, tq), nb * tq)
            kb = kb_ref[rows, :]
            vb = vb_ref[rows, :]
            z = _dot_nt(qb, kb)
            dw = _dot_nt(dob, vb)
            t_incl, t_excl = incl_ref[...], excl_ref[...]
            cols = lambda a, b: a[:, b * tq:(b + 1) * tq]

            def sticks(zc, *m):
                out = []
                for b in range(nb):
                    out += _sb_sticks(cols(zc, b), m[0] if m else None)
                return tuple(out)

            st = _row_chunks(sticks, tq, z, *mask)
            lb, hi, lo, rs_l = st[0::4], st[1::4], st[2::4], st[3::4]
            pfx = [_dot(hi[b], t_incl) + _dot(lo[b], t_incl) for b in range(nb)]
            p_before = [c_p]
            for b in range(nb):
                p_before.append(p_before[-1] + rs_l[b])

            def weights(totc, dwc, *a):
                out = []
                for b in range(nb):
                    lbc, pfxc, pbc = a[3 * b:3 * b + 3]
                    w = jnp.exp(lbc + (totc - (pbc + pfxc)))
                    if diag:
                        w = jnp.where(a[-1], w, 0.0)
                    g = w * cols(dwc, b)
                    out += [w.astype(BF16), g, g.astype(BF16), jnp.sum(g, axis=1, keepdims=True)]
                return tuple(out)

            flat = [v for b in range(nb) for v in (lb[b], pfx[b], p_before[b])]
            wt = _row_chunks(weights, tq, total, dw, *flat, *mask)
            wb, g, gb, rs_g = wt[0::4], wt[1::4], wt[2::4], wt[3::4]
            g_pre = [_dot(gb[b], t_excl) for b in range(nb)]
            g_before = [c_g]
            for b in range(nb):
                g_before.append(g_before[-1] + rs_g[b])

            def dscore(*a):
                dzs = []
                for b in range(nb):
                    lbc, gc, gprec, gbc = a[4 * b:4 * b + 4]
                    beta = jnp.exp(lbc)
                    dz = gc * (1.0 - beta) - (gbc + gprec) * beta
                    if diag:
                        dz = jnp.where(a[-1], dz, 0.0)
                    dzs.append(dz.astype(BF16))
                return (dzs[0] if nb == 1 else jnp.concatenate(dzs, axis=1),)

            flat = [v for b in range(nb) for v in (lb[b], g[b], g_pre[b], g_before[b])]
            dzb, = _row_chunks(dscore, tq, *flat, *mask)
            wcat = wb[0] if nb == 1 else jnp.concatenate(wb, axis=1)
            dk_ref[rows, :] += _dot_tn(dzb, qb)
            dv_ref[rows, :] += _dot_tn(wcat, dob)
            return p_before[-1], g_before[-1], dq + _dot(dzb, kb)

        zero = jnp.zeros((tq, 1), F32)
        carry = lax.fori_loop(0, i // 2, lambda t, cr: blocks(2 * t, 2, *cr, False),
                              (zero, zero, jnp.zeros((tq, HEAD), F32)))
        carry = lax.fori_loop(0, i % 2, lambda _, cr: blocks(i - 1, 1, *cr, False), carry)
        _, _, dq = blocks(i, 1, *carry, True)
        dq_ref[...] = dq * scale

    full = lambda off: pl.BlockSpec((s, HEAD), lambda h, i: (0, off + h))
    blk = pl.BlockSpec((tq, HEAD), lambda h, i: (i, h))
    acc = pl.BlockSpec((s, HEAD), lambda h, i: (0, h))
    tri = pl.BlockSpec((tq, tq), lambda h, i: (0, 0))
    shape = jax.ShapeDtypeStruct((s, heads * HEAD), F32)
    return pl.pallas_call(
        body, name=name, out_shape=(shape, shape, shape), grid=(heads, s // tq),
        in_specs=[pl.BlockSpec((tq, HEAD), lambda h, i: (i, q_blk + h)), full(k_blk), full(v_blk), blk, blk, tri, tri,
                  pl.BlockSpec(memory_space=pl.ANY)],
        out_specs=(blk, acc, acc),
        scratch_shapes=[pltpu.VMEM((s, HEAD), BF16), pltpu.VMEM((s, HEAD), BF16)],
        compiler_params=_params("parallel", "arbitrary"),
    )(u, u, u, tot, dy, _tri(tq, lambda r, c: r <= c), _tri(tq, lambda r, c: r < c), dep)


def _lru_time_tile(s):
    return _tile(s, (256, 128, 64, 32))


def _lru_gates(xc, wa_ref, ba_ref, wx_ref, bx_ref, lam_ref, nh):
    pr, pi = [], []
    for n in range(nh):
        xn = xc[:, n * HEAD:(n + 1) * HEAD].astype(BF16)
        pr.append(_dot(xn, wa_ref[n]))
        pi.append(_dot(xn, wx_ref[n]))
    r = _sigmoid((pr[0] if nh == 1 else jnp.concatenate(pr, axis=1)) + ba_ref[...])
    ig = _sigmoid((pi[0] if nh == 1 else jnp.concatenate(pi, axis=1)) + bx_ref[...])
    lam = lam_ref[...]
    sp = jnp.maximum(-lam, 0.0) + jnp.log(1.0 + jnp.exp(-jnp.abs(lam)))
    log_a = -LRU_C * r * sp
    a = jnp.exp(log_a)
    mult = jnp.sqrt(-_expm1(2.0 * log_a))
    return r, ig, a, mult, sp


def _lru_fwd(u, x_blk, cw, cb, wa, ba, wx, bx, lam, name):
    s = u.shape[0]
    w = lam.shape[1]
    nh = w // HEAD
    tt = _lru_time_tile(s)
    hb = tt // LRU_HALO

    def body(x_ref, xh_ref, cw_ref, cb_ref, wa_ref, ba_ref, wx_ref, bx_ref, lam_ref, y_ref,
             xbuf, abuf, bbuf, hstate, rowbuf):
        i = pl.program_id(0)

        @pl.when(i == 0)
        def _():
            hstate[...] = jnp.zeros_like(hstate)

        xbuf[0:LRU_HALO, :] = jnp.where(i > 0, xh_ref[...], 0.0)
        xbuf[LRU_HALO:LRU_HALO + tt, :] = x_ref[...]
        xc = _conv_taps(xbuf, cw_ref, tt, 4, LRU_HALO - 3) + cb_ref[...]
        _, ig, a, mult, _ = _lru_gates(xc, wa_ref, ba_ref, wx_ref, bx_ref, lam_ref, nh)
        abuf[...] = a
        bbuf[...] = mult * (ig * xc)

        def group(gi, h):
            rows = pl.ds(pl.multiple_of(gi * 8, 8), 8)
            a8 = abuf[rows, :]
            b8 = bbuf[rows, :]
            for j in range(8):
                h = a8[j:j + 1, :] * h + b8[j:j + 1, :]
                rowbuf[j:j + 1, :] = h
            y_ref[rows, :] = rowbuf[...]
            return h

        hstate[0:1, :] = lax.fori_loop(0, tt // 8, group, hstate[0:1, :])

    vec = pl.BlockSpec((1, w), lambda i: (0, 0))
    gate = pl.BlockSpec((nh, HEAD, HEAD), lambda i: (0, 0, 0))
    return pl.pallas_call(
        body, name=name, out_shape=jax.ShapeDtypeStruct((s, w), F32), grid=(s // tt,),
        in_specs=[pl.BlockSpec((tt, w), lambda i: (i, x_blk)),
                  pl.BlockSpec((LRU_HALO, w), lambda i: (jnp.maximum(i * hb - 1, 0), x_blk)),
                  pl.BlockSpec((8, w), lambda i: (0, 0)), vec, gate, vec, gate, vec, vec],
        out_specs=pl.BlockSpec((tt, w), lambda i: (i, 0)),
        scratch_shapes=[pltpu.VMEM((LRU_HALO + tt, w), F32), pltpu.VMEM((tt, w), F32), pltpu.VMEM((tt, w), F32),
                        pltpu.VMEM((8, w), F32), pltpu.VMEM((8, w), F32)],
        compiler_params=_params("arbitrary"),
    )(u, u, cw, cb, wa, ba, wx, bx, lam)


def _lru_bwd(u, x_blk, hseq, dy, cw, cb, wa, ba, wx, bx, lam, name):
    s = u.shape[0]
    w = lam.shape[1]
    nh = w // HEAD
    tt = _lru_time_tile(s)
    hb = tt // LRU_HALO
    nt = s // tt

    def body(x_ref, xh_ref, h_ref, hh_ref, dy_ref, cw_ref, cb_ref, wa_ref, ba_ref, wx_ref, bx_ref, lam_ref,
             dx_ref, dwa_ref, dwx_ref, vec_ref, xbuf, hbuf, abuf, lbuf, dbuf, cstate, dhalo, rowbuf):
        i = pl.program_id(0)
        rt = nt - 1 - i

        @pl.when(i == 0)
        def _():
            cstate[...] = jnp.zeros_like(cstate)
            dhalo[...] = jnp.zeros_like(dhalo)
            dwa_ref[...] = jnp.zeros_like(dwa_ref)
            dwx_ref[...] = jnp.zeros_like(dwx_ref)
            vec_ref[...] = jnp.zeros_like(vec_ref)

        xbuf[0:LRU_HALO, :] = jnp.where(rt > 0, xh_ref[...], 0.0)
        xbuf[LRU_HALO:LRU_HALO + tt, :] = x_ref[...]
        hbuf[0:LRU_HALO, :] = jnp.where(rt > 0, hh_ref[...], 0.0)
        hbuf[LRU_HALO:LRU_HALO + tt, :] = h_ref[...]
        xc = _conv_taps(xbuf, cw_ref, tt, 4, LRU_HALO - 3) + cb_ref[...]
        r, ig, a, mult, sp = _lru_gates(xc, wa_ref, ba_ref, wx_ref, bx_ref, lam_ref, nh)
        abuf[...] = a

        def group(gi, c):
            rows = pl.ds(pl.multiple_of((tt // 8 - 1 - gi) * 8, 8), 8)
            a8 = abuf[rows, :]
            d8 = dy_ref[rows, :]
            for j in range(7, -1, -1):
                lam_t = d8[j:j + 1, :] + c
                rowbuf[j:j + 1, :] = lam_t
                c = a8[j:j + 1, :] * lam_t
            lbuf[rows, :] = rowbuf[...]
            return c

        cstate[0:1, :] = lax.fori_loop(0, tt // 8, group, cstate[0:1, :])

        lam_t = lbuf[...]
        hprev = hbuf[pl.ds(LRU_HALO - 1, tt), :]
        ixc = ig * xc
        d_ixc = lam_t * mult
        d_ig = d_ixc * xc
        dxc = d_ixc * ig
        dlog_a = lam_t * hprev * a + lam_t * ixc * (-(a * a) / mult)
        dr = dlog_a * (-LRU_C * sp)
        lam_p = lam_ref[...]
        dsp = -_sigmoid(-lam_p)
        vec_ref[2:3, :] += jnp.sum(dlog_a * (-LRU_C * r), axis=0, keepdims=True) * dsp
        dpr = dr * r * (1.0 - r)
        dpi = d_ig * ig * (1.0 - ig)
        vec_ref[0:1, :] += jnp.sum(dpr, axis=0, keepdims=True)
        vec_ref[1:2, :] += jnp.sum(dpi, axis=0, keepdims=True)
        parts = []
        for n in range(nh):
            sl = slice(n * HEAD, (n + 1) * HEAD)
            xn = xc[:, sl].astype(BF16)
            dprn = dpr[:, sl].astype(BF16)
            dpin = dpi[:, sl].astype(BF16)
            dwa_ref[n] += _dot_tn(xn, dprn)
            dwx_ref[n] += _dot_tn(xn, dpin)
            parts.append(_dot_nt(dprn, wa_ref[n]) + _dot_nt(dpin, wx_ref[n]))
        dxc = dxc + (parts[0] if nh == 1 else jnp.concatenate(parts, axis=1))
        vec_ref[3:4, :] += jnp.sum(dxc, axis=0, keepdims=True)
        dbuf[0:tt, :] = dxc
        dbuf[tt:tt + LRU_HALO, :] = dhalo[...]
        dx = cw_ref[0:1, :] * dbuf[pl.ds(3, tt), :]
        for k in range(1, 4):
            dx = dx + cw_ref[k:k + 1, :] * dbuf[pl.ds(3 - k, tt), :]
        dx_ref[...] = dx.astype(BF16)
        for k in range(4):
            vec_ref[4 + k:5 + k, :] += jnp.sum(dxc * xbuf[pl.ds(LRU_HALO - 3 + k, tt), :], axis=0, keepdims=True)
        dhalo[...] = dbuf[0:LRU_HALO, :]

    vec = pl.BlockSpec((1, w), lambda i: (0, 0))
    gate = pl.BlockSpec((nh, HEAD, HEAD), lambda i: (0, 0, 0))
    rev = lambda i: nt - 1 - i
    tile = pl.BlockSpec((tt, w), lambda i: (rev(i), 0))
    halo = lambda col: pl.BlockSpec((LRU_HALO, w), lambda i: (jnp.maximum(rev(i) * hb - 1, 0), col))
    return pl.pallas_call(
        body, name=name,
        out_shape=(jax.ShapeDtypeStruct((s, w), BF16), jax.ShapeDtypeStruct((nh, HEAD, HEAD), F32),
                   jax.ShapeDtypeStruct((nh, HEAD, HEAD), F32), jax.ShapeDtypeStruct((8, w), F32)),
        grid=(nt,),
        in_specs=[pl.BlockSpec((tt, w), lambda i: (rev(i), x_blk)), halo(x_blk), tile, halo(0), tile,
                  pl.BlockSpec((8, w), lambda i: (0, 0)), vec, gate, vec, gate, vec, vec],
        out_specs=(tile, gate, gate, pl.BlockSpec((8, w), lambda i: (0, 0))),
        scratch_shapes=[pltpu.VMEM((LRU_HALO + tt, w), F32), pltpu.VMEM((LRU_HALO + tt, w), F32),
                        pltpu.VMEM((tt, w), F32), pltpu.VMEM((tt, w), F32), pltpu.VMEM((tt + LRU_HALO, w), F32),
                        pltpu.VMEM((8, w), F32), pltpu.VMEM((8, w), F32), pltpu.VMEM((8, w), F32)],
        compiler_params=_params("arbitrary"),
    )(u, u, hseq, hseq, dy, cw, cb, wa, ba, wx, bx, lam)


def _gate_specs(c, tr):
    return [pl.BlockSpec((tr, c), lambda i, b=b: (i, b)) for b in (2, 9, 10, 12)]


def _outgate_fwd(y_conv, y_attn, y_lru, u, n_conv, n_attn, n_lru, name):
    s, c = y_conv.shape
    tr = _tile(s, (256, 128, 64, 32, 16))

    def body(yc_ref, ya_ref, yl_ref, gc_ref, ga0_ref, ga1_ref, gl_ref, nc_ref, na_ref, nl_ref, o_ref):
        def rinv(v):
            return lax.rsqrt(jnp.mean(v * v, axis=-1, keepdims=True) + RMS_EPS)

        def silu(g):
            return g * _sigmoid(g)

        yc = yc_ref[...]
        o_ref[:, 0:c] = (yc * rinv(yc) * nc_ref[...] * silu(gc_ref[...])).astype(BF16)
        ya = ya_ref[...]
        ra = rinv(ya)
        o_ref[:, c:2 * c] = (ya[:, 0:c] * ra * na_ref[:, 0:c] * silu(ga0_ref[...])).astype(BF16)
        o_ref[:, 2 * c:3 * c] = (ya[:, c:2 * c] * ra * na_ref[:, c:2 * c] * silu(ga1_ref[...])).astype(BF16)
        yl = yl_ref[...]
        o_ref[:, 3 * c:4 * c] = (yl * rinv(yl) * nl_ref[...] * silu(gl_ref[...])).astype(BF16)

    row = lambda wd: pl.BlockSpec((tr, wd), lambda i: (i, 0))
    vec = lambda wd: pl.BlockSpec((1, wd), lambda i: (0, 0))
    return pl.pallas_call(
        body, name=name, out_shape=jax.ShapeDtypeStruct((s, 4 * c), BF16), grid=(s // tr,),
        in_specs=[row(c), row(2 * c), row(c)] + _gate_specs(c, tr) + [vec(c), vec(2 * c), vec(c)],
        out_specs=row(4 * c), compiler_params=_params("parallel"),
    )(y_conv, y_attn, y_lru, u, u, u, u, n_conv, n_attn, n_lru)


def _outgate_bwd(dy, y_conv, y_attn, y_lru, u, n_conv, n_attn, n_lru, name):
    s, c = y_conv.shape
    tr = _tile(s, (256, 128, 64, 32, 16))

    def body(dy_ref, yc_ref, ya_ref, yl_ref, gc_ref, ga0_ref, ga1_ref, gl_ref, nc_ref, na_ref, nl_ref,
             dyc_ref, dya_ref, dyl_ref, dgc_ref, dga_ref, dgl_ref, dn_ref):
        @pl.when(pl.program_id(0) == 0)
        def _():
            dn_ref[...] = jnp.zeros_like(dn_ref)

        def group(yv, gate, wv, d):
            r = lax.rsqrt(jnp.mean(yv * yv, axis=-1, keepdims=True) + RMS_EPS)
            yh = yv * r
            sg = _sigmoid(gate)
            dn = d * (gate * sg)
            dgate = d * (yh * wv) * (sg * (1.0 + gate * (1.0 - sg)))
            dw = jnp.sum(dn * yh, axis=0, keepdims=True)
            dyn = dn * wv
            dyv = r * (dyn - yh * jnp.mean(dyn * yh, axis=-1, keepdims=True))
            return dyv, dgate, dw

        dyv, dg, dw = group(yc_ref[...], gc_ref[...], nc_ref[...], dy_ref[:, 0:c])
        dyc_ref[...] = dyv
        dgc_ref[...] = dg.astype(BF16)
        dn_ref[0:1, 0:c] += dw
        gate_a = jnp.concatenate([ga0_ref[...], ga1_ref[...]], axis=1)
        dyv, dg, dw = group(ya_ref[...], gate_a, na_ref[...], dy_ref[:, c:3 * c])
        dya_ref[...] = dyv
        dga_ref[...] = dg.astype(BF16)
        dn_ref[0:1, c:3 * c] += dw
        dyv, dg, dw = group(yl_ref[...], gl_ref[...], nl_ref[...], dy_ref[:, 3 * c:4 * c])
        dyl_ref[...] = dyv
        dgl_ref[...] = dg.astype(BF16)
        dn_ref[0:1, 3 * c:4 * c] += dw

    row = lambda wd: pl.BlockSpec((tr, wd), lambda i: (i, 0))
    vec = lambda wd: pl.BlockSpec((1, wd), lambda i: (0, 0))
    sh = lambda wd, dt: jax.ShapeDtypeStruct((s, wd), dt)
    return pl.pallas_call(
        body, name=name,
        out_shape=(sh(c, F32), sh(2 * c, F32), sh(c, F32), sh(c, BF16), sh(2 * c, BF16), sh(c, BF16),
                   jax.ShapeDtypeStruct((8, 4 * c), F32)),
        grid=(s // tr,),
        in_specs=[row(4 * c), row(c), row(2 * c), row(c)] + _gate_specs(c, tr) + [vec(c), vec(2 * c), vec(c)],
        out_specs=(row(c), row(2 * c), row(c), row(c), row(2 * c), row(c),
                   pl.BlockSpec((8, 4 * c), lambda i: (0, 0))),
        compiler_params=_params("arbitrary"),
    )(dy, y_conv, y_attn, y_lru, u, u, u, u, n_conv, n_attn, n_lru)


def _xattn_probs(qh, kh, scale):
    sc = _dot_nt(qh, kh) * scale
    p = jnp.exp(sc - jnp.max(sc, axis=-1, keepdims=True))
    return p / jnp.sum(p, axis=-1, keepdims=True)


def _xattn_fwd(q, kv, name):
    s, xw = q.shape
    m = kv.shape[0]
    nh = xw // HEAD
    tq = _tile(s, (256, 128, 64, 32, 16))
    scale = HEAD ** -0.5

    def body(q_ref, kv_ref, o_ref):
        for h in range(nh):
            qh = q_ref[:, h * HEAD:(h + 1) * HEAD].astype(BF16)
            kh = kv_ref[:, h * HEAD:(h + 1) * HEAD].astype(BF16)
            vh = kv_ref[:, xw + h * HEAD:xw + (h + 1) * HEAD].astype(BF16)
            p = _xattn_probs(qh, kh, scale)
            o_ref[:, h * HEAD:(h + 1) * HEAD] = _dot(p.astype(BF16), vh).astype(BF16)

    return pl.pallas_call(
        body, name=name, out_shape=jax.ShapeDtypeStruct((s, xw), BF16), grid=(s // tq,),
        in_specs=[pl.BlockSpec((tq, xw), lambda i: (i, 0)), pl.BlockSpec((m, 2 * xw), lambda i: (0, 0))],
        out_specs=pl.BlockSpec((tq, xw), lambda i: (i, 0)), compiler_params=_params("parallel"),
    )(q, kv)


def _xattn_bwd(q, kv, do, name):
    s, xw = q.shape
    m = kv.shape[0]
    nh = xw // HEAD
    tq = _tile(s, (256, 128, 64, 32, 16))
    scale = HEAD ** -0.5

    def body(q_ref, kv_ref, do_ref, dq_ref, dkv_ref):
        @pl.when(pl.program_id(0) == 0)
        def _():
            dkv_ref[...] = jnp.zeros_like(dkv_ref)

        for h in range(nh):
            ks = slice(h * HEAD, (h + 1) * HEAD)
            vs = slice(xw + h * HEAD, xw + (h + 1) * HEAD)
            qh = q_ref[:, ks].astype(BF16)
            kh = kv_ref[:, ks].astype(BF16)
            vh = kv_ref[:, vs].astype(BF16)
            doh = do_ref[:, ks].astype(BF16)
            p = _xattn_probs(qh, kh, scale)
            dkv_ref[:, vs] += _dot_tn(p.astype(BF16), doh)
            dp = _dot_nt(doh, vh)
            ds = (p * (dp - jnp.sum(dp * p, axis=-1, keepdims=True)) * scale).astype(BF16)
            dq_ref[:, ks] = _dot(ds, kh).astype(BF16)
            dkv_ref[:, ks] += _dot_tn(ds, qh)

    row = pl.BlockSpec((tq, xw), lambda i: (i, 0))
    full = pl.BlockSpec((m, 2 * xw), lambda i: (0, 0))
    return pl.pallas_call(
        body, name=name,
        out_shape=(jax.ShapeDtypeStruct((s, xw), BF16), jax.ShapeDtypeStruct((m, 2 * xw), F32)), grid=(s // tq,),
        in_specs=[row, full, row], out_specs=(row, full), compiler_params=_params("arbitrary"),
    )(q, kv, do)


def _adamw(w, g, m, v, name):
    rows, cols = w.shape
    tr = _tile(rows, (512, 256, 128, 64, 32, 16, 8)) if rows % 8 == 0 else rows
    bc1 = 1.0 - ADAM_B1 ** ADAM_STEP
    bc2 = 1.0 - ADAM_B2 ** ADAM_STEP

    def body(w_ref, g_ref, m_ref, v_ref, d_ref, nm_ref, nv_ref):
        gv = g_ref[...]
        nm = ADAM_B1 * m_ref[...] + (1.0 - ADAM_B1) * gv
        nv = ADAM_B2 * v_ref[...] + (1.0 - ADAM_B2) * (gv * gv)
        nm_ref[...] = nm
        nv_ref[...] = nv
        d_ref[...] = -ADAM_LR * ((nm / bc1) / (jnp.sqrt(nv / bc2) + ADAM_EPS) + ADAM_WD * w_ref[...])

    spec = pl.BlockSpec((tr, cols), lambda i: (i, 0))
    sh = jax.ShapeDtypeStruct((rows, cols), F32)
    return pl.pallas_call(
        body, name=name, out_shape=(sh, sh, sh), grid=(rows // tr,), in_specs=[spec] * 4,
        out_specs=(spec, spec, spec), compiler_params=_params("parallel"),
    )(w, g, m, v)


WEIGHTS = ['mix_norm_g', 'w_in', 'conv_dw_w', 'conv_dw_b', 'conv_ln_g', 'conv_ln_b', 'conv_pw_w', 'lru_conv_w',
           'lru_conv_b', 'lru_wa', 'lru_ba', 'lru_wx', 'lru_bx', 'lru_lambda', 'out_norm_conv', 'out_norm_attn',
           'out_norm_lru', 'w_out', 'xattn_norm_g', 'mem_norm_g', 'xattn_wq', 'xattn_wkv', 'xattn_wo',
           'final_norm_g']
BIG_SHARDED = {'w_in': 2, 'conv_pw_w': 1, 'w_out': 1, 'xattn_wq': 1, 'xattn_wkv': 1, 'xattn_wo': 2}
SMALL_SHARDED = {'conv_dw_w': 2, 'lru_conv_w': 2}


def _layer_fwd(x, mem, p, l, first, rest):
    row = lambda name: p[name][l][None, :]
    c = p['conv_dw_b'].shape[1]
    heads = 2 * c // HEAD
    h = _rms_fwd(x, row('mix_norm_g'), "rms_mix")
    w_in, tied = first(h)
    u = _mm(h, w_in, dep=tied, name="in_proj")
    q_blk = 3 * c // HEAD
    y_attn, tot = _sb_fwd(u, heads, q_blk, q_blk + heads, q_blk + 2 * heads, "sb_fwd")
    fw = dict(rest(y_attn), w_in=w_in)
    wpad = jnp.pad(fw['conv_dw_w'][l], ((0, 1), (0, 0)))
    cw = jnp.pad(fw['lru_conv_w'][l], ((0, 4), (0, 0)))
    wa, wx = p['lru_wa'][l].astype(BF16), p['lru_wx'][l].astype(BF16)
    y_lru = _lru_fwd(u, 11, cw, row('lru_conv_b'), wa, row('lru_ba'), wx, row('lru_bx'), row('lru_lambda'), "lru_fwd")
    y_conv = _conv_fwd(u, wpad, row('conv_dw_b'), row('conv_ln_g'), row('conv_ln_b'), fw['conv_pw_w'][None], 0,
                       "conv_fwd")
    yc = _outgate_fwd(y_conv, y_attn, y_lru, u, row('out_norm_conv'), row('out_norm_attn'), row('out_norm_lru'),
                      "outgate_fwd")
    x1 = _mm(yc, fw['w_out'], add=x, name="out_proj")
    h2 = _rms_fwd(x1, row('xattn_norm_g'), "rms_xattn")
    memn = _rms_fwd(mem, row('mem_norm_g'), "rms_mem")
    q2 = _mm(h2, fw['xattn_wq'], name="xq_proj")
    kv = _mm(memn, fw['xattn_wkv'], name="xkv_proj")
    o2 = _xattn_fwd(q2, kv, "xattn_fwd")
    x2 = _mm(o2, fw['xattn_wo'], add=x1, name="xo_proj")
    saved = dict(x=x, h=h, u=u, wpad=wpad, cw=cw, wa=wa, wx=wx, y_conv=y_conv, y_attn=y_attn, tot=tot, y_lru=y_lru,
                 yc=yc, x1=x1, h2=h2, memn=memn, q2=q2, kv=kv, o2=o2, fw=fw)
    return x2, saved


def _layer_bwd(dx2, mem, p, l, sv, rest_ready, w_in_ready):
    row = lambda name: p[name][l][None, :]
    c = p['conv_dw_b'].shape[1]
    heads = 2 * c // HEAD
    fw = sv['fw']
    g, big = {}, {}
    big['xattn_wo'] = _mm(sv['o2'], dx2, ta=True, out_dtype=BF16, name="d_wo")
    do2 = _mm(dx2, fw['xattn_wo'], tb=True, name="d_o2")
    dq2, dkv = _xattn_bwd(sv['q2'], sv['kv'], do2, "xattn_bwd")
    big['xattn_wq'] = _mm(sv['h2'], dq2, ta=True, out_dtype=BF16, name="d_wq")
    dh2 = _mm(dq2, fw['xattn_wq'], tb=True, name="d_h2")
    big['xattn_wkv'] = _mm(sv['memn'], dkv, ta=True, out_dtype=BF16, name="d_wkv")
    dmemn = _mm(dkv, fw['xattn_wkv'], tb=True, name="d_memn")
    _, g['mem_norm_g'] = _rms_bwd(mem, row('mem_norm_g'), dmemn, None, "rms_mem_bwd")
    dx1, g['xattn_norm_g'] = _rms_bwd(sv['x1'], row('xattn_norm_g'), dh2, dx2, "rms_xattn_bwd")
    big['w_out'] = _mm(sv['yc'], dx1, ta=True, out_dtype=BF16, name="d_wout")
    dyc = _mm(dx1, fw['w_out'], tb=True, name="d_yc")
    u = sv['u']
    d_yconv, d_yattn, d_ylru, dgc, dga, dgl, dn = _outgate_bwd(
        dyc, sv['y_conv'], sv['y_attn'], sv['y_lru'], u, row('out_norm_conv'), row('out_norm_attn'),
        row('out_norm_lru'), "outgate_bwd")
    g['out_norm_conv'], g['out_norm_attn'], g['out_norm_lru'] = dn[0, 0:c], dn[0, c:3 * c], dn[0, 3 * c:4 * c]
    dd, gl, dpw, cvec = _conv_bwd_post(u, d_yconv, sv['wpad'], row('conv_dw_b'), row('conv_ln_g'),
                                       row('conv_ln_b'), fw['conv_pw_w'][None], 0, "conv_bwd_post")
    big['conv_pw_w'] = dpw.astype(BF16)
    g['conv_ln_g'], g['conv_ln_b'], g['conv_dw_b'] = cvec[0], cvec[1], cvec[2]
    dval, dglu, ddw = _conv_bwd_dw(u, dd, gl, sv['wpad'], "conv_bwd_dw")
    g['conv_dw_w'] = ddw[0:31]
    dxr, g['lru_wa'], g['lru_wx'], lvec = _lru_bwd(
        u, 11, sv['y_lru'], d_ylru, sv['cw'], row('lru_conv_b'), sv['wa'], row('lru_ba'), sv['wx'], row('lru_bx'),
        row('lru_lambda'), "lru_bwd")
    g['lru_ba'], g['lru_bx'], g['lru_lambda'], g['lru_conv_b'] = lvec[0], lvec[1], lvec[2], lvec[3]
    g['lru_conv_w'] = lvec[4:8]
    tied = rest_ready(big, g)
    q_blk = 3 * c // HEAD
    dq, dk, dv = _sb_bwd(u, sv['tot'], d_yattn, heads, q_blk, q_blk + heads, q_blk + 2 * heads, tied, "sb_bwd")
    du = jnp.concatenate([dval, dglu, dgc, dq.astype(BF16), dk.astype(BF16), dv.astype(BF16), dga, dxr, dgl], axis=1)
    tied = w_in_ready(_mm(sv['h'], du, ta=True, out_dtype=BF16, name="d_win"))
    dh = _mm(du, fw['w_in'], tb=True, dep=tied, name="d_h")
    dx0, g['mix_norm_g'] = _rms_bwd(sv['x'], row('mix_norm_g'), dh, dx1, "rms_mix_bwd")
    return dx0, g


def kernel(x, mem, mix_norm_g, w_in, conv_dw_w, conv_dw_b, conv_ln_g, conv_ln_b, conv_pw_w, lru_conv_w, lru_conv_b, lru_wa, lru_ba, lru_wx, lru_bx, lru_lambda, out_norm_conv, out_norm_attn, out_norm_lru, w_out, xattn_norm_g, mem_norm_g, xattn_wq, xattn_wkv, xattn_wo, final_norm_g, loss_target, m_mix_norm_g, m_w_in, m_conv_dw_w, m_conv_dw_b, m_conv_ln_g, m_conv_ln_b, m_conv_pw_w, m_lru_conv_w, m_lru_conv_b, m_lru_wa, m_lru_ba, m_lru_wx, m_lru_bx, m_lru_lambda, m_out_norm_conv, m_out_norm_attn, m_out_norm_lru, m_w_out, m_xattn_norm_g, m_mem_norm_g, m_xattn_wq, m_xattn_wkv, m_xattn_wo, m_final_norm_g, v_mix_norm_g, v_w_in, v_conv_dw_w, v_conv_dw_b, v_conv_ln_g, v_conv_ln_b, v_conv_pw_w, v_lru_conv_w, v_lru_conv_b, v_lru_wa, v_lru_ba, v_lru_wx, v_lru_bx, v_lru_lambda, v_out_norm_conv, v_out_norm_attn, v_out_norm_lru, v_w_out, v_xattn_norm_g, v_mem_norm_g, v_xattn_wq, v_xattn_wkv, v_xattn_wo, v_final_norm_g):
    args = locals()
    w = {n: args[n] for n in WEIGHTS}
    mom = {n: args["m_" + n] for n in WEIGHTS}
    var = {n: args["v_" + n] for n in WEIGHTS}
    depth = w_in.shape[0]
    c = conv_dw_b.shape[1]
    assert out_norm_attn.shape[1] == 2 * c and lru_lambda.shape[1] == c and w_in.shape[2] * N_DEV == 13 * c
    assert c % HEAD == 0 and x.shape[0] == 1 and mem.shape[0] == 1
    xs, mems, tgt = x[0], mem[0], loss_target[0]
    me = 4 * lax.axis_index("x") + 2 * lax.axis_index("y") + lax.axis_index("c")

    assert depth == 2 and (2 * w_in.shape[2]) % LANE == 0 and w_in.shape[2] % LANE in (0, LANE // 2)
    my_c = lax.axis_index("c")
    big = list(BIG_SHARDED)
    dev = [n for n in big if n != 'w_in']
    my_x, my_y = lax.axis_index("x"), lax.axis_index("y")
    small = list(SMALL_SHARDED)
    p = dict(w)
    blk_width = w_in.shape[2]
    w_in_blk = jnp.pad(w_in.astype(BF16), ((0, 0), (0, 0), (0, -blk_width % LANE)))
    w_in0_blocks = _all_gather(w_in_blk[0], "gather_w_in0")
    w_in0 = _join_cols(w_in0_blocks, blk_width, "join_w_in")
    taps = _pack([w[n] for n in small], F32)[None]

    def gather_group(names, l, stage, dep, tag, with_taps=False):
        srcs, shapes, plans = [], [], []
        if with_taps:
            srcs, shapes, plans = [taps], [(N_DEV,) + taps.shape[1:]], [_gather_plan(0, 1, stage)]
        for n in names:
            if n == 'w_in':
                srcs.append(w_in_blk[l][None])
                shapes.append((N_DEV,) + w_in_blk.shape[1:])
                plans.append(_gather_plan(0, 1, stage))
            else:
                axis = BIG_SHARDED[n] - 1
                blk = w[n][l].astype(BF16)
                width = blk.shape[axis]
                srcs.append(blk)
                shapes.append(blk.shape[:axis] + (N_DEV * width,) + blk.shape[axis + 1:])
                plans.append(_gather_plan(axis, width, stage))
        return _split_start(srcs, shapes, plans, dep, "gather_start_" + tag)

    def forward_group(names, landed, dep, tag):
        plans = [_forward_plan(0, 1) if n == 'w_in' else _forward_plan(BIG_SHARDED[n] - 1, w[n].shape[BIG_SHARDED[n]])
                 for n in names]
        return _split_start(landed, [None] * len(names), plans, dep, "gather_start_" + tag)

    g0, tied = gather_group(dev, 0, "direct", w_in0_blocks, "rest0", with_taps=True)
    g1, tied = gather_group(big, 1, "first", tied, "layer1")
    layer1, full_taps = {}, {}

    def first_of(l):
        def first(after):
            if l == 0:
                return w_in0, tied
            layer1.update(zip(big, _split_wait(layer1.pop('forwarding'), after, "gather_wait_layer1_passed")))
            layer1['w_in'] = _join_cols(layer1['w_in'], blk_width, "join_w_in")
            return layer1['w_in'], None
        return first

    def rest_of(l):
        def rest(after):
            if l == 0:
                landed = _split_wait(g0, after, "gather_wait_rest0")
                full_taps.update({n: _join_blocks(blk, SMALL_SHARDED[n]) for n, blk in
                                  zip(small, _unpack(landed[0], [w[n].shape for n in small], lead=N_DEV))})
                got = dict(zip(dev, landed[1:]), **full_taps)
                landed = _split_wait(g1, after, "gather_wait_layer1")
                layer1['forwarding'], token = forward_group(big, landed, landed[0], "layer1_passed")
                got['conv_pw_w'] = got['conv_pw_w'] + token[0:1, 0:1].astype(BF16)
                return got
            return dict({n: layer1[n] for n in dev}, **full_taps)
        return rest

    saved = []
    act = xs
    for l in range(depth):
        act, sv = _layer_fwd(act, mems, p, l, first_of(l), rest_of(l))
        saved.append(sv)
    loss_part, dact, d_final = _loss_bwd(act, final_norm_g[None, :], tgt, "loss_bwd")

    def grad_window(n):
        axis = BIG_SHARDED[n] - 1
        blk = w[n].shape[axis + 1]
        pad = blk % LANE if axis == 1 else 0
        return axis, blk + pad, lambda px, py, pc: blk * (4 * px + 2 * py + pc) - pad * pc

    scattering = {}

    def scatter_group(names, arrs, l, tag):
        shapes, plans = [], []
        for n, g in zip(names, arrs):
            axis, width, start = grad_window(n)
            shapes.append((N_DEV,) + g.shape[:axis] + (width,) + g.shape[axis + 1:])
            plans.append(_scatter_plan(axis, width, start))
        scattering[(l, tag)], token = _split_start(arrs, shapes, plans, arrs[0], f"scatter_start_{tag}{l}")
        return token

    rest = [n for n in WEIGHTS if n not in BIG_SHARDED]
    early = [n for n in rest if n != 'mix_norm_g']
    small_pending, partial = {}, {}

    def start_small(names, dep, tag):
        vec = _pack([partial[n] for n in names], F32)
        small_pending[tag], token = _split_start([vec[None]], [(N_DEV,) + vec.shape], [_gather_plan(0, 1, "direct")],
                                                 dep, "small_grads_start_" + tag)
        return token

    layer_grads = [None] * depth

    def rest_ready(big_grads, g_small, l):
        token = scatter_group(dev, [big_grads[n] for n in dev], l, "rest")
        if l == 0:
            partial.update({n: jnp.stack([g_small[n], layer_grads[1][n]]) for n in early if n != 'final_norm_g'})
            partial.update(final_norm_g=d_final[0], loss=loss_part)
            token = start_small(early + ['loss'], token, "early")
        return token

    for l in reversed(range(depth)):
        dact, layer_grads[l] = _layer_bwd(
            dact, mems, p, l, saved[l], functools.partial(rest_ready, l=l),
            lambda g_w_in, l=l: scatter_group(['w_in'], [g_w_in], l, "w_in"))
    grad_x = dact[None]
    partial['mix_norm_g'] = jnp.stack([layer_grads[l]['mix_norm_g'] for l in range(depth)])
    tied = start_small(['mix_norm_g'], partial['mix_norm_g'], "late")

    grads, delta, new_m, new_v = {}, {}, {}, {}

    def update(n, received):
        summed = jnp.stack([_sum_blocks(received[l], "sum_" + n) for l in range(depth)])
        width = w[n].shape[-1]
        if summed.shape[-1] != width:
            summed = jnp.where(my_c == 0, summed[..., :width], summed[..., summed.shape[-1] - width:])
        grads[n] = summed
        shape = w[n].shape
        two_d = lambda a: a.reshape(-1, shape[-1])
        d, nm, nv = _adamw(two_d(w[n]), two_d(summed), two_d(mom[n]), two_d(var[n]), "adamw_" + n)
        delta[n], new_m[n], new_v[n] = d.reshape(shape), nm.reshape(shape), nv.reshape(shape)

    landed = [_split_wait(scattering[(l, "rest")], tied, f"scatter_wait_rest{l}") for l in range(depth)]
    for t, n in enumerate(dev):
        update(n, [landed[l][t] for l in range(depth)])
    def update_small(names, tag, after):
        sent = names + ['loss'] if tag == "early" else names
        total = _sum_blocks(_split_wait(small_pending[tag], after, "small_grads_wait_" + tag)[0], "sum_small_" + tag)
        grads.update(zip(sent, _unpack(total, [partial[n].shape for n in sent])))
        for n in names:
            if n in SMALL_SHARDED:
                width = w[n].shape[2]
                grads[n] = lax.dynamic_slice_in_dim(grads[n], me * width, width, axis=2)
        packed = [_pack([src[n] for n in names], F32) for src in (w, grads, mom, var)]
        outs = _adamw(*packed, "adamw_small_" + tag)
        for dst, o in zip((delta, new_m, new_v), outs):
            dst.update(dict(zip(names, _unpack(o, [w[n].shape for n in names]))))
        return outs[0]

    after = update_small(early, "early", sum(delta[n].reshape(-1)[:1] for n in dev))
    loss = grads.pop('loss')[0, 0]
    update('w_in', [_split_wait(scattering[(l, "w_in")], after, f"scatter_wait_w_in{l}")[0] for l in range(depth)])
    update_small(['mix_norm_g'], "late", delta['w_in'])

    return (loss, grad_x, *[grads[n] for n in WEIGHTS], *[delta[n] for n in WEIGHTS],
            *[new_m[n] for n in WEIGHTS], *[new_v[n] for n in WEIGHTS])
```

```python
import functools
import math

import jax
import jax.numpy as jnp
from jax import lax
from jax.experimental import pallas as pl
from jax.experimental.pallas import tpu as pltpu

F32 = jnp.float32
BF16 = jnp.bfloat16

N_DEV = 8
LANE = 128
HEAD = 128
VMEM_LIMIT = 56 * 1024 * 1024
PACK_COLS = 512
RMS_EPS = 1e-6
LN_EPS = 1e-5
LRU_C = 8.0
CONV_HALO = 32
LRU_HALO = 8

ADAM_LR, ADAM_B1, ADAM_B2, ADAM_EPS, ADAM_WD, ADAM_STEP = 0.001, 0.9, 0.999, 1e-08, 0.01, 10

MESH = pl.DeviceIdType.MESH


def _tile(n, cands):
    for c in cands:
        if n % c == 0:
            return c
    raise ValueError(f"no tile of {cands} divides {n}")


def _params(*sem):
    return pltpu.CompilerParams(dimension_semantics=sem, vmem_limit_bytes=VMEM_LIMIT)


def _dot(a, b):
    return lax.dot_general(a, b, (((1,), (0,)), ((), ())), preferred_element_type=F32)


def _dot_nt(a, b):
    return lax.dot_general(a, b, (((1,), (1,)), ((), ())), preferred_element_type=F32)


def _dot_tn(a, b):
    return lax.dot_general(a, b, (((0,), (0,)), ((), ())), preferred_element_type=F32)


def _sigmoid(x):
    return 1.0 / (1.0 + jnp.exp(-x))


def _expm1(x):
    series = x * (1.0 + x * (0.5 + x * (1.0 / 6.0 + x * (1.0 / 24.0))))
    return jnp.where(jnp.abs(x) < 0.05, series, jnp.exp(x) - 1.0)


def _my_place():
    return lax.axis_index("x"), lax.axis_index("y"), lax.axis_index("c")


def _flip(v, d):
    return 1 - v if d else v


def _window(ref, axis, start, size):
    return ref.at[tuple(pl.ds(start, size) if a == axis else pl.ds(0, ref.shape[a]) for a in range(len(ref.shape)))]


def _all_gather(x2d, name):
    rows, cols = x2d.shape

    def body(x_ref, out_ref, send_sems, recv_sems, local_sem):
        x, y, c = _my_place()
        me, sibling = (x, y, c), (x, y, 1 - c)
        chips = [(1 - x, y), (x, 1 - y), (1 - x, 1 - y)]

        def blk(px, py, pc):
            return out_ref.at[4 * px + 2 * py + pc]

        def copy(k, block, to, src=None):
            return pltpu.make_async_remote_copy(
                src_ref=blk(*block) if src is None else src, dst_ref=blk(*block),
                send_sem=send_sems.at[k], recv_sem=recv_sems.at[k], device_id=to, device_id_type=MESH)

        mine = pltpu.make_async_copy(x_ref, blk(*me), local_sem)
        mine.start()
        first = [copy(0, me, sibling, src=x_ref)]
        first += [copy(1 + j, me, (*chip, c), src=x_ref) for j, chip in enumerate(chips)]
        for cp in first:
            cp.start()
        passed = [copy(4 + j, (*chip, c), sibling) for j, chip in enumerate(chips)]
        for j, chip in enumerate(chips):
            copy(1 + j, (*chip, c), me).wait_recv()
            passed[j].start()
        copy(0, sibling, me).wait_recv()
        for j, chip in enumerate(chips):
            copy(4 + j, (*chip, 1 - c), me).wait_recv()
        for cp in first + passed:
            cp.wait_send()
        mine.wait()

    return pl.pallas_call(
        body, name=name,
        out_shape=jax.ShapeDtypeStruct((N_DEV, rows, cols), x2d.dtype),
        in_specs=[pl.BlockSpec(memory_space=pl.ANY)],
        out_specs=pl.BlockSpec(memory_space=pl.ANY),
        scratch_shapes=[pltpu.SemaphoreType.DMA((7,)), pltpu.SemaphoreType.DMA((7,)), pltpu.SemaphoreType.DMA],
    )(x2d)


HBM_SPEC = pl.BlockSpec(memory_space=pltpu.HBM)
SEM_SPEC = pl.BlockSpec(memory_space=pltpu.SEMAPHORE)
SPLIT_COPY = pltpu.SideEffectType.DATAFLOW_SIDE_EFFECTING


class _Plan:
    def __init__(self, copies, own, total):
        self.copies, self.own, self.total = copies, own, total


def _scatter_plan(axis, width, start):
    def win(src, px, py, pc):
        return _window(src, axis, pl.multiple_of(start(px, py, pc), math.gcd(width, 1024)), width)

    def copies(src, zone, x, y, c):
        out = []
        for k in range(1, N_DEV):
            px, py, pc = _flip(x, k & 4), _flip(y, k & 2), _flip(c, k & 1)
            out.append((win(src, px, py, pc), zone.at[4 * x + 2 * y + c], (px, py, pc)))
        return out

    return _Plan(copies, lambda src, zone, x, y, c: (win(src, x, y, c), zone.at[4 * x + 2 * y + c]),
                 lambda zone: zone.at[pl.ds(0, N_DEV - 1)])


def _gather_plan(axis, width, stage):
    flips = range(1, N_DEV) if stage == "direct" else (1, 2, 4, 6)

    def mine(zone, x, y, c):
        return _window(zone, axis, width * (4 * x + 2 * y + c), width)

    def copies(src, zone, x, y, c):
        return [(src, mine(zone, x, y, c), (_flip(x, k & 4), _flip(y, k & 2), _flip(c, k & 1))) for k in flips]

    return _Plan(copies, lambda src, zone, x, y, c: (src, mine(zone, x, y, c)),
                 lambda zone: _window(zone, axis, 0, len(flips) * width))


def _forward_plan(axis, width):
    def copies(src, zone, x, y, c):
        out = []
        for px, py in ((1 - x, y), (x, 1 - y), (1 - x, 1 - y)):
            win = _window(zone, axis, width * (4 * px + 2 * py + c), width)
            out.append((win, win, (x, y, 1 - c)))
        return out

    return _Plan(copies, lambda src, zone, x, y, c: None, lambda zone: _window(zone, axis, 0, 3 * width))


def _split_start(srcs, zone_shapes, plans, dep, name):
    n = len(srcs)
    zones = [lax.empty(shape, a.dtype) for shape, a in zip(zone_shapes, srcs) if shape is not None]
    m = len(zones)
    zone_of = [None if shape is None else sum(s is not None for s in zone_shapes[:t])
               for t, shape in enumerate(zone_shapes)]

    def body(*refs):
        ins, fresh = refs[:n], refs[n:n + m]
        sems, token = refs[n + m + 1:n + m + 1 + 3 * n], refs[-1]
        send_sems, recv_sems, local_sems = sems[:n], sems[n:2 * n], sems[2 * n:]
        x, y, c = _my_place()
        for t in range(n):
            land = ins[t] if zone_of[t] is None else fresh[zone_of[t]]
            own = plans[t].own(ins[t], land, x, y, c)
            if own is not None:
                pltpu.make_async_copy(*own, local_sems[t]).start()
            for src, dst, target in plans[t].copies(ins[t], land, x, y, c):
                pltpu.make_async_remote_copy(src_ref=src, dst_ref=dst, send_sem=send_sems[t], recv_sem=recv_sems[t],
                                             device_id=target, device_id_type=MESH).start()
        token[...] = jnp.zeros_like(token)

    thru = [pltpu.HBM(a.shape, a.dtype) for a in list(srcs) + zones]
    outs = pl.pallas_call(
        body, name=name,
        out_shape=[pltpu.SemaphoreType.DMA(())] * (3 * n) + thru + [jax.ShapeDtypeStruct((8, LANE), F32)],
        in_specs=[HBM_SPEC] * (n + m) + [pl.BlockSpec(memory_space=pl.ANY)],
        out_specs=[SEM_SPEC] * (3 * n) + [HBM_SPEC] * (n + m) + [pl.BlockSpec(memory_space=pltpu.VMEM)],
        input_output_aliases={i: 3 * n + i for i in range(n + m)},
        compiler_params=pltpu.CompilerParams(has_side_effects=SPLIT_COPY),
    )(*[pltpu.with_memory_space_constraint(a, pltpu.HBM) for a in list(srcs) + zones], dep)
    return (outs[:3 * n], outs[3 * n:4 * n], outs[4 * n:4 * n + m], plans, zone_of), outs[-1]


def _split_wait(pending, after, name):
    sems, srcs, zones, plans, zone_of = pending
    n, m = len(srcs), len(zones)

    def body(*refs):
        ins, fresh, sems = refs[:n], refs[n:n + m], refs[n + m:n + m + 3 * n]
        send_sems, recv_sems, local_sems = sems[:n], sems[n:2 * n], sems[2 * n:]
        x, y, c = _my_place()
        for t in range(n):
            land = ins[t] if zone_of[t] is None else fresh[zone_of[t]]
            total = plans[t].total(land)
            done = pltpu.make_async_remote_copy(src_ref=total, dst_ref=total, send_sem=send_sems[t],
                                                recv_sem=recv_sems[t], device_id=(x, y, 1 - c), device_id_type=MESH)
            done.wait_send()
            done.wait_recv()
            own = plans[t].own(ins[t], land, x, y, c)
            if own is not None:
                pltpu.make_async_copy(*own, local_sems[t]).wait()

    thru = [pltpu.HBM(a.shape, a.dtype) for a in list(srcs) + list(zones)]
    outs = pl.pallas_call(
        body, name=name, out_shape=thru,
        in_specs=[HBM_SPEC] * (n + m) + [SEM_SPEC] * (3 * n) + [pl.BlockSpec(memory_space=pl.ANY)],
        out_specs=[HBM_SPEC] * (n + m), input_output_aliases={i: i for i in range(n + m)},
        compiler_params=pltpu.CompilerParams(has_side_effects=SPLIT_COPY),
    )(*srcs, *zones, *sems, after)
    return [outs[t] if zone_of[t] is None else outs[n + zone_of[t]] for t in range(n)]


def _join_cols(blocks, width, name):
    nb, rows, padded = blocks.shape
    tr = _tile(rows, (256, 128, 64, 32, 16))

    def body(x_ref, o_ref):
        for b in range(nb):
            o_ref[:, b * width:(b + 1) * width] = x_ref[b, :, 0:width]

    return pl.pallas_call(
        body, name=name, out_shape=jax.ShapeDtypeStruct((rows, nb * width), blocks.dtype), grid=(rows // tr,),
        in_specs=[pl.BlockSpec((nb, tr, padded), lambda i: (0, i, 0))],
        out_specs=pl.BlockSpec((tr, nb * width), lambda i: (i, 0)), compiler_params=_params("parallel"),
    )(blocks)


def _sum_blocks(x3d, name):
    n, rows, cols = x3d.shape
    tr = _tile(rows, (512, 256, 128, 64, 32, 16))

    def body(x_ref, o_ref):
        acc = x_ref[0].astype(F32)
        for j in range(1, n):
            acc = acc + x_ref[j].astype(F32)
        o_ref[...] = acc

    return pl.pallas_call(
        body, name=name, out_shape=jax.ShapeDtypeStruct((rows, cols), F32), grid=(rows // tr,),
        in_specs=[pl.BlockSpec((n, tr, cols), lambda i: (0, i, 0))],
        out_specs=pl.BlockSpec((tr, cols), lambda i: (i, 0)),
        compiler_params=_params("parallel"),
    )(x3d)


def _pack(arrs, dtype, lead=None):
    if lead is None:
        flat = jnp.concatenate([a.reshape(-1).astype(dtype) for a in arrs])
        n = flat.shape[0]
        total = -(-n // (16 * PACK_COLS)) * (16 * PACK_COLS)
        return jnp.pad(flat, (0, total - n)).reshape(-1, PACK_COLS)
    flat = jnp.concatenate([a.reshape(lead, -1).astype(dtype) for a in arrs], axis=1)
    n = flat.shape[1]
    total = -(-n // (16 * PACK_COLS)) * (16 * PACK_COLS)
    return jnp.pad(flat, ((0, 0), (0, total - n))).reshape(lead, -1, PACK_COLS)


def _unpack(packed, shapes, lead=None):
    out, off = [], 0
    if lead is None:
        flat = packed.reshape(-1)
        for s in shapes:
            n = math.prod(s)
            out.append(flat[off:off + n].reshape(s))
            off += n
        return out
    flat = packed.reshape(lead, -1)
    for s in shapes:
        n = math.prod(s)
        out.append(flat[:, off:off + n].reshape((lead,) + tuple(s)))
        off += n
    return out


def _join_blocks(g, axis):
    g = jnp.moveaxis(g, 0, axis)
    s = g.shape
    return g.reshape(s[:axis] + (s[axis] * s[axis + 1],) + s[axis + 2:])


def _mm_tiles(m, n, kdim, a_bytes):
    tk = kdim if kdim <= 2048 else _tile(kdim, (2048, 1664, 1024, 832, 512, 416, 256, 128))
    tm = _tile(m, (1024, 512, 256, 128, 64, 32, 16))
    tn = _tile(n, (1024, 512, 256, 128))

    def vmem(tm, tn):
        return 2 * tm * tk * a_bytes + 2 * tn * tk * 2 + 3 * tm * tn * 4

    while vmem(tm, tn) > VMEM_LIMIT * 3 // 4 and tn > 128 and tn % 256 == 0:
        tn //= 2
    while vmem(tm, tn) > VMEM_LIMIT * 3 // 4 and tm > 128 and tm % 256 == 0:
        tm //= 2
    return tm, tn, tk


def _mm(a, b, *, ta=False, tb=False, bl=None, out_dtype=F32, add=None, dep=None, name):
    if ta:
        kdim, m = a.shape
    else:
        m, kdim = a.shape
    bshape = b.shape if bl is None else b.shape[1:]
    n = bshape[0] if tb else bshape[1]
    tm, tn, tk = _mm_tiles(m, n, kdim, a.dtype.itemsize)
    nk = kdim // tk
    a_spec = pl.BlockSpec((tk, tm), lambda i, j, k: (k, i)) if ta else pl.BlockSpec((tm, tk), lambda i, j, k: (i, k))
    b_blk, b_idx = ((tn, tk), lambda i, j, k: (j, k)) if tb else ((tk, tn), lambda i, j, k: (k, j))
    if bl is None:
        b_spec = pl.BlockSpec(b_blk, b_idx)
    else:
        b_spec = pl.BlockSpec((None,) + b_blk, lambda i, j, k: (bl,) + b_idx(i, j, k))
    o_spec = pl.BlockSpec((tm, tn), lambda i, j, k: (i, j))
    dims = (((0 if ta else 1,), (1 if tb else 0,)), ((), ()))

    n_in = 2 + (add is not None) + (dep is not None)

    def body(*refs):
        a_ref, b_ref = refs[:2]
        add_ref = refs[2] if add is not None else None
        o_ref = refs[n_in]

        def finish(r):
            if add is not None:
                r = r + add_ref[...]
            o_ref[...] = r.astype(out_dtype)

        part = lax.dot_general(a_ref[...].astype(BF16), b_ref[...].astype(BF16), dims, preferred_element_type=F32)
        if nk == 1:
            finish(part)
            return
        acc_ref = refs[-1]
        k = pl.program_id(2)

        @pl.when(k == 0)
        def _():
            acc_ref[...] = part

        @pl.when(k > 0)
        def _():
            acc_ref[...] += part

        @pl.when(k == nk - 1)
        def _():
            finish(acc_ref[...])

    ins, specs = [a, b], [a_spec, b_spec]
    if add is not None:
        ins.append(add)
        specs.append(o_spec)
    if dep is not None:
        ins.append(dep)
        specs.append(pl.BlockSpec(memory_space=pl.ANY))
    return pl.pallas_call(
        body, name=name, out_shape=jax.ShapeDtypeStruct((m, n), out_dtype), grid=(m // tm, n // tn, nk),
        in_specs=specs, out_specs=o_spec, scratch_shapes=[pltpu.VMEM((tm, tn), F32)] if nk > 1 else [],
        compiler_params=_params("parallel", "parallel", "arbitrary"),
    )(*ins)


def _rms_fwd(x, g, name):
    s, d = x.shape
    tr = _tile(s, (256, 128, 64, 32, 16))

    def body(x_ref, g_ref, o_ref):
        xv = x_ref[...]
        r = lax.rsqrt(jnp.mean(xv * xv, axis=-1, keepdims=True) + RMS_EPS)
        o_ref[...] = (xv * r * g_ref[...]).astype(BF16)

    return pl.pallas_call(
        body, name=name, out_shape=jax.ShapeDtypeStruct((s, d), BF16), grid=(s // tr,),
        in_specs=[pl.BlockSpec((tr, d), lambda i: (i, 0)), pl.BlockSpec((1, d), lambda i: (0, 0))],
        out_specs=pl.BlockSpec((tr, d), lambda i: (i, 0)), compiler_params=_params("parallel"),
    )(x, g)


def _rms_bwd(x, g, dh, resid, name):
    s, d = x.shape
    tr = _tile(s, (256, 128, 64, 32, 16))

    def body(*refs):
        if resid is None:
            x_ref, g_ref, dh_ref, dx_ref, dg_ref = refs
        else:
            x_ref, g_ref, dh_ref, res_ref, dx_ref, dg_ref = refs

        @pl.when(pl.program_id(0) == 0)
        def _():
            dg_ref[...] = jnp.zeros_like(dg_ref)

        xv = x_ref[...]
        r = lax.rsqrt(jnp.mean(xv * xv, axis=-1, keepdims=True) + RMS_EPS)
        xh = xv * r
        dhv = dh_ref[...]
        dg_ref[0:1, :] += jnp.sum(dhv * xh, axis=0, keepdims=True)
        dyn = dhv * g_ref[...]
        dx = r * (dyn - xh * jnp.mean(dyn * xh, axis=-1, keepdims=True))
        if resid is not None:
            dx = dx + res_ref[...]
        dx_ref[...] = dx

    row = pl.BlockSpec((tr, d), lambda i: (i, 0))
    ins = [x, g, dh] + ([] if resid is None else [resid])
    specs = [row, pl.BlockSpec((1, d), lambda i: (0, 0)), row] + ([] if resid is None else [row])
    dx, dg = pl.pallas_call(
        body, name=name,
        out_shape=(jax.ShapeDtypeStruct((s, d), F32), jax.ShapeDtypeStruct((8, d), F32)), grid=(s // tr,),
        in_specs=specs, out_specs=(row, pl.BlockSpec((8, d), lambda i: (0, 0))),
        compiler_params=_params("arbitrary"),
    )(*ins)
    return dx, dg[0]


def _loss_bwd(x, g, tgt, name):
    s, d = x.shape
    tr = _tile(s, (256, 128, 64, 32, 16))

    def body(x_ref, g_ref, t_ref, dx_ref, dg_ref, loss_ref):
        @pl.when(pl.program_id(0) == 0)
        def _():
            dg_ref[...] = jnp.zeros_like(dg_ref)
            loss_ref[...] = jnp.zeros_like(loss_ref)

        xv = x_ref[...]
        r = lax.rsqrt(jnp.mean(xv * xv, axis=-1, keepdims=True) + RMS_EPS)
        xh = xv * r
        e = xh * g_ref[...] - t_ref[...]
        per_tok = jnp.mean(e * e, axis=-1, keepdims=True)
        loss_ref[...] += 0.5 * jnp.sum(per_tok, axis=0, keepdims=True)
        dy = e * (1.0 / d)
        dg_ref[0:1, :] += jnp.sum(dy * xh, axis=0, keepdims=True)
        dyn = dy * g_ref[...]
        dx_ref[...] = r * (dyn - xh * jnp.mean(dyn * xh, axis=-1, keepdims=True))

    row = pl.BlockSpec((tr, d), lambda i: (i, 0))
    dx, dg, loss = pl.pallas_call(
        body, name=name,
        out_shape=(jax.ShapeDtypeStruct((s, d), F32), jax.ShapeDtypeStruct((8, d), F32),
                   jax.ShapeDtypeStruct((8, LANE), F32)),
        grid=(s // tr,),
        in_specs=[row, pl.BlockSpec((1, d), lambda i: (0, 0)), row],
        out_specs=(row, pl.BlockSpec((8, d), lambda i: (0, 0)), pl.BlockSpec((8, LANE), lambda i: (0, 0))),
        compiler_params=_params("arbitrary"),
    )(x, g, tgt)
    return loss[0:1, 0:1], dx, dg[0:1]


def _conv_taps(gbuf, w_ref, tt, ntap, lo):
    acc = w_ref[0:1, :] * gbuf[pl.ds(lo, tt), :]
    for k in range(1, ntap):
        acc = acc + w_ref[k:k + 1, :] * gbuf[pl.ds(lo + k, tt), :]
    return acc


def _conv_time_tile(s):
    return _tile(s, (256, 128, 64, 32))


def _conv_fwd(u, wpad, dw_b, ln_g, ln_b, pw, l, name):
    s = u.shape[0]
    c = pw.shape[1]
    ntap = 31
    tt = _conv_time_tile(s)
    hb = tt // CONV_HALO

    def body(val_ref, glu_ref, valh_ref, gluh_ref, w_ref, b_ref, lg_ref, lb_ref, pw_ref, o_ref, gbuf):
        i = pl.program_id(0)
        glh = valh_ref[...] * _sigmoid(gluh_ref[...])
        gbuf[0:CONV_HALO, :] = jnp.where(i > 0, glh, 0.0)
        gbuf[CONV_HALO:CONV_HALO + tt, :] = val_ref[...] * _sigmoid(glu_ref[...])
        acc = _conv_taps(gbuf, w_ref, tt, ntap, CONV_HALO - (ntap - 1)) + b_ref[...]
        xc = acc - jnp.mean(acc, axis=-1, keepdims=True)
        rstd = lax.rsqrt(jnp.mean(xc * xc, axis=-1, keepdims=True) + LN_EPS)
        ln = xc * rstd * lg_ref[...] + lb_ref[...]
        sw = ln * _sigmoid(ln)
        o_ref[...] = _dot(sw.astype(BF16), pw_ref[...])

    vec = pl.BlockSpec((1, c), lambda i: (0, 0))
    return pl.pallas_call(
        body, name=name, out_shape=jax.ShapeDtypeStruct((s, c), F32), grid=(s // tt,),
        in_specs=[pl.BlockSpec((tt, c), lambda i: (i, 0)), pl.BlockSpec((tt, c), lambda i: (i, 1)),
                  pl.BlockSpec((CONV_HALO, c), lambda i: (jnp.maximum(i * hb - 1, 0), 0)),
                  pl.BlockSpec((CONV_HALO, c), lambda i: (jnp.maximum(i * hb - 1, 0), 1)),
                  pl.BlockSpec((32, c), lambda i: (0, 0)), vec, vec, vec,
                  pl.BlockSpec((None, c, c), lambda i: (l, 0, 0))],
        out_specs=pl.BlockSpec((tt, c), lambda i: (i, 0)),
        scratch_shapes=[pltpu.VMEM((CONV_HALO + tt, c), F32)],
        compiler_params=_params("parallel"),
    )(u, u, u, u, wpad, dw_b, ln_g, ln_b, pw)


def _conv_bwd_post(u, dyc, wpad, dw_b, ln_g, ln_b, pw, l, name):
    s = u.shape[0]
    c = pw.shape[1]
    ntap = 31
    tt = _conv_time_tile(s)
    hb = tt // CONV_HALO

    def body(val_ref, glu_ref, valh_ref, gluh_ref, dy_ref, w_ref, b_ref, lg_ref, lb_ref, pw_ref,
             dd_ref, gl_ref, dpw_ref, vec_ref, gbuf):
        i = pl.program_id(0)

        @pl.when(i == 0)
        def _():
            dpw_ref[...] = jnp.zeros_like(dpw_ref)
            vec_ref[...] = jnp.zeros_like(vec_ref)

        glh = valh_ref[...] * _sigmoid(gluh_ref[...])
        gbuf[0:CONV_HALO, :] = jnp.where(i > 0, glh, 0.0)
        gl = val_ref[...] * _sigmoid(glu_ref[...])
        gbuf[CONV_HALO:CONV_HALO + tt, :] = gl
        gl_ref[...] = gl
        acc = _conv_taps(gbuf, w_ref, tt, ntap, CONV_HALO - (ntap - 1)) + b_ref[...]
        xc = acc - jnp.mean(acc, axis=-1, keepdims=True)
        rstd = lax.rsqrt(jnp.mean(xc * xc, axis=-1, keepdims=True) + LN_EPS)
        xh = xc * rstd
        ln = xh * lg_ref[...] + lb_ref[...]
        sig = _sigmoid(ln)
        sw = ln * sig
        dyb = dy_ref[...].astype(BF16)
        dpw_ref[...] += _dot_tn(sw.astype(BF16), dyb)
        dsw = _dot_nt(dyb, pw_ref[...])
        dln = dsw * (sig * (1.0 + ln * (1.0 - sig)))
        vec_ref[0:1, :] += jnp.sum(dln * xh, axis=0, keepdims=True)
        vec_ref[1:2, :] += jnp.sum(dln, axis=0, keepdims=True)
        dxh = dln * lg_ref[...]
        dd = rstd * (dxh - jnp.mean(dxh, axis=-1, keepdims=True)
                     - xh * jnp.mean(dxh * xh, axis=-1, keepdims=True))
        vec_ref[2:3, :] += jnp.sum(dd, axis=0, keepdims=True)
        dd_ref[...] = dd

    vec = pl.BlockSpec((1, c), lambda i: (0, 0))
    tile = pl.BlockSpec((tt, c), lambda i: (i, 0))
    return pl.pallas_call(
        body, name=name,
        out_shape=(jax.ShapeDtypeStruct((s, c), F32), jax.ShapeDtypeStruct((s, c), F32),
                   jax.ShapeDtypeStruct((c, c), F32), jax.ShapeDtypeStruct((8, c), F32)),
        grid=(s // tt,),
        in_specs=[tile, pl.BlockSpec((tt, c), lambda i: (i, 1)),
                  pl.BlockSpec((CONV_HALO, c), lambda i: (jnp.maximum(i * hb - 1, 0), 0)),
                  pl.BlockSpec((CONV_HALO, c), lambda i: (jnp.maximum(i * hb - 1, 0), 1)),
                  tile, pl.BlockSpec((32, c), lambda i: (0, 0)), vec, vec, vec,
                  pl.BlockSpec((None, c, c), lambda i: (l, 0, 0))],
        out_specs=(tile, tile, pl.BlockSpec((c, c), lambda i: (0, 0)), pl.BlockSpec((8, c), lambda i: (0, 0))),
        scratch_shapes=[pltpu.VMEM((CONV_HALO + tt, c), F32)],
        compiler_params=_params("arbitrary"),
    )(u, u, u, u, dyc, wpad, dw_b, ln_g, ln_b, pw)


def _conv_bwd_dw(u, dd, gl, wpad, name):
    s, c = dd.shape
    ntap = 31
    tt = _conv_time_tile(s)
    hb = tt // CONV_HALO
    nt = s // tt
    last_halo = s // CONV_HALO - 1

    def body(val_ref, glu_ref, dd_ref, ddn_ref, gl_ref, glh_ref, w_ref, dval_ref, dglu_ref, dw_ref, dbuf, gbuf):
        i = pl.program_id(0)

        @pl.when(i == 0)
        def _():
            dw_ref[...] = jnp.zeros_like(dw_ref)

        d = dd_ref[...]
        dbuf[0:tt, :] = d
        dbuf[tt:tt + CONV_HALO, :] = jnp.where(i < nt - 1, ddn_ref[...], 0.0)
        gbuf[0:CONV_HALO, :] = jnp.where(i > 0, glh_ref[...], 0.0)
        gbuf[CONV_HALO:CONV_HALO + tt, :] = gl_ref[...]
        dgl = w_ref[0:1, :] * dbuf[pl.ds(ntap - 1, tt), :]
        for k in range(1, ntap):
            dgl = dgl + w_ref[k:k + 1, :] * dbuf[pl.ds(ntap - 1 - k, tt), :]
        for k in range(ntap):
            dw_ref[k:k + 1, :] += jnp.sum(d * gbuf[pl.ds(CONV_HALO - (ntap - 1) + k, tt), :], axis=0, keepdims=True)
        sg = _sigmoid(glu_ref[...])
        dval_ref[...] = (dgl * sg).astype(BF16)
        dglu_ref[...] = (dgl * val_ref[...] * sg * (1.0 - sg)).astype(BF16)

    tile = pl.BlockSpec((tt, c), lambda i: (i, 0))
    return pl.pallas_call(
        body, name=name,
        out_shape=(jax.ShapeDtypeStruct((s, c), BF16), jax.ShapeDtypeStruct((s, c), BF16),
                   jax.ShapeDtypeStruct((32, c), F32)),
        grid=(nt,),
        in_specs=[tile, pl.BlockSpec((tt, c), lambda i: (i, 1)), tile,
                  pl.BlockSpec((CONV_HALO, c), lambda i: (jnp.minimum((i + 1) * hb, last_halo), 0)),
                  tile, pl.BlockSpec((CONV_HALO, c), lambda i: (jnp.maximum(i * hb - 1, 0), 0)),
                  pl.BlockSpec((32, c), lambda i: (0, 0))],
        out_specs=(tile, tile, pl.BlockSpec((32, c), lambda i: (0, 0))),
        scratch_shapes=[pltpu.VMEM((tt + CONV_HALO, c), F32), pltpu.VMEM((CONV_HALO + tt, c), F32)],
        compiler_params=_params("arbitrary"),
    )(u, u, dd, dd, gl, gl, wpad)


SB_ROWS = 64


def _tri(n, cmp):
    r = lax.broadcasted_iota(jnp.int32, (n, n), 0)
    c = lax.broadcasted_iota(jnp.int32, (n, n), 1)
    return jnp.where(cmp(r, c), 1.0, 0.0).astype(BF16)


def _row_chunks(fn, n, *arrs):
    outs = [fn(*[a[r:r + SB_ROWS] for a in arrs]) for r in range(0, n, SB_ROWS)]
    return tuple(jnp.concatenate(list(o), axis=0) for o in zip(*outs))


def _hi_lo(v):
    hi = v.astype(BF16)
    return hi, (v - hi.astype(F32)).astype(BF16)


def _sb_sticks(z, causal):
    l1p = jnp.log(1.0 + jnp.exp(-jnp.abs(z)))
    lb = jnp.minimum(z, 0.0) - l1p
    ell = lb - z
    if causal is not None:
        ell = jnp.where(causal, ell, 0.0)
    hi, lo = _hi_lo(ell)
    return lb, hi, lo, jnp.sum(ell, axis=1, keepdims=True)


def _sb_fwd(u, heads, q_blk, k_blk, v_blk, name):
    s = u.shape[0]
    tq = _tile(s, (256, 128))
    scale = HEAD ** -0.5

    def body(q_ref, k_ref, v_ref, tri_ref, o_ref, tot_ref, kb_ref, vb_ref):
        i = pl.program_id(1)

        @pl.when(i == 0)
        def _():
            kb_ref[...] = k_ref[...].astype(BF16)
            vb_ref[...] = v_ref[...].astype(BF16)

        qb = (q_ref[...] * scale).astype(BF16)
        below_diag = lax.broadcasted_iota(jnp.int32, (tq, tq), 1) < lax.broadcasted_iota(jnp.int32, (tq, tq), 0)

        def blocks(j0, nb, c_a, acc, diag):
            mask = [below_diag] if diag else []
            rows = pl.ds(pl.multiple_of(j0 * tq, tq), nb * tq)
            kb = kb_ref[rows, :]
            vb = vb_ref[rows, :]
            z = _dot_nt(qb, kb)
            t_sfx = tri_ref[...]

            def sticks(zc, *m):
                out = []
                for b in range(nb):
                    out += _sb_sticks(zc[:, b * tq:(b + 1) * tq], m[0] if m and b == nb - 1 else None)
                return tuple(out)

            st = _row_chunks(sticks, tq, z, *mask)
            lb, hi, lo, rs = st[0::4], st[1::4], st[2::4], st[3::4]
            sfx = [_dot(hi[b], t_sfx) + _dot(lo[b], t_sfx) for b in range(nb)]
            before, run = [None] * nb, c_a
            for b in reversed(range(nb)):
                before[b], run = run, run + rs[b]

            def weights(*a):
                ws = []
                for b in range(nb):
                    lbc, sfxc, befc = a[3 * b:3 * b + 3]
                    w = jnp.exp(lbc + (befc + sfxc))
                    if diag and b == nb - 1:
                        w = jnp.where(a[-1], w, 0.0)
                    ws.append(w.astype(BF16))
                return (ws[0] if nb == 1 else jnp.concatenate(ws, axis=1),)

            flat = [v for b in range(nb) for v in (lb[b], sfx[b], before[b])]
            wb, = _row_chunks(weights, tq, *flat, *mask)
            return run, acc + _dot(wb, vb)

        carry = (jnp.zeros((tq, 1), F32), jnp.zeros((tq, HEAD), F32))
        has_nbr = jnp.minimum(i, 1)
        carry = lax.fori_loop(0, has_nbr, lambda _, cr: blocks(i - 1, 2, *cr, True), carry)
        carry = lax.fori_loop(0, 1 - has_nbr, lambda _, cr: blocks(0, 1, *cr, True), carry)
        r = jnp.maximum(i - 1, 0)
        carry = lax.fori_loop(0, r // 4, lambda t, cr: blocks(r - 4 - 4 * t, 4, *cr, False), carry)
        carry = lax.fori_loop(0, (r % 4) // 2, lambda _, cr: blocks(r % 2, 2, *cr, False), carry)
        c_a, acc = lax.fori_loop(0, r % 2, lambda _, cr: blocks(0, 1, *cr, False), carry)
        o_ref[...] = acc
        tot_ref[...] = jnp.broadcast_to(c_a, (tq, HEAD))

    full = lambda off: pl.BlockSpec((s, HEAD), lambda h, i: (0, off + h))
    out = pl.BlockSpec((tq, HEAD), lambda h, i: (i, h))
    return pl.pallas_call(
        body, name=name,
        out_shape=(jax.ShapeDtypeStruct((s, heads * HEAD), F32), jax.ShapeDtypeStruct((s, heads * HEAD), F32)),
        grid=(heads, s // tq),
        in_specs=[pl.BlockSpec((tq, HEAD), lambda h, i: (i, q_blk + h)), full(k_blk), full(v_blk),
                  pl.BlockSpec((tq, tq), lambda h, i: (0, 0))],
        out_specs=(out, out),
        scratch_shapes=[pltpu.VMEM((s, HEAD), BF16), pltpu.VMEM((s, HEAD), BF16)],
        compiler_params=_params("parallel", "arbitrary"),
    )(u, u, u, _tri(tq, lambda r, c: r > c))


def _sb_bwd(u, tot, dy, heads, q_blk, k_blk, v_blk, dep, name):
    s = u.shape[0]
    tq = _tile(s, (256, 128))
    scale = HEAD ** -0.5

    def body(q_ref, k_ref, v_ref, tot_ref, dy_ref, incl_ref, excl_ref, dep_ref, dq_ref, dk_ref, dv_ref,
             kb_ref, vb_ref):
        i = pl.program_id(1)

        @pl.when(i == 0)
        def _():
            dk_ref[...] = jnp.zeros_like(dk_ref)
            dv_ref[...] = jnp.zeros_like(dv_ref)
            kb_ref[...] = k_ref[...].astype(BF16)
            vb_ref[...] = v_ref[...].astype(BF16)

        qb = (q_ref[...] * scale).astype(BF16)
        dob = dy_ref[...].astype(BF16)
        total = tot_ref[:, 0:1]
        below_diag = lax.broadcasted_iota(jnp.int32, (tq, tq), 1) < lax.broadcasted_iota(jnp.int32, (tq, tq), 0)

        def blocks(j0, nb, c_p, c_g, dq, diag):
            mask = [below_diag] if diag else []
            rows = pl.ds(pl.multiple_of(j0 * tq, tq), nb * tq)
            kb = kb_ref[rows, :]
            vb = vb_ref[rows, :]
            z = _dot_nt(qb, kb)
            dw = _dot_nt(dob, vb)
            t_incl, t_excl = incl_ref[...], excl_ref[...]
            cols = lambda a, b: a[:, b * tq:(b + 1) * tq]

            def sticks(zc, *m):
                out = []
                for b in range(nb):
                    out += _sb_sticks(cols(zc, b), m[0] if m and b == nb - 1 else None)
                return tuple(out)

            st = _row_chunks(sticks, tq, z, *mask)
            lb, hi, lo, rs_l = st[0::4], st[1::4], st[2::4], st[3::4]
            pfx = [_dot(hi[b], t_incl) + _dot(lo[b], t_incl) for b in range(nb)]
            p_before = [c_p]
            for b in range(nb):
                p_before.append(p_before[-1] + rs_l[b])

            def weights(totc, dwc, *a):
                out = []
                for b in range(nb):
                    lbc, pfxc, pbc = a[3 * b:3 * b + 3]
                    w = jnp.exp(lbc + (totc - (pbc + pfxc)))
                    if diag and b == nb - 1:
                        w = jnp.where(a[-1], w, 0.0)
                    g = w * cols(dwc, b)
                    out += [w.astype(BF16), g, g.astype(BF16), jnp.sum(g, axis=1, keepdims=True)]
                return tuple(out)

            flat = [v for b in range(nb) for v in (lb[b], pfx[b], p_before[b])]
            wt = _row_chunks(weights, tq, total, dw, *flat, *mask)
            wb, g, gb, rs_g = wt[0::4], wt[1::4], wt[2::4], wt[3::4]
            g_pre = [_dot(gb[b], t_excl) for b in range(nb)]
            g_before = [c_g]
            for b in range(nb):
                g_before.append(g_before[-1] + rs_g[b])

            def dscore(*a):
                dzs = []
                for b in range(nb):
                    lbc, gc, gprec, gbc = a[4 * b:4 * b + 4]
                    beta = jnp.exp(lbc)
                    dz = gc * (1.0 - beta) - (gbc + gprec) * beta
                    if diag and b == nb - 1:
                        dz = jnp.where(a[-1], dz, 0.0)
                    dzs.append(dz.astype(BF16))
                return (dzs[0] if nb == 1 else jnp.concatenate(dzs, axis=1),)

            flat = [v for b in range(nb) for v in (lb[b], g[b], g_pre[b], g_before[b])]
            dzb, = _row_chunks(dscore, tq, *flat, *mask)
            wcat = wb[0] if nb == 1 else jnp.concatenate(wb, axis=1)
            dk_ref[rows, :] += _dot_tn(dzb, qb)
            dv_ref[rows, :] += _dot_tn(wcat, dob)
            return p_before[-1], g_before[-1], dq + _dot(dzb, kb)

        zero = jnp.zeros((tq, 1), F32)
        carry = lax.fori_loop(0, i // 2, lambda t, cr: blocks(2 * t, 2, *cr, False),
                              (zero, zero, jnp.zeros((tq, HEAD), F32)))
        carry = lax.fori_loop(0, i % 2, lambda _, cr: blocks(i - 1, 2, *cr, True), carry)
        _, _, dq = lax.fori_loop(0, 1 - i % 2, lambda _, cr: blocks(i, 1, *cr, True), carry)
        dq_ref[...] = dq * scale

    full = lambda off: pl.BlockSpec((s, HEAD), lambda h, i: (0, off + h))
    blk = pl.BlockSpec((tq, HEAD), lambda h, i: (i, h))
    acc = pl.BlockSpec((s, HEAD), lambda h, i: (0, h))
    tri = pl.BlockSpec((tq, tq), lambda h, i: (0, 0))
    shape = jax.ShapeDtypeStruct((s, heads * HEAD), F32)
    return pl.pallas_call(
        body, name=name, out_shape=(shape, shape, shape), grid=(heads, s // tq),
        in_specs=[pl.BlockSpec((tq, HEAD), lambda h, i: (i, q_blk + h)), full(k_blk), full(v_blk), blk, blk, tri, tri,
                  pl.BlockSpec(memory_space=pl.ANY)],
        out_specs=(blk, acc, acc),
        scratch_shapes=[pltpu.VMEM((s, HEAD), BF16), pltpu.VMEM((s, HEAD), BF16)],
        compiler_params=_params("parallel", "arbitrary"),
    )(u, u, u, tot, dy, _tri(tq, lambda r, c: r <= c), _tri(tq, lambda r, c: r < c), dep)


def _lru_time_tile(s):
    return _tile(s, (256, 128, 64, 32))


def _lru_gates(xc, wa_ref, ba_ref, wx_ref, bx_ref, lam_ref, nh):
    pr, pi = [], []
    for n in range(nh):
        xn = xc[:, n * HEAD:(n + 1) * HEAD].astype(BF16)
        pr.append(_dot(xn, wa_ref[n]))
        pi.append(_dot(xn, wx_ref[n]))
    r = _sigmoid((pr[0] if nh == 1 else jnp.concatenate(pr, axis=1)) + ba_ref[...])
    ig = _sigmoid((pi[0] if nh == 1 else jnp.concatenate(pi, axis=1)) + bx_ref[...])
    lam = lam_ref[...]
    sp = jnp.maximum(-lam, 0.0) + jnp.log(1.0 + jnp.exp(-jnp.abs(lam)))
    log_a = -LRU_C * r * sp
    a = jnp.exp(log_a)
    mult = jnp.sqrt(-_expm1(2.0 * log_a))
    return r, ig, a, mult, sp


def _lru_fwd(u, x_blk, cw, cb, wa, ba, wx, bx, lam, name):
    s = u.shape[0]
    w = lam.shape[1]
    nh = w // HEAD
    tt = _lru_time_tile(s)
    hb = tt // LRU_HALO

    def body(x_ref, xh_ref, cw_ref, cb_ref, wa_ref, ba_ref, wx_ref, bx_ref, lam_ref, y_ref,
             xbuf, abuf, bbuf, hstate, rowbuf):
        i = pl.program_id(0)

        @pl.when(i == 0)
        def _():
            hstate[...] = jnp.zeros_like(hstate)

        xbuf[0:LRU_HALO, :] = jnp.where(i > 0, xh_ref[...], 0.0)
        xbuf[LRU_HALO:LRU_HALO + tt, :] = x_ref[...]
        xc = _conv_taps(xbuf, cw_ref, tt, 4, LRU_HALO - 3) + cb_ref[...]
        _, ig, a, mult, _ = _lru_gates(xc, wa_ref, ba_ref, wx_ref, bx_ref, lam_ref, nh)
        abuf[...] = a
        bbuf[...] = mult * (ig * xc)

        def group(gi, h):
            rows = pl.ds(pl.multiple_of(gi * 8, 8), 8)
            a8 = abuf[rows, :]
            b8 = bbuf[rows, :]
            for j in range(8):
                h = a8[j:j + 1, :] * h + b8[j:j + 1, :]
                rowbuf[j:j + 1, :] = h
            y_ref[rows, :] = rowbuf[...]
            return h

        hstate[0:1, :] = lax.fori_loop(0, tt // 8, group, hstate[0:1, :])

    vec = pl.BlockSpec((1, w), lambda i: (0, 0))
    gate = pl.BlockSpec((nh, HEAD, HEAD), lambda i: (0, 0, 0))
    return pl.pallas_call(
        body, name=name, out_shape=jax.ShapeDtypeStruct((s, w), F32), grid=(s // tt,),
        in_specs=[pl.BlockSpec((tt, w), lambda i: (i, x_blk)),
                  pl.BlockSpec((LRU_HALO, w), lambda i: (jnp.maximum(i * hb - 1, 0), x_blk)),
                  pl.BlockSpec((8, w), lambda i: (0, 0)), vec, gate, vec, gate, vec, vec],
        out_specs=pl.BlockSpec((tt, w), lambda i: (i, 0)),
        scratch_shapes=[pltpu.VMEM((LRU_HALO + tt, w), F32), pltpu.VMEM((tt, w), F32), pltpu.VMEM((tt, w), F32),
                        pltpu.VMEM((8, w), F32), pltpu.VMEM((8, w), F32)],
        compiler_params=_params("arbitrary"),
    )(u, u, cw, cb, wa, ba, wx, bx, lam)


def _lru_bwd(u, x_blk, hseq, dy, cw, cb, wa, ba, wx, bx, lam, name):
    s = u.shape[0]
    w = lam.shape[1]
    nh = w // HEAD
    tt = _lru_time_tile(s)
    hb = tt // LRU_HALO
    nt = s // tt

    def body(x_ref, xh_ref, h_ref, hh_ref, dy_ref, cw_ref, cb_ref, wa_ref, ba_ref, wx_ref, bx_ref, lam_ref,
             dx_ref, dwa_ref, dwx_ref, vec_ref, xbuf, hbuf, abuf, lbuf, dbuf, cstate, dhalo, rowbuf):
        i = pl.program_id(0)
        rt = nt - 1 - i

        @pl.when(i == 0)
        def _():
            cstate[...] = jnp.zeros_like(cstate)
            dhalo[...] = jnp.zeros_like(dhalo)
            dwa_ref[...] = jnp.zeros_like(dwa_ref)
            dwx_ref[...] = jnp.zeros_like(dwx_ref)
            vec_ref[...] = jnp.zeros_like(vec_ref)

        xbuf[0:LRU_HALO, :] = jnp.where(rt > 0, xh_ref[...], 0.0)
        xbuf[LRU_HALO:LRU_HALO + tt, :] = x_ref[...]
        hbuf[0:LRU_HALO, :] = jnp.where(rt > 0, hh_ref[...], 0.0)
        hbuf[LRU_HALO:LRU_HALO + tt, :] = h_ref[...]
        xc = _conv_taps(xbuf, cw_ref, tt, 4, LRU_HALO - 3) + cb_ref[...]
        r, ig, a, mult, sp = _lru_gates(xc, wa_ref, ba_ref, wx_ref, bx_ref, lam_ref, nh)
        abuf[...] = a

        def group(gi, c):
            rows = pl.ds(pl.multiple_of((tt // 8 - 1 - gi) * 8, 8), 8)
            a8 = abuf[rows, :]
            d8 = dy_ref[rows, :]
            for j in range(7, -1, -1):
                lam_t = d8[j:j + 1, :] + c
                rowbuf[j:j + 1, :] = lam_t
                c = a8[j:j + 1, :] * lam_t
            lbuf[rows, :] = rowbuf[...]
            return c

        cstate[0:1, :] = lax.fori_loop(0, tt // 8, group, cstate[0:1, :])

        lam_t = lbuf[...]
        hprev = hbuf[pl.ds(LRU_HALO - 1, tt), :]
        ixc = ig * xc
        d_ixc = lam_t * mult
        d_ig = d_ixc * xc
        dxc = d_ixc * ig
        dlog_a = lam_t * hprev * a + lam_t * ixc * (-(a * a) / mult)
        dr = dlog_a * (-LRU_C * sp)
        lam_p = lam_ref[...]
        dsp = -_sigmoid(-lam_p)
        vec_ref[2:3, :] += jnp.sum(dlog_a * (-LRU_C * r), axis=0, keepdims=True) * dsp
        dpr = dr * r * (1.0 - r)
        dpi = d_ig * ig * (1.0 - ig)
        vec_ref[0:1, :] += jnp.sum(dpr, axis=0, keepdims=True)
        vec_ref[1:2, :] += jnp.sum(dpi, axis=0, keepdims=True)
        parts = []
        for n in range(nh):
            sl = slice(n * HEAD, (n + 1) * HEAD)
            xn = xc[:, sl].astype(BF16)
            dprn = dpr[:, sl].astype(BF16)
            dpin = dpi[:, sl].astype(BF16)
            dwa_ref[n] += _dot_tn(xn, dprn)
            dwx_ref[n] += _dot_tn(xn, dpin)
            parts.append(_dot_nt(dprn, wa_ref[n]) + _dot_nt(dpin, wx_ref[n]))
        dxc = dxc + (parts[0] if nh == 1 else jnp.concatenate(parts, axis=1))
        vec_ref[3:4, :] += jnp.sum(dxc, axis=0, keepdims=True)
        dbuf[0:tt, :] = dxc
        dbuf[tt:tt + LRU_HALO, :] = dhalo[...]
        dx = cw_ref[0:1, :] * dbuf[pl.ds(3, tt), :]
        for k in range(1, 4):
            dx = dx + cw_ref[k:k + 1, :] * dbuf[pl.ds(3 - k, tt), :]
        dx_ref[...] = dx.astype(BF16)
        for k in range(4):
            vec_ref[4 + k:5 + k, :] += jnp.sum(dxc * xbuf[pl.ds(LRU_HALO - 3 + k, tt), :], axis=0, keepdims=True)
        dhalo[...] = dbuf[0:LRU_HALO, :]

    vec = pl.BlockSpec((1, w), lambda i: (0, 0))
    gate = pl.BlockSpec((nh, HEAD, HEAD), lambda i: (0, 0, 0))
    rev = lambda i: nt - 1 - i
    tile = pl.BlockSpec((tt, w), lambda i: (rev(i), 0))
    halo = lambda col: pl.BlockSpec((LRU_HALO, w), lambda i: (jnp.maximum(rev(i) * hb - 1, 0), col))
    return pl.pallas_call(
        body, name=name,
        out_shape=(jax.ShapeDtypeStruct((s, w), BF16), jax.ShapeDtypeStruct((nh, HEAD, HEAD), F32),
                   jax.ShapeDtypeStruct((nh, HEAD, HEAD), F32), jax.ShapeDtypeStruct((8, w), F32)),
        grid=(nt,),
        in_specs=[pl.BlockSpec((tt, w), lambda i: (rev(i), x_blk)), halo(x_blk), tile, halo(0), tile,
                  pl.BlockSpec((8, w), lambda i: (0, 0)), vec, gate, vec, gate, vec, vec],
        out_specs=(tile, gate, gate, pl.BlockSpec((8, w), lambda i: (0, 0))),
        scratch_shapes=[pltpu.VMEM((LRU_HALO + tt, w), F32), pltpu.VMEM((LRU_HALO + tt, w), F32),
                        pltpu.VMEM((tt, w), F32), pltpu.VMEM((tt, w), F32), pltpu.VMEM((tt + LRU_HALO, w), F32),
                        pltpu.VMEM((8, w), F32), pltpu.VMEM((8, w), F32), pltpu.VMEM((8, w), F32)],
        compiler_params=_params("arbitrary"),
    )(u, u, hseq, hseq, dy, cw, cb, wa, ba, wx, bx, lam)


def _gate_specs(c, tr):
    return [pl.BlockSpec((tr, c), lambda i, b=b: (i, b)) for b in (2, 9, 10, 12)]


def _outgate_fwd(y_conv, y_attn, y_lru, u, n_conv, n_attn, n_lru, name):
    s, c = y_conv.shape
    tr = _tile(s, (256, 128, 64, 32, 16))

    def body(yc_ref, ya_ref, yl_ref, gc_ref, ga0_ref, ga1_ref, gl_ref, nc_ref, na_ref, nl_ref, o_ref):
        def rinv(v):
            return lax.rsqrt(jnp.mean(v * v, axis=-1, keepdims=True) + RMS_EPS)

        def silu(g):
            return g * _sigmoid(g)

        yc = yc_ref[...]
        o_ref[:, 0:c] = (yc * rinv(yc) * nc_ref[...] * silu(gc_ref[...])).astype(BF16)
        ya = ya_ref[...]
        ra = rinv(ya)
        o_ref[:, c:2 * c] = (ya[:, 0:c] * ra * na_ref[:, 0:c] * silu(ga0_ref[...])).astype(BF16)
        o_ref[:, 2 * c:3 * c] = (ya[:, c:2 * c] * ra * na_ref[:, c:2 * c] * silu(ga1_ref[...])).astype(BF16)
        yl = yl_ref[...]
        o_ref[:, 3 * c:4 * c] = (yl * rinv(yl) * nl_ref[...] * silu(gl_ref[...])).astype(BF16)

    row = lambda wd: pl.BlockSpec((tr, wd), lambda i: (i, 0))
    vec = lambda wd: pl.BlockSpec((1, wd), lambda i: (0, 0))
    return pl.pallas_call(
        body, name=name, out_shape=jax.ShapeDtypeStruct((s, 4 * c), BF16), grid=(s // tr,),
        in_specs=[row(c), row(2 * c), row(c)] + _gate_specs(c, tr) + [vec(c), vec(2 * c), vec(c)],
        out_specs=row(4 * c), compiler_params=_params("parallel"),
    )(y_conv, y_attn, y_lru, u, u, u, u, n_conv, n_attn, n_lru)


def _outgate_bwd(dy, y_conv, y_attn, y_lru, u, n_conv, n_attn, n_lru, name):
    s, c = y_conv.shape
    tr = _tile(s, (256, 128, 64, 32, 16))

    def body(dy_ref, yc_ref, ya_ref, yl_ref, gc_ref, ga0_ref, ga1_ref, gl_ref, nc_ref, na_ref, nl_ref,
             dyc_ref, dya_ref, dyl_ref, dgc_ref, dga_ref, dgl_ref, dn_ref):
        @pl.when(pl.program_id(0) == 0)
        def _():
            dn_ref[...] = jnp.zeros_like(dn_ref)

        def group(yv, gate, wv, d):
            r = lax.rsqrt(jnp.mean(yv * yv, axis=-1, keepdims=True) + RMS_EPS)
            yh = yv * r
            sg = _sigmoid(gate)
            dn = d * (gate * sg)
            dgate = d * (yh * wv) * (sg * (1.0 + gate * (1.0 - sg)))
            dw = jnp.sum(dn * yh, axis=0, keepdims=True)
            dyn = dn * wv
            dyv = r * (dyn - yh * jnp.mean(dyn * yh, axis=-1, keepdims=True))
            return dyv, dgate, dw

        dyv, dg, dw = group(yc_ref[...], gc_ref[...], nc_ref[...], dy_ref[:, 0:c])
        dyc_ref[...] = dyv
        dgc_ref[...] = dg.astype(BF16)
        dn_ref[0:1, 0:c] += dw
        gate_a = jnp.concatenate([ga0_ref[...], ga1_ref[...]], axis=1)
        dyv, dg, dw = group(ya_ref[...], gate_a, na_ref[...], dy_ref[:, c:3 * c])
        dya_ref[...] = dyv
        dga_ref[...] = dg.astype(BF16)
        dn_ref[0:1, c:3 * c] += dw
        dyv, dg, dw = group(yl_ref[...], gl_ref[...], nl_ref[...], dy_ref[:, 3 * c:4 * c])
        dyl_ref[...] = dyv
        dgl_ref[...] = dg.astype(BF16)
        dn_ref[0:1, 3 * c:4 * c] += dw

    row = lambda wd: pl.BlockSpec((tr, wd), lambda i: (i, 0))
    vec = lambda wd: pl.BlockSpec((1, wd), lambda i: (0, 0))
    sh = lambda wd, dt: jax.ShapeDtypeStruct((s, wd), dt)
    return pl.pallas_call(
        body, name=name,
        out_shape=(sh(c, F32), sh(2 * c, F32), sh(c, F32), sh(c, BF16), sh(2 * c, BF16), sh(c, BF16),
                   jax.ShapeDtypeStruct((8, 4 * c), F32)),
        grid=(s // tr,),
        in_specs=[row(4 * c), row(c), row(2 * c), row(c)] + _gate_specs(c, tr) + [vec(c), vec(2 * c), vec(c)],
        out_specs=(row(c), row(2 * c), row(c), row(c), row(2 * c), row(c),
                   pl.BlockSpec((8, 4 * c), lambda i: (0, 0))),
        compiler_params=_params("arbitrary"),
    )(dy, y_conv, y_attn, y_lru, u, u, u, u, n_conv, n_attn, n_lru)


def _xattn_probs(qh, kh, scale):
    sc = _dot_nt(qh, kh) * scale
    p = jnp.exp(sc - jnp.max(sc, axis=-1, keepdims=True))
    return p / jnp.sum(p, axis=-1, keepdims=True)


def _xattn_fwd(q, kv, name):
    s, xw = q.shape
    m = kv.shape[0]
    nh = xw // HEAD
    tq = _tile(s, (256, 128, 64, 32, 16))
    scale = HEAD ** -0.5

    def body(q_ref, kv_ref, o_ref):
        for h in range(nh):
            qh = q_ref[:, h * HEAD:(h + 1) * HEAD].astype(BF16)
            kh = kv_ref[:, h * HEAD:(h + 1) * HEAD].astype(BF16)
            vh = kv_ref[:, xw + h * HEAD:xw + (h + 1) * HEAD].astype(BF16)
            p = _xattn_probs(qh, kh, scale)
            o_ref[:, h * HEAD:(h + 1) * HEAD] = _dot(p.astype(BF16), vh).astype(BF16)

    return pl.pallas_call(
        body, name=name, out_shape=jax.ShapeDtypeStruct((s, xw), BF16), grid=(s // tq,),
        in_specs=[pl.BlockSpec((tq, xw), lambda i: (i, 0)), pl.BlockSpec((m, 2 * xw), lambda i: (0, 0))],
        out_specs=pl.BlockSpec((tq, xw), lambda i: (i, 0)), compiler_params=_params("parallel"),
    )(q, kv)


def _xattn_bwd(q, kv, do, name):
    s, xw = q.shape
    m = kv.shape[0]
    nh = xw // HEAD
    tq = _tile(s, (256, 128, 64, 32, 16))
    scale = HEAD ** -0.5

    def body(q_ref, kv_ref, do_ref, dq_ref, dkv_ref):
        @pl.when(pl.program_id(0) == 0)
        def _():
            dkv_ref[...] = jnp.zeros_like(dkv_ref)

        for h in range(nh):
            ks = slice(h * HEAD, (h + 1) * HEAD)
            vs = slice(xw + h * HEAD, xw + (h + 1) * HEAD)
            qh = q_ref[:, ks].astype(BF16)
            kh = kv_ref[:, ks].astype(BF16)
            vh = kv_ref[:, vs].astype(BF16)
            doh = do_ref[:, ks].astype(BF16)
            p = _xattn_probs(qh, kh, scale)
            dkv_ref[:, vs] += _dot_tn(p.astype(BF16), doh)
            dp = _dot_nt(doh, vh)
            ds = (p * (dp - jnp.sum(dp * p, axis=-1, keepdims=True)) * scale).astype(BF16)
            dq_ref[:, ks] = _dot(ds, kh).astype(BF16)
            dkv_ref[:, ks] += _dot_tn(ds, qh)

    row = pl.BlockSpec((tq, xw), lambda i: (i, 0))
    full = pl.BlockSpec((m, 2 * xw), lambda i: (0, 0))
    return pl.pallas_call(
        body, name=name,
        out_shape=(jax.ShapeDtypeStruct((s, xw), BF16), jax.ShapeDtypeStruct((m, 2 * xw), F32)), grid=(s // tq,),
        in_specs=[row, full, row], out_specs=(row, full), compiler_params=_params("arbitrary"),
    )(q, kv, do)


def _adamw(w, g, m, v, name):
    rows, cols = w.shape
    tr = _tile(rows, (512, 256, 128, 64, 32, 16, 8)) if rows % 8 == 0 else rows
    bc1 = 1.0 - ADAM_B1 ** ADAM_STEP
    bc2 = 1.0 - ADAM_B2 ** ADAM_STEP

    def body(w_ref, g_ref, m_ref, v_ref, d_ref, nm_ref, nv_ref):
        gv = g_ref[...]
        nm = ADAM_B1 * m_ref[...] + (1.0 - ADAM_B1) * gv
        nv = ADAM_B2 * v_ref[...] + (1.0 - ADAM_B2) * (gv * gv)
        nm_ref[...] = nm
        nv_ref[...] = nv
        d_ref[...] = -ADAM_LR * ((nm / bc1) / (jnp.sqrt(nv / bc2) + ADAM_EPS) + ADAM_WD * w_ref[...])

    spec = pl.BlockSpec((tr, cols), lambda i: (i, 0))
    sh = jax.ShapeDtypeStruct((rows, cols), F32)
    return pl.pallas_call(
        body, name=name, out_shape=(sh, sh, sh), grid=(rows // tr,), in_specs=[spec] * 4,
        out_specs=(spec, spec, spec), compiler_params=_params("parallel"),
    )(w, g, m, v)


WEIGHTS = ['mix_norm_g', 'w_in', 'conv_dw_w', 'conv_dw_b', 'conv_ln_g', 'conv_ln_b', 'conv_pw_w', 'lru_conv_w',
           'lru_conv_b', 'lru_wa', 'lru_ba', 'lru_wx', 'lru_bx', 'lru_lambda', 'out_norm_conv', 'out_norm_attn',
           'out_norm_lru', 'w_out', 'xattn_norm_g', 'mem_norm_g', 'xattn_wq', 'xattn_wkv', 'xattn_wo',
           'final_norm_g']
BIG_SHARDED = {'w_in': 2, 'conv_pw_w': 1, 'w_out': 1, 'xattn_wq': 1, 'xattn_wkv': 1, 'xattn_wo': 2}
SMALL_SHARDED = {'conv_dw_w': 2, 'lru_conv_w': 2}


def _layer_fwd(x, mem, p, l, first, rest):
    row = lambda name: p[name][l][None, :]
    c = p['conv_dw_b'].shape[1]
    heads = 2 * c // HEAD
    h = _rms_fwd(x, row('mix_norm_g'), "rms_mix")
    w_in, tied = first(h)
    u = _mm(h, w_in, dep=tied, name="in_proj")
    q_blk = 3 * c // HEAD
    y_attn, tot = _sb_fwd(u, heads, q_blk, q_blk + heads, q_blk + 2 * heads, "sb_fwd")
    fw = dict(rest(y_attn), w_in=w_in)
    wpad = jnp.pad(fw['conv_dw_w'][l], ((0, 1), (0, 0)))
    cw = jnp.pad(fw['lru_conv_w'][l], ((0, 4), (0, 0)))
    wa, wx = p['lru_wa'][l].astype(BF16), p['lru_wx'][l].astype(BF16)
    y_lru = _lru_fwd(u, 11, cw, row('lru_conv_b'), wa, row('lru_ba'), wx, row('lru_bx'), row('lru_lambda'), "lru_fwd")
    y_conv = _conv_fwd(u, wpad, row('conv_dw_b'), row('conv_ln_g'), row('conv_ln_b'), fw['conv_pw_w'][None], 0,
                       "conv_fwd")
    yc = _outgate_fwd(y_conv, y_attn, y_lru, u, row('out_norm_conv'), row('out_norm_attn'), row('out_norm_lru'),
                      "outgate_fwd")
    x1 = _mm(yc, fw['w_out'], add=x, name="out_proj")
    h2 = _rms_fwd(x1, row('xattn_norm_g'), "rms_xattn")
    memn = _rms_fwd(mem, row('mem_norm_g'), "rms_mem")
    q2 = _mm(h2, fw['xattn_wq'], name="xq_proj")
    kv = _mm(memn, fw['xattn_wkv'], name="xkv_proj")
    o2 = _xattn_fwd(q2, kv, "xattn_fwd")
    x2 = _mm(o2, fw['xattn_wo'], add=x1, name="xo_proj")
    saved = dict(x=x, h=h, u=u, wpad=wpad, cw=cw, wa=wa, wx=wx, y_conv=y_conv, y_attn=y_attn, tot=tot, y_lru=y_lru,
                 yc=yc, x1=x1, h2=h2, memn=memn, q2=q2, kv=kv, o2=o2, fw=fw)
    return x2, saved


def _layer_bwd(dx2, mem, p, l, sv, rest_ready, w_in_ready):
    row = lambda name: p[name][l][None, :]
    c = p['conv_dw_b'].shape[1]
    heads = 2 * c // HEAD
    fw = sv['fw']
    g, big = {}, {}
    big['xattn_wo'] = _mm(sv['o2'], dx2, ta=True, out_dtype=BF16, name="d_wo")
    do2 = _mm(dx2, fw['xattn_wo'], tb=True, name="d_o2")
    dq2, dkv = _xattn_bwd(sv['q2'], sv['kv'], do2, "xattn_bwd")
    big['xattn_wq'] = _mm(sv['h2'], dq2, ta=True, out_dtype=BF16, name="d_wq")
    dh2 = _mm(dq2, fw['xattn_wq'], tb=True, name="d_h2")
    big['xattn_wkv'] = _mm(sv['memn'], dkv, ta=True, out_dtype=BF16, name="d_wkv")
    dmemn = _mm(dkv, fw['xattn_wkv'], tb=True, name="d_memn")
    _, g['mem_norm_g'] = _rms_bwd(mem, row('mem_norm_g'), dmemn, None, "rms_mem_bwd")
    dx1, g['xattn_norm_g'] = _rms_bwd(sv['x1'], row('xattn_norm_g'), dh2, dx2, "rms_xattn_bwd")
    big['w_out'] = _mm(sv['yc'], dx1, ta=True, out_dtype=BF16, name="d_wout")
    dyc = _mm(dx1, fw['w_out'], tb=True, name="d_yc")
    u = sv['u']
    d_yconv, d_yattn, d_ylru, dgc, dga, dgl, dn = _outgate_bwd(
        dyc, sv['y_conv'], sv['y_attn'], sv['y_lru'], u, row('out_norm_conv'), row('out_norm_attn'),
        row('out_norm_lru'), "outgate_bwd")
    g['out_norm_conv'], g['out_norm_attn'], g['out_norm_lru'] = dn[0, 0:c], dn[0, c:3 * c], dn[0, 3 * c:4 * c]
    dd, gl, dpw, cvec = _conv_bwd_post(u, d_yconv, sv['wpad'], row('conv_dw_b'), row('conv_ln_g'),
                                       row('conv_ln_b'), fw['conv_pw_w'][None], 0, "conv_bwd_post")
    big['conv_pw_w'] = dpw.astype(BF16)
    g['conv_ln_g'], g['conv_ln_b'], g['conv_dw_b'] = cvec[0], cvec[1], cvec[2]
    dval, dglu, ddw = _conv_bwd_dw(u, dd, gl, sv['wpad'], "conv_bwd_dw")
    g['conv_dw_w'] = ddw[0:31]
    dxr, g['lru_wa'], g['lru_wx'], lvec = _lru_bwd(
        u, 11, sv['y_lru'], d_ylru, sv['cw'], row('lru_conv_b'), sv['wa'], row('lru_ba'), sv['wx'], row('lru_bx'),
        row('lru_lambda'), "lru_bwd")
    g['lru_ba'], g['lru_bx'], g['lru_lambda'], g['lru_conv_b'] = lvec[0], lvec[1], lvec[2], lvec[3]
    g['lru_conv_w'] = lvec[4:8]
    tied = rest_ready(big, g)
    q_blk = 3 * c // HEAD
    dq, dk, dv = _sb_bwd(u, sv['tot'], d_yattn, heads, q_blk, q_blk + heads, q_blk + 2 * heads, tied, "sb_bwd")
    du = jnp.concatenate([dval, dglu, dgc, dq.astype(BF16), dk.astype(BF16), dv.astype(BF16), dga, dxr, dgl], axis=1)
    tied = w_in_ready(_mm(sv['h'], du, ta=True, out_dtype=BF16, name="d_win"))
    dh = _mm(du, fw['w_in'], tb=True, dep=tied, name="d_h")
    dx0, g['mix_norm_g'] = _rms_bwd(sv['x'], row('mix_norm_g'), dh, dx1, "rms_mix_bwd")
    return dx0, g


def kernel(x, mem, mix_norm_g, w_in, conv_dw_w, conv_dw_b, conv_ln_g, conv_ln_b, conv_pw_w, lru_conv_w, lru_conv_b, lru_wa, lru_ba, lru_wx, lru_bx, lru_lambda, out_norm_conv, out_norm_attn, out_norm_lru, w_out, xattn_norm_g, mem_norm_g, xattn_wq, xattn_wkv, xattn_wo, final_norm_g, loss_target, m_mix_norm_g, m_w_in, m_conv_dw_w, m_conv_dw_b, m_conv_ln_g, m_conv_ln_b, m_conv_pw_w, m_lru_conv_w, m_lru_conv_b, m_lru_wa, m_lru_ba, m_lru_wx, m_lru_bx, m_lru_lambda, m_out_norm_conv, m_out_norm_attn, m_out_norm_lru, m_w_out, m_xattn_norm_g, m_mem_norm_g, m_xattn_wq, m_xattn_wkv, m_xattn_wo, m_final_norm_g, v_mix_norm_g, v_w_in, v_conv_dw_w, v_conv_dw_b, v_conv_ln_g, v_conv_ln_b, v_conv_pw_w, v_lru_conv_w, v_lru_conv_b, v_lru_wa, v_lru_ba, v_lru_wx, v_lru_bx, v_lru_lambda, v_out_norm_conv, v_out_norm_attn, v_out_norm_lru, v_w_out, v_xattn_norm_g, v_mem_norm_g, v_xattn_wq, v_xattn_wkv, v_xattn_wo, v_final_norm_g):
    args = locals()
    w = {n: args[n] for n in WEIGHTS}
    mom = {n: args["m_" + n] for n in WEIGHTS}
    var = {n: args["v_" + n] for n in WEIGHTS}
    depth = w_in.shape[0]
    c = conv_dw_b.shape[1]
    assert out_norm_attn.shape[1] == 2 * c and lru_lambda.shape[1] == c and w_in.shape[2] * N_DEV == 13 * c
    assert c % HEAD == 0 and x.shape[0] == 1 and mem.shape[0] == 1
    xs, mems, tgt = x[0], mem[0], loss_target[0]
    me = 4 * lax.axis_index("x") + 2 * lax.axis_index("y") + lax.axis_index("c")

    assert depth == 2 and (2 * w_in.shape[2]) % LANE == 0 and w_in.shape[2] % LANE in (0, LANE // 2)
    my_c = lax.axis_index("c")
    big = list(BIG_SHARDED)
    dev = [n for n in big if n != 'w_in']
    my_x, my_y = lax.axis_index("x"), lax.axis_index("y")
    small = list(SMALL_SHARDED)
    p = dict(w)
    blk_width = w_in.shape[2]
    w_in_blk = jnp.pad(w_in.astype(BF16), ((0, 0), (0, 0), (0, -blk_width % LANE)))
    w_in0_blocks = _all_gather(w_in_blk[0], "gather_w_in0")
    w_in0 = _join_cols(w_in0_blocks, blk_width, "join_w_in")
    taps = _pack([w[n] for n in small], F32)[None]

    def gather_group(names, l, stage, dep, tag, with_taps=False):
        srcs, shapes, plans = [], [], []
        if with_taps:
            srcs, shapes, plans = [taps], [(N_DEV,) + taps.shape[1:]], [_gather_plan(0, 1, stage)]
        for n in names:
            if n == 'w_in':
                srcs.append(w_in_blk[l][None])
                shapes.append((N_DEV,) + w_in_blk.shape[1:])
                plans.append(_gather_plan(0, 1, stage))
            else:
                axis = BIG_SHARDED[n] - 1
                blk = w[n][l].astype(BF16)
                width = blk.shape[axis]
                srcs.append(blk)
                shapes.append(blk.shape[:axis] + (N_DEV * width,) + blk.shape[axis + 1:])
                plans.append(_gather_plan(axis, width, stage))
        return _split_start(srcs, shapes, plans, dep, "gather_start_" + tag)

    def forward_group(names, landed, dep, tag):
        plans = [_forward_plan(0, 1) if n == 'w_in' else _forward_plan(BIG_SHARDED[n] - 1, w[n].shape[BIG_SHARDED[n]])
                 for n in names]
        return _split_start(landed, [None] * len(names), plans, dep, "gather_start_" + tag)

    g0, tied = gather_group(dev, 0, "direct", w_in0_blocks, "rest0", with_taps=True)
    g1, tied = gather_group(big, 1, "first", tied, "layer1")
    layer1, full_taps = {}, {}

    def first_of(l):
        def first(after):
            if l == 0:
                return w_in0, tied
            layer1.update(zip(big, _split_wait(layer1.pop('forwarding'), after, "gather_wait_layer1_passed")))
            layer1['w_in'] = _join_cols(layer1['w_in'], blk_width, "join_w_in")
            return layer1['w_in'], None
        return first

    def rest_of(l):
        def rest(after):
            if l == 0:
                landed = _split_wait(g0, after, "gather_wait_rest0")
                full_taps.update({n: _join_blocks(blk, SMALL_SHARDED[n]) for n, blk in
                                  zip(small, _unpack(landed[0], [w[n].shape for n in small], lead=N_DEV))})
                got = dict(zip(dev, landed[1:]), **full_taps)
                landed = _split_wait(g1, after, "gather_wait_layer1")
                layer1['forwarding'], token = forward_group(big, landed, landed[0], "layer1_passed")
                got['conv_pw_w'] = got['conv_pw_w'] + token[0:1, 0:1].astype(BF16)
                return got
            return dict({n: layer1[n] for n in dev}, **full_taps)
        return rest

    saved = []
    act = xs
    for l in range(depth):
        act, sv = _layer_fwd(act, mems, p, l, first_of(l), rest_of(l))
        saved.append(sv)
    loss_part, dact, d_final = _loss_bwd(act, final_norm_g[None, :], tgt, "loss_bwd")

    def grad_window(n):
        axis = BIG_SHARDED[n] - 1
        blk = w[n].shape[axis + 1]
        pad = blk % LANE if axis == 1 else 0
        return axis, blk + pad, lambda px, py, pc: blk * (4 * px + 2 * py + pc) - pad * pc

    scattering = {}

    def scatter_group(names, arrs, l, tag):
        shapes, plans = [], []
        for n, g in zip(names, arrs):
            axis, width, start = grad_window(n)
            shapes.append((N_DEV,) + g.shape[:axis] + (width,) + g.shape[axis + 1:])
            plans.append(_scatter_plan(axis, width, start))
        scattering[(l, tag)], token = _split_start(arrs, shapes, plans, arrs[0], f"scatter_start_{tag}{l}")
        return token

    rest = [n for n in WEIGHTS if n not in BIG_SHARDED]
    early = [n for n in rest if n != 'mix_norm_g']
    small_pending, partial = {}, {}

    def start_small(names, dep, tag):
        vec = _pack([partial[n] for n in names], F32)
        small_pending[tag], token = _split_start([vec[None]], [(N_DEV,) + vec.shape], [_gather_plan(0, 1, "direct")],
                                                 dep, "small_grads_start_" + tag)
        return token

    layer_grads = [None] * depth

    def rest_ready(big_grads, g_small, l):
        token = scatter_group(dev, [big_grads[n] for n in dev], l, "rest")
        if l == 0:
            partial.update({n: jnp.stack([g_small[n], layer_grads[1][n]]) for n in early if n != 'final_norm_g'})
            partial.update(final_norm_g=d_final[0], loss=loss_part)
            token = start_small(early + ['loss'], token, "early")
        return token

    for l in reversed(range(depth)):
        dact, layer_grads[l] = _layer_bwd(
            dact, mems, p, l, saved[l], functools.partial(rest_ready, l=l),
            lambda g_w_in, l=l: scatter_group(['w_in'], [g_w_in], l, "w_in"))
    grad_x = dact[None]
    partial['mix_norm_g'] = jnp.stack([layer_grads[l]['mix_norm_g'] for l in range(depth)])
    tied = start_small(['mix_norm_g'], partial['mix_norm_g'], "late")

    grads, delta, new_m, new_v = {}, {}, {}, {}

    def update(n, received):
        summed = jnp.stack([_sum_blocks(received[l], "sum_" + n) for l in range(depth)])
        width = w[n].shape[-1]
        if summed.shape[-1] != width:
            summed = jnp.where(my_c == 0, summed[..., :width], summed[..., summed.shape[-1] - width:])
        grads[n] = summed
        shape = w[n].shape
        two_d = lambda a: a.reshape(-1, shape[-1])
        d, nm, nv = _adamw(two_d(w[n]), two_d(summed), two_d(mom[n]), two_d(var[n]), "adamw_" + n)
        delta[n], new_m[n], new_v[n] = d.reshape(shape), nm.reshape(shape), nv.reshape(shape)

    landed = [_split_wait(scattering[(l, "rest")], tied, f"scatter_wait_rest{l}") for l in range(depth)]
    for t, n in enumerate(dev):
        update(n, [landed[l][t] for l in range(depth)])
    def update_small(names, tag, after):
        sent = names + ['loss'] if tag == "early" else names
        total = _sum_blocks(_split_wait(small_pending[tag], after, "small_grads_wait_" + tag)[0], "sum_small_" + tag)
        grads.update(zip(sent, _unpack(total, [partial[n].shape for n in sent])))
        for n in names:
            if n in SMALL_SHARDED:
                width = w[n].shape[2]
                grads[n] = lax.dynamic_slice_in_dim(grads[n], me * width, width, axis=2)
        packed = [_pack([src[n] for n in names], F32) for src in (w, grads, mom, var)]
        outs = _adamw(*packed, "adamw_small_" + tag)
        for dst, o in zip((delta, new_m, new_v), outs):
            dst.update(dict(zip(names, _unpack(o, [w[n].shape for n in names]))))
        return outs[0]

    after = update_small(early, "early", sum(delta[n].reshape(-1)[:1] for n in dev))
    loss = grads.pop('loss')[0, 0]
    update('w_in', [_split_wait(scattering[(l, "w_in")], after, f"scatter_wait_w_in{l}")[0] for l in range(depth)])
    update_small(['mix_norm_g'], "late", delta['w_in'])

    return (loss, grad_x, *[grads[n] for n in WEIGHTS], *[delta[n] for n in WEIGHTS],
            *[new_m[n] for n in WEIGHTS], *[new_v[n] for n in WEIGHTS])
```

```python
import functools
import math

import jax
import jax.numpy as jnp
from jax import lax
from jax.experimental import pallas as pl
from jax.experimental.pallas import tpu as pltpu

F32 = jnp.float32
BF16 = jnp.bfloat16

N_DEV = 8
LANE = 128
HEAD = 128
VMEM_LIMIT = 56 * 1024 * 1024
PACK_COLS = 512
RMS_EPS = 1e-6
LN_EPS = 1e-5
LRU_C = 8.0
CONV_HALO = 32
LRU_HALO = 8

ADAM_LR, ADAM_B1, ADAM_B2, ADAM_EPS, ADAM_WD, ADAM_STEP = 0.001, 0.9, 0.999, 1e-08, 0.01, 10

MESH = pl.DeviceIdType.MESH


def _tile(n, cands):
    for c in cands:
        if n % c == 0:
            return c
    raise ValueError(f"no tile of {cands} divides {n}")


def _params(*sem):
    return pltpu.CompilerParams(dimension_semantics=sem, vmem_limit_bytes=VMEM_LIMIT)


def _dot(a, b):
    return lax.dot_general(a, b, (((1,), (0,)), ((), ())), preferred_element_type=F32)


def _dot_nt(a, b):
    return lax.dot_general(a, b, (((1,), (1,)), ((), ())), preferred_element_type=F32)


def _dot_tn(a, b):
    return lax.dot_general(a, b, (((0,), (0,)), ((), ())), preferred_element_type=F32)


def _sigmoid(x):
    return 1.0 / (1.0 + jnp.exp(-x))


def _expm1(x):
    series = x * (1.0 + x * (0.5 + x * (1.0 / 6.0 + x * (1.0 / 24.0))))
    return jnp.where(jnp.abs(x) < 0.05, series, jnp.exp(x) - 1.0)


def _my_place():
    return lax.axis_index("x"), lax.axis_index("y"), lax.axis_index("c")


def _flip(v, d):
    return 1 - v if d else v


def _window(ref, axis, start, size):
    return ref.at[tuple(pl.ds(start, size) if a == axis else pl.ds(0, ref.shape[a]) for a in range(len(ref.shape)))]


def _all_gather(x2d, name):
    rows, cols = x2d.shape

    def body(x_ref, out_ref, send_sems, recv_sems, local_sem):
        x, y, c = _my_place()
        me, sibling = (x, y, c), (x, y, 1 - c)
        chips = [(1 - x, y), (x, 1 - y), (1 - x, 1 - y)]

        def blk(px, py, pc):
            return out_ref.at[4 * px + 2 * py + pc]

        def copy(k, block, to, src=None):
            return pltpu.make_async_remote_copy(
                src_ref=blk(*block) if src is None else src, dst_ref=blk(*block),
                send_sem=send_sems.at[k], recv_sem=recv_sems.at[k], device_id=to, device_id_type=MESH)

        mine = pltpu.make_async_copy(x_ref, blk(*me), local_sem)
        mine.start()
        first = [copy(0, me, sibling, src=x_ref)]
        first += [copy(1 + j, me, (*chip, c), src=x_ref) for j, chip in enumerate(chips)]
        for cp in first:
            cp.start()
        passed = [copy(4 + j, (*chip, c), sibling) for j, chip in enumerate(chips)]
        for j, chip in enumerate(chips):
            copy(1 + j, (*chip, c), me).wait_recv()
            passed[j].start()
        copy(0, sibling, me).wait_recv()
        for j, chip in enumerate(chips):
            copy(4 + j, (*chip, 1 - c), me).wait_recv()
        for cp in first + passed:
            cp.wait_send()
        mine.wait()

    return pl.pallas_call(
        body, name=name,
        out_shape=jax.ShapeDtypeStruct((N_DEV, rows, cols), x2d.dtype),
        in_specs=[pl.BlockSpec(memory_space=pl.ANY)],
        out_specs=pl.BlockSpec(memory_space=pl.ANY),
        scratch_shapes=[pltpu.SemaphoreType.DMA((7,)), pltpu.SemaphoreType.DMA((7,)), pltpu.SemaphoreType.DMA],
    )(x2d)


HBM_SPEC = pl.BlockSpec(memory_space=pltpu.HBM)
SEM_SPEC = pl.BlockSpec(memory_space=pltpu.SEMAPHORE)
SPLIT_COPY = pltpu.SideEffectType.DATAFLOW_SIDE_EFFECTING


class _Plan:
    def __init__(self, copies, own, total):
        self.copies, self.own, self.total = copies, own, total


def _scatter_plan(axis, width, start):
    def win(src, px, py, pc):
        return _window(src, axis, pl.multiple_of(start(px, py, pc), math.gcd(width, 1024)), width)

    def copies(src, zone, x, y, c):
        out = []
        for k in range(1, N_DEV):
            px, py, pc = _flip(x, k & 4), _flip(y, k & 2), _flip(c, k & 1)
            out.append((win(src, px, py, pc), zone.at[4 * x + 2 * y + c], (px, py, pc)))
        return out

    return _Plan(copies, lambda src, zone, x, y, c: (win(src, x, y, c), zone.at[4 * x + 2 * y + c]),
                 lambda zone: zone.at[pl.ds(0, N_DEV - 1)])


def _gather_plan(axis, width, stage):
    flips = range(1, N_DEV) if stage == "direct" else (1, 2, 4, 6)

    def mine(zone, x, y, c):
        return _window(zone, axis, width * (4 * x + 2 * y + c), width)

    def copies(src, zone, x, y, c):
        return [(src, mine(zone, x, y, c), (_flip(x, k & 4), _flip(y, k & 2), _flip(c, k & 1))) for k in flips]

    return _Plan(copies, lambda src, zone, x, y, c: (src, mine(zone, x, y, c)),
                 lambda zone: _window(zone, axis, 0, len(flips) * width))


def _forward_plan(axis, width):
    def copies(src, zone, x, y, c):
        out = []
        for px, py in ((1 - x, y), (x, 1 - y), (1 - x, 1 - y)):
            win = _window(zone, axis, width * (4 * px + 2 * py + c), width)
            out.append((win, win, (x, y, 1 - c)))
        return out

    return _Plan(copies, lambda src, zone, x, y, c: None, lambda zone: _window(zone, axis, 0, 3 * width))


def _split_start(srcs, zone_shapes, plans, dep, name):
    n = len(srcs)
    zones = [lax.empty(shape, a.dtype) for shape, a in zip(zone_shapes, srcs) if shape is not None]
    m = len(zones)
    zone_of = [None if shape is None else sum(s is not None for s in zone_shapes[:t])
               for t, shape in enumerate(zone_shapes)]

    def body(*refs):
        ins, fresh = refs[:n], refs[n:n + m]
        sems, token = refs[n + m + 1:n + m + 1 + 3 * n], refs[-1]
        send_sems, recv_sems, local_sems = sems[:n], sems[n:2 * n], sems[2 * n:]
        x, y, c = _my_place()
        for t in range(n):
            land = ins[t] if zone_of[t] is None else fresh[zone_of[t]]
            own = plans[t].own(ins[t], land, x, y, c)
            if own is not None:
                pltpu.make_async_copy(*own, local_sems[t]).start()
            for src, dst, target in plans[t].copies(ins[t], land, x, y, c):
                pltpu.make_async_remote_copy(src_ref=src, dst_ref=dst, send_sem=send_sems[t], recv_sem=recv_sems[t],
                                             device_id=target, device_id_type=MESH).start()
        token[...] = jnp.zeros_like(token)

    thru = [pltpu.HBM(a.shape, a.dtype) for a in list(srcs) + zones]
    outs = pl.pallas_call(
        body, name=name,
        out_shape=[pltpu.SemaphoreType.DMA(())] * (3 * n) + thru + [jax.ShapeDtypeStruct((8, LANE), F32)],
        in_specs=[HBM_SPEC] * (n + m) + [pl.BlockSpec(memory_space=pl.ANY)],
        out_specs=[SEM_SPEC] * (3 * n) + [HBM_SPEC] * (n + m) + [pl.BlockSpec(memory_space=pltpu.VMEM)],
        input_output_aliases={i: 3 * n + i for i in range(n + m)},
        compiler_params=pltpu.CompilerParams(has_side_effects=SPLIT_COPY),
    )(*[pltpu.with_memory_space_constraint(a, pltpu.HBM) for a in list(srcs) + zones], dep)
    return (outs[:3 * n], outs[3 * n:4 * n], outs[4 * n:4 * n + m], plans, zone_of), outs[-1]


def _split_wait(pending, after, name):
    sems, srcs, zones, plans, zone_of = pending
    n, m = len(srcs), len(zones)

    def body(*refs):
        ins, fresh, sems = refs[:n], refs[n:n + m], refs[n + m:n + m + 3 * n]
        send_sems, recv_sems, local_sems = sems[:n], sems[n:2 * n], sems[2 * n:]
        x, y, c = _my_place()
        for t in range(n):
            land = ins[t] if zone_of[t] is None else fresh[zone_of[t]]
            total = plans[t].total(land)
            done = pltpu.make_async_remote_copy(src_ref=total, dst_ref=total, send_sem=send_sems[t],
                                                recv_sem=recv_sems[t], device_id=(x, y, 1 - c), device_id_type=MESH)
            done.wait_send()
            done.wait_recv()
            own = plans[t].own(ins[t], land, x, y, c)
            if own is not None:
                pltpu.make_async_copy(*own, local_sems[t]).wait()

    thru = [pltpu.HBM(a.shape, a.dtype) for a in list(srcs) + list(zones)]
    outs = pl.pallas_call(
        body, name=name, out_shape=thru,
        in_specs=[HBM_SPEC] * (n + m) + [SEM_SPEC] * (3 * n) + [pl.BlockSpec(memory_space=pl.ANY)],
        out_specs=[HBM_SPEC] * (n + m), input_output_aliases={i: i for i in range(n + m)},
        compiler_params=pltpu.CompilerParams(has_side_effects=SPLIT_COPY),
    )(*srcs, *zones, *sems, after)
    return [outs[t] if zone_of[t] is None else outs[n + zone_of[t]] for t in range(n)]


def _join_cols(blocks, width, name):
    nb, rows, padded = blocks.shape
    tr = _tile(rows, (256, 128, 64, 32, 16))

    def body(x_ref, o_ref):
        for b in range(nb):
            o_ref[:, b * width:(b + 1) * width] = x_ref[b, :, 0:width]

    return pl.pallas_call(
        body, name=name, out_shape=jax.ShapeDtypeStruct((rows, nb * width), blocks.dtype), grid=(rows // tr,),
        in_specs=[pl.BlockSpec((nb, tr, padded), lambda i: (0, i, 0))],
        out_specs=pl.BlockSpec((tr, nb * width), lambda i: (i, 0)), compiler_params=_params("parallel"),
    )(blocks)


def _sum_blocks(x3d, name):
    n, rows, cols = x3d.shape
    tr = _tile(rows, (512, 256, 128, 64, 32, 16))

    def body(x_ref, o_ref):
        acc = x_ref[0].astype(F32)
        for j in range(1, n):
            acc = acc + x_ref[j].astype(F32)
        o_ref[...] = acc

    return pl.pallas_call(
        body, name=name, out_shape=jax.ShapeDtypeStruct((rows, cols), F32), grid=(rows // tr,),
        in_specs=[pl.BlockSpec((n, tr, cols), lambda i: (0, i, 0))],
        out_specs=pl.BlockSpec((tr, cols), lambda i: (i, 0)),
        compiler_params=_params("parallel"),
    )(x3d)


def _pack(arrs, dtype, lead=None):
    if lead is None:
        flat = jnp.concatenate([a.reshape(-1).astype(dtype) for a in arrs])
        n = flat.shape[0]
        total = -(-n // (16 * PACK_COLS)) * (16 * PACK_COLS)
        return jnp.pad(flat, (0, total - n)).reshape(-1, PACK_COLS)
    flat = jnp.concatenate([a.reshape(lead, -1).astype(dtype) for a in arrs], axis=1)
    n = flat.shape[1]
    total = -(-n // (16 * PACK_COLS)) * (16 * PACK_COLS)
    return jnp.pad(flat, ((0, 0), (0, total - n))).reshape(lead, -1, PACK_COLS)


def _unpack(packed, shapes, lead=None):
    out, off = [], 0
    if lead is None:
        flat = packed.reshape(-1)
        for s in shapes:
            n = math.prod(s)
            out.append(flat[off:off + n].reshape(s))
            off += n
        return out
    flat = packed.reshape(lead, -1)
    for s in shapes:
        n = math.prod(s)
        out.append(flat[:, off:off + n].reshape((lead,) + tuple(s)))
        off += n
    return out


def _join_blocks(g, axis):
    g = jnp.moveaxis(g, 0, axis)
    s = g.shape
    return g.reshape(s[:axis] + (s[axis] * s[axis + 1],) + s[axis + 2:])


def _mm_tiles(m, n, kdim, a_bytes):
    tk = kdim if kdim <= 2048 else _tile(kdim, (2048, 1664, 1024, 832, 512, 416, 256, 128))
    tm = _tile(m, (1024, 512, 256, 128, 64, 32, 16))
    tn = _tile(n, (1024, 512, 256, 128))

    def vmem(tm, tn):
        return 2 * tm * tk * a_bytes + 2 * tn * tk * 2 + 3 * tm * tn * 4

    while vmem(tm, tn) > VMEM_LIMIT * 3 // 4 and tn > 128 and tn % 256 == 0:
        tn //= 2
    while vmem(tm, tn) > VMEM_LIMIT * 3 // 4 and tm > 128 and tm % 256 == 0:
        tm //= 2
    return tm, tn, tk


def _mm(a, b, *, ta=False, tb=False, bl=None, out_dtype=F32, add=None, dep=None, name):
    if ta:
        kdim, m = a.shape
    else:
        m, kdim = a.shape
    bshape = b.shape if bl is None else b.shape[1:]
    n = bshape[0] if tb else bshape[1]
    tm, tn, tk = _mm_tiles(m, n, kdim, a.dtype.itemsize)
    nk = kdim // tk
    a_spec = pl.BlockSpec((tk, tm), lambda i, j, k: (k, i)) if ta else pl.BlockSpec((tm, tk), lambda i, j, k: (i, k))
    b_blk, b_idx = ((tn, tk), lambda i, j, k: (j, k)) if tb else ((tk, tn), lambda i, j, k: (k, j))
    if bl is None:
        b_spec = pl.BlockSpec(b_blk, b_idx)
    else:
        b_spec = pl.BlockSpec((None,) + b_blk, lambda i, j, k: (bl,) + b_idx(i, j, k))
    o_spec = pl.BlockSpec((tm, tn), lambda i, j, k: (i, j))
    dims = (((0 if ta else 1,), (1 if tb else 0,)), ((), ()))

    n_in = 2 + (add is not None) + (dep is not None)

    def body(*refs):
        a_ref, b_ref = refs[:2]
        add_ref = refs[2] if add is not None else None
        o_ref = refs[n_in]

        def finish(r):
            if add is not None:
                r = r + add_ref[...]
            o_ref[...] = r.astype(out_dtype)

        part = lax.dot_general(a_ref[...].astype(BF16), b_ref[...].astype(BF16), dims, preferred_element_type=F32)
        if nk == 1:
            finish(part)
            return
        acc_ref = refs[-1]
        k = pl.program_id(2)

        @pl.when(k == 0)
        def _():
            acc_ref[...] = part

        @pl.when(k > 0)
        def _():
            acc_ref[...] += part

        @pl.when(k == nk - 1)
        def _():
            finish(acc_ref[...])

    ins, specs = [a, b], [a_spec, b_spec]
    if add is not None:
        ins.append(add)
        specs.append(o_spec)
    if dep is not None:
        ins.append(dep)
        specs.append(pl.BlockSpec(memory_space=pl.ANY))
    return pl.pallas_call(
        body, name=name, out_shape=jax.ShapeDtypeStruct((m, n), out_dtype), grid=(m // tm, n // tn, nk),
        in_specs=specs, out_specs=o_spec, scratch_shapes=[pltpu.VMEM((tm, tn), F32)] if nk > 1 else [],
        compiler_params=_params("parallel", "parallel", "arbitrary"),
    )(*ins)


def _rms_fwd(x, g, name):
    s, d = x.shape
    tr = _tile(s, (256, 128, 64, 32, 16))

    def body(x_ref, g_ref, o_ref):
        xv = x_ref[...]
        r = lax.rsqrt(jnp.mean(xv * xv, axis=-1, keepdims=True) + RMS_EPS)
        o_ref[...] = (xv * r * g_ref[...]).astype(BF16)

    return pl.pallas_call(
        body, name=name, out_shape=jax.ShapeDtypeStruct((s, d), BF16), grid=(s // tr,),
        in_specs=[pl.BlockSpec((tr, d), lambda i: (i, 0)), pl.BlockSpec((1, d), lambda i: (0, 0))],
        out_specs=pl.BlockSpec((tr, d), lambda i: (i, 0)), compiler_params=_params("parallel"),
    )(x, g)


def _rms_bwd(x, g, dh, resid, name):
    s, d = x.shape
    tr = _tile(s, (256, 128, 64, 32, 16))

    def body(*refs):
        if resid is None:
            x_ref, g_ref, dh_ref, dx_ref, dg_ref = refs
        else:
            x_ref, g_ref, dh_ref, res_ref, dx_ref, dg_ref = refs

        @pl.when(pl.program_id(0) == 0)
        def _():
            dg_ref[...] = jnp.zeros_like(dg_ref)

        xv = x_ref[...]
        r = lax.rsqrt(jnp.mean(xv * xv, axis=-1, keepdims=True) + RMS_EPS)
        xh = xv * r
        dhv = dh_ref[...]
        dg_ref[0:1, :] += jnp.sum(dhv * xh, axis=0, keepdims=True)
        dyn = dhv * g_ref[...]
        dx = r * (dyn - xh * jnp.mean(dyn * xh, axis=-1, keepdims=True))
        if resid is not None:
            dx = dx + res_ref[...]
        dx_ref[...] = dx

    row = pl.BlockSpec((tr, d), lambda i: (i, 0))
    ins = [x, g, dh] + ([] if resid is None else [resid])
    specs = [row, pl.BlockSpec((1, d), lambda i: (0, 0)), row] + ([] if resid is None else [row])
    dx, dg = pl.pallas_call(
        body, name=name,
        out_shape=(jax.ShapeDtypeStruct((s, d), F32), jax.ShapeDtypeStruct((8, d), F32)), grid=(s // tr,),
        in_specs=specs, out_specs=(row, pl.BlockSpec((8, d), lambda i: (0, 0))),
        compiler_params=_params("arbitrary"),
    )(*ins)
    return dx, dg[0]


def _loss_bwd(x, g, tgt, name):
    s, d = x.shape
    tr = _tile(s, (256, 128, 64, 32, 16))

    def body(x_ref, g_ref, t_ref, dx_ref, dg_ref, loss_ref):
        @pl.when(pl.program_id(0) == 0)
        def _():
            dg_ref[...] = jnp.zeros_like(dg_ref)
            loss_ref[...] = jnp.zeros_like(loss_ref)

        xv = x_ref[...]
        r = lax.rsqrt(jnp.mean(xv * xv, axis=-1, keepdims=True) + RMS_EPS)
        xh = xv * r
        e = xh * g_ref[...] - t_ref[...]
        per_tok = jnp.mean(e * e, axis=-1, keepdims=True)
        loss_ref[...] += 0.5 * jnp.sum(per_tok, axis=0, keepdims=True)
        dy = e * (1.0 / d)
        dg_ref[0:1, :] += jnp.sum(dy * xh, axis=0, keepdims=True)
        dyn = dy * g_ref[...]
        dx_ref[...] = r * (dyn - xh * jnp.mean(dyn * xh, axis=-1, keepdims=True))

    row = pl.BlockSpec((tr, d), lambda i: (i, 0))
    dx, dg, loss = pl.pallas_call(
        body, name=name,
        out_shape=(jax.ShapeDtypeStruct((s, d), F32), jax.ShapeDtypeStruct((8, d), F32),
                   jax.ShapeDtypeStruct((8, LANE), F32)),
        grid=(s // tr,),
        in_specs=[row, pl.BlockSpec((1, d), lambda i: (0, 0)), row],
        out_specs=(row, pl.BlockSpec((8, d), lambda i: (0, 0)), pl.BlockSpec((8, LANE), lambda i: (0, 0))),
        compiler_params=_params("arbitrary"),
    )(x, g, tgt)
    return loss[0:1, 0:1], dx, dg[0:1]


def _conv_taps(gbuf, w_ref, tt, ntap, lo):
    acc = w_ref[0:1, :] * gbuf[pl.ds(lo, tt), :]
    for k in range(1, ntap):
        acc = acc + w_ref[k:k + 1, :] * gbuf[pl.ds(lo + k, tt), :]
    return acc


def _conv_time_tile(s):
    return _tile(s, (256, 128, 64, 32))


def _conv_fwd(u, wpad, dw_b, ln_g, ln_b, pw, l, name):
    s = u.shape[0]
    c = pw.shape[1]
    ntap = 31
    tt = _conv_time_tile(s)
    hb = tt // CONV_HALO

    def body(val_ref, glu_ref, valh_ref, gluh_ref, w_ref, b_ref, lg_ref, lb_ref, pw_ref, o_ref, gbuf):
        i = pl.program_id(0)
        glh = valh_ref[...] * _sigmoid(gluh_ref[...])
        gbuf[0:CONV_HALO, :] = jnp.where(i > 0, glh, 0.0)
        gbuf[CONV_HALO:CONV_HALO + tt, :] = val_ref[...] * _sigmoid(glu_ref[...])
        acc = _conv_taps(gbuf, w_ref, tt, ntap, CONV_HALO - (ntap - 1)) + b_ref[...]
        xc = acc - jnp.mean(acc, axis=-1, keepdims=True)
        rstd = lax.rsqrt(jnp.mean(xc * xc, axis=-1, keepdims=True) + LN_EPS)
        ln = xc * rstd * lg_ref[...] + lb_ref[...]
        sw = ln * _sigmoid(ln)
        o_ref[...] = _dot(sw.astype(BF16), pw_ref[...])

    vec = pl.BlockSpec((1, c), lambda i: (0, 0))
    return pl.pallas_call(
        body, name=name, out_shape=jax.ShapeDtypeStruct((s, c), F32), grid=(s // tt,),
        in_specs=[pl.BlockSpec((tt, c), lambda i: (i, 0)), pl.BlockSpec((tt, c), lambda i: (i, 1)),
                  pl.BlockSpec((CONV_HALO, c), lambda i: (jnp.maximum(i * hb - 1, 0), 0)),
                  pl.BlockSpec((CONV_HALO, c), lambda i: (jnp.maximum(i * hb - 1, 0), 1)),
                  pl.BlockSpec((32, c), lambda i: (0, 0)), vec, vec, vec,
                  pl.BlockSpec((None, c, c), lambda i: (l, 0, 0))],
        out_specs=pl.BlockSpec((tt, c), lambda i: (i, 0)),
        scratch_shapes=[pltpu.VMEM((CONV_HALO + tt, c), F32)],
        compiler_params=_params("parallel"),
    )(u, u, u, u, wpad, dw_b, ln_g, ln_b, pw)


def _conv_bwd_post(u, dyc, wpad, dw_b, ln_g, ln_b, pw, l, name):
    s = u.shape[0]
    c = pw.shape[1]
    ntap = 31
    tt = _conv_time_tile(s)
    hb = tt // CONV_HALO

    def body(val_ref, glu_ref, valh_ref, gluh_ref, dy_ref, w_ref, b_ref, lg_ref, lb_ref, pw_ref,
             dd_ref, gl_ref, dpw_ref, vec_ref, gbuf):
        i = pl.program_id(0)

        @pl.when(i == 0)
        def _():
            dpw_ref[...] = jnp.zeros_like(dpw_ref)
            vec_ref[...] = jnp.zeros_like(vec_ref)

        glh = valh_ref[...] * _sigmoid(gluh_ref[...])
        gbuf[0:CONV_HALO, :] = jnp.where(i > 0, glh, 0.0)
        gl = val_ref[...] * _sigmoid(glu_ref[...])
        gbuf[CONV_HALO:CONV_HALO + tt, :] = gl
        gl_ref[...] = gl
        acc = _conv_taps(gbuf, w_ref, tt, ntap, CONV_HALO - (ntap - 1)) + b_ref[...]
        xc = acc - jnp.mean(acc, axis=-1, keepdims=True)
        rstd = lax.rsqrt(jnp.mean(xc * xc, axis=-1, keepdims=True) + LN_EPS)
        xh = xc * rstd
        ln = xh * lg_ref[...] + lb_ref[...]
        sig = _sigmoid(ln)
        sw = ln * sig
        dyb = dy_ref[...].astype(BF16)
        dpw_ref[...] += _dot_tn(sw.astype(BF16), dyb)
        dsw = _dot_nt(dyb, pw_ref[...])
        dln = dsw * (sig * (1.0 + ln * (1.0 - sig)))
        vec_ref[0:1, :] += jnp.sum(dln * xh, axis=0, keepdims=True)
        vec_ref[1:2, :] += jnp.sum(dln, axis=0, keepdims=True)
        dxh = dln * lg_ref[...]
        dd = rstd * (dxh - jnp.mean(dxh, axis=-1, keepdims=True)
                     - xh * jnp.mean(dxh * xh, axis=-1, keepdims=True))
        vec_ref[2:3, :] += jnp.sum(dd, axis=0, keepdims=True)
        dd_ref[...] = dd

    vec = pl.BlockSpec((1, c), lambda i: (0, 0))
    tile = pl.BlockSpec((tt, c), lambda i: (i, 0))
    return pl.pallas_call(
        body, name=name,
        out_shape=(jax.ShapeDtypeStruct((s, c), F32), jax.ShapeDtypeStruct((s, c), F32),
                   jax.ShapeDtypeStruct((c, c), F32), jax.ShapeDtypeStruct((8, c), F32)),
        grid=(s // tt,),
        in_specs=[tile, pl.BlockSpec((tt, c), lambda i: (i, 1)),
                  pl.BlockSpec((CONV_HALO, c), lambda i: (jnp.maximum(i * hb - 1, 0), 0)),
                  pl.BlockSpec((CONV_HALO, c), lambda i: (jnp.maximum(i * hb - 1, 0), 1)),
                  tile, pl.BlockSpec((32, c), lambda i: (0, 0)), vec, vec, vec,
                  pl.BlockSpec((None, c, c), lambda i: (l, 0, 0))],
        out_specs=(tile, tile, pl.BlockSpec((c, c), lambda i: (0, 0)), pl.BlockSpec((8, c), lambda i: (0, 0))),
        scratch_shapes=[pltpu.VMEM((CONV_HALO + tt, c), F32)],
        compiler_params=_params("arbitrary"),
    )(u, u, u, u, dyc, wpad, dw_b, ln_g, ln_b, pw)


def _conv_bwd_dw(u, dd, gl, wpad, name):
    s, c = dd.shape
    ntap = 31
    tt = _conv_time_tile(s)
    hb = tt // CONV_HALO
    nt = s // tt
    last_halo = s // CONV_HALO - 1

    def body(val_ref, glu_ref, dd_ref, ddn_ref, gl_ref, glh_ref, w_ref, dval_ref, dglu_ref, dw_ref, dbuf, gbuf):
        i = pl.program_id(0)

        @pl.when(i == 0)
        def _():
            dw_ref[...] = jnp.zeros_like(dw_ref)

        d = dd_ref[...]
        dbuf[0:tt, :] = d
        dbuf[tt:tt + CONV_HALO, :] = jnp.where(i < nt - 1, ddn_ref[...], 0.0)
        gbuf[0:CONV_HALO, :] = jnp.where(i > 0, glh_ref[...], 0.0)
        gbuf[CONV_HALO:CONV_HALO + tt, :] = gl_ref[...]
        for lo in range(0, c, LANE):
            ch = slice(lo, lo + LANE)
            dgl = w_ref[0:1, ch] * dbuf[pl.ds(ntap - 1, tt), ch]
            for k in range(1, ntap):
                dgl = dgl + w_ref[k:k + 1, ch] * dbuf[pl.ds(ntap - 1 - k, tt), ch]
            dc = dd_ref[:, ch]
            for k in range(ntap):
                dw_ref[k:k + 1, ch] += jnp.sum(dc * gbuf[pl.ds(CONV_HALO - (ntap - 1) + k, tt), ch], axis=0,
                                               keepdims=True)
            sg = _sigmoid(glu_ref[:, ch])
            dval_ref[:, ch] = (dgl * sg).astype(BF16)
            dglu_ref[:, ch] = (dgl * val_ref[:, ch] * sg * (1.0 - sg)).astype(BF16)

    tile = pl.BlockSpec((tt, c), lambda i: (i, 0))
    return pl.pallas_call(
        body, name=name,
        out_shape=(jax.ShapeDtypeStruct((s, c), BF16), jax.ShapeDtypeStruct((s, c), BF16),
                   jax.ShapeDtypeStruct((32, c), F32)),
        grid=(nt,),
        in_specs=[tile, pl.BlockSpec((tt, c), lambda i: (i, 1)), tile,
                  pl.BlockSpec((CONV_HALO, c), lambda i: (jnp.minimum((i + 1) * hb, last_halo), 0)),
                  tile, pl.BlockSpec((CONV_HALO, c), lambda i: (jnp.maximum(i * hb - 1, 0), 0)),
                  pl.BlockSpec((32, c), lambda i: (0, 0))],
        out_specs=(tile, tile, pl.BlockSpec((32, c), lambda i: (0, 0))),
        scratch_shapes=[pltpu.VMEM((tt + CONV_HALO, c), F32), pltpu.VMEM((CONV_HALO + tt, c), F32)],
        compiler_params=_params("arbitrary"),
    )(u, u, dd, dd, gl, gl, wpad)


SB_ROWS = 64


def _tri(n, cmp):
    r = lax.broadcasted_iota(jnp.int32, (n, n), 0)
    c = lax.broadcasted_iota(jnp.int32, (n, n), 1)
    return jnp.where(cmp(r, c), 1.0, 0.0).astype(BF16)


def _row_chunks(fn, n, *arrs):
    outs = [fn(*[a[r:r + SB_ROWS] for a in arrs]) for r in range(0, n, SB_ROWS)]
    return tuple(jnp.concatenate(list(o), axis=0) for o in zip(*outs))


def _hi_lo(v):
    hi = v.astype(BF16)
    return hi, (v - hi.astype(F32)).astype(BF16)


def _sb_sticks(z, causal):
    l1p = jnp.log(1.0 + jnp.exp(-jnp.abs(z)))
    lb = jnp.minimum(z, 0.0) - l1p
    ell = lb - z
    if causal is not None:
        ell = jnp.where(causal, ell, 0.0)
    hi, lo = _hi_lo(ell)
    return lb, hi, lo, jnp.sum(ell, axis=1, keepdims=True)


def _sb_fwd(u, heads, q_blk, k_blk, v_blk, name):
    s = u.shape[0]
    tq = _tile(s, (256, 128))
    scale = HEAD ** -0.5

    def body(q_ref, k_ref, v_ref, tri_ref, o_ref, tot_ref, kb_ref, vb_ref):
        i = pl.program_id(1)

        @pl.when(i == 0)
        def _():
            kb_ref[...] = k_ref[...].astype(BF16)
            vb_ref[...] = v_ref[...].astype(BF16)

        qb = (q_ref[...] * scale).astype(BF16)
        below_diag = lax.broadcasted_iota(jnp.int32, (tq, tq), 1) < lax.broadcasted_iota(jnp.int32, (tq, tq), 0)

        def blocks(j0, nb, c_a, acc, diag):
            mask = [below_diag] if diag else []
            rows = pl.ds(pl.multiple_of(j0 * tq, tq), nb * tq)
            kb = kb_ref[rows, :]
            vb = vb_ref[rows, :]
            z = _dot_nt(qb, kb)
            t_sfx = tri_ref[...]

            def sticks(zc, *m):
                out = []
                for b in range(nb):
                    out += _sb_sticks(zc[:, b * tq:(b + 1) * tq], m[0] if m and b == nb - 1 else None)
                return tuple(out)

            st = _row_chunks(sticks, tq, z, *mask)
            lb, hi, lo, rs = st[0::4], st[1::4], st[2::4], st[3::4]
            sfx = [_dot(hi[b], t_sfx) + _dot(lo[b], t_sfx) for b in range(nb)]
            before, run = [None] * nb, c_a
            for b in reversed(range(nb)):
                before[b], run = run, run + rs[b]

            def weights(*a):
                ws = []
                for b in range(nb):
                    lbc, sfxc, befc = a[3 * b:3 * b + 3]
                    w = jnp.exp(lbc + (befc + sfxc))
                    if diag and b == nb - 1:
                        w = jnp.where(a[-1], w, 0.0)
                    ws.append(w.astype(BF16))
                return (ws[0] if nb == 1 else jnp.concatenate(ws, axis=1),)

            flat = [v for b in range(nb) for v in (lb[b], sfx[b], before[b])]
            wb, = _row_chunks(weights, tq, *flat, *mask)
            return run, acc + _dot(wb, vb)

        carry = (jnp.zeros((tq, 1), F32), jnp.zeros((tq, HEAD), F32))
        has_nbr = jnp.minimum(i, 1)
        carry = lax.fori_loop(0, has_nbr, lambda _, cr: blocks(i - 1, 2, *cr, True), carry)
        carry = lax.fori_loop(0, 1 - has_nbr, lambda _, cr: blocks(0, 1, *cr, True), carry)
        r = jnp.maximum(i - 1, 0)
        carry = lax.fori_loop(0, r // 4, lambda t, cr: blocks(r - 4 - 4 * t, 4, *cr, False), carry)
        carry = lax.fori_loop(0, (r % 4) // 2, lambda _, cr: blocks(r % 2, 2, *cr, False), carry)
        c_a, acc = lax.fori_loop(0, r % 2, lambda _, cr: blocks(0, 1, *cr, False), carry)
        o_ref[...] = acc
        tot_ref[...] = jnp.broadcast_to(c_a, (tq, HEAD))

    full = lambda off: pl.BlockSpec((s, HEAD), lambda h, i: (0, off + h))
    out = pl.BlockSpec((tq, HEAD), lambda h, i: (i, h))
    return pl.pallas_call(
        body, name=name,
        out_shape=(jax.ShapeDtypeStruct((s, heads * HEAD), F32), jax.ShapeDtypeStruct((s, heads * HEAD), F32)),
        grid=(heads, s // tq),
        in_specs=[pl.BlockSpec((tq, HEAD), lambda h, i: (i, q_blk + h)), full(k_blk), full(v_blk),
                  pl.BlockSpec((tq, tq), lambda h, i: (0, 0))],
        out_specs=(out, out),
        scratch_shapes=[pltpu.VMEM((s, HEAD), BF16), pltpu.VMEM((s, HEAD), BF16)],
        compiler_params=_params("parallel", "arbitrary"),
    )(u, u, u, _tri(tq, lambda r, c: r > c))


def _sb_bwd(u, tot, dy, heads, q_blk, k_blk, v_blk, dep, name):
    s = u.shape[0]
    tq = _tile(s, (256, 128))
    scale = HEAD ** -0.5

    def body(q_ref, k_ref, v_ref, tot_ref, dy_ref, incl_ref, excl_ref, dep_ref, dq_ref, dk_ref, dv_ref,
             kb_ref, vb_ref):
        i = pl.program_id(1)

        @pl.when(i == 0)
        def _():
            dk_ref[...] = jnp.zeros_like(dk_ref)
            dv_ref[...] = jnp.zeros_like(dv_ref)
            kb_ref[...] = k_ref[...].astype(BF16)
            vb_ref[...] = v_ref[...].astype(BF16)

        qb = (q_ref[...] * scale).astype(BF16)
        dob = dy_ref[...].astype(BF16)
        total = tot_ref[:, 0:1]
        below_diag = lax.broadcasted_iota(jnp.int32, (tq, tq), 1) < lax.broadcasted_iota(jnp.int32, (tq, tq), 0)

        def blocks(j0, nb, c_p, c_g, dq, diag):
            mask = [below_diag] if diag else []
            rows = pl.ds(pl.multiple_of(j0 * tq, tq), nb * tq)
            kb = kb_ref[rows, :]
            vb = vb_ref[rows, :]
            z = _dot_nt(qb, kb)
            dw = _dot_nt(dob, vb)
            t_incl, t_excl = incl_ref[...], excl_ref[...]
            cols = lambda a, b: a[:, b * tq:(b + 1) * tq]

            def sticks(zc, *m):
                out = []
                for b in range(nb):
                    out += _sb_sticks(cols(zc, b), m[0] if m and b == nb - 1 else None)
                return tuple(out)

            st = _row_chunks(sticks, tq, z, *mask)
            lb, hi, lo, rs_l = st[0::4], st[1::4], st[2::4], st[3::4]
            pfx = [_dot(hi[b], t_incl) + _dot(lo[b], t_incl) for b in range(nb)]
            p_before = [c_p]
            for b in range(nb):
                p_before.append(p_before[-1] + rs_l[b])

            def weights(totc, dwc, *a):
                out = []
                for b in range(nb):
                    lbc, pfxc, pbc = a[3 * b:3 * b + 3]
                    w = jnp.exp(lbc + (totc - (pbc + pfxc)))
                    if diag and b == nb - 1:
                        w = jnp.where(a[-1], w, 0.0)
                    g = w * cols(dwc, b)
                    out += [w.astype(BF16), g, g.astype(BF16), jnp.sum(g, axis=1, keepdims=True)]
                return tuple(out)

            flat = [v for b in range(nb) for v in (lb[b], pfx[b], p_before[b])]
            wt = _row_chunks(weights, tq, total, dw, *flat, *mask)
            wb, g, gb, rs_g = wt[0::4], wt[1::4], wt[2::4], wt[3::4]
            g_pre = [_dot(gb[b], t_excl) for b in range(nb)]
            g_before = [c_g]
            for b in range(nb):
                g_before.append(g_before[-1] + rs_g[b])

            def dscore(*a):
                dzs = []
                for b in range(nb):
                    lbc, gc, gprec, gbc = a[4 * b:4 * b + 4]
                    beta = jnp.exp(lbc)
                    dz = gc * (1.0 - beta) - (gbc + gprec) * beta
                    if diag and b == nb - 1:
                        dz = jnp.where(a[-1], dz, 0.0)
                    dzs.append(dz.astype(BF16))
                return (dzs[0] if nb == 1 else jnp.concatenate(dzs, axis=1),)

            flat = [v for b in range(nb) for v in (lb[b], g[b], g_pre[b], g_before[b])]
            dzb, = _row_chunks(dscore, tq, *flat, *mask)
            wcat = wb[0] if nb == 1 else jnp.concatenate(wb, axis=1)
            dk_ref[rows, :] += _dot_tn(dzb, qb)
            dv_ref[rows, :] += _dot_tn(wcat, dob)
            return p_before[-1], g_before[-1], dq + _dot(dzb, kb)

        zero = jnp.zeros((tq, 1), F32)
        carry = lax.fori_loop(0, i // 2, lambda t, cr: blocks(2 * t, 2, *cr, False),
                              (zero, zero, jnp.zeros((tq, HEAD), F32)))
        carry = lax.fori_loop(0, i % 2, lambda _, cr: blocks(i - 1, 2, *cr, True), carry)
        _, _, dq = lax.fori_loop(0, 1 - i % 2, lambda _, cr: blocks(i, 1, *cr, True), carry)
        dq_ref[...] = dq * scale

    full = lambda off: pl.BlockSpec((s, HEAD), lambda h, i: (0, off + h))
    blk = pl.BlockSpec((tq, HEAD), lambda h, i: (i, h))
    acc = pl.BlockSpec((s, HEAD), lambda h, i: (0, h))
    tri = pl.BlockSpec((tq, tq), lambda h, i: (0, 0))
    shape = jax.ShapeDtypeStruct((s, heads * HEAD), F32)
    return pl.pallas_call(
        body, name=name, out_shape=(shape, shape, shape), grid=(heads, s // tq),
        in_specs=[pl.BlockSpec((tq, HEAD), lambda h, i: (i, q_blk + h)), full(k_blk), full(v_blk), blk, blk, tri, tri,
                  pl.BlockSpec(memory_space=pl.ANY)],
        out_specs=(blk, acc, acc),
        scratch_shapes=[pltpu.VMEM((s, HEAD), BF16), pltpu.VMEM((s, HEAD), BF16)],
        compiler_params=_params("parallel", "arbitrary"),
    )(u, u, u, tot, dy, _tri(tq, lambda r, c: r <= c), _tri(tq, lambda r, c: r < c), dep)


def _lru_time_tile(s):
    return _tile(s, (256, 128, 64, 32))


def _lru_gates(xc, wa_ref, ba_ref, wx_ref, bx_ref, lam_ref, nh):
    pr, pi = [], []
    for n in range(nh):
        xn = xc[:, n * HEAD:(n + 1) * HEAD].astype(BF16)
        pr.append(_dot(xn, wa_ref[n]))
        pi.append(_dot(xn, wx_ref[n]))
    r = _sigmoid((pr[0] if nh == 1 else jnp.concatenate(pr, axis=1)) + ba_ref[...])
    ig = _sigmoid((pi[0] if nh == 1 else jnp.concatenate(pi, axis=1)) + bx_ref[...])
    lam = lam_ref[...]
    sp = jnp.maximum(-lam, 0.0) + jnp.log(1.0 + jnp.exp(-jnp.abs(lam)))
    log_a = -LRU_C * r * sp
    a = jnp.exp(log_a)
    mult = jnp.sqrt(-_expm1(2.0 * log_a))
    return r, ig, a, mult, sp


def _lru_fwd(u, x_blk, cw, cb, wa, ba, wx, bx, lam, name):
    s = u.shape[0]
    w = lam.shape[1]
    nh = w // HEAD
    tt = _lru_time_tile(s)
    hb = tt // LRU_HALO

    def body(x_ref, xh_ref, cw_ref, cb_ref, wa_ref, ba_ref, wx_ref, bx_ref, lam_ref, y_ref,
             xbuf, abuf, bbuf, hstate, rowbuf):
        i = pl.program_id(0)

        @pl.when(i == 0)
        def _():
            hstate[...] = jnp.zeros_like(hstate)

        xbuf[0:LRU_HALO, :] = jnp.where(i > 0, xh_ref[...], 0.0)
        xbuf[LRU_HALO:LRU_HALO + tt, :] = x_ref[...]
        xc = _conv_taps(xbuf, cw_ref, tt, 4, LRU_HALO - 3) + cb_ref[...]
        _, ig, a, mult, _ = _lru_gates(xc, wa_ref, ba_ref, wx_ref, bx_ref, lam_ref, nh)
        abuf[...] = a
        bbuf[...] = mult * (ig * xc)

        def group(gi, h):
            rows = pl.ds(pl.multiple_of(gi * 8, 8), 8)
            a8 = abuf[rows, :]
            b8 = bbuf[rows, :]
            for j in range(8):
                h = a8[j:j + 1, :] * h + b8[j:j + 1, :]
                rowbuf[j:j + 1, :] = h
            y_ref[rows, :] = rowbuf[...]
            return h

        hstate[0:1, :] = lax.fori_loop(0, tt // 8, group, hstate[0:1, :])

    vec = pl.BlockSpec((1, w), lambda i: (0, 0))
    gate = pl.BlockSpec((nh, HEAD, HEAD), lambda i: (0, 0, 0))
    return pl.pallas_call(
        body, name=name, out_shape=jax.ShapeDtypeStruct((s, w), F32), grid=(s // tt,),
        in_specs=[pl.BlockSpec((tt, w), lambda i: (i, x_blk)),
                  pl.BlockSpec((LRU_HALO, w), lambda i: (jnp.maximum(i * hb - 1, 0), x_blk)),
                  pl.BlockSpec((8, w), lambda i: (0, 0)), vec, gate, vec, gate, vec, vec],
        out_specs=pl.BlockSpec((tt, w), lambda i: (i, 0)),
        scratch_shapes=[pltpu.VMEM((LRU_HALO + tt, w), F32), pltpu.VMEM((tt, w), F32), pltpu.VMEM((tt, w), F32),
                        pltpu.VMEM((8, w), F32), pltpu.VMEM((8, w), F32)],
        compiler_params=_params("arbitrary"),
    )(u, u, cw, cb, wa, ba, wx, bx, lam)


def _lru_bwd(u, x_blk, hseq, dy, cw, cb, wa, ba, wx, bx, lam, name):
    s = u.shape[0]
    w = lam.shape[1]
    nh = w // HEAD
    tt = _lru_time_tile(s)
    hb = tt // LRU_HALO
    nt = s // tt

    def body(x_ref, xh_ref, h_ref, hh_ref, dy_ref, cw_ref, cb_ref, wa_ref, ba_ref, wx_ref, bx_ref, lam_ref,
             dx_ref, dwa_ref, dwx_ref, vec_ref, xbuf, hbuf, abuf, lbuf, dbuf, cstate, dhalo, rowbuf):
        i = pl.program_id(0)
        rt = nt - 1 - i

        @pl.when(i == 0)
        def _():
            cstate[...] = jnp.zeros_like(cstate)
            dhalo[...] = jnp.zeros_like(dhalo)
            dwa_ref[...] = jnp.zeros_like(dwa_ref)
            dwx_ref[...] = jnp.zeros_like(dwx_ref)
            vec_ref[...] = jnp.zeros_like(vec_ref)

        xbuf[0:LRU_HALO, :] = jnp.where(rt > 0, xh_ref[...], 0.0)
        xbuf[LRU_HALO:LRU_HALO + tt, :] = x_ref[...]
        hbuf[0:LRU_HALO, :] = jnp.where(rt > 0, hh_ref[...], 0.0)
        hbuf[LRU_HALO:LRU_HALO + tt, :] = h_ref[...]
        xc = _conv_taps(xbuf, cw_ref, tt, 4, LRU_HALO - 3) + cb_ref[...]
        r, ig, a, mult, sp = _lru_gates(xc, wa_ref, ba_ref, wx_ref, bx_ref, lam_ref, nh)
        abuf[...] = a

        def group(gi, c):
            rows = pl.ds(pl.multiple_of((tt // 8 - 1 - gi) * 8, 8), 8)
            a8 = abuf[rows, :]
            d8 = dy_ref[rows, :]
            for j in range(7, -1, -1):
                lam_t = d8[j:j + 1, :] + c
                rowbuf[j:j + 1, :] = lam_t
                c = a8[j:j + 1, :] * lam_t
            lbuf[rows, :] = rowbuf[...]
            return c

        cstate[0:1, :] = lax.fori_loop(0, tt // 8, group, cstate[0:1, :])

        lam_t = lbuf[...]
        hprev = hbuf[pl.ds(LRU_HALO - 1, tt), :]
        ixc = ig * xc
        d_ixc = lam_t * mult
        d_ig = d_ixc * xc
        dxc = d_ixc * ig
        dlog_a = lam_t * hprev * a + lam_t * ixc * (-(a * a) / mult)
        dr = dlog_a * (-LRU_C * sp)
        lam_p = lam_ref[...]
        dsp = -_sigmoid(-lam_p)
        vec_ref[2:3, :] += jnp.sum(dlog_a * (-LRU_C * r), axis=0, keepdims=True) * dsp
        dpr = dr * r * (1.0 - r)
        dpi = d_ig * ig * (1.0 - ig)
        vec_ref[0:1, :] += jnp.sum(dpr, axis=0, keepdims=True)
        vec_ref[1:2, :] += jnp.sum(dpi, axis=0, keepdims=True)
        parts = []
        for n in range(nh):
            sl = slice(n * HEAD, (n + 1) * HEAD)
            xn = xc[:, sl].astype(BF16)
            dprn = dpr[:, sl].astype(BF16)
            dpin = dpi[:, sl].astype(BF16)
            dwa_ref[n] += _dot_tn(xn, dprn)
            dwx_ref[n] += _dot_tn(xn, dpin)
            parts.append(_dot_nt(dprn, wa_ref[n]) + _dot_nt(dpin, wx_ref[n]))
        dxc = dxc + (parts[0] if nh == 1 else jnp.concatenate(parts, axis=1))
        vec_ref[3:4, :] += jnp.sum(dxc, axis=0, keepdims=True)
        dbuf[0:tt, :] = dxc
        dbuf[tt:tt + LRU_HALO, :] = dhalo[...]
        dx = cw_ref[0:1, :] * dbuf[pl.ds(3, tt), :]
        for k in range(1, 4):
            dx = dx + cw_ref[k:k + 1, :] * dbuf[pl.ds(3 - k, tt), :]
        dx_ref[...] = dx.astype(BF16)
        for k in range(4):
            vec_ref[4 + k:5 + k, :] += jnp.sum(dxc * xbuf[pl.ds(LRU_HALO - 3 + k, tt), :], axis=0, keepdims=True)
        dhalo[...] = dbuf[0:LRU_HALO, :]

    vec = pl.BlockSpec((1, w), lambda i: (0, 0))
    gate = pl.BlockSpec((nh, HEAD, HEAD), lambda i: (0, 0, 0))
    rev = lambda i: nt - 1 - i
    tile = pl.BlockSpec((tt, w), lambda i: (rev(i), 0))
    halo = lambda col: pl.BlockSpec((LRU_HALO, w), lambda i: (jnp.maximum(rev(i) * hb - 1, 0), col))
    return pl.pallas_call(
        body, name=name,
        out_shape=(jax.ShapeDtypeStruct((s, w), BF16), jax.ShapeDtypeStruct((nh, HEAD, HEAD), F32),
                   jax.ShapeDtypeStruct((nh, HEAD, HEAD), F32), jax.ShapeDtypeStruct((8, w), F32)),
        grid=(nt,),
        in_specs=[pl.BlockSpec((tt, w), lambda i: (rev(i), x_blk)), halo(x_blk), tile, halo(0), tile,
                  pl.BlockSpec((8, w), lambda i: (0, 0)), vec, gate, vec, gate, vec, vec],
        out_specs=(tile, gate, gate, pl.BlockSpec((8, w), lambda i: (0, 0))),
        scratch_shapes=[pltpu.VMEM((LRU_HALO + tt, w), F32), pltpu.VMEM((LRU_HALO + tt, w), F32),
                        pltpu.VMEM((tt, w), F32), pltpu.VMEM((tt, w), F32), pltpu.VMEM((tt + LRU_HALO, w), F32),
                        pltpu.VMEM((8, w), F32), pltpu.VMEM((8, w), F32), pltpu.VMEM((8, w), F32)],
        compiler_params=_params("arbitrary"),
    )(u, u, hseq, hseq, dy, cw, cb, wa, ba, wx, bx, lam)


def _gate_specs(c, tr):
    return [pl.BlockSpec((tr, c), lambda i, b=b: (i, b)) for b in (2, 9, 10, 12)]


def _outgate_fwd(y_conv, y_attn, y_lru, u, n_conv, n_attn, n_lru, name):
    s, c = y_conv.shape
    tr = _tile(s, (256, 128, 64, 32, 16))

    def body(yc_ref, ya_ref, yl_ref, gc_ref, ga0_ref, ga1_ref, gl_ref, nc_ref, na_ref, nl_ref, o_ref):
        def rinv(v):
            return lax.rsqrt(jnp.mean(v * v, axis=-1, keepdims=True) + RMS_EPS)

        def silu(g):
            return g * _sigmoid(g)

        yc = yc_ref[...]
        o_ref[:, 0:c] = (yc * rinv(yc) * nc_ref[...] * silu(gc_ref[...])).astype(BF16)
        ya = ya_ref[...]
        ra = rinv(ya)
        o_ref[:, c:2 * c] = (ya[:, 0:c] * ra * na_ref[:, 0:c] * silu(ga0_ref[...])).astype(BF16)
        o_ref[:, 2 * c:3 * c] = (ya[:, c:2 * c] * ra * na_ref[:, c:2 * c] * silu(ga1_ref[...])).astype(BF16)
        yl = yl_ref[...]
        o_ref[:, 3 * c:4 * c] = (yl * rinv(yl) * nl_ref[...] * silu(gl_ref[...])).astype(BF16)

    row = lambda wd: pl.BlockSpec((tr, wd), lambda i: (i, 0))
    vec = lambda wd: pl.BlockSpec((1, wd), lambda i: (0, 0))
    return pl.pallas_call(
        body, name=name, out_shape=jax.ShapeDtypeStruct((s, 4 * c), BF16), grid=(s // tr,),
        in_specs=[row(c), row(2 * c), row(c)] + _gate_specs(c, tr) + [vec(c), vec(2 * c), vec(c)],
        out_specs=row(4 * c), compiler_params=_params("parallel"),
    )(y_conv, y_attn, y_lru, u, u, u, u, n_conv, n_attn, n_lru)


def _outgate_bwd(dy, y_conv, y_attn, y_lru, u, n_conv, n_attn, n_lru, name):
    s, c = y_conv.shape
    tr = _tile(s, (256, 128, 64, 32, 16))

    def body(dy_ref, yc_ref, ya_ref, yl_ref, gc_ref, ga0_ref, ga1_ref, gl_ref, nc_ref, na_ref, nl_ref,
             dyc_ref, dya_ref, dyl_ref, dgc_ref, dga_ref, dgl_ref, dn_ref):
        @pl.when(pl.program_id(0) == 0)
        def _():
            dn_ref[...] = jnp.zeros_like(dn_ref)

        def group(yv, gate, wv, d):
            r = lax.rsqrt(jnp.mean(yv * yv, axis=-1, keepdims=True) + RMS_EPS)
            yh = yv * r
            sg = _sigmoid(gate)
            dn = d * (gate * sg)
            dgate = d * (yh * wv) * (sg * (1.0 + gate * (1.0 - sg)))
            dw = jnp.sum(dn * yh, axis=0, keepdims=True)
            dyn = dn * wv
            dyv = r * (dyn - yh * jnp.mean(dyn * yh, axis=-1, keepdims=True))
            return dyv, dgate, dw

        dyv, dg, dw = group(yc_ref[...], gc_ref[...], nc_ref[...], dy_ref[:, 0:c])
        dyc_ref[...] = dyv
        dgc_ref[...] = dg.astype(BF16)
        dn_ref[0:1, 0:c] += dw
        gate_a = jnp.concatenate([ga0_ref[...], ga1_ref[...]], axis=1)
        dyv, dg, dw = group(ya_ref[...], gate_a, na_ref[...], dy_ref[:, c:3 * c])
        dya_ref[...] = dyv
        dga_ref[...] = dg.astype(BF16)
        dn_ref[0:1, c:3 * c] += dw
        dyv, dg, dw = group(yl_ref[...], gl_ref[...], nl_ref[...], dy_ref[:, 3 * c:4 * c])
        dyl_ref[...] = dyv
        dgl_ref[...] = dg.astype(BF16)
        dn_ref[0:1, 3 * c:4 * c] += dw

    row = lambda wd: pl.BlockSpec((tr, wd), lambda i: (i, 0))
    vec = lambda wd: pl.BlockSpec((1, wd), lambda i: (0, 0))
    sh = lambda wd, dt: jax.ShapeDtypeStruct((s, wd), dt)
    return pl.pallas_call(
        body, name=name,
        out_shape=(sh(c, F32), sh(2 * c, F32), sh(c, F32), sh(c, BF16), sh(2 * c, BF16), sh(c, BF16),
                   jax.ShapeDtypeStruct((8, 4 * c), F32)),
        grid=(s // tr,),
        in_specs=[row(4 * c), row(c), row(2 * c), row(c)] + _gate_specs(c, tr) + [vec(c), vec(2 * c), vec(c)],
        out_specs=(row(c), row(2 * c), row(c), row(c), row(2 * c), row(c),
                   pl.BlockSpec((8, 4 * c), lambda i: (0, 0))),
        compiler_params=_params("arbitrary"),
    )(dy, y_conv, y_attn, y_lru, u, u, u, u, n_conv, n_attn, n_lru)


def _xattn_probs(qh, kh, scale):
    sc = _dot_nt(qh, kh) * scale
    p = jnp.exp(sc - jnp.max(sc, axis=-1, keepdims=True))
    return p / jnp.sum(p, axis=-1, keepdims=True)


def _xattn_fwd(q, kv, name):
    s, xw = q.shape
    m = kv.shape[0]
    nh = xw // HEAD
    tq = _tile(s, (256, 128, 64, 32, 16))
    scale = HEAD ** -0.5

    def body(q_ref, kv_ref, o_ref):
        for h in range(nh):
            qh = q_ref[:, h * HEAD:(h + 1) * HEAD].astype(BF16)
            kh = kv_ref[:, h * HEAD:(h + 1) * HEAD].astype(BF16)
            vh = kv_ref[:, xw + h * HEAD:xw + (h + 1) * HEAD].astype(BF16)
            p = _xattn_probs(qh, kh, scale)
            o_ref[:, h * HEAD:(h + 1) * HEAD] = _dot(p.astype(BF16), vh).astype(BF16)

    return pl.pallas_call(
        body, name=name, out_shape=jax.ShapeDtypeStruct((s, xw), BF16), grid=(s // tq,),
        in_specs=[pl.BlockSpec((tq, xw), lambda i: (i, 0)), pl.BlockSpec((m, 2 * xw), lambda i: (0, 0))],
        out_specs=pl.BlockSpec((tq, xw), lambda i: (i, 0)), compiler_params=_params("parallel"),
    )(q, kv)


def _xattn_bwd(q, kv, do, name):
    s, xw = q.shape
    m = kv.shape[0]
    nh = xw // HEAD
    tq = _tile(s, (256, 128, 64, 32, 16))
    scale = HEAD ** -0.5

    def body(q_ref, kv_ref, do_ref, dq_ref, dkv_ref):
        @pl.when(pl.program_id(0) == 0)
        def _():
            dkv_ref[...] = jnp.zeros_like(dkv_ref)

        for h in range(nh):
            ks = slice(h * HEAD, (h + 1) * HEAD)
            vs = slice(xw + h * HEAD, xw + (h + 1) * HEAD)
            qh = q_ref[:, ks].astype(BF16)
            kh = kv_ref[:, ks].astype(BF16)
            vh = kv_ref[:, vs].astype(BF16)
            doh = do_ref[:, ks].astype(BF16)
            p = _xattn_probs(qh, kh, scale)
            dkv_ref[:, vs] += _dot_tn(p.astype(BF16), doh)
            dp = _dot_nt(doh, vh)
            ds = (p * (dp - jnp.sum(dp * p, axis=-1, keepdims=True)) * scale).astype(BF16)
            dq_ref[:, ks] = _dot(ds, kh).astype(BF16)
            dkv_ref[:, ks] += _dot_tn(ds, qh)

    row = pl.BlockSpec((tq, xw), lambda i: (i, 0))
    full = pl.BlockSpec((m, 2 * xw), lambda i: (0, 0))
    return pl.pallas_call(
        body, name=name,
        out_shape=(jax.ShapeDtypeStruct((s, xw), BF16), jax.ShapeDtypeStruct((m, 2 * xw), F32)), grid=(s // tq,),
        in_specs=[row, full, row], out_specs=(row, full), compiler_params=_params("arbitrary"),
    )(q, kv, do)


def _adamw(w, g, m, v, name):
    rows, cols = w.shape
    tr = _tile(rows, (512, 256, 128, 64, 32, 16, 8)) if rows % 8 == 0 else rows
    bc1 = 1.0 - ADAM_B1 ** ADAM_STEP
    bc2 = 1.0 - ADAM_B2 ** ADAM_STEP

    def body(w_ref, g_ref, m_ref, v_ref, d_ref, nm_ref, nv_ref):
        gv = g_ref[...]
        nm = ADAM_B1 * m_ref[...] + (1.0 - ADAM_B1) * gv
        nv = ADAM_B2 * v_ref[...] + (1.0 - ADAM_B2) * (gv * gv)
        nm_ref[...] = nm
        nv_ref[...] = nv
        d_ref[...] = -ADAM_LR * ((nm / bc1) / (jnp.sqrt(nv / bc2) + ADAM_EPS) + ADAM_WD * w_ref[...])

    spec = pl.BlockSpec((tr, cols), lambda i: (i, 0))
    sh = jax.ShapeDtypeStruct((rows, cols), F32)
    return pl.pallas_call(
        body, name=name, out_shape=(sh, sh, sh), grid=(rows // tr,), in_specs=[spec] * 4,
        out_specs=(spec, spec, spec), compiler_params=_params("parallel"),
    )(w, g, m, v)


WEIGHTS = ['mix_norm_g', 'w_in', 'conv_dw_w', 'conv_dw_b', 'conv_ln_g', 'conv_ln_b', 'conv_pw_w', 'lru_conv_w',
           'lru_conv_b', 'lru_wa', 'lru_ba', 'lru_wx', 'lru_bx', 'lru_lambda', 'out_norm_conv', 'out_norm_attn',
           'out_norm_lru', 'w_out', 'xattn_norm_g', 'mem_norm_g', 'xattn_wq', 'xattn_wkv', 'xattn_wo',
           'final_norm_g']
BIG_SHARDED = {'w_in': 2, 'conv_pw_w': 1, 'w_out': 1, 'xattn_wq': 1, 'xattn_wkv': 1, 'xattn_wo': 2}
SMALL_SHARDED = {'conv_dw_w': 2, 'lru_conv_w': 2}


def _layer_fwd(x, mem, p, l, first, rest):
    row = lambda name: p[name][l][None, :]
    c = p['conv_dw_b'].shape[1]
    heads = 2 * c // HEAD
    h = _rms_fwd(x, row('mix_norm_g'), "rms_mix")
    w_in, tied = first(h)
    u = _mm(h, w_in, dep=tied, name="in_proj")
    q_blk = 3 * c // HEAD
    y_attn, tot = _sb_fwd(u, heads, q_blk, q_blk + heads, q_blk + 2 * heads, "sb_fwd")
    fw = dict(rest(y_attn), w_in=w_in)
    wpad = jnp.pad(fw['conv_dw_w'][l], ((0, 1), (0, 0)))
    cw = jnp.pad(fw['lru_conv_w'][l], ((0, 4), (0, 0)))
    wa, wx = p['lru_wa'][l].astype(BF16), p['lru_wx'][l].astype(BF16)
    y_lru = _lru_fwd(u, 11, cw, row('lru_conv_b'), wa, row('lru_ba'), wx, row('lru_bx'), row('lru_lambda'), "lru_fwd")
    y_conv = _conv_fwd(u, wpad, row('conv_dw_b'), row('conv_ln_g'), row('conv_ln_b'), fw['conv_pw_w'][None], 0,
                       "conv_fwd")
    yc = _outgate_fwd(y_conv, y_attn, y_lru, u, row('out_norm_conv'), row('out_norm_attn'), row('out_norm_lru'),
                      "outgate_fwd")
    x1 = _mm(yc, fw['w_out'], add=x, name="out_proj")
    h2 = _rms_fwd(x1, row('xattn_norm_g'), "rms_xattn")
    memn = _rms_fwd(mem, row('mem_norm_g'), "rms_mem")
    q2 = _mm(h2, fw['xattn_wq'], name="xq_proj")
    kv = _mm(memn, fw['xattn_wkv'], name="xkv_proj")
    o2 = _xattn_fwd(q2, kv, "xattn_fwd")
    x2 = _mm(o2, fw['xattn_wo'], add=x1, name="xo_proj")
    saved = dict(x=x, h=h, u=u, wpad=wpad, cw=cw, wa=wa, wx=wx, y_conv=y_conv, y_attn=y_attn, tot=tot, y_lru=y_lru,
                 yc=yc, x1=x1, h2=h2, memn=memn, q2=q2, kv=kv, o2=o2, fw=fw)
    return x2, saved


def _layer_bwd(dx2, mem, p, l, sv, rest_ready, w_in_ready):
    row = lambda name: p[name][l][None, :]
    c = p['conv_dw_b'].shape[1]
    heads = 2 * c // HEAD
    fw = sv['fw']
    g, big = {}, {}
    big['xattn_wo'] = _mm(sv['o2'], dx2, ta=True, out_dtype=BF16, name="d_wo")
    do2 = _mm(dx2, fw['xattn_wo'], tb=True, name="d_o2")
    dq2, dkv = _xattn_bwd(sv['q2'], sv['kv'], do2, "xattn_bwd")
    big['xattn_wq'] = _mm(sv['h2'], dq2, ta=True, out_dtype=BF16, name="d_wq")
    dh2 = _mm(dq2, fw['xattn_wq'], tb=True, name="d_h2")
    big['xattn_wkv'] = _mm(sv['memn'], dkv, ta=True, out_dtype=BF16, name="d_wkv")
    dmemn = _mm(dkv, fw['xattn_wkv'], tb=True, name="d_memn")
    _, g['mem_norm_g'] = _rms_bwd(mem, row('mem_norm_g'), dmemn, None, "rms_mem_bwd")
    dx1, g['xattn_norm_g'] = _rms_bwd(sv['x1'], row('xattn_norm_g'), dh2, dx2, "rms_xattn_bwd")
    big['w_out'] = _mm(sv['yc'], dx1, ta=True, out_dtype=BF16, name="d_wout")
    dyc = _mm(dx1, fw['w_out'], tb=True, name="d_yc")
    u = sv['u']
    d_yconv, d_yattn, d_ylru, dgc, dga, dgl, dn = _outgate_bwd(
        dyc, sv['y_conv'], sv['y_attn'], sv['y_lru'], u, row('out_norm_conv'), row('out_norm_attn'),
        row('out_norm_lru'), "outgate_bwd")
    g['out_norm_conv'], g['out_norm_attn'], g['out_norm_lru'] = dn[0, 0:c], dn[0, c:3 * c], dn[0, 3 * c:4 * c]
    dd, gl, dpw, cvec = _conv_bwd_post(u, d_yconv, sv['wpad'], row('conv_dw_b'), row('conv_ln_g'),
                                       row('conv_ln_b'), fw['conv_pw_w'][None], 0, "conv_bwd_post")
    big['conv_pw_w'] = dpw.astype(BF16)
    g['conv_ln_g'], g['conv_ln_b'], g['conv_dw_b'] = cvec[0], cvec[1], cvec[2]
    dval, dglu, ddw = _conv_bwd_dw(u, dd, gl, sv['wpad'], "conv_bwd_dw")
    g['conv_dw_w'] = ddw[0:31]
    dxr, g['lru_wa'], g['lru_wx'], lvec = _lru_bwd(
        u, 11, sv['y_lru'], d_ylru, sv['cw'], row('lru_conv_b'), sv['wa'], row('lru_ba'), sv['wx'], row('lru_bx'),
        row('lru_lambda'), "lru_bwd")
    g['lru_ba'], g['lru_bx'], g['lru_lambda'], g['lru_conv_b'] = lvec[0], lvec[1], lvec[2], lvec[3]
    g['lru_conv_w'] = lvec[4:8]
    tied = rest_ready(big, g)
    q_blk = 3 * c // HEAD
    dq, dk, dv = _sb_bwd(u, sv['tot'], d_yattn, heads, q_blk, q_blk + heads, q_blk + 2 * heads, tied, "sb_bwd")
    du = jnp.concatenate([dval, dglu, dgc, dq.astype(BF16), dk.astype(BF16), dv.astype(BF16), dga, dxr, dgl], axis=1)
    tied = w_in_ready(_mm(sv['h'], du, ta=True, out_dtype=BF16, name="d_win"))
    dh = _mm(du, fw['w_in'], tb=True, dep=tied, name="d_h")
    dx0, g['mix_norm_g'] = _rms_bwd(sv['x'], row('mix_norm_g'), dh, dx1, "rms_mix_bwd")
    return dx0, g


def kernel(x, mem, mix_norm_g, w_in, conv_dw_w, conv_dw_b, conv_ln_g, conv_ln_b, conv_pw_w, lru_conv_w, lru_conv_b, lru_wa, lru_ba, lru_wx, lru_bx, lru_lambda, out_norm_conv, out_norm_attn, out_norm_lru, w_out, xattn_norm_g, mem_norm_g, xattn_wq, xattn_wkv, xattn_wo, final_norm_g, loss_target, m_mix_norm_g, m_w_in, m_conv_dw_w, m_conv_dw_b, m_conv_ln_g, m_conv_ln_b, m_conv_pw_w, m_lru_conv_w, m_lru_conv_b, m_lru_wa, m_lru_ba, m_lru_wx, m_lru_bx, m_lru_lambda, m_out_norm_conv, m_out_norm_attn, m_out_norm_lru, m_w_out, m_xattn_norm_g, m_mem_norm_g, m_xattn_wq, m_xattn_wkv, m_xattn_wo, m_final_norm_g, v_mix_norm_g, v_w_in, v_conv_dw_w, v_conv_dw_b, v_conv_ln_g, v_conv_ln_b, v_conv_pw_w, v_lru_conv_w, v_lru_conv_b, v_lru_wa, v_lru_ba, v_lru_wx, v_lru_bx, v_lru_lambda, v_out_norm_conv, v_out_norm_attn, v_out_norm_lru, v_w_out, v_xattn_norm_g, v_mem_norm_g, v_xattn_wq, v_xattn_wkv, v_xattn_wo, v_final_norm_g):
    args = locals()
    w = {n: args[n] for n in WEIGHTS}
    mom = {n: args["m_" + n] for n in WEIGHTS}
    var = {n: args["v_" + n] for n in WEIGHTS}
    depth = w_in.shape[0]
    c = conv_dw_b.shape[1]
    assert out_norm_attn.shape[1] == 2 * c and lru_lambda.shape[1] == c and w_in.shape[2] * N_DEV == 13 * c
    assert c % HEAD == 0 and x.shape[0] == 1 and mem.shape[0] == 1
    xs, mems, tgt = x[0], mem[0], loss_target[0]
    me = 4 * lax.axis_index("x") + 2 * lax.axis_index("y") + lax.axis_index("c")

    assert depth == 2 and (2 * w_in.shape[2]) % LANE == 0 and w_in.shape[2] % LANE in (0, LANE // 2)
    my_c = lax.axis_index("c")
    big = list(BIG_SHARDED)
    dev = [n for n in big if n != 'w_in']
    my_x, my_y = lax.axis_index("x"), lax.axis_index("y")
    small = list(SMALL_SHARDED)
    p = dict(w)
    blk_width = w_in.shape[2]
    w_in_blk = jnp.pad(w_in.astype(BF16), ((0, 0), (0, 0), (0, -blk_width % LANE)))
    w_in0_blocks = _all_gather(w_in_blk[0], "gather_w_in0")
    w_in0 = _join_cols(w_in0_blocks, blk_width, "join_w_in")
    taps = _pack([w[n] for n in small], F32)[None]

    def gather_group(names, l, stage, dep, tag, with_taps=False):
        srcs, shapes, plans = [], [], []
        if with_taps:
            srcs, shapes, plans = [taps], [(N_DEV,) + taps.shape[1:]], [_gather_plan(0, 1, stage)]
        for n in names:
            if n == 'w_in':
                srcs.append(w_in_blk[l][None])
                shapes.append((N_DEV,) + w_in_blk.shape[1:])
                plans.append(_gather_plan(0, 1, stage))
            else:
                axis = BIG_SHARDED[n] - 1
                blk = w[n][l].astype(BF16)
                width = blk.shape[axis]
                srcs.append(blk)
                shapes.append(blk.shape[:axis] + (N_DEV * width,) + blk.shape[axis + 1:])
                plans.append(_gather_plan(axis, width, stage))
        return _split_start(srcs, shapes, plans, dep, "gather_start_" + tag)

    def forward_group(names, landed, dep, tag):
        plans = [_forward_plan(0, 1) if n == 'w_in' else _forward_plan(BIG_SHARDED[n] - 1, w[n].shape[BIG_SHARDED[n]])
                 for n in names]
        return _split_start(landed, [None] * len(names), plans, dep, "gather_start_" + tag)

    g0, tied = gather_group(dev, 0, "direct", w_in0_blocks, "rest0", with_taps=True)
    g1, tied = gather_group(big, 1, "first", tied, "layer1")
    layer1, full_taps = {}, {}

    def first_of(l):
        def first(after):
            if l == 0:
                return w_in0, tied
            layer1.update(zip(big, _split_wait(layer1.pop('forwarding'), after, "gather_wait_layer1_passed")))
            layer1['w_in'] = _join_cols(layer1['w_in'], blk_width, "join_w_in")
            return layer1['w_in'], None
        return first

    def rest_of(l):
        def rest(after):
            if l == 0:
                landed = _split_wait(g0, after, "gather_wait_rest0")
                full_taps.update({n: _join_blocks(blk, SMALL_SHARDED[n]) for n, blk in
                                  zip(small, _unpack(landed[0], [w[n].shape for n in small], lead=N_DEV))})
                got = dict(zip(dev, landed[1:]), **full_taps)
                landed = _split_wait(g1, after, "gather_wait_layer1")
                layer1['forwarding'], token = forward_group(big, landed, landed[0], "layer1_passed")
                got['conv_pw_w'] = got['conv_pw_w'] + token[0:1, 0:1].astype(BF16)
                return got
            return dict({n: layer1[n] for n in dev}, **full_taps)
        return rest

    saved = []
    act = xs
    for l in range(depth):
        act, sv = _layer_fwd(act, mems, p, l, first_of(l), rest_of(l))
        saved.append(sv)
    loss_part, dact, d_final = _loss_bwd(act, final_norm_g[None, :], tgt, "loss_bwd")

    def grad_window(n):
        axis = BIG_SHARDED[n] - 1
        blk = w[n].shape[axis + 1]
        pad = blk % LANE if axis == 1 else 0
        return axis, blk + pad, lambda px, py, pc: blk * (4 * px + 2 * py + pc) - pad * pc

    scattering = {}

    def scatter_group(names, arrs, l, tag):
        shapes, plans = [], []
        for n, g in zip(names, arrs):
            axis, width, start = grad_window(n)
            shapes.append((N_DEV,) + g.shape[:axis] + (width,) + g.shape[axis + 1:])
            plans.append(_scatter_plan(axis, width, start))
        scattering[(l, tag)], token = _split_start(arrs, shapes, plans, arrs[0], f"scatter_start_{tag}{l}")
        return token

    rest = [n for n in WEIGHTS if n not in BIG_SHARDED]
    early = [n for n in rest if n != 'mix_norm_g']
    small_pending, partial = {}, {}

    def start_small(names, dep, tag):
        vec = _pack([partial[n] for n in names], F32)
        small_pending[tag], token = _split_start([vec[None]], [(N_DEV,) + vec.shape], [_gather_plan(0, 1, "direct")],
                                                 dep, "small_grads_start_" + tag)
        return token

    layer_grads = [None] * depth

    def rest_ready(big_grads, g_small, l):
        token = scatter_group(dev, [big_grads[n] for n in dev], l, "rest")
        if l == 0:
            partial.update({n: jnp.stack([g_small[n], layer_grads[1][n]]) for n in early if n != 'final_norm_g'})
            partial.update(final_norm_g=d_final[0], loss=loss_part)
            token = start_small(early + ['loss'], token, "early")
        return token

    for l in reversed(range(depth)):
        dact, layer_grads[l] = _layer_bwd(
            dact, mems, p, l, saved[l], functools.partial(rest_ready, l=l),
            lambda g_w_in, l=l: scatter_group(['w_in'], [g_w_in], l, "w_in"))
    grad_x = dact[None]
    partial['mix_norm_g'] = jnp.stack([layer_grads[l]['mix_norm_g'] for l in range(depth)])
    tied = start_small(['mix_norm_g'], partial['mix_norm_g'], "late")

    grads, delta, new_m, new_v = {}, {}, {}, {}

    def update(n, received):
        summed = jnp.stack([_sum_blocks(received[l], "sum_" + n) for l in range(depth)])
        width = w[n].shape[-1]
        if summed.shape[-1] != width:
            summed = jnp.where(my_c == 0, summed[..., :width], summed[..., summed.shape[-1] - width:])
        grads[n] = summed
        shape = w[n].shape
        two_d = lambda a: a.reshape(-1, shape[-1])
        d, nm, nv = _adamw(two_d(w[n]), two_d(summed), two_d(mom[n]), two_d(var[n]), "adamw_" + n)
        delta[n], new_m[n], new_v[n] = d.reshape(shape), nm.reshape(shape), nv.reshape(shape)

    landed = [_split_wait(scattering[(l, "rest")], tied, f"scatter_wait_rest{l}") for l in range(depth)]
    for t, n in enumerate(dev):
        update(n, [landed[l][t] for l in range(depth)])
    def update_small(names, tag, after):
        sent = names + ['loss'] if tag == "early" else names
        total = _sum_blocks(_split_wait(small_pending[tag], after, "small_grads_wait_" + tag)[0], "sum_small_" + tag)
        grads.update(zip(sent, _unpack(total, [partial[n].shape for n in sent])))
        for n in names:
            if n in SMALL_SHARDED:
                width = w[n].shape[2]
                grads[n] = lax.dynamic_slice_in_dim(grads[n], me * width, width, axis=2)
        packed = [_pack([src[n] for n in names], F32) for src in (w, grads, mom, var)]
        outs = _adamw(*packed, "adamw_small_" + tag)
        for dst, o in zip((delta, new_m, new_v), outs):
            dst.update(dict(zip(names, _unpack(o, [w[n].shape for n in names]))))
        return outs[0]

    after = update_small(early, "early", sum(delta[n].reshape(-1)[:1] for n in dev))
    loss = grads.pop('loss')[0, 0]
    update('w_in', [_split_wait(scattering[(l, "w_in")], after, f"scatter_wait_w_in{l}")[0] for l in range(depth)])
    update_small(['mix_norm_g'], "late", delta['w_in'])

    return (loss, grad_x, *[grads[n] for n in WEIGHTS], *[delta[n] for n in WEIGHTS],
            *[new_m[n] for n in WEIGHTS], *[new_v[n] for n in WEIGHTS])
```
